```python
import math
import jax, jax.numpy as jnp
from jax import lax
import numpy as np

D_MODEL = 1024
BATCH = 32
SEQ = 256
DEPTH = 2
DEC_BATCH = 2
DEC_SEQ = 1024
PAST_LEN = 512

GRID_W = 64
FNET_GROUPS = 4
FNET_GROUP_W = 96
FNET_W = FNET_GROUPS * FNET_GROUP_W
GLA_HEADS = 4
GLA_DK = 64
GLA_DV = 128
GLA_GATE_RANK = 16
GLA_TAU = 16.0
GLA_CHUNK = 64
MLA_HEADS = 8
MLA_NOPE = 64
MLA_ROPE = 32
MLA_V = 64
MLA_Q_LORA = 384
MLA_KV_LORA = 256
ROPE_BASE = 10000.0
ATTN_BLOCK = 128
N_EXPERTS = 64
TOP_K = 8
N_GROUPS = 8
TOPK_GROUPS = 4
EXPERT_DIM = 256
SHARED_DIM = 256
ROUTED_SCALE = 2.5
N_BRANCH = 3
NORM_EPS = 1e-6
IN_SPLIT_SIZES = (FNET_W, GLA_HEADS * GLA_DK, GLA_HEADS * GLA_DK, GLA_HEADS * GLA_DV, GLA_HEADS * GLA_DV, GLA_GATE_RANK, GLA_GATE_RANK, MLA_Q_LORA, MLA_KV_LORA, MLA_ROPE, N_BRANCH * D_MODEL)
IN_COLS = sum(IN_SPLIT_SIZES)

kernel_name = 'hybrid_fnet_gla_mla_moe_dit_step'


def rmsnorm(x, g):
    xf = x.astype(jnp.float32)
    y = xf * lax.rsqrt(jnp.mean(xf * xf, axis=-1, keepdims=True) + NORM_EPS)
    return (y * g.astype(jnp.float32)).astype(x.dtype)


def modulation(cond, w_mod, b_mod):
    m = jax.nn.silu(cond) @ w_mod + b_mod
    return jnp.split(m[:, None, :], 6, axis=-1)


def split_in(z):
    idx = np.cumsum(IN_SPLIT_SIZES)[:-1].tolist()
    return jnp.split(z, idx, axis=-1)


def axial_rope(x):
    n = x.shape[1]
    rows = n // GRID_W
    row = jnp.repeat(jnp.arange(rows, dtype=jnp.float32), GRID_W)
    col = jnp.tile(jnp.arange(GRID_W, dtype=jnp.float32), rows)
    half = MLA_ROPE // 2
    inv = ROPE_BASE ** (-jnp.arange(0, half, 2, dtype=jnp.float32) / half)
    ang = jnp.concatenate([row[:, None] * inv, col[:, None] * inv], axis=-1)
    ang = ang.reshape((n,) + (1,) * (x.ndim - 3) + (MLA_ROPE // 2,))
    cos, sin = jnp.cos(ang), jnp.sin(ang)
    xf = x.astype(jnp.float32).reshape(x.shape[:-1] + (MLA_ROPE // 2, 2))
    x1, x2 = xf[..., 0], xf[..., 1]
    out = jnp.stack([x1 * cos - x2 * sin, x1 * sin + x2 * cos], axis=-1)
    return out.reshape(x.shape).astype(x.dtype)


def fourier_mix(u):
    B, N, _ = u.shape
    ug = u.astype(jnp.float32).reshape(B, N, FNET_GROUPS, FNET_GROUP_W)
    f = jnp.fft.fft2(ug, axes=(1, 3), norm='ortho').real
    return f.reshape(B, N, FNET_W).astype(u.dtype)


def gla_chunked(q, k, v, logg, s0):
    B, N, H, DK = q.shape
    DV = v.shape[-1]
    C = GLA_CHUNK
    nc = N // C

    def to_chunks(a):
        return a.astype(jnp.float32).reshape(B, nc, C, H, a.shape[-1]).transpose(1, 0, 3, 2, 4)

    qc, kc, vc, gc = to_chunks(q), to_chunks(k), to_chunks(v), to_chunks(logg)
    causal = jnp.tril(jnp.ones((C, C), dtype=bool))

    def step(S, inp):
        qi, ki, vi, gi = inp
        b = jnp.cumsum(gi, axis=2)
        inter = jnp.einsum('bhtd,bhde->bhte', qi * jnp.exp(b), S)
        diff = b[:, :, :, None, :] - b[:, :, None, :, :]
        decay = jnp.exp(jnp.where(causal[:, :, None], diff, -jnp.inf))
        scores = jnp.einsum('bhtd,bhsd,bhtsd->bhts', qi, ki, decay)
        intra = jnp.einsum('bhts,bhse->bhte', scores, vi)
        b_last = b[:, :, -1:, :]
        S_new = jnp.exp(b_last[:, :, 0, :])[..., None] * S + jnp.einsum('bhsd,bhse->bhde', ki * jnp.exp(b_last - b), vi)
        return S_new, inter + intra

    S_fin, oc = lax.scan(step, s0.astype(jnp.float32), (qc, kc, vc, gc))
    o = oc.transpose(1, 0, 3, 2, 4).reshape(B, N, H, DV)
    return o, S_fin


def gla_branch(zq, zk, zv, zr, zgf, zgb, w_gf, b_gf, w_gb, b_gb, g_out, s0f, s0b):
    B, N, _ = zq.shape
    q = zq.reshape(B, N, GLA_HEADS, GLA_DK) * (GLA_DK ** -0.5)
    k = zk.reshape(B, N, GLA_HEADS, GLA_DK)
    v = zv.reshape(B, N, GLA_HEADS, GLA_DV)
    lg_f = jax.nn.log_sigmoid((zgf @ w_gf + b_gf).astype(jnp.float32)).reshape(B, N, GLA_HEADS, GLA_DK) / GLA_TAU
    lg_b = jax.nn.log_sigmoid((zgb @ w_gb + b_gb).astype(jnp.float32)).reshape(B, N, GLA_HEADS, GLA_DK) / GLA_TAU
    o_f, s_f = gla_chunked(q, k, v, lg_f, s0f)
    fl = lambda a: jnp.flip(a, axis=1)
    o_b, s_b = gla_chunked(fl(q), fl(k), fl(v), fl(lg_b), s0b)
    o = o_f + fl(o_b)
    o = o * lax.rsqrt(jnp.mean(o * o, axis=-1, keepdims=True) + NORM_EPS) * g_out.astype(jnp.float32)
    o = o.reshape(B, N, GLA_HEADS * GLA_DV).astype(zq.dtype) * jax.nn.silu(zr)
    return o, s_f, s_b


def attention(q, k, v):
    B, Sq, H, Dq = q.shape
    nb = Sq // ATTN_BLOCK
    scale = Dq ** -0.5
    qb = q.reshape(B, nb, ATTN_BLOCK, H, Dq).transpose(1, 0, 2, 3, 4)

    def one(qi):
        s = jnp.einsum('bqhd,bkhd->bhqk', qi, k).astype(jnp.float32) * scale
        pr = jax.nn.softmax(s, axis=-1).astype(v.dtype)
        return jnp.einsum('bhqk,bkhe->bqhe', pr, v)

    o = lax.map(one, qb)
    return o.transpose(1, 0, 2, 3, 4).reshape(B, Sq, H, v.shape[-1])


def mla_branch(zcq, zckv, zkr, g_q_a, w_q_up, g_kv_a, w_kv_up, ctx):
    B, N, _ = zcq.shape
    qf = (rmsnorm(zcq, g_q_a) @ w_q_up).reshape(B, N, MLA_HEADS, MLA_NOPE + MLA_ROPE)
    q_nope, q_rope = qf[..., :MLA_NOPE], qf[..., MLA_NOPE:]
    ckv = rmsnorm(zckv, g_kv_a)

    def kv_up(cc):
        kv = (cc @ w_kv_up).reshape(cc.shape[0], cc.shape[1], MLA_HEADS, MLA_NOPE + MLA_V)
        return kv[..., :MLA_NOPE], kv[..., MLA_NOPE:]

    k_nope, v = kv_up(ckv)
    k_rope = zkr
    if ctx is not None:
        ckv_ctx, kr_ctx = ctx
        q_rope = axial_rope(q_rope)
        kn_ctx, v_ctx = kv_up(ckv_ctx.astype(ckv.dtype))
        k_nope = jnp.concatenate([kn_ctx, k_nope], axis=1)
        v = jnp.concatenate([v_ctx, v], axis=1)
        k_rope = jnp.concatenate([kr_ctx.astype(zkr.dtype), axial_rope(zkr)], axis=1)
    Sk = k_nope.shape[1]
    q = jnp.concatenate([q_nope, q_rope], axis=-1)
    k = jnp.concatenate([k_nope, jnp.broadcast_to(k_rope[:, :, None, :], (B, Sk, MLA_HEADS, MLA_ROPE))], axis=-1)
    o = attention(q, k, v).reshape(B, N, MLA_HEADS * MLA_V)
    return o, ckv


def moe(h, w_router, b_router, w_eg, w_eu, w_ed, w_sg, w_su, w_sd):
    B, N, D = h.shape
    t = h.reshape(-1, D)
    T = t.shape[0]
    scores = jax.nn.sigmoid((t @ w_router).astype(jnp.float32))
    sel = scores + b_router.astype(jnp.float32)
    sel_g = sel.reshape(T, N_GROUPS, N_EXPERTS // N_GROUPS)
    group_score = lax.top_k(sel_g, 2)[0].sum(-1)
    _, gidx = lax.top_k(group_score, TOPK_GROUPS)
    gmask = jax.nn.one_hot(gidx, N_GROUPS, dtype=jnp.float32).sum(1) > 0
    masked = jnp.where(gmask[:, :, None], sel_g, -jnp.inf).reshape(T, N_EXPERTS)
    _, eidx = lax.top_k(masked, TOP_K)
    w = jnp.take_along_axis(scores, eidx, axis=-1)
    w = w / jnp.sum(w, axis=-1, keepdims=True) * ROUTED_SCALE
    gates = jnp.einsum('tk,tke->te', w, jax.nn.one_hot(eidx, N_EXPERTS, dtype=jnp.float32)).astype(h.dtype)

    def expert_step(acc, ew):
        wg, wu, wd, ge = ew
        hid = jax.nn.silu(t @ wg) * (t @ wu)
        return acc + (ge[:, None] * hid) @ wd, None

    routed, _ = lax.scan(expert_step, jnp.zeros_like(t), (w_eg, w_eu, w_ed, gates.T))
    shared = (jax.nn.silu(t @ w_sg) * (t @ w_su)) @ w_sd
    return (routed + shared).reshape(B, N, D)


def trunk_layer(x, mod, lp, ctx):
    sh1, sc1, g1, sh2, sc2, g2 = mod
    B, N, D = x.shape
    h = rmsnorm(x, lp['g_norm1']) * (1 + sc1) + sh1
    (u_f, zq, zk, zv, zr, zgf, zgb, zcq, zckv, zkr, zg) = split_in(h @ lp['w_in'])
    out_a = fourier_mix(u_f) @ lp['w_br_fourier']
    if ctx is None:
        s0 = jnp.zeros((B, GLA_HEADS, GLA_DK, GLA_DV), jnp.float32)
        s0f, s0b, mla_ctx = s0, s0, None
    else:
        s0f, s0b, ckv_ctx, kr_ctx = ctx
        mla_ctx = (ckv_ctx, kr_ctx)
    o_gla, s_f, s_b = gla_branch(zq, zk, zv, zr, zgf, zgb, lp['w_gla_gate_f'], lp['b_gla_gate_f'], lp['w_gla_gate_b'], lp['b_gla_gate_b'], lp['g_gla_out'], s0f, s0b)
    o_mla, ckv = mla_branch(zcq, zckv, zkr, lp['g_q_a'], lp['w_q_up'], lp['g_kv_a'], lp['w_kv_up'], mla_ctx)
    gates = jax.nn.sigmoid(zg).reshape(B, N, N_BRANCH, D)
    merged = gates[:, :, 0] * out_a + gates[:, :, 1] * (o_gla @ lp['w_br_gla']) + gates[:, :, 2] * (o_mla @ lp['w_br_mla'])
    x = x + g1 * (merged @ lp['w_out'])
    h2 = rmsnorm(x, lp['g_norm2']) * (1 + sc2) + sh2
    x = x + g2 * moe(h2, lp['w_router'], lp['b_router'], lp['w_exp_gate'], lp['w_exp_up'], lp['w_exp_down'], lp['w_sh_gate'], lp['w_sh_up'], lp['w_sh_down'])
    return x, (s_f, s_b, ckv, zkr)


def setup_inputs(seed: int = 0) -> dict:
    key = jax.random.key(seed)
    ks = jax.random.split(key, 40)
    f32 = jnp.float32
    L, D = DEPTH, D_MODEL

    def nrm(i, shape, scale):
        return jax.random.normal(ks[i], shape, f32) * scale

    def gain(i, shape):
        return 1.0 + 0.1 * jax.random.normal(ks[i], shape, f32)

    return {
        'x_prompt': nrm(0, (BATCH, SEQ, D), 1.0),
        'x_sample': nrm(1, (DEC_BATCH, DEC_SEQ, D), 1.0),
        'state_gla_fwd': nrm(2, (DEC_BATCH, L, GLA_HEADS, GLA_DK, GLA_DV), 1.0),
        'state_gla_bwd': nrm(3, (DEC_BATCH, L, GLA_HEADS, GLA_DK, GLA_DV), 1.0),
        'cache_mla_ckv': nrm(4, (DEC_BATCH, L, PAST_LEN, MLA_KV_LORA), 1.0),
        'cache_mla_krope': nrm(5, (DEC_BATCH, L, PAST_LEN, MLA_ROPE), 1.0),
        'c': nrm(6, (DEC_BATCH, D), 1.0),
        'c_ctx': nrm(7, (D,), 1.0),
        'w_mod': nrm(8, (L, D, 6 * D), 0.5 * D ** -0.5),
        'b_mod': nrm(9, (L, 6 * D), 0.01),
        'g_norm1': gain(10, (L, D)),
        'g_norm2': gain(11, (L, D)),
        'w_in': nrm(12, (L, D, IN_COLS), D ** -0.5),
        'w_gla_gate_f': nrm(13, (L, GLA_GATE_RANK, GLA_HEADS * GLA_DK), GLA_GATE_RANK ** -0.5),
        'b_gla_gate_f': nrm(14, (L, GLA_HEADS * GLA_DK), 0.1),
        'w_gla_gate_b': nrm(15, (L, GLA_GATE_RANK, GLA_HEADS * GLA_DK), GLA_GATE_RANK ** -0.5),
        'b_gla_gate_b': nrm(16, (L, GLA_HEADS * GLA_DK), 0.1),
        'g_gla_out': gain(17, (L, GLA_HEADS, GLA_DV)),
        'g_q_a': gain(18, (L, MLA_Q_LORA)),
        'w_q_up': nrm(19, (L, MLA_Q_LORA, MLA_HEADS * (MLA_NOPE + MLA_ROPE)), MLA_Q_LORA ** -0.5),
        'g_kv_a': gain(20, (L, MLA_KV_LORA)),
        'w_kv_up': nrm(21, (L, MLA_KV_LORA, MLA_HEADS * (MLA_NOPE + MLA_V)), MLA_KV_LORA ** -0.5),
        'w_br_fourier': nrm(22, (L, FNET_W, D), FNET_W ** -0.5),
        'w_br_gla': nrm(23, (L, GLA_HEADS * GLA_DV, D), (GLA_HEADS * GLA_DV) ** -0.5),
        'w_br_mla': nrm(24, (L, MLA_HEADS * MLA_V, D), (MLA_HEADS * MLA_V) ** -0.5),
        'w_out': nrm(25, (L, D, D), D ** -0.5),
        'w_router': nrm(26, (L, D, N_EXPERTS), D ** -0.5),
        'b_router': nrm(27, (L, N_EXPERTS), 0.01),
        'w_exp_gate': nrm(28, (L, N_EXPERTS, D, EXPERT_DIM), D ** -0.5),
        'w_exp_up': nrm(29, (L, N_EXPERTS, D, EXPERT_DIM), D ** -0.5),
        'w_exp_down': nrm(30, (L, N_EXPERTS, EXPERT_DIM, D), EXPERT_DIM ** -0.5),
        'w_sh_gate': nrm(31, (L, D, SHARED_DIM), D ** -0.5),
        'w_sh_up': nrm(32, (L, D, SHARED_DIM), D ** -0.5),
        'w_sh_down': nrm(33, (L, SHARED_DIM, D), SHARED_DIM ** -0.5),
        'g_final': gain(34, (D,)),
    }


def reference(x_prompt, x_sample, state_gla_fwd, state_gla_bwd, cache_mla_ckv, cache_mla_krope, c, c_ctx, w_mod, b_mod, g_norm1, g_norm2, w_in, w_gla_gate_f, b_gla_gate_f, w_gla_gate_b, b_gla_gate_b, g_gla_out, g_q_a, w_q_up, g_kv_a, w_kv_up, w_br_fourier, w_br_gla, w_br_mla, w_out, w_router, b_router, w_exp_gate, w_exp_up, w_exp_down, w_sh_gate, w_sh_up, w_sh_down, g_final):
    xp = x_prompt
    xs = x_sample
    new_f, new_b, new_ckv, new_kr = [], [], [], []
    for l in range(DEPTH):
        lp = dict(g_norm1=g_norm1[l], g_norm2=g_norm2[l], w_in=w_in[l], w_gla_gate_f=w_gla_gate_f[l], b_gla_gate_f=b_gla_gate_f[l], w_gla_gate_b=w_gla_gate_b[l], b_gla_gate_b=b_gla_gate_b[l], g_gla_out=g_gla_out[l], g_q_a=g_q_a[l], w_q_up=w_q_up[l], g_kv_a=g_kv_a[l], w_kv_up=w_kv_up[l], w_br_fourier=w_br_fourier[l], w_br_gla=w_br_gla[l], w_br_mla=w_br_mla[l], w_out=w_out[l], w_router=w_router[l], b_router=b_router[l], w_exp_gate=w_exp_gate[l], w_exp_up=w_exp_up[l], w_exp_down=w_exp_down[l], w_sh_gate=w_sh_gate[l], w_sh_up=w_sh_up[l], w_sh_down=w_sh_down[l])
        mod_ctx = modulation(c_ctx[None, :], w_mod[l], b_mod[l])
        mod_lat = modulation(c, w_mod[l], b_mod[l])
        xp, (s_f, s_b, ckv, kr) = trunk_layer(xp, mod_ctx, lp, None)
        new_f.append(s_f)
        new_b.append(s_b)
        new_ckv.append(ckv)
        new_kr.append(kr)
        ctx = (state_gla_fwd[:, l], state_gla_bwd[:, l], cache_mla_ckv[:, l], cache_mla_krope[:, l])
        xs, _ = trunk_layer(xs, mod_lat, lp, ctx)
    y_prompt = rmsnorm(xp, g_final)
    y_sample = rmsnorm(xs, g_final)
    return (y_prompt, y_sample, jnp.stack(new_f, axis=1), jnp.stack(new_b, axis=1), jnp.stack(new_ckv, axis=1), jnp.stack(new_kr, axis=1))
```

```python
import functools

import numpy as np
import jax
import jax.numpy as jnp
from jax import lax
from jax.experimental import pallas as pl
from jax.experimental.pallas import tpu as pltpu

F32 = jnp.float32
BF = jnp.bfloat16

D = 1024
GRID_W = 64
FN_G, FN_GW = 4, 96
FN_W = FN_G * FN_GW
GH, GDK, GDV = 4, 64, 128
G_RANK = 16
G_TAU = 16.0
G_CHUNK = 64
MH, M_NOPE, M_ROPE, M_V = 8, 64, 32, 64
M_QL, M_KVL = 384, 256
ROPE_BASE = 10000.0
N_EXP, TOP_K, N_GRP, TOPK_GRP = 64, 8, 8, 4
E_DIM = 256
ROUTED_SCALE = 2.5
EPS = 1e-6

C_F = (0, 384)
C_QK = (384, 896)
C_VR = (896, 1920)
C_MLA = (1920, 2560)
C_SM = (2560, 2688)
C_G = (2688, 5760)
IN_COLS_R = 5760

VMEM_LIMIT_V7X = 56 * 1024 * 1024
TOKEN_BLOCK = 512
MOE_BLOCK = 1024
Q_BLOCK = 256


def _params(*sem):
    return pltpu.CompilerParams(dimension_semantics=sem, vmem_limit_bytes=VMEM_LIMIT_V7X)


def _dot(a, b):
    return jnp.dot(a, b, preferred_element_type=F32)


def _dot_hi(a, b):
    return jnp.dot(a, b, precision=lax.Precision.HIGHEST, preferred_element_type=F32)


def _dot_nt(a, b, precision=None):
    return lax.dot_general(a, b, (((1,), (1,)), ((), ())), precision=precision, preferred_element_type=F32)


def _dot_tn(a, b):
    return lax.dot_general(a, b, (((0,), (0,)), ((), ())), preferred_element_type=F32)


def _sigmoid(x):
    return 1.0 / (1.0 + jnp.exp(-x))


def _rms(x, g):
    return x * lax.rsqrt(jnp.mean(x * x, axis=-1, keepdims=True) + EPS) * g


def _iota(shape, dim):
    return lax.broadcasted_iota(jnp.int32, shape, dim)


def _mod_row(n_ctx_blocks, blocks_per_lat, i):
    return jnp.where(i < n_ctx_blocks, 0, 1 + (i - n_ctx_blocks) // blocks_per_lat)


def _mod_kernel(c_ref, w_ref, b_ref, o_ref):
    c = c_ref[...]
    o_ref[0] = _dot_hi(c * _sigmoid(c), w_ref[0]) + b_ref[0]


def _modulation(cond, w_mod, b_mod):
    L = w_mod.shape[0]
    rows = cond.shape[0]
    tn = 1536
    return pl.pallas_call(
        _mod_kernel,
        grid=(L, 6 * D // tn),
        in_specs=[pl.BlockSpec((rows, D), lambda l, j: (0, 0)),
                  pl.BlockSpec((1, D, tn), lambda l, j: (l, 0, j)),
                  pl.BlockSpec((1, 1, tn), lambda l, j: (l, 0, j))],
        out_specs=pl.BlockSpec((1, rows, tn), lambda l, j: (l, 0, j)),
        out_shape=jax.ShapeDtypeStruct((L, rows, 6 * D), F32),
        compiler_params=_params("parallel", "parallel"),
        name="modulation",
    )(cond, w_mod, b_mod.reshape(L, 1, 6 * D))


def _inproj_kernel(x_ref, m_ref, g_ref, w_ref, of_ref, oqk_ref, ovr_ref, omla_ref, osm_ref, og_ref):
    y = _rms(x_ref[...], g_ref[...])
    h = (y * (1.0 + m_ref[0, :, D:2 * D]) + m_ref[0, :, 0:D]).astype(BF)
    of_ref[...] = _dot(h, w_ref[:, C_F[0]:C_F[1]]).astype(BF)
    oqk_ref[...] = _dot(h, w_ref[:, C_QK[0]:C_QK[1]])
    ovr_ref[...] = _dot(h, w_ref[:, C_VR[0]:C_VR[1]]).astype(BF)
    omla_ref[...] = _dot(h, w_ref[:, C_MLA[0]:C_MLA[1]])
    osm_ref[...] = _dot(h, w_ref[:, C_SM[0]:C_SM[1]])
    og_ref[...] = _dot(h, w_ref[:, C_G[0]:C_G[1]]).astype(BF)


def _in_projection(x, mods, g1, w_in_r, n_ctx_blocks, blocks_per_lat):
    T = x.shape[0]
    tb = TOKEN_BLOCK
    row = functools.partial(_mod_row, n_ctx_blocks, blocks_per_lat)
    widths = [(C_F, BF), (C_QK, F32), (C_VR, BF), (C_MLA, F32), (C_SM, F32), (C_G, BF)]
    return pl.pallas_call(
        _inproj_kernel,
        grid=(T // tb,),
        in_specs=[pl.BlockSpec((tb, D), lambda i: (i, 0)),
                  pl.BlockSpec((1, 1, 2 * D), lambda i: (row(i), 0, 0)),
                  pl.BlockSpec((1, D), lambda i: (0, 0)),
                  pl.BlockSpec((D, IN_COLS_R), lambda i: (0, 0))],
        out_specs=[pl.BlockSpec((tb, c[1] - c[0]), lambda i: (i, 0)) for c, _ in widths],
        out_shape=[jax.ShapeDtypeStruct((T, c[1] - c[0]), dt) for c, dt in widths],
        compiler_params=_params("parallel"),
        name="in_projection",
    )(x, mods, g1, w_in_r)


def _seq_call(kernel, name, n, nseq, blk_off, seq_ins, const_ins, out_widths, extra_outs=(), scratch=()):
    in_specs = [pl.BlockSpec((n, a.shape[1]), lambda i: (i + blk_off, 0)) for a in seq_ins]
    in_specs += [pl.BlockSpec(bs, im) for _, bs, im in const_ins]
    args = list(seq_ins) + [a for a, _, _ in const_ins]
    out_specs = [pl.BlockSpec((n, w), lambda i: (i, 0)) for w, _ in out_widths]
    out_shape = [jax.ShapeDtypeStruct((nseq * n, w), dt) for w, dt in out_widths]
    out_specs += [pl.BlockSpec(bs, im) for _, _, bs, im in extra_outs]
    out_shape += [jax.ShapeDtypeStruct(s, dt) for s, dt, _, _ in extra_outs]
    return pl.pallas_call(
        kernel, grid=(nseq,), in_specs=in_specs, out_specs=out_specs, out_shape=out_shape,
        scratch_shapes=list(scratch), compiler_params=_params("parallel"), name=name,
    )(*args)


def _fourier_kernel(u_ref, r_ref, lc_ref, ls_ref, o_ref):
    y = _dot(u_ref[...], r_ref[...])
    o_ref[...] = (_dot(lc_ref[...], y[:, :FN_W].astype(BF)) + _dot(ls_ref[...], y[:, FN_W:].astype(BF))).astype(BF)


def _dft_tables(n):
    k = np.arange(FN_GW)
    ang = 2.0 * np.pi * ((k[:, None] * k[None, :]) % FN_GW) / FN_GW
    eye = np.eye(FN_G)
    right = np.concatenate([np.kron(eye, np.cos(ang)), np.kron(eye, np.sin(ang))], axis=1)
    p = np.arange(n)
    angn = 2.0 * np.pi * ((p[:, None] * p[None, :]) % n) / n
    scale = 1.0 / np.sqrt(float(n * FN_GW))
    return (jnp.asarray(right, F32).astype(BF), jnp.asarray(np.cos(angn) * scale, F32).astype(BF),
            jnp.asarray(-np.sin(angn) * scale, F32).astype(BF))


def _fourier(zf, n, nseq, blk_off):
    right, lc, ls = _dft_tables(n)
    consts = [(right, (FN_W, 2 * FN_W), lambda i: (0, 0)), (lc, (n, n), lambda i: (0, 0)), (ls, (n, n), lambda i: (0, 0))]
    return _seq_call(_fourier_kernel, "fourier_mix", n, nseq, blk_off, [zf], consts, [(FN_W, BF)])


def _log_gate(z, w_ref, b_ref):
    pre = _dot_hi(z, w_ref[...]) + b_ref[...]
    return (jnp.minimum(pre, 0.0) - jnp.log1p(jnp.exp(-jnp.abs(pre)))) * (1.0 / G_TAU)


def _gla_kernel(has_state, n, *refs):
    if has_state:
        (zqk, zvr, zsm, wgf, bgf, wgb, bgb, gout, s0f, s0b, o_ref, sf_ref, sb_ref, oacc, lg, st) = refs
    else:
        (zqk, zvr, zsm, wgf, bgf, wgb, bgb, gout, o_ref, sf_ref, sb_ref, oacc, lg, st) = refs
        s0f = s0b = None
    C = G_CHUNK
    nc = n // C
    ri, ci = _iota((C, C), 0), _iota((C, C), 1)
    t_idx, lane = _iota((C, 128), 0), _iota((C, 128), 1)
    s_idx = lane & (C - 1)
    left = lane < GDK
    vleft = _iota((C, 2 * GDV), 1) < GDV
    blockdiag = (_iota((2 * GDV, 2 * GDK), 0) >> 7) == (_iota((2 * GDV, 2 * GDK), 1) >> 6)

    def load_state(s0_ref):
        for p in range(2):
            if s0_ref is None:
                st[p] = jnp.zeros((2 * GDV, 2 * GDK), F32)
            else:
                z = jnp.zeros((GDK, GDV), F32)
                blk = jnp.concatenate([jnp.concatenate([s0_ref[0, 2 * p], z], axis=1),
                                       jnp.concatenate([z, s0_ref[0, 2 * p + 1]], axis=1)], axis=0)
                st[p] = blk.T

    def store_state(out_ref):
        for p in range(2):
            blk = st[p].T
            out_ref[0, 2 * p] = blk[0:GDK, 0:GDV]
            out_ref[0, 2 * p + 1] = blk[GDK:2 * GDK, GDV:2 * GDV]

    def run(reverse):
        tri = (ci >= ri).astype(F32) if reverse else (ci <= ri).astype(F32)
        mask = (s_idx >= t_idx) if reverse else (s_idx <= t_idx)

        def chunk(step, carry):
            c = (nc - 1 - step) if reverse else step
            rows = pl.ds(pl.multiple_of(c * C, C), C)
            cum = _dot_hi(tri, lg[rows, :])
            tot = cum[0:1] if reverse else cum[C - 1:C]
            q = zqk[rows, 0:GH * GDK] * (GDK ** -0.5)
            k = zqk[rows, GH * GDK:2 * GH * GDK]
            qh = (q * jnp.exp(cum)).astype(BF)
            kh = (k * jnp.exp(-cum)).astype(BF)
            kb = (k * jnp.exp(tot - cum)).astype(BF)
            dec = jnp.exp(tot)
            for p in range(2):
                ls = slice(128 * p, 128 * p + 128)
                vs = slice(256 * p, 256 * p + 256)
                qp, kp = qh[:, ls], kh[:, ls]
                zk = jnp.zeros_like(kp)
                kblk = jnp.concatenate([jnp.where(left, kp, zk), jnp.where(left, zk, kp)], axis=0)
                sc = jnp.where(mask, _dot_nt(qp, kblk), 0.0).astype(BF)
                vp = zvr[rows, vs]
                zv = jnp.zeros_like(vp)
                vblk = jnp.concatenate([jnp.where(vleft, vp, zv), jnp.where(vleft, zv, vp)], axis=0)
                stp = st[p]
                o_p = _dot(sc, vblk) + _dot_nt(qp, stp.astype(BF))
                st[p] = dec[:, ls] * stp + jnp.where(blockdiag, _dot_tn(vp, kb[:, ls]), 0.0)
                if not reverse:
                    oacc[rows, vs] = o_p
                else:
                    o_t = oacc[rows, vs] + o_p
                    for hh in range(2):
                        hs = slice(256 * p + 128 * hh, 256 * p + 128 * hh + 128)
                        oh = o_t[:, 128 * hh:128 * hh + 128]
                        oh = oh * lax.rsqrt(jnp.mean(oh * oh, axis=-1, keepdims=True) + EPS) * gout[:, hs]
                        r = zvr[rows, GH * GDV + hs.start:GH * GDV + hs.stop].astype(F32)
                        o_ref[rows, hs] = (oh * (r * _sigmoid(r))).astype(BF)
            return carry

        lax.fori_loop(0, nc, chunk, 0)

    lg[...] = _log_gate(zsm[:, 0:G_RANK], wgf, bgf)
    load_state(s0f)
    run(False)
    store_state(sf_ref)
    lg[...] = _log_gate(zsm[:, G_RANK:2 * G_RANK], wgb, bgb)
    load_state(s0b)
    run(True)
    store_state(sb_ref)


def _gla(zqk, zvr, zsm, gate_w, n, nseq, blk_off, states):
    wgf, bgf, wgb, bgb, gout = gate_w
    c2 = lambda i: (0, 0)
    consts = [(wgf, wgf.shape, c2), (bgf, bgf.shape, c2), (wgb, wgb.shape, c2), (bgb, bgb.shape, c2), (gout, gout.shape, c2)]
    st_blk = (1, GH, GDK, GDV)
    st_map = lambda i: (i, 0, 0, 0)
    if states is not None:
        consts += [(s, st_blk, st_map) for s in states]
    extra = [((nseq, GH, GDK, GDV), F32, st_blk, st_map)] * 2
    scratch = [pltpu.VMEM((n, GH * GDV), F32), pltpu.VMEM((n, GH * GDK), F32), pltpu.VMEM((2, 2 * GDV, 2 * GDK), F32)]
    return _seq_call(functools.partial(_gla_kernel, states is not None, n), "gla_mixer", n, nseq, blk_off,
                     [zqk, zvr, zsm], consts, [(GH * GDV, BF)], extra_outs=extra, scratch=scratch)


def _mla_kernel(latent, n, past, *refs):
    if latent:
        (zmla, zsm, gq, wq, gkv, wkv, cckv, ckr, cosq, sinq, cosk, sink, o_ref, qs, kns, vs, krs) = refs
    else:
        (zmla, zsm, gq, wq, gkv, wkv, o_ref, ckv_ref, qs, kns, vs, krs) = refs
    sk = past + n
    scale = (M_NOPE + M_ROPE) ** -0.5
    nw, rw = MH * M_NOPE, MH * M_ROPE
    qa = _dot(_rms(zmla[:, 0:M_QL], gq[...]).astype(BF), wq[...])
    qr = qa[:, nw:nw + rw]
    if latent:
        qr = qr * cosq[...] + qa[:, nw + rw:nw + 2 * rw] * sinq[...]
    qs[:, 0:nw] = qa[:, 0:nw] * scale
    qs[:, nw:nw + rw] = qr * scale
    ckv = _rms(zmla[:, M_QL:M_QL + M_KVL], gkv[...])
    kv = _dot(ckv.astype(BF), wkv[...])
    kr = zsm[:, 32:64]
    if latent:
        kr = kr * cosk[...] + zsm[:, 64:96] * sink[...]
        kvc = _dot(cckv[0].astype(BF), wkv[...])
        kns[0:past, :] = kvc[:, 0:nw].astype(BF)
        vs[0:past, :] = kvc[:, nw:].astype(BF)
        krs[0:past, :] = jnp.concatenate([ckr[0]] * 4, axis=1).astype(BF)
    else:
        ckv_ref[...] = ckv
    kns[past:sk, :] = kv[:, 0:nw].astype(BF)
    vs[past:sk, :] = kv[:, nw:].astype(BF)
    krs[past:sk, :] = jnp.concatenate([kr] * 4, axis=1).astype(BF)

    qb = min(Q_BLOCK, n)
    lane = _iota((qb, 128), 1)

    def block(step, carry):
        rows = pl.ds(pl.multiple_of(step * qb, qb), qb)
        for p in range(MH // 2):
            ls = slice(128 * p, 128 * p + 128)
            qn = qs[rows, ls]
            quad = (2 * p) // 4
            qrp = qs[rows, nw + 128 * quad:nw + 128 * quad + 128]
            rhs = jnp.concatenate([kns[:, ls], krs[...]], axis=1)
            vp = vs[:, ls]
            o_pair = None
            for hh in range(2):
                j = (2 * p + hh) % 4
                qn_m = jnp.where((lane >> 6) == hh, qn, 0.0).astype(BF)
                qr_m = jnp.where((lane >> 5) == j, qrp, 0.0).astype(BF)
                s = _dot_nt(jnp.concatenate([qn_m, qr_m], axis=1), rhs)
                e = jnp.exp(s - jnp.max(s, axis=-1, keepdims=True))
                pv = _dot(e.astype(BF), vp) / jnp.sum(e, axis=-1, keepdims=True)
                o_pair = pv if hh == 0 else jnp.where(lane < M_V, o_pair, pv)
            o_ref[rows, ls] = o_pair.astype(BF)
        return carry

    lax.fori_loop(0, n // qb, block, 0)


def _rope_tables(n):
    half = M_ROPE // 2
    pos = jnp.arange(n)
    row = (pos // GRID_W).astype(F32)
    col = (pos % GRID_W).astype(F32)
    inv = ROPE_BASE ** (-jnp.arange(0, half, 2, dtype=F32) / half)
    ang = jnp.concatenate([row[:, None] * inv, col[:, None] * inv], axis=-1)
    cos = jnp.repeat(jnp.cos(ang), 2, axis=-1)
    sin = jnp.repeat(jnp.sin(ang), 2, axis=-1) * jnp.tile(jnp.asarray([-1.0, 1.0], F32), half)
    return jnp.tile(cos, (1, MH)), jnp.tile(sin, (1, MH)), cos, sin


def _mla(zmla, zsm, w, n, nseq, blk_off, cache):
    gq, wq, gkv, wkv = w
    c2 = lambda i: (0, 0)
    consts = [(gq, gq.shape, c2), (wq, wq.shape, c2), (gkv, gkv.shape, c2), (wkv, wkv.shape, c2)]
    past = 0
    extra = []
    if cache is not None:
        cckv, ckr = cache
        past = cckv.shape[1]
        c3 = lambda i: (i, 0, 0)
        consts += [(cckv, (1, past, M_KVL), c3), (ckr, (1, past, M_ROPE), c3)]
        consts += [(t, t.shape, c2) for t in _rope_tables(n)]
    else:
        extra = [((nseq * n, M_KVL), F32, (n, M_KVL), lambda i: (i, 0))]
    sk = past + n
    scratch = [pltpu.VMEM((n, MH * (M_NOPE + M_ROPE)), F32), pltpu.VMEM((sk, MH * M_NOPE), BF),
               pltpu.VMEM((sk, MH * M_V), BF), pltpu.VMEM((sk, 128), BF)]
    return _seq_call(functools.partial(_mla_kernel, cache is not None, n, past), "mla_mixer", n, nseq, blk_off,
                     [zmla, zsm], consts, [(MH * M_V, BF)], extra_outs=extra, scratch=scratch)


def _route(logits_t, bias):
    nt = logits_t.shape[1]
    gsz = N_EXP // N_GRP
    scores = _sigmoid(logits_t)
    sel = scores + bias
    neg = -jnp.inf
    sub = _iota((gsz, nt), 0)
    tops = []
    for g in range(N_GRP):
        blk = sel[gsz * g:gsz * g + gsz]
        m1 = jnp.max(blk, axis=0, keepdims=True)
        first = jnp.min(jnp.where(blk == m1, sub, gsz), axis=0, keepdims=True)
        m2 = jnp.max(jnp.where(sub == first, neg, blk), axis=0, keepdims=True)
        tops.append(m1 + m2)
    gs = jnp.concatenate(tops, axis=0)
    gidx = _iota((N_GRP, nt), 0)
    grank = jnp.zeros((N_GRP, nt), jnp.int32)
    for j in range(N_GRP):
        rj = gs[j:j + 1]
        grank += ((rj > gs) | ((rj == gs) & (gidx > j))).astype(jnp.int32)
    keep = grank < TOPK_GRP
    masked = jnp.concatenate(
        [jnp.where(jnp.broadcast_to(keep[g:g + 1], (gsz, nt)), sel[gsz * g:gsz * g + gsz], neg) for g in range(N_GRP)], axis=0)
    eidx = _iota((N_EXP, nt), 0)
    rank = jnp.zeros((N_EXP, nt), jnp.int32)
    for j in range(N_EXP):
        rj = masked[j:j + 1]
        rank += ((rj > masked) | ((rj == masked) & (eidx > j))).astype(jnp.int32)
    w = jnp.where(rank < TOP_K, scores, 0.0)
    return w / jnp.sum(w, axis=0, keepdims=True) * ROUTED_SCALE


def _merge_kernel(n_ctx_blocks, x_ref, fc_ref, fl_ref, ogc_ref, ogl_ref, omc_ref, oml_ref, zg_ref, m_ref,
                  wbf, wbg, wbm, wout, gn2, wrt, brt, xm_ref, h2_ref, gt_ref):
    is_ctx = pl.program_id(0) < n_ctx_blocks
    ya = _dot(jnp.where(is_ctx, fc_ref[...], fl_ref[...]), wbf[...])
    yb = _dot(jnp.where(is_ctx, ogc_ref[...], ogl_ref[...]), wbg[...])
    yc = _dot(jnp.where(is_ctx, omc_ref[...], oml_ref[...]), wbm[...])
    merged = (_sigmoid(zg_ref[:, 0:D].astype(F32)) * ya + _sigmoid(zg_ref[:, D:2 * D].astype(F32)) * yb
              + _sigmoid(zg_ref[:, 2 * D:3 * D].astype(F32)) * yc)
    xm = x_ref[...] + m_ref[0, :, 2 * D:3 * D] * _dot(merged.astype(BF), wout[...])
    xm_ref[...] = xm
    h2 = _rms(xm, gn2[...]) * (1.0 + m_ref[0, :, 4 * D:5 * D]) + m_ref[0, :, 3 * D:4 * D]
    h2_ref[...] = h2.astype(BF)
    gates_t = _route(_dot_nt(wrt[...], h2, precision=lax.Precision.HIGHEST), brt[...])
    tb = gates_t.shape[1]
    gt_ref[...] = jnp.concatenate([gates_t, jnp.zeros((128 - N_EXP, tb), F32)], axis=0).T


def _merge(x, mix_ctx, mix_lat, zg, mods, w, n_ctx_blocks, blocks_per_lat):
    T = x.shape[0]
    tb = TOKEN_BLOCK
    row = functools.partial(_mod_row, n_ctx_blocks, blocks_per_lat)
    rb = lambda wd: pl.BlockSpec((tb, wd), lambda i: (i, 0))
    cb = lambda a: pl.BlockSpec(a.shape, lambda i: (0, 0))
    ctx_b = lambda wd: pl.BlockSpec((tb, wd), lambda i: (jnp.minimum(i, n_ctx_blocks - 1), 0))
    lat_b = lambda wd: pl.BlockSpec((tb, wd), lambda i: (jnp.maximum(i - n_ctx_blocks, 0), 0))
    mix_specs, mix_args = [], []
    for a_c, a_l in zip(mix_ctx, mix_lat):
        mix_specs += [ctx_b(a_c.shape[1]), lat_b(a_l.shape[1])]
        mix_args += [a_c, a_l]
    return pl.pallas_call(
        functools.partial(_merge_kernel, n_ctx_blocks),
        grid=(T // tb,),
        in_specs=[rb(D)] + mix_specs + [rb(3 * D),
                  pl.BlockSpec((1, 1, 6 * D), lambda i: (row(i), 0, 0))] + [cb(a) for a in w],
        out_specs=[rb(D), rb(D), rb(128)],
        out_shape=[jax.ShapeDtypeStruct((T, D), F32), jax.ShapeDtypeStruct((T, D), BF), jax.ShapeDtypeStruct((T, 128), F32)],
        compiler_params=_params("parallel"),
        name="merge_route",
    )(x, *mix_args, zg, mods, *w)


def _silu_mul(a, b):
    return a * _sigmoid(a) * b


def _moe_kernel(final, h_ref, gt_ref, wg_ref, wu_ref, wd_ref, sg_ref, su_ref, sd_ref, x_ref, m_ref, gf_ref, o_ref, acc_ref):
    e = pl.program_id(1)
    h = h_ref[...]

    @pl.when(e == 0)
    def _():
        acc_ref[...] = jnp.zeros_like(acc_ref)

    hid = _silu_mul(_dot(h, wg_ref[0, 0].astype(BF)), _dot(h, wu_ref[0, 0].astype(BF)))
    onehot = (_iota((128, 128), 0) == e).astype(BF)
    gcol = _dot(gt_ref[...].astype(BF), onehot)
    hs = (hid * jnp.concatenate([gcol, gcol], axis=1)).astype(BF)
    acc_ref[...] += _dot(hs, wd_ref[0, 0].astype(BF))

    @pl.when(e == N_EXP - 1)
    def _():
        sh = _silu_mul(_dot(h, sg_ref[0].astype(BF)), _dot(h, su_ref[0].astype(BF)))
        out = x_ref[...] + m_ref[0] * (acc_ref[...] + _dot(sh.astype(BF), sd_ref[0].astype(BF)))
        if final:
            out = _rms(out, gf_ref[...])
        o_ref[...] = out


def _moe(layer, final, h2, gates, w_eg, w_eu, w_ed, w_sg, w_su, w_sd, xm, mods, g_final, n_ctx_blocks, blocks_per_lat):
    T = h2.shape[0]
    tt = MOE_BLOCK
    row = functools.partial(_mod_row, n_ctx_blocks, blocks_per_lat)
    return pl.pallas_call(
        functools.partial(_moe_kernel, final),
        grid=(T // tt, N_EXP),
        in_specs=[pl.BlockSpec((tt, D), lambda i, e: (i, 0)),
                  pl.BlockSpec((tt, 128), lambda i, e: (i, 0)),
                  pl.BlockSpec((1, 1, D, E_DIM), lambda i, e: (layer, e, 0, 0)),
                  pl.BlockSpec((1, 1, D, E_DIM), lambda i, e: (layer, e, 0, 0)),
                  pl.BlockSpec((1, 1, E_DIM, D), lambda i, e: (layer, e, 0, 0)),
                  pl.BlockSpec((1, D, E_DIM), lambda i, e: (layer, 0, 0)),
                  pl.BlockSpec((1, D, E_DIM), lambda i, e: (layer, 0, 0)),
                  pl.BlockSpec((1, E_DIM, D), lambda i, e: (layer, 0, 0)),
                  pl.BlockSpec((tt, D), lambda i, e: (i, 0)),
                  pl.BlockSpec((1, 1, D), lambda i, e: (row(i), 0, 5)),
                  pl.BlockSpec((1, D), lambda i, e: (0, 0))],
        out_specs=pl.BlockSpec((tt, D), lambda i, e: (i, 0)),
        out_shape=jax.ShapeDtypeStruct((T, D), F32),
        scratch_shapes=[pltpu.VMEM((tt, D), F32)],
        compiler_params=_params("parallel", "arbitrary"),
        name="moe_experts",
    )(h2, gates, w_eg, w_eu, w_ed, w_sg, w_su, w_sd, xm, mods, g_final)


def _reorder_w_in(w_in):
    kr = w_in[:, :, 2592:2624]
    kr_sw = kr.reshape(kr.shape[0], D, M_ROPE // 2, 2)[..., ::-1].reshape(kr.shape)
    pad = jnp.zeros(kr.shape, w_in.dtype)
    out = jnp.concatenate([w_in[:, :, :1920], w_in[:, :, 1952:2592], w_in[:, :, 1920:1952], kr, kr_sw, pad,
                           w_in[:, :, 2624:]], axis=2)
    return out.astype(BF)


def _reorder_w_q(w_q_up, with_swap):
    L = w_q_up.shape[0]
    w = w_q_up.reshape(L, M_QL, MH, M_NOPE + M_ROPE)
    nope = w[..., :M_NOPE].reshape(L, M_QL, MH * M_NOPE)
    rope = w[..., M_NOPE:]
    parts = [nope, rope.reshape(L, M_QL, MH * M_ROPE)]
    if with_swap:
        parts.append(rope.reshape(L, M_QL, MH, M_ROPE // 2, 2)[..., ::-1].reshape(L, M_QL, MH * M_ROPE))
    return jnp.concatenate(parts, axis=2).astype(BF)


def _reorder_w_kv(w_kv_up):
    L = w_kv_up.shape[0]
    w = w_kv_up.reshape(L, M_KVL, MH, M_NOPE + M_V)
    return jnp.concatenate([w[..., :M_NOPE].reshape(L, M_KVL, MH * M_NOPE),
                            w[..., M_NOPE:].reshape(L, M_KVL, MH * M_V)], axis=2).astype(BF)


def kernel(x_prompt, x_sample, state_gla_fwd, state_gla_bwd, cache_mla_ckv, cache_mla_krope, c, c_ctx, w_mod, b_mod, g_norm1, g_norm2, w_in, w_gla_gate_f, b_gla_gate_f, w_gla_gate_b, b_gla_gate_b, g_gla_out, g_q_a, w_q_up, g_kv_a, w_kv_up, w_br_fourier, w_br_gla, w_br_mla, w_out, w_router, b_router, w_exp_gate, w_exp_up, w_exp_down, w_sh_gate, w_sh_up, w_sh_down, g_final):
    nb, sl, _ = x_prompt.shape
    db, dl, _ = x_sample.shape
    L = w_mod.shape[0]
    t_ctx, t_lat = nb * sl, db * dl
    T = t_ctx + t_lat
    assert sl % G_CHUNK == 0 and dl % G_CHUNK == 0 and dl % GRID_W == 0
    assert t_ctx % dl == 0 and t_ctx % MOE_BLOCK == 0 and dl % MOE_BLOCK == 0 and dl % TOKEN_BLOCK == 0
    assert t_ctx % TOKEN_BLOCK == 0 and dl % Q_BLOCK == 0 and 1 + db <= 8

    x = jnp.concatenate([x_prompt.reshape(t_ctx, D), x_sample.reshape(t_lat, D)], axis=0)
    cond = jnp.concatenate([c_ctx[None, :], c, jnp.zeros((7 - db, D), F32)], axis=0)
    mods_all = _modulation(cond, w_mod, b_mod)

    w_in_r = _reorder_w_in(w_in)
    wq_ctx = _reorder_w_q(w_q_up, False)
    wq_lat = _reorder_w_q(w_q_up, True)
    wkv_r = _reorder_w_kv(w_kv_up)
    lat_off = t_ctx // dl

    new_f, new_b, new_ckv, new_kr = [], [], [], []
    for l in range(L):
        mods = mods_all[l].reshape(8, 1, 6 * D)
        tok = (t_ctx // TOKEN_BLOCK, dl // TOKEN_BLOCK)
        zf, zqk, zvr, zmla, zsm, zg = _in_projection(x, mods, g_norm1[l][None, :], w_in_r[l], *tok)

        (f_c,) = _fourier(zf, sl, nb, 0)
        (f_l,) = _fourier(zf, dl, db, lat_off)

        gate_w = (w_gla_gate_f[l], b_gla_gate_f[l][None, :], w_gla_gate_b[l], b_gla_gate_b[l][None, :],
                  g_gla_out[l].reshape(1, GH * GDV))
        og_c, s_f, s_b = _gla(zqk, zvr, zsm, gate_w, sl, nb, 0, None)
        og_l, _, _ = _gla(zqk, zvr, zsm, gate_w, dl, db, lat_off, (state_gla_fwd[:, l], state_gla_bwd[:, l]))

        gq, gkv = g_q_a[l][None, :], g_kv_a[l][None, :]
        om_c, ckv = _mla(zmla, zsm, (gq, wq_ctx[l], gkv, wkv_r[l]), sl, nb, 0, None)
        (om_l,) = _mla(zmla, zsm, (gq, wq_lat[l], gkv, wkv_r[l]), dl, db, lat_off,
                       (cache_mla_ckv[:, l], cache_mla_krope[:, l]))

        mw = (w_br_fourier[l].astype(BF), w_br_gla[l].astype(BF), w_br_mla[l].astype(BF), w_out[l].astype(BF),
              g_norm2[l][None, :], w_router[l].T, b_router[l][:, None])
        xm, h2, gates = _merge(x, (f_c, og_c, om_c), (f_l, og_l, om_l), zg, mods, mw, *tok)

        moe_blocks = (t_ctx // MOE_BLOCK, dl // MOE_BLOCK)
        x = _moe(l, l == L - 1, h2, gates, w_exp_gate, w_exp_up, w_exp_down, w_sh_gate, w_sh_up, w_sh_down,
                 xm, mods, g_final[None, :], *moe_blocks)

        new_f.append(s_f)
        new_b.append(s_b)
        new_ckv.append(ckv.reshape(nb, sl, M_KVL))
        new_kr.append(zsm[:t_ctx, 32:64].reshape(nb, sl, M_ROPE))

    y_prompt = x[:t_ctx].reshape(nb, sl, D)
    y_sample = x[t_ctx:].reshape(db, dl, D)
    return (y_prompt, y_sample, jnp.stack(new_f, axis=1), jnp.stack(new_b, axis=1),
            jnp.stack(new_ckv, axis=1), jnp.stack(new_kr, axis=1))
```

```python
import functools

import numpy as np
import jax
import jax.numpy as jnp
from jax import lax
from jax.experimental import pallas as pl
from jax.experimental.pallas import tpu as pltpu

F32 = jnp.float32
BF = jnp.bfloat16

D = 1024
GRID_W = 64
FN_G, FN_GW = 4, 96
FN_W = FN_G * FN_GW
GH, GDK, GDV = 4, 64, 128
G_RANK = 16
G_TAU = 16.0
G_CHUNK = 64
MH, M_NOPE, M_ROPE, M_V = 8, 64, 32, 64
M_QL, M_KVL = 384, 256
ROPE_BASE = 10000.0
N_EXP, TOP_K, N_GRP, TOPK_GRP = 64, 8, 8, 4
E_DIM = 256
ROUTED_SCALE = 2.5
EPS = 1e-6

C_F = (0, 384)
C_QK = (384, 896)
C_VR = (896, 1920)
C_MLA = (1920, 2560)
C_SM = (2560, 2688)
C_G = (2688, 5760)
IN_COLS_R = 5760

VMEM_LIMIT_V7X = 56 * 1024 * 1024
TOKEN_BLOCK = 512
Q_BLOCK = 256
DISP_BLOCK = 256
ROW_GRANULE = 16
DISP_ROWS = -(-(TOP_K * DISP_BLOCK + N_EXP * (ROW_GRANULE - 1)) // 256) * 256
EXP_TILE = 256
CHUNKS_PER_TILE = EXP_TILE // ROW_GRANULE
BLOCK_CHUNKS = DISP_ROWS // ROW_GRANULE


def _params(*sem):
    return pltpu.CompilerParams(dimension_semantics=sem, vmem_limit_bytes=VMEM_LIMIT_V7X)


def _dot(a, b):
    return jnp.dot(a, b, preferred_element_type=F32)


def _dot_hi(a, b):
    return jnp.dot(a, b, precision=lax.Precision.HIGHEST, preferred_element_type=F32)


def _dot_nt(a, b, precision=None):
    return lax.dot_general(a, b, (((1,), (1,)), ((), ())), precision=precision, preferred_element_type=F32)


def _dot_tn(a, b):
    return lax.dot_general(a, b, (((0,), (0,)), ((), ())), preferred_element_type=F32)


def _sigmoid(x):
    return 1.0 / (1.0 + jnp.exp(-x))


def _rms(x, g):
    return x * lax.rsqrt(jnp.mean(x * x, axis=-1, keepdims=True) + EPS) * g


def _iota(shape, dim):
    return lax.broadcasted_iota(jnp.int32, shape, dim)


def _mod_row(n_ctx_blocks, blocks_per_lat, i):
    return jnp.where(i < n_ctx_blocks, 0, 1 + (i - n_ctx_blocks) // blocks_per_lat)


def _mod_kernel(c_ref, w_ref, b_ref, o_ref):
    c = c_ref[...]
    o_ref[0] = _dot_hi(c * _sigmoid(c), w_ref[0]) + b_ref[0]


def _modulation(cond, w_mod, b_mod):
    L = w_mod.shape[0]
    rows = cond.shape[0]
    tn = 1536
    return pl.pallas_call(
        _mod_kernel,
        grid=(L, 6 * D // tn),
        in_specs=[pl.BlockSpec((rows, D), lambda l, j: (0, 0)),
                  pl.BlockSpec((1, D, tn), lambda l, j: (l, 0, j)),
                  pl.BlockSpec((1, 1, tn), lambda l, j: (l, 0, j))],
        out_specs=pl.BlockSpec((1, rows, tn), lambda l, j: (l, 0, j)),
        out_shape=jax.ShapeDtypeStruct((L, rows, 6 * D), F32),
        compiler_params=_params("parallel", "parallel"),
        name="modulation",
    )(cond, w_mod, b_mod.reshape(L, 1, 6 * D))


def _inproj_kernel(x_ref, m_ref, g_ref, w_ref, of_ref, oqk_ref, ovr_ref, omla_ref, osm_ref, og_ref):
    y = _rms(x_ref[...], g_ref[...])
    h = (y * (1.0 + m_ref[0, :, D:2 * D]) + m_ref[0, :, 0:D]).astype(BF)
    of_ref[...] = _dot(h, w_ref[:, C_F[0]:C_F[1]]).astype(BF)
    oqk_ref[...] = _dot(h, w_ref[:, C_QK[0]:C_QK[1]])
    ovr_ref[...] = _dot(h, w_ref[:, C_VR[0]:C_VR[1]]).astype(BF)
    omla_ref[...] = _dot(h, w_ref[:, C_MLA[0]:C_MLA[1]])
    osm_ref[...] = _dot(h, w_ref[:, C_SM[0]:C_SM[1]])
    og_ref[...] = _dot(h, w_ref[:, C_G[0]:C_G[1]]).astype(BF)


def _in_projection(x, mods, g1, w_in_r, n_ctx_blocks, blocks_per_lat):
    T = x.shape[0]
    tb = TOKEN_BLOCK
    row = functools.partial(_mod_row, n_ctx_blocks, blocks_per_lat)
    widths = [(C_F, BF), (C_QK, F32), (C_VR, BF), (C_MLA, F32), (C_SM, F32), (C_G, BF)]
    return pl.pallas_call(
        _inproj_kernel,
        grid=(T // tb,),
        in_specs=[pl.BlockSpec((tb, D), lambda i: (i, 0)),
                  pl.BlockSpec((1, 1, 2 * D), lambda i: (row(i), 0, 0)),
                  pl.BlockSpec((1, D), lambda i: (0, 0)),
                  pl.BlockSpec((D, IN_COLS_R), lambda i: (0, 0))],
        out_specs=[pl.BlockSpec((tb, c[1] - c[0]), lambda i: (i, 0)) for c, _ in widths],
        out_shape=[jax.ShapeDtypeStruct((T, c[1] - c[0]), dt) for c, dt in widths],
        compiler_params=_params("parallel"),
        name="in_projection",
    )(x, mods, g1, w_in_r)


def _seq_call(kernel, name, n, nseq, blk_off, seq_ins, const_ins, out_widths, extra_outs=(), scratch=()):
    in_specs = [pl.BlockSpec((n, a.shape[1]), lambda i: (i + blk_off, 0)) for a in seq_ins]
    in_specs += [pl.BlockSpec(bs, im) for _, bs, im in const_ins]
    args = list(seq_ins) + [a for a, _, _ in const_ins]
    out_specs = [pl.BlockSpec((n, w), lambda i: (i, 0)) for w, _ in out_widths]
    out_shape = [jax.ShapeDtypeStruct((nseq * n, w), dt) for w, dt in out_widths]
    out_specs += [pl.BlockSpec(bs, im) for _, _, bs, im in extra_outs]
    out_shape += [jax.ShapeDtypeStruct(s, dt) for s, dt, _, _ in extra_outs]
    return pl.pallas_call(
        kernel, grid=(nseq,), in_specs=in_specs, out_specs=out_specs, out_shape=out_shape,
        scratch_shapes=list(scratch), compiler_params=_params("parallel"), name=name,
    )(*args)


def _fourier_kernel(u_ref, r_ref, lc_ref, ls_ref, o_ref):
    y = _dot(u_ref[...], r_ref[...])
    o_ref[...] = (_dot(lc_ref[...], y[:, :FN_W].astype(BF)) + _dot(ls_ref[...], y[:, FN_W:].astype(BF))).astype(BF)


def _dft_tables(n):
    k = np.arange(FN_GW)
    ang = 2.0 * np.pi * ((k[:, None] * k[None, :]) % FN_GW) / FN_GW
    eye = np.eye(FN_G)
    right = np.concatenate([np.kron(eye, np.cos(ang)), np.kron(eye, np.sin(ang))], axis=1)
    p = np.arange(n)
    angn = 2.0 * np.pi * ((p[:, None] * p[None, :]) % n) / n
    scale = 1.0 / np.sqrt(float(n * FN_GW))
    return (jnp.asarray(right, F32).astype(BF), jnp.asarray(np.cos(angn) * scale, F32).astype(BF),
            jnp.asarray(-np.sin(angn) * scale, F32).astype(BF))


def _fourier(zf, n, nseq, blk_off):
    right, lc, ls = _dft_tables(n)
    consts = [(right, (FN_W, 2 * FN_W), lambda i: (0, 0)), (lc, (n, n), lambda i: (0, 0)), (ls, (n, n), lambda i: (0, 0))]
    return _seq_call(_fourier_kernel, "fourier_mix", n, nseq, blk_off, [zf], consts, [(FN_W, BF)])


def _log_gate(z, w_ref, b_ref):
    pre = _dot_hi(z, w_ref[...]) + b_ref[...]
    return (jnp.minimum(pre, 0.0) - jnp.log1p(jnp.exp(-jnp.abs(pre)))) * (1.0 / G_TAU)


def _gla_kernel(has_state, n, *refs):
    if has_state:
        (zqk, zvr, zsm, wgf, bgf, wgb, bgb, gout, s0f, s0b, o_ref, sf_ref, sb_ref, oacc, lg, st) = refs
    else:
        (zqk, zvr, zsm, wgf, bgf, wgb, bgb, gout, o_ref, sf_ref, sb_ref, oacc, lg, st) = refs
        s0f = s0b = None
    C = G_CHUNK
    nc = n // C
    ri, ci = _iota((C, C), 0), _iota((C, C), 1)
    t_idx, lane = _iota((C, 128), 0), _iota((C, 128), 1)
    s_idx = lane & (C - 1)
    left = lane < GDK
    vleft = _iota((C, 2 * GDV), 1) < GDV
    blockdiag = (_iota((2 * GDV, 2 * GDK), 0) >> 7) == (_iota((2 * GDV, 2 * GDK), 1) >> 6)

    def load_state(s0_ref):
        for p in range(2):
            if s0_ref is None:
                st[p] = jnp.zeros((2 * GDV, 2 * GDK), F32)
            else:
                z = jnp.zeros((GDK, GDV), F32)
                blk = jnp.concatenate([jnp.concatenate([s0_ref[0, 2 * p], z], axis=1),
                                       jnp.concatenate([z, s0_ref[0, 2 * p + 1]], axis=1)], axis=0)
                st[p] = blk.T

    def store_state(out_ref):
        for p in range(2):
            blk = st[p].T
            out_ref[0, 2 * p] = blk[0:GDK, 0:GDV]
            out_ref[0, 2 * p + 1] = blk[GDK:2 * GDK, GDV:2 * GDV]

    def run(reverse):
        tri = (ci >= ri).astype(F32) if reverse else (ci <= ri).astype(F32)
        mask = (s_idx >= t_idx) if reverse else (s_idx <= t_idx)

        def chunk(step, carry):
            c = (nc - 1 - step) if reverse else step
            rows = pl.ds(pl.multiple_of(c * C, C), C)
            cum = _dot_hi(tri, lg[rows, :])
            tot = cum[0:1] if reverse else cum[C - 1:C]
            q = zqk[rows, 0:GH * GDK] * (GDK ** -0.5)
            k = zqk[rows, GH * GDK:2 * GH * GDK]
            qh = (q * jnp.exp(cum)).astype(BF)
            kh = (k * jnp.exp(-cum)).astype(BF)
            kb = (k * jnp.exp(tot - cum)).astype(BF)
            dec = jnp.exp(tot)
            for p in range(2):
                ls = slice(128 * p, 128 * p + 128)
                vs = slice(256 * p, 256 * p + 256)
                qp, kp = qh[:, ls], kh[:, ls]
                zk = jnp.zeros_like(kp)
                kblk = jnp.concatenate([jnp.where(left, kp, zk), jnp.where(left, zk, kp)], axis=0)
                sc = jnp.where(mask, _dot_nt(qp, kblk), 0.0).astype(BF)
                vp = zvr[rows, vs]
                zv = jnp.zeros_like(vp)
                vblk = jnp.concatenate([jnp.where(vleft, vp, zv), jnp.where(vleft, zv, vp)], axis=0)
                stp = st[p]
                o_p = _dot(sc, vblk) + _dot_nt(qp, stp.astype(BF))
                st[p] = dec[:, ls] * stp + jnp.where(blockdiag, _dot_tn(vp, kb[:, ls]), 0.0)
                if not reverse:
                    oacc[rows, vs] = o_p
                else:
                    o_t = oacc[rows, vs] + o_p
                    for hh in range(2):
                        hs = slice(256 * p + 128 * hh, 256 * p + 128 * hh + 128)
                        oh = o_t[:, 128 * hh:128 * hh + 128]
                        oh = oh * lax.rsqrt(jnp.mean(oh * oh, axis=-1, keepdims=True) + EPS) * gout[:, hs]
                        r = zvr[rows, GH * GDV + hs.start:GH * GDV + hs.stop].astype(F32)
                        o_ref[rows, hs] = (oh * (r * _sigmoid(r))).astype(BF)
            return carry

        lax.fori_loop(0, nc, chunk, 0)

    lg[...] = _log_gate(zsm[:, 0:G_RANK], wgf, bgf)
    load_state(s0f)
    run(False)
    store_state(sf_ref)
    lg[...] = _log_gate(zsm[:, G_RANK:2 * G_RANK], wgb, bgb)
    load_state(s0b)
    run(True)
    store_state(sb_ref)


def _gla(zqk, zvr, zsm, gate_w, n, nseq, blk_off, states):
    wgf, bgf, wgb, bgb, gout = gate_w
    c2 = lambda i: (0, 0)
    consts = [(wgf, wgf.shape, c2), (bgf, bgf.shape, c2), (wgb, wgb.shape, c2), (bgb, bgb.shape, c2), (gout, gout.shape, c2)]
    st_blk = (1, GH, GDK, GDV)
    st_map = lambda i: (i, 0, 0, 0)
    if states is not None:
        consts += [(s, st_blk, st_map) for s in states]
    extra = [((nseq, GH, GDK, GDV), F32, st_blk, st_map)] * 2
    scratch = [pltpu.VMEM((n, GH * GDV), F32), pltpu.VMEM((n, GH * GDK), F32), pltpu.VMEM((2, 2 * GDV, 2 * GDK), F32)]
    return _seq_call(functools.partial(_gla_kernel, states is not None, n), "gla_mixer", n, nseq, blk_off,
                     [zqk, zvr, zsm], consts, [(GH * GDV, BF)], extra_outs=extra, scratch=scratch)


def _mla_kernel(latent, n, past, *refs):
    if latent:
        (zmla, zsm, gq, wq, gkv, wkv, cckv, ckr, cosq, sinq, cosk, sink, o_ref, qs, kns, vs, krs) = refs
    else:
        (zmla, zsm, gq, wq, gkv, wkv, o_ref, ckv_ref, qs, kns, vs, krs) = refs
    sk = past + n
    scale = (M_NOPE + M_ROPE) ** -0.5
    nw, rw = MH * M_NOPE, MH * M_ROPE
    qa = _dot(_rms(zmla[:, 0:M_QL], gq[...]).astype(BF), wq[...])
    qr = qa[:, nw:nw + rw]
    if latent:
        qr = qr * cosq[...] + qa[:, nw + rw:nw + 2 * rw] * sinq[...]
    qs[:, 0:nw] = qa[:, 0:nw] * scale
    qs[:, nw:nw + rw] = qr * scale
    ckv = _rms(zmla[:, M_QL:M_QL + M_KVL], gkv[...])
    kv = _dot(ckv.astype(BF), wkv[...])
    kr = zsm[:, 32:64]
    if latent:
        kr = kr * cosk[...] + zsm[:, 64:96] * sink[...]
        kvc = _dot(cckv[0].astype(BF), wkv[...])
        kns[0:past, :] = kvc[:, 0:nw].astype(BF)
        vs[0:past, :] = kvc[:, nw:].astype(BF)
        krs[0:past, :] = jnp.concatenate([ckr[0]] * 4, axis=1).astype(BF)
    else:
        ckv_ref[...] = ckv
    kns[past:sk, :] = kv[:, 0:nw].astype(BF)
    vs[past:sk, :] = kv[:, nw:].astype(BF)
    krs[past:sk, :] = jnp.concatenate([kr] * 4, axis=1).astype(BF)

    qb = min(Q_BLOCK, n)
    lane = _iota((qb, 128), 1)

    def block(step, carry):
        rows = pl.ds(pl.multiple_of(step * qb, qb), qb)
        for p in range(MH // 2):
            ls = slice(128 * p, 128 * p + 128)
            qn = qs[rows, ls]
            quad = (2 * p) // 4
            qrp = qs[rows, nw + 128 * quad:nw + 128 * quad + 128]
            rhs = jnp.concatenate([kns[:, ls], krs[...]], axis=1)
            vp = vs[:, ls]
            o_pair = None
            for hh in range(2):
                j = (2 * p + hh) % 4
                qn_m = jnp.where((lane >> 6) == hh, qn, 0.0).astype(BF)
                qr_m = jnp.where((lane >> 5) == j, qrp, 0.0).astype(BF)
                s = _dot_nt(jnp.concatenate([qn_m, qr_m], axis=1), rhs)
                e = jnp.exp(s - jnp.max(s, axis=-1, keepdims=True))
                pv = _dot(e.astype(BF), vp) / jnp.sum(e, axis=-1, keepdims=True)
                o_pair = pv if hh == 0 else jnp.where(lane < M_V, o_pair, pv)
            o_ref[rows, ls] = o_pair.astype(BF)
        return carry

    lax.fori_loop(0, n // qb, block, 0)


def _rope_tables(n):
    half = M_ROPE // 2
    pos = jnp.arange(n)
    row = (pos // GRID_W).astype(F32)
    col = (pos % GRID_W).astype(F32)
    inv = ROPE_BASE ** (-jnp.arange(0, half, 2, dtype=F32) / half)
    ang = jnp.concatenate([row[:, None] * inv, col[:, None] * inv], axis=-1)
    cos = jnp.repeat(jnp.cos(ang), 2, axis=-1)
    sin = jnp.repeat(jnp.sin(ang), 2, axis=-1) * jnp.tile(jnp.asarray([-1.0, 1.0], F32), half)
    return jnp.tile(cos, (1, MH)), jnp.tile(sin, (1, MH)), cos, sin


def _mla(zmla, zsm, w, n, nseq, blk_off, cache):
    gq, wq, gkv, wkv = w
    c2 = lambda i: (0, 0)
    consts = [(gq, gq.shape, c2), (wq, wq.shape, c2), (gkv, gkv.shape, c2), (wkv, wkv.shape, c2)]
    past = 0
    extra = []
    if cache is not None:
        cckv, ckr = cache
        past = cckv.shape[1]
        c3 = lambda i: (i, 0, 0)
        consts += [(cckv, (1, past, M_KVL), c3), (ckr, (1, past, M_ROPE), c3)]
        consts += [(t, t.shape, c2) for t in _rope_tables(n)]
    else:
        extra = [((nseq * n, M_KVL), F32, (n, M_KVL), lambda i: (i, 0))]
    sk = past + n
    scratch = [pltpu.VMEM((n, MH * (M_NOPE + M_ROPE)), F32), pltpu.VMEM((sk, MH * M_NOPE), BF),
               pltpu.VMEM((sk, MH * M_V), BF), pltpu.VMEM((sk, 128), BF)]
    return _seq_call(functools.partial(_mla_kernel, cache is not None, n, past), "mla_mixer", n, nseq, blk_off,
                     [zmla, zsm], consts, [(MH * M_V, BF)], extra_outs=extra, scratch=scratch)


def _route(logits_t, bias):
    nt = logits_t.shape[1]
    gsz = N_EXP // N_GRP
    scores = _sigmoid(logits_t)
    sel = scores + bias
    neg = -jnp.inf
    sub = _iota((gsz, nt), 0)
    tops = []
    for g in range(N_GRP):
        blk = sel[gsz * g:gsz * g + gsz]
        m1 = jnp.max(blk, axis=0, keepdims=True)
        first = jnp.min(jnp.where(blk == m1, sub, gsz), axis=0, keepdims=True)
        m2 = jnp.max(jnp.where(sub == first, neg, blk), axis=0, keepdims=True)
        tops.append(m1 + m2)
    gs = jnp.concatenate(tops, axis=0)
    gidx = _iota((N_GRP, nt), 0)
    grank = jnp.zeros((N_GRP, nt), jnp.int32)
    for j in range(N_GRP):
        rj = gs[j:j + 1]
        grank += ((rj > gs) | ((rj == gs) & (gidx > j))).astype(jnp.int32)
    keep = grank < TOPK_GRP
    masked = jnp.concatenate(
        [jnp.where(jnp.broadcast_to(keep[g:g + 1], (gsz, nt)), sel[gsz * g:gsz * g + gsz], neg) for g in range(N_GRP)], axis=0)
    eidx = _iota((N_EXP, nt), 0)
    rank = jnp.zeros((N_EXP, nt), jnp.int32)
    for j in range(N_EXP):
        rj = masked[j:j + 1]
        rank += ((rj > masked) | ((rj == masked) & (eidx > j))).astype(jnp.int32)
    chosen = rank < TOP_K
    w = jnp.where(chosen, scores, 0.0)
    return chosen, w / jnp.sum(w, axis=0, keepdims=True) * ROUTED_SCALE


def _dispatch_meta(chosen, gates_t):
    tb = chosen.shape[1]
    sel = chosen.astype(F32)
    selb = sel.astype(BF)
    earlier = (_iota((tb, tb), 0) < _iota((tb, tb), 1)).astype(BF)
    rank = _dot(selb, earlier)
    cnt = jnp.sum(sel, axis=1, keepdims=True)
    padded = jnp.floor((cnt + (ROW_GRANULE - 1)) * (1.0 / ROW_GRANULE)) * ROW_GRANULE
    below = (_iota((N_EXP, N_EXP), 1) < _iota((N_EXP, N_EXP), 0)).astype(BF)
    start = _dot(below, jnp.broadcast_to(padded, (N_EXP, 128)).astype(BF))[:, 0:1]
    pos = start + rank
    kidx = _dot(below, selb)
    pos8, gate8 = [], []
    for k in range(TOP_K):
        hit = chosen & (kidx == float(k))
        pos8.append(jnp.sum(jnp.where(hit, pos, 0.0), axis=0, keepdims=True))
        gate8.append(jnp.sum(jnp.where(hit, gates_t, 0.0), axis=0, keepdims=True))
    return (jnp.concatenate(pos8, axis=0).astype(jnp.int32), jnp.concatenate(gate8, axis=0), cnt)


def _merge_kernel(n_ctx_blocks, x_ref, fc_ref, fl_ref, ogc_ref, ogl_ref, omc_ref, oml_ref, zg_ref, m_ref,
                  wbf, wbg, wbm, wout, gn2, wrt, brt, xm_ref, h2_ref, pos_ref, gate_ref, cnt_ref):
    is_ctx = pl.program_id(0) < n_ctx_blocks
    ya = _dot(jnp.where(is_ctx, fc_ref[...], fl_ref[...]), wbf[...])
    yb = _dot(jnp.where(is_ctx, ogc_ref[...], ogl_ref[...]), wbg[...])
    yc = _dot(jnp.where(is_ctx, omc_ref[...], oml_ref[...]), wbm[...])
    merged = (_sigmoid(zg_ref[:, 0:D].astype(F32)) * ya + _sigmoid(zg_ref[:, D:2 * D].astype(F32)) * yb
              + _sigmoid(zg_ref[:, 2 * D:3 * D].astype(F32)) * yc)
    xm = x_ref[...] + m_ref[0, :, 2 * D:3 * D] * _dot(merged.astype(BF), wout[...])
    xm_ref[...] = xm
    h2 = _rms(xm, gn2[...]) * (1.0 + m_ref[0, :, 4 * D:5 * D]) + m_ref[0, :, 3 * D:4 * D]
    h2_ref[...] = h2.astype(BF)
    chosen, gates_t = _route(_dot_nt(wrt[...], h2, precision=lax.Precision.HIGHEST), brt[...])
    for sb in range(gates_t.shape[1] // DISP_BLOCK):
        ls = slice(sb * DISP_BLOCK, (sb + 1) * DISP_BLOCK)
        pos8, gate8, cnt = _dispatch_meta(chosen[:, ls], gates_t[:, ls])
        pos_ref[:, ls] = pos8
        gate_ref[:, ls] = gate8
        cnt_ref[sb] = jnp.broadcast_to(cnt, (N_EXP, 128))


def _merge(x, mix_ctx, mix_lat, zg, mods, w, n_ctx_blocks, blocks_per_lat):
    T = x.shape[0]
    tb = TOKEN_BLOCK
    row = functools.partial(_mod_row, n_ctx_blocks, blocks_per_lat)
    rb = lambda wd: pl.BlockSpec((tb, wd), lambda i: (i, 0))
    cb = lambda a: pl.BlockSpec(a.shape, lambda i: (0, 0))
    ctx_b = lambda wd: pl.BlockSpec((tb, wd), lambda i: (jnp.minimum(i, n_ctx_blocks - 1), 0))
    lat_b = lambda wd: pl.BlockSpec((tb, wd), lambda i: (jnp.maximum(i - n_ctx_blocks, 0), 0))
    mix_specs, mix_args = [], []
    for a_c, a_l in zip(mix_ctx, mix_lat):
        mix_specs += [ctx_b(a_c.shape[1]), lat_b(a_l.shape[1])]
        mix_args += [a_c, a_l]
    return pl.pallas_call(
        functools.partial(_merge_kernel, n_ctx_blocks),
        grid=(T // tb,),
        in_specs=[rb(D)] + mix_specs + [rb(3 * D),
                  pl.BlockSpec((1, 1, 6 * D), lambda i: (row(i), 0, 0))] + [cb(a) for a in w],
        out_specs=[rb(D), rb(D), pl.BlockSpec((TOP_K, tb), lambda i: (0, i)), pl.BlockSpec((TOP_K, tb), lambda i: (0, i)),
                   pl.BlockSpec((tb // DISP_BLOCK, N_EXP, 128), lambda i: (i, 0, 0))],
        out_shape=[jax.ShapeDtypeStruct((T, D), F32), jax.ShapeDtypeStruct((T, D), BF),
                   jax.ShapeDtypeStruct((TOP_K, T), jnp.int32), jax.ShapeDtypeStruct((TOP_K, T), F32),
                   jax.ShapeDtypeStruct((T // DISP_BLOCK, N_EXP, 128), F32)],
        compiler_params=_params("parallel"),
        name="merge_route",
    )(x, *mix_args, zg, mods, *w)


def _silu_mul(a, b):
    return a * _sigmoid(a) * b


def _dispatch_kernel(h_ref, pos_ref, gate_ref, xs_ref, gt_ref):
    h = h_ref[...]
    rc = 256
    for r0 in range(0, DISP_ROWS, rc):
        rows = _iota((rc, DISP_BLOCK), 0) + r0
        gt = jnp.zeros((rc, DISP_BLOCK), F32)
        for k in range(TOP_K):
            gt = jnp.where(rows == pos_ref[k:k + 1, :], gate_ref[k:k + 1, :], gt)
        gt_ref[r0:r0 + rc, :] = gt.astype(BF)
        place = jnp.where(gt != 0.0, 1.0, 0.0).astype(BF)
        xs_ref[r0:r0 + rc, :] = _dot(place, h).astype(BF)


def _dispatch(h2, pos8, gate8):
    T = h2.shape[0]
    nblk = T // DISP_BLOCK
    return pl.pallas_call(
        _dispatch_kernel,
        grid=(nblk,),
        in_specs=[pl.BlockSpec((DISP_BLOCK, D), lambda b: (b, 0)),
                  pl.BlockSpec((TOP_K, DISP_BLOCK), lambda b: (0, b)),
                  pl.BlockSpec((TOP_K, DISP_BLOCK), lambda b: (0, b))],
        out_specs=[pl.BlockSpec((DISP_ROWS, D), lambda b: (b, 0)), pl.BlockSpec((DISP_ROWS, DISP_BLOCK), lambda b: (b, 0))],
        out_shape=[jax.ShapeDtypeStruct((nblk * DISP_ROWS, D), BF), jax.ShapeDtypeStruct((nblk * DISP_ROWS, DISP_BLOCK), BF)],
        compiler_params=_params("parallel"),
        name="moe_dispatch",
    )(h2, pos8, gate8)


def _tile_tables(cnt, n_chunks_max, n_tiles_max):
    nblk = cnt.shape[0]
    nch = (cnt + (ROW_GRANULE - 1)) // ROW_GRANULE
    first = jnp.arange(nblk, dtype=jnp.int32)[:, None] * BLOCK_CHUNKS + jnp.cumsum(nch, axis=1) - nch
    tiles_e = (jnp.sum(nch, axis=0) + (CHUNKS_PER_TILE - 1)) // CHUNKS_PER_TILE
    pad_e = tiles_e * CHUNKS_PER_TILE - jnp.sum(nch, axis=0)
    seg_len = jnp.concatenate([nch.T, pad_e[:, None]], axis=1).reshape(-1)
    seg_first = jnp.concatenate([first.T, jnp.full((N_EXP, 1), -1, jnp.int32)], axis=1).reshape(-1)
    seg_end = jnp.cumsum(seg_len)
    j = jnp.arange(n_chunks_max, dtype=jnp.int32)
    seg = jnp.minimum(jnp.searchsorted(seg_end, j, side="right"), seg_len.shape[0] - 1).astype(jnp.int32)
    within = j - (seg_end[seg] - seg_len[seg])
    real = (seg_first[seg] >= 0) & (j < seg_end[-1])
    src = jnp.where(real, seg_first[seg] + within, -1).astype(jnp.int32)
    tile_end = jnp.cumsum(tiles_e)
    i = jnp.arange(n_tiles_max, dtype=jnp.int32)
    tile_expert = jnp.minimum(jnp.searchsorted(tile_end, i, side="right"), N_EXP - 1).astype(jnp.int32)
    return src, tile_expert, tile_end[-1:].astype(jnp.int32)


def _expert_kernel(src_ref, texp_ref, nused_ref, xs_hbm, wg_ref, wu_ref, wd_ref, ys_hbm,
                   xbuf, ybuf, wgu_bf, wd_bf, gsem, ssem):
    i = pl.program_id(0)
    n_used = nused_ref[0]
    slot = lax.rem(i, 2)
    g = ROW_GRANULE

    def chunk_copies(tile, slot_, to_buffer, do):
        for c in range(CHUNKS_PER_TILE):
            src = src_ref[tile * CHUNKS_PER_TILE + c]

            @pl.when(src >= 0)
            def _():
                hbm_rows = pl.ds(pl.multiple_of(src * g, g), g)
                if to_buffer:
                    cp = pltpu.make_async_copy(xs_hbm.at[hbm_rows], xbuf.at[slot_, pl.ds(c * g, g)], gsem.at[slot_])
                else:
                    cp = pltpu.make_async_copy(ybuf.at[slot_, pl.ds(c * g, g)], ys_hbm.at[hbm_rows], ssem.at[slot_])
                do(cp)

    start = lambda cp: cp.start()
    wait = lambda cp: cp.wait()

    @pl.when(i == 0)
    def _():
        xbuf[...] = jnp.zeros_like(xbuf)
        chunk_copies(0, 0, True, start)

    @pl.when(i < n_used)
    def _():
        chunk_copies(i, slot, True, wait)

        @pl.when(i + 1 < n_used)
        def _():
            chunk_copies(i + 1, 1 - slot, True, start)

        @pl.when((i == 0) | (texp_ref[i] != texp_ref[jnp.maximum(i - 1, 0)]))
        def _():
            wgu_bf[:, 0:E_DIM] = wg_ref[0, 0].astype(BF)
            wgu_bf[:, E_DIM:2 * E_DIM] = wu_ref[0, 0].astype(BF)
            wd_bf[...] = wd_ref[0, 0].astype(BF)

        gu = _dot(xbuf[slot], wgu_bf[...])
        hid = _silu_mul(gu[:, 0:E_DIM], gu[:, E_DIM:2 * E_DIM])
        ybuf[slot] = _dot(hid.astype(BF), wd_bf[...]).astype(BF)
        chunk_copies(i, slot, False, start)

        @pl.when(i >= 1)
        def _():
            chunk_copies(i - 1, 1 - slot, False, wait)

        @pl.when(i == n_used - 1)
        def _():
            chunk_copies(i, slot, False, wait)


def _experts(layer, xs, tables, w_eg, w_eu, w_ed, n_tiles_max):
    src, tile_expert, n_used = tables
    wmap = lambda i, src_, texp, nu: (layer, texp[i], 0, 0)
    grid_spec = pltpu.PrefetchScalarGridSpec(
        num_scalar_prefetch=3,
        grid=(n_tiles_max,),
        in_specs=[pl.BlockSpec(memory_space=pl.ANY),
                  pl.BlockSpec((1, 1, D, E_DIM), wmap),
                  pl.BlockSpec((1, 1, D, E_DIM), wmap),
                  pl.BlockSpec((1, 1, E_DIM, D), wmap)],
        out_specs=pl.BlockSpec(memory_space=pl.ANY),
        scratch_shapes=[pltpu.VMEM((2, EXP_TILE, D), BF), pltpu.VMEM((2, EXP_TILE, D), BF),
                        pltpu.VMEM((D, 2 * E_DIM), BF), pltpu.VMEM((E_DIM, D), BF),
                        pltpu.SemaphoreType.DMA((2,)), pltpu.SemaphoreType.DMA((2,))],
    )
    return pl.pallas_call(
        _expert_kernel,
        grid_spec=grid_spec,
        out_shape=jax.ShapeDtypeStruct(xs.shape, xs.dtype),
        input_output_aliases={3: 0},
        compiler_params=_params("arbitrary"),
        name="moe_experts",
    )(src, tile_expert, n_used, xs, w_eg, w_eu, w_ed)


def _combine_kernel(final, ys_ref, gt_ref, h_ref, sg_ref, su_ref, sd_ref, x_ref, m_ref, gf_ref, o_ref):
    routed = _dot_tn(gt_ref[...], ys_ref[...])
    h = h_ref[...]
    sh = _silu_mul(_dot(h, sg_ref[0].astype(BF)), _dot(h, su_ref[0].astype(BF)))
    out = x_ref[...] + m_ref[0] * (routed + _dot(sh.astype(BF), sd_ref[0].astype(BF)))
    if final:
        out = _rms(out, gf_ref[...])
    o_ref[...] = out


def _combine(layer, final, ys, gt, h2, w_sg, w_su, w_sd, xm, mods, g_final, n_ctx_blocks, blocks_per_lat):
    T = h2.shape[0]
    tb = DISP_BLOCK
    row = functools.partial(_mod_row, n_ctx_blocks, blocks_per_lat)
    return pl.pallas_call(
        functools.partial(_combine_kernel, final),
        grid=(T // tb,),
        in_specs=[pl.BlockSpec((DISP_ROWS, D), lambda b: (b, 0)),
                  pl.BlockSpec((DISP_ROWS, tb), lambda b: (b, 0)),
                  pl.BlockSpec((tb, D), lambda b: (b, 0)),
                  pl.BlockSpec((1, D, E_DIM), lambda b: (layer, 0, 0)),
                  pl.BlockSpec((1, D, E_DIM), lambda b: (layer, 0, 0)),
                  pl.BlockSpec((1, E_DIM, D), lambda b: (layer, 0, 0)),
                  pl.BlockSpec((tb, D), lambda b: (b, 0)),
                  pl.BlockSpec((1, 1, D), lambda b: (row(b), 0, 5)),
                  pl.BlockSpec((1, D), lambda b: (0, 0))],
        out_specs=pl.BlockSpec((tb, D), lambda b: (b, 0)),
        out_shape=jax.ShapeDtypeStruct((T, D), F32),
        compiler_params=_params("parallel"),
        name="moe_combine",
    )(ys, gt, h2, w_sg, w_su, w_sd, xm, mods, g_final)


def _moe(layer, final, h2, pos8, gate8, cnt, w_eg, w_eu, w_ed, w_sg, w_su, w_sd, xm, mods, g_final, disp_blocks):
    T = h2.shape[0]
    nblk = T // DISP_BLOCK
    n_chunks_max = (TOP_K * T + N_EXP * nblk * (ROW_GRANULE - 1)) // ROW_GRANULE + N_EXP * (CHUNKS_PER_TILE - 1)
    n_tiles_max = -(-n_chunks_max // CHUNKS_PER_TILE)
    xs, gt = _dispatch(h2, pos8, gate8)
    tables = _tile_tables(cnt[:, :, 0].astype(jnp.int32), n_tiles_max * CHUNKS_PER_TILE, n_tiles_max)
    ys = _experts(layer, xs, tables, w_eg, w_eu, w_ed, n_tiles_max)
    return _combine(layer, final, ys, gt, h2, w_sg, w_su, w_sd, xm, mods, g_final, *disp_blocks)


def _reorder_w_in(w_in):
    kr = w_in[:, :, 2592:2624]
    kr_sw = kr.reshape(kr.shape[0], D, M_ROPE // 2, 2)[..., ::-1].reshape(kr.shape)
    pad = jnp.zeros(kr.shape, w_in.dtype)
    out = jnp.concatenate([w_in[:, :, :1920], w_in[:, :, 1952:2592], w_in[:, :, 1920:1952], kr, kr_sw, pad,
                           w_in[:, :, 2624:]], axis=2)
    return out.astype(BF)


def _reorder_w_q(w_q_up, with_swap):
    L = w_q_up.shape[0]
    w = w_q_up.reshape(L, M_QL, MH, M_NOPE + M_ROPE)
    nope = w[..., :M_NOPE].reshape(L, M_QL, MH * M_NOPE)
    rope = w[..., M_NOPE:]
    parts = [nope, rope.reshape(L, M_QL, MH * M_ROPE)]
    if with_swap:
        parts.append(rope.reshape(L, M_QL, MH, M_ROPE // 2, 2)[..., ::-1].reshape(L, M_QL, MH * M_ROPE))
    return jnp.concatenate(parts, axis=2).astype(BF)


def _reorder_w_kv(w_kv_up):
    L = w_kv_up.shape[0]
    w = w_kv_up.reshape(L, M_KVL, MH, M_NOPE + M_V)
    return jnp.concatenate([w[..., :M_NOPE].reshape(L, M_KVL, MH * M_NOPE),
                            w[..., M_NOPE:].reshape(L, M_KVL, MH * M_V)], axis=2).astype(BF)


def kernel(x_prompt, x_sample, state_gla_fwd, state_gla_bwd, cache_mla_ckv, cache_mla_krope, c, c_ctx, w_mod, b_mod, g_norm1, g_norm2, w_in, w_gla_gate_f, b_gla_gate_f, w_gla_gate_b, b_gla_gate_b, g_gla_out, g_q_a, w_q_up, g_kv_a, w_kv_up, w_br_fourier, w_br_gla, w_br_mla, w_out, w_router, b_router, w_exp_gate, w_exp_up, w_exp_down, w_sh_gate, w_sh_up, w_sh_down, g_final):
    nb, sl, _ = x_prompt.shape
    db, dl, _ = x_sample.shape
    L = w_mod.shape[0]
    t_ctx, t_lat = nb * sl, db * dl
    T = t_ctx + t_lat
    assert sl % G_CHUNK == 0 and dl % G_CHUNK == 0 and dl % GRID_W == 0
    assert t_ctx % dl == 0 and dl % TOKEN_BLOCK == 0 and TOKEN_BLOCK % DISP_BLOCK == 0
    assert t_ctx % TOKEN_BLOCK == 0 and dl % Q_BLOCK == 0 and 1 + db <= 8

    x = jnp.concatenate([x_prompt.reshape(t_ctx, D), x_sample.reshape(t_lat, D)], axis=0)
    cond = jnp.concatenate([c_ctx[None, :], c, jnp.zeros((7 - db, D), F32)], axis=0)
    mods_all = _modulation(cond, w_mod, b_mod)

    w_in_r = _reorder_w_in(w_in)
    wq_ctx = _reorder_w_q(w_q_up, False)
    wq_lat = _reorder_w_q(w_q_up, True)
    wkv_r = _reorder_w_kv(w_kv_up)
    lat_off = t_ctx // dl

    new_f, new_b, new_ckv, new_kr = [], [], [], []
    for l in range(L):
        mods = mods_all[l].reshape(8, 1, 6 * D)
        tok = (t_ctx // TOKEN_BLOCK, dl // TOKEN_BLOCK)
        zf, zqk, zvr, zmla, zsm, zg = _in_projection(x, mods, g_norm1[l][None, :], w_in_r[l], *tok)

        (f_c,) = _fourier(zf, sl, nb, 0)
        (f_l,) = _fourier(zf, dl, db, lat_off)

        gate_w = (w_gla_gate_f[l], b_gla_gate_f[l][None, :], w_gla_gate_b[l], b_gla_gate_b[l][None, :],
                  g_gla_out[l].reshape(1, GH * GDV))
        og_c, s_f, s_b = _gla(zqk, zvr, zsm, gate_w, sl, nb, 0, None)
        og_l, _, _ = _gla(zqk, zvr, zsm, gate_w, dl, db, lat_off, (state_gla_fwd[:, l], state_gla_bwd[:, l]))

        gq, gkv = g_q_a[l][None, :], g_kv_a[l][None, :]
        om_c, ckv = _mla(zmla, zsm, (gq, wq_ctx[l], gkv, wkv_r[l]), sl, nb, 0, None)
        (om_l,) = _mla(zmla, zsm, (gq, wq_lat[l], gkv, wkv_r[l]), dl, db, lat_off,
                       (cache_mla_ckv[:, l], cache_mla_krope[:, l]))

        mw = (w_br_fourier[l].astype(BF), w_br_gla[l].astype(BF), w_br_mla[l].astype(BF), w_out[l].astype(BF),
              g_norm2[l][None, :], w_router[l].T, b_router[l][:, None])
        xm, h2, pos8, gate8, cnt = _merge(x, (f_c, og_c, om_c), (f_l, og_l, om_l), zg, mods, mw, *tok)

        x = _moe(l, l == L - 1, h2, pos8, gate8, cnt, w_exp_gate, w_exp_up, w_exp_down, w_sh_gate, w_sh_up, w_sh_down,
                 xm, mods, g_final[None, :], (t_ctx // DISP_BLOCK, dl // DISP_BLOCK))

        new_f.append(s_f)
        new_b.append(s_b)
        new_ckv.append(ckv.reshape(nb, sl, M_KVL))
        new_kr.append(zsm[:t_ctx, 32:64].reshape(nb, sl, M_ROPE))

    y_prompt = x[:t_ctx].reshape(nb, sl, D)
    y_sample = x[t_ctx:].reshape(db, dl, D)
    return (y_prompt, y_sample, jnp.stack(new_f, axis=1), jnp.stack(new_b, axis=1),
            jnp.stack(new_ckv, axis=1), jnp.stack(new_kr, axis=1))
```

```python
import functools

import numpy as np
import jax
import jax.numpy as jnp
from jax import lax
from jax.experimental import pallas as pl
from jax.experimental.pallas import tpu as pltpu

F32 = jnp.float32
BF = jnp.bfloat16

D = 1024
GRID_W = 64
FN_G, FN_GW = 4, 96
FN_W = FN_G * FN_GW
GH, GDK, GDV = 4, 64, 128
G_RANK = 16
G_TAU = 16.0
G_CHUNK = 64
MH, M_NOPE, M_ROPE, M_V = 8, 64, 32, 64
M_QL, M_KVL = 384, 256
ROPE_BASE = 10000.0
N_EXP, TOP_K, N_GRP, TOPK_GRP = 64, 8, 8, 4
E_DIM = 256
ROUTED_SCALE = 2.5
EPS = 1e-6

C_F = (0, 384)
C_QK = (384, 896)
C_VR = (896, 1920)
C_MLA = (1920, 2560)
C_SM = (2560, 2688)
C_G = (2688, 5760)
IN_COLS_R = 5760

VMEM_LIMIT_V7X = 56 * 1024 * 1024
TOKEN_BLOCK = 512
Q_BLOCK = 256
DISP_BLOCK = 256
ROW_GRANULE = 16
DISP_ROWS = -(-(TOP_K * DISP_BLOCK + N_EXP * (ROW_GRANULE - 1)) // 256) * 256
EXP_TILE = 256
CHUNKS_PER_TILE = EXP_TILE // ROW_GRANULE
BLOCK_CHUNKS = DISP_ROWS // ROW_GRANULE


def _params(*sem):
    return pltpu.CompilerParams(dimension_semantics=sem, vmem_limit_bytes=VMEM_LIMIT_V7X)


def _dot(a, b):
    return jnp.dot(a, b, preferred_element_type=F32)


def _dot_hi(a, b):
    return jnp.dot(a, b, precision=lax.Precision.HIGHEST, preferred_element_type=F32)


def _dot_nt(a, b, precision=None):
    return lax.dot_general(a, b, (((1,), (1,)), ((), ())), precision=precision, preferred_element_type=F32)


def _dot_tn(a, b):
    return lax.dot_general(a, b, (((0,), (0,)), ((), ())), preferred_element_type=F32)


def _sigmoid(x):
    return 1.0 / (1.0 + jnp.exp(-x))


def _rms(x, g):
    return x * lax.rsqrt(jnp.mean(x * x, axis=-1, keepdims=True) + EPS) * g


def _iota(shape, dim):
    return lax.broadcasted_iota(jnp.int32, shape, dim)


def _mod_row(n_ctx_blocks, blocks_per_lat, i):
    return jnp.where(i < n_ctx_blocks, 0, 1 + (i - n_ctx_blocks) // blocks_per_lat)


def _mod_kernel(c_ref, w_ref, b_ref, o_ref):
    c = c_ref[...]
    o_ref[0] = _dot_hi(c * _sigmoid(c), w_ref[0]) + b_ref[0]


def _modulation(cond, w_mod, b_mod):
    L = w_mod.shape[0]
    rows = cond.shape[0]
    tn = 1536
    return pl.pallas_call(
        _mod_kernel,
        grid=(L, 6 * D // tn),
        in_specs=[pl.BlockSpec((rows, D), lambda l, j: (0, 0)),
                  pl.BlockSpec((1, D, tn), lambda l, j: (l, 0, j)),
                  pl.BlockSpec((1, 1, tn), lambda l, j: (l, 0, j))],
        out_specs=pl.BlockSpec((1, rows, tn), lambda l, j: (l, 0, j)),
        out_shape=jax.ShapeDtypeStruct((L, rows, 6 * D), F32),
        compiler_params=_params("parallel", "parallel"),
        name="modulation",
    )(cond, w_mod, b_mod.reshape(L, 1, 6 * D))


def _inproj_kernel(x_ref, m_ref, g_ref, w_ref, of_ref, oqk_ref, ovr_ref, omla_ref, osm_ref, og_ref):
    y = _rms(x_ref[...], g_ref[...])
    h = (y * (1.0 + m_ref[0, :, D:2 * D]) + m_ref[0, :, 0:D]).astype(BF)
    of_ref[...] = _dot(h, w_ref[:, C_F[0]:C_F[1]]).astype(BF)
    oqk_ref[...] = _dot(h, w_ref[:, C_QK[0]:C_QK[1]])
    ovr_ref[...] = _dot(h, w_ref[:, C_VR[0]:C_VR[1]]).astype(BF)
    omla_ref[...] = _dot(h, w_ref[:, C_MLA[0]:C_MLA[1]])
    osm_ref[...] = _dot(h, w_ref[:, C_SM[0]:C_SM[1]])
    og_ref[...] = _dot(h, w_ref[:, C_G[0]:C_G[1]]).astype(BF)


def _in_projection(x, mods, g1, w_in_r, n_ctx_blocks, blocks_per_lat):
    T = x.shape[0]
    tb = TOKEN_BLOCK
    row = functools.partial(_mod_row, n_ctx_blocks, blocks_per_lat)
    widths = [(C_F, BF), (C_QK, F32), (C_VR, BF), (C_MLA, F32), (C_SM, F32), (C_G, BF)]
    return pl.pallas_call(
        _inproj_kernel,
        grid=(T // tb,),
        in_specs=[pl.BlockSpec((tb, D), lambda i: (i, 0)),
                  pl.BlockSpec((1, 1, 2 * D), lambda i: (row(i), 0, 0)),
                  pl.BlockSpec((1, D), lambda i: (0, 0)),
                  pl.BlockSpec((D, IN_COLS_R), lambda i: (0, 0))],
        out_specs=[pl.BlockSpec((tb, c[1] - c[0]), lambda i: (i, 0)) for c, _ in widths],
        out_shape=[jax.ShapeDtypeStruct((T, c[1] - c[0]), dt) for c, dt in widths],
        compiler_params=_params("parallel"),
        name="in_projection",
    )(x, mods, g1, w_in_r)


def _seq_call(kernel, name, n, nseq, blk_off, seq_ins, const_ins, out_widths, extra_outs=(), scratch=()):
    in_specs = [pl.BlockSpec((n, a.shape[1]), lambda i: (i + blk_off, 0)) for a in seq_ins]
    in_specs += [pl.BlockSpec(bs, im) for _, bs, im in const_ins]
    args = list(seq_ins) + [a for a, _, _ in const_ins]
    out_specs = [pl.BlockSpec((n, w), lambda i: (i, 0)) for w, _ in out_widths]
    out_shape = [jax.ShapeDtypeStruct((nseq * n, w), dt) for w, dt in out_widths]
    out_specs += [pl.BlockSpec(bs, im) for _, _, bs, im in extra_outs]
    out_shape += [jax.ShapeDtypeStruct(s, dt) for s, dt, _, _ in extra_outs]
    return pl.pallas_call(
        kernel, grid=(nseq,), in_specs=in_specs, out_specs=out_specs, out_shape=out_shape,
        scratch_shapes=list(scratch), compiler_params=_params("parallel"), name=name,
    )(*args)


def _fourier_kernel(u_ref, r_ref, lc_ref, ls_ref, o_ref):
    y = _dot(u_ref[...], r_ref[...])
    o_ref[...] = (_dot(lc_ref[...], y[:, :FN_W].astype(BF)) + _dot(ls_ref[...], y[:, FN_W:].astype(BF))).astype(BF)


def _dft_tables(n):
    k = np.arange(FN_GW)
    ang = 2.0 * np.pi * ((k[:, None] * k[None, :]) % FN_GW) / FN_GW
    eye = np.eye(FN_G)
    right = np.concatenate([np.kron(eye, np.cos(ang)), np.kron(eye, np.sin(ang))], axis=1)
    p = np.arange(n)
    angn = 2.0 * np.pi * ((p[:, None] * p[None, :]) % n) / n
    scale = 1.0 / np.sqrt(float(n * FN_GW))
    return (jnp.asarray(right, F32).astype(BF), jnp.asarray(np.cos(angn) * scale, F32).astype(BF),
            jnp.asarray(-np.sin(angn) * scale, F32).astype(BF))


def _fourier(zf, n, nseq, blk_off):
    right, lc, ls = _dft_tables(n)
    consts = [(right, (FN_W, 2 * FN_W), lambda i: (0, 0)), (lc, (n, n), lambda i: (0, 0)), (ls, (n, n), lambda i: (0, 0))]
    return _seq_call(_fourier_kernel, "fourier_mix", n, nseq, blk_off, [zf], consts, [(FN_W, BF)])


def _log_gate(z, w_ref, b_ref):
    pre = _dot_hi(z, w_ref[...]) + b_ref[...]
    return (jnp.minimum(pre, 0.0) - jnp.log1p(jnp.exp(-jnp.abs(pre)))) * (1.0 / G_TAU)


def _gla_kernel(has_state, n, *refs):
    if has_state:
        (zqk, zvr, zsm, wgf, bgf, wgb, bgb, gout, s0f, s0b, o_ref, sf_ref, sb_ref, oacc, lg, st) = refs
    else:
        (zqk, zvr, zsm, wgf, bgf, wgb, bgb, gout, o_ref, sf_ref, sb_ref, oacc, lg, st) = refs
        s0f = s0b = None
    C = G_CHUNK
    nc = n // C
    ri, ci = _iota((C, C), 0), _iota((C, C), 1)
    t_idx, lane = _iota((C, 128), 0), _iota((C, 128), 1)
    s_idx = lane & (C - 1)
    left = lane < GDK
    vleft = _iota((C, 2 * GDV), 1) < GDV
    blockdiag = (_iota((2 * GDV, 2 * GDK), 0) >> 7) == (_iota((2 * GDV, 2 * GDK), 1) >> 6)

    def load_state(s0_ref):
        for p in range(2):
            if s0_ref is None:
                st[p] = jnp.zeros((2 * GDV, 2 * GDK), F32)
            else:
                z = jnp.zeros((GDK, GDV), F32)
                blk = jnp.concatenate([jnp.concatenate([s0_ref[0, 2 * p], z], axis=1),
                                       jnp.concatenate([z, s0_ref[0, 2 * p + 1]], axis=1)], axis=0)
                st[p] = blk.T

    def store_state(out_ref):
        for p in range(2):
            blk = st[p].T
            out_ref[0, 2 * p] = blk[0:GDK, 0:GDV]
            out_ref[0, 2 * p + 1] = blk[GDK:2 * GDK, GDV:2 * GDV]

    def run(reverse):
        tri = (ci >= ri).astype(F32) if reverse else (ci <= ri).astype(F32)
        mask = (s_idx >= t_idx) if reverse else (s_idx <= t_idx)

        def chunk(step, carry):
            c = (nc - 1 - step) if reverse else step
            rows = pl.ds(pl.multiple_of(c * C, C), C)
            cum = _dot_hi(tri, lg[rows, :])
            tot = cum[0:1] if reverse else cum[C - 1:C]
            q = zqk[rows, 0:GH * GDK] * (GDK ** -0.5)
            k = zqk[rows, GH * GDK:2 * GH * GDK]
            qh = (q * jnp.exp(cum)).astype(BF)
            kh = (k * jnp.exp(-cum)).astype(BF)
            kb = (k * jnp.exp(tot - cum)).astype(BF)
            dec = jnp.exp(tot)
            for p in range(2):
                ls = slice(128 * p, 128 * p + 128)
                vs = slice(256 * p, 256 * p + 256)
                qp, kp = qh[:, ls], kh[:, ls]
                zk = jnp.zeros_like(kp)
                kblk = jnp.concatenate([jnp.where(left, kp, zk), jnp.where(left, zk, kp)], axis=0)
                sc = jnp.where(mask, _dot_nt(qp, kblk), 0.0).astype(BF)
                vp = zvr[rows, vs]
                zv = jnp.zeros_like(vp)
                vblk = jnp.concatenate([jnp.where(vleft, vp, zv), jnp.where(vleft, zv, vp)], axis=0)
                stp = st[p]
                o_p = _dot(sc, vblk) + _dot_nt(qp, stp.astype(BF))
                st[p] = dec[:, ls] * stp + jnp.where(blockdiag, _dot_tn(vp, kb[:, ls]), 0.0)
                if not reverse:
                    oacc[rows, vs] = o_p
                else:
                    o_t = oacc[rows, vs] + o_p
                    for hh in range(2):
                        hs = slice(256 * p + 128 * hh, 256 * p + 128 * hh + 128)
                        oh = o_t[:, 128 * hh:128 * hh + 128]
                        oh = oh * lax.rsqrt(jnp.mean(oh * oh, axis=-1, keepdims=True) + EPS) * gout[:, hs]
                        r = zvr[rows, GH * GDV + hs.start:GH * GDV + hs.stop].astype(F32)
                        o_ref[rows, hs] = (oh * (r * _sigmoid(r))).astype(BF)
            return carry

        lax.fori_loop(0, nc, chunk, 0)

    lg[...] = _log_gate(zsm[:, 0:G_RANK], wgf, bgf)
    load_state(s0f)
    run(False)
    store_state(sf_ref)
    lg[...] = _log_gate(zsm[:, G_RANK:2 * G_RANK], wgb, bgb)
    load_state(s0b)
    run(True)
    store_state(sb_ref)


def _gla(zqk, zvr, zsm, gate_w, n, nseq, blk_off, states):
    wgf, bgf, wgb, bgb, gout = gate_w
    c2 = lambda i: (0, 0)
    consts = [(wgf, wgf.shape, c2), (bgf, bgf.shape, c2), (wgb, wgb.shape, c2), (bgb, bgb.shape, c2), (gout, gout.shape, c2)]
    st_blk = (1, GH, GDK, GDV)
    st_map = lambda i: (i, 0, 0, 0)
    if states is not None:
        consts += [(s, st_blk, st_map) for s in states]
    extra = [((nseq, GH, GDK, GDV), F32, st_blk, st_map)] * 2
    scratch = [pltpu.VMEM((n, GH * GDV), F32), pltpu.VMEM((n, GH * GDK), F32), pltpu.VMEM((2, 2 * GDV, 2 * GDK), F32)]
    return _seq_call(functools.partial(_gla_kernel, states is not None, n), "gla_mixer", n, nseq, blk_off,
                     [zqk, zvr, zsm], consts, [(GH * GDV, BF)], extra_outs=extra, scratch=scratch)


def _mla_kernel(latent, n, past, *refs):
    if latent:
        (zmla, zsm, gq, wq, gkv, wkv, cckv, ckr, cosq, sinq, cosk, sink, o_ref, qs, kns, vs, krs) = refs
    else:
        (zmla, zsm, gq, wq, gkv, wkv, o_ref, ckv_ref, qs, kns, vs, krs) = refs
    sk = past + n
    scale = (M_NOPE + M_ROPE) ** -0.5
    nw, rw = MH * M_NOPE, MH * M_ROPE
    qa = _dot(_rms(zmla[:, 0:M_QL], gq[...]).astype(BF), wq[...])
    qr = qa[:, nw:nw + rw]
    if latent:
        qr = qr * cosq[...] + qa[:, nw + rw:nw + 2 * rw] * sinq[...]
    qs[:, 0:nw] = qa[:, 0:nw] * scale
    qs[:, nw:nw + rw] = qr * scale
    ckv = _rms(zmla[:, M_QL:M_QL + M_KVL], gkv[...])
    kv = _dot(ckv.astype(BF), wkv[...])
    kr = zsm[:, 32:64]
    if latent:
        kr = kr * cosk[...] + zsm[:, 64:96] * sink[...]
        kvc = _dot(cckv[0].astype(BF), wkv[...])
        kns[0:past, :] = kvc[:, 0:nw].astype(BF)
        vs[0:past, :] = kvc[:, nw:].astype(BF)
        krs[0:past, :] = jnp.concatenate([ckr[0]] * 4, axis=1).astype(BF)
    else:
        ckv_ref[...] = ckv
    kns[past:sk, :] = kv[:, 0:nw].astype(BF)
    vs[past:sk, :] = kv[:, nw:].astype(BF)
    krs[past:sk, :] = jnp.concatenate([kr] * 4, axis=1).astype(BF)

    qb = min(Q_BLOCK, n)
    lane = _iota((qb, 128), 1)

    def block(step, carry):
        rows = pl.ds(pl.multiple_of(step * qb, qb), qb)
        for p in range(MH // 2):
            ls = slice(128 * p, 128 * p + 128)
            qn = qs[rows, ls]
            quad = (2 * p) // 4
            qrp = qs[rows, nw + 128 * quad:nw + 128 * quad + 128]
            rhs = jnp.concatenate([kns[:, ls], krs[...]], axis=1)
            vp = vs[:, ls]
            o_pair = None
            for hh in range(2):
                j = (2 * p + hh) % 4
                qn_m = jnp.where((lane >> 6) == hh, qn, 0.0).astype(BF)
                qr_m = jnp.where((lane >> 5) == j, qrp, 0.0).astype(BF)
                s = _dot_nt(jnp.concatenate([qn_m, qr_m], axis=1), rhs)
                e = jnp.exp(s - jnp.max(s, axis=-1, keepdims=True))
                pv = _dot(e.astype(BF), vp) / jnp.sum(e, axis=-1, keepdims=True)
                o_pair = pv if hh == 0 else jnp.where(lane < M_V, o_pair, pv)
            o_ref[rows, ls] = o_pair.astype(BF)
        return carry

    lax.fori_loop(0, n // qb, block, 0)


def _rope_tables(n):
    half = M_ROPE // 2
    pos = jnp.arange(n)
    row = (pos // GRID_W).astype(F32)
    col = (pos % GRID_W).astype(F32)
    inv = ROPE_BASE ** (-jnp.arange(0, half, 2, dtype=F32) / half)
    ang = jnp.concatenate([row[:, None] * inv, col[:, None] * inv], axis=-1)
    cos = jnp.repeat(jnp.cos(ang), 2, axis=-1)
    sin = jnp.repeat(jnp.sin(ang), 2, axis=-1) * jnp.tile(jnp.asarray([-1.0, 1.0], F32), half)
    return jnp.tile(cos, (1, MH)), jnp.tile(sin, (1, MH)), cos, sin


def _mla(zmla, zsm, w, n, nseq, blk_off, cache):
    gq, wq, gkv, wkv = w
    c2 = lambda i: (0, 0)
    consts = [(gq, gq.shape, c2), (wq, wq.shape, c2), (gkv, gkv.shape, c2), (wkv, wkv.shape, c2)]
    past = 0
    extra = []
    if cache is not None:
        cckv, ckr = cache
        past = cckv.shape[1]
        c3 = lambda i: (i, 0, 0)
        consts += [(cckv, (1, past, M_KVL), c3), (ckr, (1, past, M_ROPE), c3)]
        consts += [(t, t.shape, c2) for t in _rope_tables(n)]
    else:
        extra = [((nseq * n, M_KVL), F32, (n, M_KVL), lambda i: (i, 0))]
    sk = past + n
    scratch = [pltpu.VMEM((n, MH * (M_NOPE + M_ROPE)), F32), pltpu.VMEM((sk, MH * M_NOPE), BF),
               pltpu.VMEM((sk, MH * M_V), BF), pltpu.VMEM((sk, 128), BF)]
    return _seq_call(functools.partial(_mla_kernel, cache is not None, n, past), "mla_mixer", n, nseq, blk_off,
                     [zmla, zsm], consts, [(MH * M_V, BF)], extra_outs=extra, scratch=scratch)


def _route(logits_t, bias):
    nt = logits_t.shape[1]
    gsz = N_EXP // N_GRP
    scores = _sigmoid(logits_t)
    sel = scores + bias
    neg = -jnp.inf
    sub = _iota((gsz, nt), 0)
    tops = []
    for g in range(N_GRP):
        blk = sel[gsz * g:gsz * g + gsz]
        m1 = jnp.max(blk, axis=0, keepdims=True)
        first = jnp.min(jnp.where(blk == m1, sub, gsz), axis=0, keepdims=True)
        m2 = jnp.max(jnp.where(sub == first, neg, blk), axis=0, keepdims=True)
        tops.append(m1 + m2)
    gs = jnp.concatenate(tops, axis=0)
    gidx = _iota((N_GRP, nt), 0)
    grank = jnp.zeros((N_GRP, nt), jnp.int32)
    for j in range(N_GRP):
        rj = gs[j:j + 1]
        grank += ((rj > gs) | ((rj == gs) & (gidx > j))).astype(jnp.int32)
    keep = grank < TOPK_GRP
    masked = jnp.concatenate(
        [jnp.where(jnp.broadcast_to(keep[g:g + 1], (gsz, nt)), sel[gsz * g:gsz * g + gsz], neg) for g in range(N_GRP)], axis=0)
    eidx = _iota((N_EXP, nt), 0)
    rank = jnp.zeros((N_EXP, nt), jnp.int32)
    for j in range(N_EXP):
        rj = masked[j:j + 1]
        rank += ((rj > masked) | ((rj == masked) & (eidx > j))).astype(jnp.int32)
    chosen = rank < TOP_K
    w = jnp.where(chosen, scores, 0.0)
    return chosen, w / jnp.sum(w, axis=0, keepdims=True) * ROUTED_SCALE


def _dispatch_meta(chosen, gates_t):
    tb = chosen.shape[1]
    sel = chosen.astype(F32)
    selb = sel.astype(BF)
    earlier = (_iota((tb, tb), 0) < _iota((tb, tb), 1)).astype(BF)
    rank = _dot(selb, earlier)
    cnt = jnp.sum(sel, axis=1, keepdims=True)
    padded = jnp.floor((cnt + (ROW_GRANULE - 1)) * (1.0 / ROW_GRANULE)) * ROW_GRANULE
    below = (_iota((N_EXP, N_EXP), 1) < _iota((N_EXP, N_EXP), 0)).astype(BF)
    start = _dot(below, jnp.broadcast_to(padded, (N_EXP, 128)).astype(BF))[:, 0:1]
    pos = start + rank
    kidx = _dot(below, selb)
    pos8, gate8 = [], []
    for k in range(TOP_K):
        hit = chosen & (kidx == float(k))
        pos8.append(jnp.sum(jnp.where(hit, pos, 0.0), axis=0, keepdims=True))
        gate8.append(jnp.sum(jnp.where(hit, gates_t, 0.0), axis=0, keepdims=True))
    return (jnp.concatenate(pos8, axis=0).astype(jnp.int32), jnp.concatenate(gate8, axis=0), cnt)


def _merge_kernel(n_ctx_blocks, x_ref, fc_ref, fl_ref, ogc_ref, ogl_ref, omc_ref, oml_ref, zg_ref, m_ref,
                  wbf, wbg, wbm, wout, gn2, wrt, brt, xm_ref, h2_ref, pos_ref, gate_ref, cnt_ref):
    is_ctx = pl.program_id(0) < n_ctx_blocks
    ya = _dot(jnp.where(is_ctx, fc_ref[...], fl_ref[...]), wbf[...])
    yb = _dot(jnp.where(is_ctx, ogc_ref[...], ogl_ref[...]), wbg[...])
    yc = _dot(jnp.where(is_ctx, omc_ref[...], oml_ref[...]), wbm[...])
    merged = (_sigmoid(zg_ref[:, 0:D].astype(F32)) * ya + _sigmoid(zg_ref[:, D:2 * D].astype(F32)) * yb
              + _sigmoid(zg_ref[:, 2 * D:3 * D].astype(F32)) * yc)
    xm = x_ref[...] + m_ref[0, :, 2 * D:3 * D] * _dot(merged.astype(BF), wout[...])
    xm_ref[...] = xm
    h2 = _rms(xm, gn2[...]) * (1.0 + m_ref[0, :, 4 * D:5 * D]) + m_ref[0, :, 3 * D:4 * D]
    h2_ref[...] = h2.astype(BF)
    chosen, gates_t = _route(_dot_nt(wrt[...], h2, precision=lax.Precision.HIGHEST), brt[...])
    for sb in range(gates_t.shape[1] // DISP_BLOCK):
        ls = slice(sb * DISP_BLOCK, (sb + 1) * DISP_BLOCK)
        pos8, gate8, cnt = _dispatch_meta(chosen[:, ls], gates_t[:, ls])
        pos_ref[:, ls] = pos8
        gate_ref[:, ls] = gate8
        cnt_ref[sb] = jnp.broadcast_to(cnt, (N_EXP, 128))


def _merge(x, mix_ctx, mix_lat, zg, mods, w, n_ctx_blocks, blocks_per_lat):
    T = x.shape[0]
    tb = TOKEN_BLOCK
    row = functools.partial(_mod_row, n_ctx_blocks, blocks_per_lat)
    rb = lambda wd: pl.BlockSpec((tb, wd), lambda i: (i, 0))
    cb = lambda a: pl.BlockSpec(a.shape, lambda i: (0, 0))
    ctx_b = lambda wd: pl.BlockSpec((tb, wd), lambda i: (jnp.minimum(i, n_ctx_blocks - 1), 0))
    lat_b = lambda wd: pl.BlockSpec((tb, wd), lambda i: (jnp.maximum(i - n_ctx_blocks, 0), 0))
    mix_specs, mix_args = [], []
    for a_c, a_l in zip(mix_ctx, mix_lat):
        mix_specs += [ctx_b(a_c.shape[1]), lat_b(a_l.shape[1])]
        mix_args += [a_c, a_l]
    return pl.pallas_call(
        functools.partial(_merge_kernel, n_ctx_blocks),
        grid=(T // tb,),
        in_specs=[rb(D)] + mix_specs + [rb(3 * D),
                  pl.BlockSpec((1, 1, 6 * D), lambda i: (row(i), 0, 0))] + [cb(a) for a in w],
        out_specs=[rb(D), rb(D), pl.BlockSpec((TOP_K, tb), lambda i: (0, i)), pl.BlockSpec((TOP_K, tb), lambda i: (0, i)),
                   pl.BlockSpec((tb // DISP_BLOCK, N_EXP, 128), lambda i: (i, 0, 0))],
        out_shape=[jax.ShapeDtypeStruct((T, D), F32), jax.ShapeDtypeStruct((T, D), BF),
                   jax.ShapeDtypeStruct((TOP_K, T), jnp.int32), jax.ShapeDtypeStruct((TOP_K, T), F32),
                   jax.ShapeDtypeStruct((T // DISP_BLOCK, N_EXP, 128), F32)],
        compiler_params=_params("parallel"),
        name="merge_route",
    )(x, *mix_args, zg, mods, *w)


def _silu_mul(a, b):
    return a * _sigmoid(a) * b


def _dispatch_kernel(h_ref, pos_ref, gate_ref, xs_ref, gt_ref):
    h = h_ref[...]
    rc = 256
    for r0 in range(0, DISP_ROWS, rc):
        rows = _iota((rc, DISP_BLOCK), 0) + r0
        gt = jnp.zeros((rc, DISP_BLOCK), F32)
        for k in range(TOP_K):
            gt = jnp.where(rows == pos_ref[k:k + 1, :], gate_ref[k:k + 1, :], gt)
        gt_ref[r0:r0 + rc, :] = gt.astype(BF)
        place = jnp.where(gt != 0.0, 1.0, 0.0).astype(BF)
        xs_ref[r0:r0 + rc, :] = _dot(place, h).astype(BF)


def _dispatch(h2, pos8, gate8):
    T = h2.shape[0]
    nblk = T // DISP_BLOCK
    return pl.pallas_call(
        _dispatch_kernel,
        grid=(nblk,),
        in_specs=[pl.BlockSpec((DISP_BLOCK, D), lambda b: (b, 0)),
                  pl.BlockSpec((TOP_K, DISP_BLOCK), lambda b: (0, b)),
                  pl.BlockSpec((TOP_K, DISP_BLOCK), lambda b: (0, b))],
        out_specs=[pl.BlockSpec((DISP_ROWS, D), lambda b: (b, 0)), pl.BlockSpec((DISP_ROWS, DISP_BLOCK), lambda b: (b, 0))],
        out_shape=[jax.ShapeDtypeStruct((nblk * DISP_ROWS, D), BF), jax.ShapeDtypeStruct((nblk * DISP_ROWS, DISP_BLOCK), BF)],
        compiler_params=_params("parallel"),
        name="moe_dispatch",
    )(h2, pos8, gate8)


def _tile_tables(cnt, n_chunks_max, n_tiles_max):
    nblk = cnt.shape[0]
    nch = (cnt + (ROW_GRANULE - 1)) // ROW_GRANULE
    first = jnp.arange(nblk, dtype=jnp.int32)[:, None] * BLOCK_CHUNKS + jnp.cumsum(nch, axis=1) - nch
    tiles_e = (jnp.sum(nch, axis=0) + (CHUNKS_PER_TILE - 1)) // CHUNKS_PER_TILE
    pad_e = tiles_e * CHUNKS_PER_TILE - jnp.sum(nch, axis=0)
    seg_len = jnp.concatenate([nch.T, pad_e[:, None]], axis=1).reshape(-1)
    seg_first = jnp.concatenate([first.T, jnp.full((N_EXP, 1), -1, jnp.int32)], axis=1).reshape(-1)
    seg_start = jnp.cumsum(seg_len) - seg_len
    j = jnp.arange(n_chunks_max, dtype=jnp.int32)
    started = seg_start[None, :] <= j[:, None]
    inc = lambda v: v - jnp.concatenate([jnp.zeros((1,), v.dtype), v[:-1]])
    seg_first_j = jnp.sum(jnp.where(started, inc(seg_first)[None, :], 0), axis=1)
    seg_start_j = jnp.sum(jnp.where(started, inc(seg_start)[None, :], 0), axis=1)
    real = (seg_first_j >= 0) & (j < jnp.sum(seg_len))
    src = jnp.where(real, seg_first_j + (j - seg_start_j), -1).astype(jnp.int32)
    tile_end = jnp.cumsum(tiles_e)
    i = jnp.arange(n_tiles_max, dtype=jnp.int32)
    tile_expert = jnp.minimum(jnp.sum((tile_end[None, :] <= i[:, None]).astype(jnp.int32), axis=1), N_EXP - 1)
    return src, tile_expert.astype(jnp.int32), tile_end[-1:].astype(jnp.int32)


def _expert_kernel(src_ref, texp_ref, nused_ref, xs_hbm, wg_ref, wu_ref, wd_ref, ys_hbm,
                   xbuf, ybuf, wgu_bf, wd_bf, gsem, ssem):
    i = pl.program_id(0)
    n_used = nused_ref[0]
    slot = lax.rem(i, 2)
    g = ROW_GRANULE

    def chunk_copies(tile, slot_, to_buffer, do):
        for c in range(CHUNKS_PER_TILE):
            src = src_ref[tile * CHUNKS_PER_TILE + c]

            @pl.when(src >= 0)
            def _():
                hbm_rows = pl.ds(pl.multiple_of(src * g, g), g)
                if to_buffer:
                    cp = pltpu.make_async_copy(xs_hbm.at[hbm_rows], xbuf.at[slot_, pl.ds(c * g, g)], gsem.at[slot_])
                else:
                    cp = pltpu.make_async_copy(ybuf.at[slot_, pl.ds(c * g, g)], ys_hbm.at[hbm_rows], ssem.at[slot_])
                do(cp)

    start = lambda cp: cp.start()
    wait = lambda cp: cp.wait()

    @pl.when(i == 0)
    def _():
        xbuf[...] = jnp.zeros_like(xbuf)
        chunk_copies(0, 0, True, start)

    @pl.when(i < n_used)
    def _():
        chunk_copies(i, slot, True, wait)

        @pl.when(i + 1 < n_used)
        def _():
            chunk_copies(i + 1, 1 - slot, True, start)

        @pl.when((i == 0) | (texp_ref[i] != texp_ref[jnp.maximum(i - 1, 0)]))
        def _():
            wgu_bf[:, 0:E_DIM] = wg_ref[0, 0].astype(BF)
            wgu_bf[:, E_DIM:2 * E_DIM] = wu_ref[0, 0].astype(BF)
            wd_bf[...] = wd_ref[0, 0].astype(BF)

        gu = _dot(xbuf[slot], wgu_bf[...])
        hid = _silu_mul(gu[:, 0:E_DIM], gu[:, E_DIM:2 * E_DIM])
        ybuf[slot] = _dot(hid.astype(BF), wd_bf[...]).astype(BF)
        chunk_copies(i, slot, False, start)

        @pl.when(i >= 1)
        def _():
            chunk_copies(i - 1, 1 - slot, False, wait)

        @pl.when(i == n_used - 1)
        def _():
            chunk_copies(i, slot, False, wait)


def _experts(layer, xs, tables, w_eg, w_eu, w_ed, n_tiles_max):
    src, tile_expert, n_used = tables
    wmap = lambda i, src_, texp, nu: (layer, texp[i], 0, 0)
    grid_spec = pltpu.PrefetchScalarGridSpec(
        num_scalar_prefetch=3,
        grid=(n_tiles_max,),
        in_specs=[pl.BlockSpec(memory_space=pl.ANY),
                  pl.BlockSpec((1, 1, D, E_DIM), wmap),
                  pl.BlockSpec((1, 1, D, E_DIM), wmap),
                  pl.BlockSpec((1, 1, E_DIM, D), wmap)],
        out_specs=pl.BlockSpec(memory_space=pl.ANY),
        scratch_shapes=[pltpu.VMEM((2, EXP_TILE, D), BF), pltpu.VMEM((2, EXP_TILE, D), BF),
                        pltpu.VMEM((D, 2 * E_DIM), BF), pltpu.VMEM((E_DIM, D), BF),
                        pltpu.SemaphoreType.DMA((2,)), pltpu.SemaphoreType.DMA((2,))],
    )
    return pl.pallas_call(
        _expert_kernel,
        grid_spec=grid_spec,
        out_shape=jax.ShapeDtypeStruct(xs.shape, xs.dtype),
        input_output_aliases={3: 0},
        compiler_params=_params("arbitrary"),
        name="moe_experts",
    )(src, tile_expert, n_used, xs, w_eg, w_eu, w_ed)


def _combine_kernel(final, ys_ref, gt_ref, h_ref, sg_ref, su_ref, sd_ref, x_ref, m_ref, gf_ref, o_ref):
    routed = _dot_tn(gt_ref[...], ys_ref[...])
    h = h_ref[...]
    sh = _silu_mul(_dot(h, sg_ref[0].astype(BF)), _dot(h, su_ref[0].astype(BF)))
    out = x_ref[...] + m_ref[0] * (routed + _dot(sh.astype(BF), sd_ref[0].astype(BF)))
    if final:
        out = _rms(out, gf_ref[...])
    o_ref[...] = out


def _combine(layer, final, ys, gt, h2, w_sg, w_su, w_sd, xm, mods, g_final, n_ctx_blocks, blocks_per_lat):
    T = h2.shape[0]
    tb = DISP_BLOCK
    row = functools.partial(_mod_row, n_ctx_blocks, blocks_per_lat)
    return pl.pallas_call(
        functools.partial(_combine_kernel, final),
        grid=(T // tb,),
        in_specs=[pl.BlockSpec((DISP_ROWS, D), lambda b: (b, 0)),
                  pl.BlockSpec((DISP_ROWS, tb), lambda b: (b, 0)),
                  pl.BlockSpec((tb, D), lambda b: (b, 0)),
                  pl.BlockSpec((1, D, E_DIM), lambda b: (layer, 0, 0)),
                  pl.BlockSpec((1, D, E_DIM), lambda b: (layer, 0, 0)),
                  pl.BlockSpec((1, E_DIM, D), lambda b: (layer, 0, 0)),
                  pl.BlockSpec((tb, D), lambda b: (b, 0)),
                  pl.BlockSpec((1, 1, D), lambda b: (row(b), 0, 5)),
                  pl.BlockSpec((1, D), lambda b: (0, 0))],
        out_specs=pl.BlockSpec((tb, D), lambda b: (b, 0)),
        out_shape=jax.ShapeDtypeStruct((T, D), F32),
        compiler_params=_params("parallel"),
        name="moe_combine",
    )(ys, gt, h2, w_sg, w_su, w_sd, xm, mods, g_final)


def _moe(layer, final, h2, pos8, gate8, cnt, w_eg, w_eu, w_ed, w_sg, w_su, w_sd, xm, mods, g_final, disp_blocks):
    T = h2.shape[0]
    nblk = T // DISP_BLOCK
    n_chunks_max = (TOP_K * T + N_EXP * nblk * (ROW_GRANULE - 1)) // ROW_GRANULE + N_EXP * (CHUNKS_PER_TILE - 1)
    n_tiles_max = -(-n_chunks_max // CHUNKS_PER_TILE)
    xs, gt = _dispatch(h2, pos8, gate8)
    tables = _tile_tables(cnt[:, :, 0].astype(jnp.int32), n_tiles_max * CHUNKS_PER_TILE, n_tiles_max)
    ys = _experts(layer, xs, tables, w_eg, w_eu, w_ed, n_tiles_max)
    return _combine(layer, final, ys, gt, h2, w_sg, w_su, w_sd, xm, mods, g_final, *disp_blocks)


def _reorder_w_in(w_in):
    kr = w_in[:, :, 2592:2624]
    kr_sw = kr.reshape(kr.shape[0], D, M_ROPE // 2, 2)[..., ::-1].reshape(kr.shape)
    pad = jnp.zeros(kr.shape, w_in.dtype)
    out = jnp.concatenate([w_in[:, :, :1920], w_in[:, :, 1952:2592], w_in[:, :, 1920:1952], kr, kr_sw, pad,
                           w_in[:, :, 2624:]], axis=2)
    return out.astype(BF)


def _reorder_w_q(w_q_up, with_swap):
    L = w_q_up.shape[0]
    w = w_q_up.reshape(L, M_QL, MH, M_NOPE + M_ROPE)
    nope = w[..., :M_NOPE].reshape(L, M_QL, MH * M_NOPE)
    rope = w[..., M_NOPE:]
    parts = [nope, rope.reshape(L, M_QL, MH * M_ROPE)]
    if with_swap:
        parts.append(rope.reshape(L, M_QL, MH, M_ROPE // 2, 2)[..., ::-1].reshape(L, M_QL, MH * M_ROPE))
    return jnp.concatenate(parts, axis=2).astype(BF)


def _reorder_w_kv(w_kv_up):
    L = w_kv_up.shape[0]
    w = w_kv_up.reshape(L, M_KVL, MH, M_NOPE + M_V)
    return jnp.concatenate([w[..., :M_NOPE].reshape(L, M_KVL, MH * M_NOPE),
                            w[..., M_NOPE:].reshape(L, M_KVL, MH * M_V)], axis=2).astype(BF)


def kernel(x_prompt, x_sample, state_gla_fwd, state_gla_bwd, cache_mla_ckv, cache_mla_krope, c, c_ctx, w_mod, b_mod, g_norm1, g_norm2, w_in, w_gla_gate_f, b_gla_gate_f, w_gla_gate_b, b_gla_gate_b, g_gla_out, g_q_a, w_q_up, g_kv_a, w_kv_up, w_br_fourier, w_br_gla, w_br_mla, w_out, w_router, b_router, w_exp_gate, w_exp_up, w_exp_down, w_sh_gate, w_sh_up, w_sh_down, g_final):
    nb, sl, _ = x_prompt.shape
    db, dl, _ = x_sample.shape
    L = w_mod.shape[0]
    t_ctx, t_lat = nb * sl, db * dl
    T = t_ctx + t_lat
    assert sl % G_CHUNK == 0 and dl % G_CHUNK == 0 and dl % GRID_W == 0
    assert t_ctx % dl == 0 and dl % TOKEN_BLOCK == 0 and TOKEN_BLOCK % DISP_BLOCK == 0
    assert t_ctx % TOKEN_BLOCK == 0 and dl % Q_BLOCK == 0 and 1 + db <= 8

    x = jnp.concatenate([x_prompt.reshape(t_ctx, D), x_sample.reshape(t_lat, D)], axis=0)
    cond = jnp.concatenate([c_ctx[None, :], c, jnp.zeros((7 - db, D), F32)], axis=0)
    mods_all = _modulation(cond, w_mod, b_mod)

    w_in_r = _reorder_w_in(w_in)
    wq_ctx = _reorder_w_q(w_q_up, False)
    wq_lat = _reorder_w_q(w_q_up, True)
    wkv_r = _reorder_w_kv(w_kv_up)
    lat_off = t_ctx // dl

    new_f, new_b, new_ckv, new_kr = [], [], [], []
    for l in range(L):
        mods = mods_all[l].reshape(8, 1, 6 * D)
        tok = (t_ctx // TOKEN_BLOCK, dl // TOKEN_BLOCK)
        zf, zqk, zvr, zmla, zsm, zg = _in_projection(x, mods, g_norm1[l][None, :], w_in_r[l], *tok)

        (f_c,) = _fourier(zf, sl, nb, 0)
        (f_l,) = _fourier(zf, dl, db, lat_off)

        gate_w = (w_gla_gate_f[l], b_gla_gate_f[l][None, :], w_gla_gate_b[l], b_gla_gate_b[l][None, :],
                  g_gla_out[l].reshape(1, GH * GDV))
        og_c, s_f, s_b = _gla(zqk, zvr, zsm, gate_w, sl, nb, 0, None)
        og_l, _, _ = _gla(zqk, zvr, zsm, gate_w, dl, db, lat_off, (state_gla_fwd[:, l], state_gla_bwd[:, l]))

        gq, gkv = g_q_a[l][None, :], g_kv_a[l][None, :]
        om_c, ckv = _mla(zmla, zsm, (gq, wq_ctx[l], gkv, wkv_r[l]), sl, nb, 0, None)
        (om_l,) = _mla(zmla, zsm, (gq, wq_lat[l], gkv, wkv_r[l]), dl, db, lat_off,
                       (cache_mla_ckv[:, l], cache_mla_krope[:, l]))

        mw = (w_br_fourier[l].astype(BF), w_br_gla[l].astype(BF), w_br_mla[l].astype(BF), w_out[l].astype(BF),
              g_norm2[l][None, :], w_router[l].T, b_router[l][:, None])
        xm, h2, pos8, gate8, cnt = _merge(x, (f_c, og_c, om_c), (f_l, og_l, om_l), zg, mods, mw, *tok)

        x = _moe(l, l == L - 1, h2, pos8, gate8, cnt, w_exp_gate, w_exp_up, w_exp_down, w_sh_gate, w_sh_up, w_sh_down,
                 xm, mods, g_final[None, :], (t_ctx // DISP_BLOCK, dl // DISP_BLOCK))

        new_f.append(s_f)
        new_b.append(s_b)
        new_ckv.append(ckv.reshape(nb, sl, M_KVL))
        new_kr.append(zsm[:t_ctx, 32:64].reshape(nb, sl, M_ROPE))

    y_prompt = x[:t_ctx].reshape(nb, sl, D)
    y_sample = x[t_ctx:].reshape(db, dl, D)
    return (y_prompt, y_sample, jnp.stack(new_f, axis=1), jnp.stack(new_b, axis=1),
            jnp.stack(new_ckv, axis=1), jnp.stack(new_kr, axis=1))
```

```python
import functools

import numpy as np
import jax
import jax.numpy as jnp
from jax import lax
from jax.experimental import pallas as pl
from jax.experimental.pallas import tpu as pltpu

F32 = jnp.float32
BF = jnp.bfloat16

D = 1024
GRID_W = 64
FN_G, FN_GW = 4, 96
FN_W = FN_G * FN_GW
GH, GDK, GDV = 4, 64, 128
G_RANK = 16
G_TAU = 16.0
G_CHUNK = 64
MH, M_NOPE, M_ROPE, M_V = 8, 64, 32, 64
M_QL, M_KVL = 384, 256
ROPE_BASE = 10000.0
N_EXP, TOP_K, N_GRP, TOPK_GRP = 64, 8, 8, 4
E_DIM = 256
ROUTED_SCALE = 2.5
EPS = 1e-6

C_F = (0, 384)
C_QK = (384, 896)
C_VR = (896, 1920)
C_MLA = (1920, 2560)
C_SM = (2560, 2688)
C_G = (2688, 5760)
IN_COLS_R = 5760

VMEM_LIMIT_V7X = 56 * 1024 * 1024
TOKEN_BLOCK = 512
Q_BLOCK = 256
DISP_BLOCK = 256
ROW_GRANULE = 16
DISP_ROWS = -(-(TOP_K * DISP_BLOCK + N_EXP * (ROW_GRANULE - 1)) // 256) * 256
EXP_TILE = 512
CHUNKS_PER_TILE = EXP_TILE // ROW_GRANULE
BLOCK_CHUNKS = DISP_ROWS // ROW_GRANULE


def _params(*sem):
    return pltpu.CompilerParams(dimension_semantics=sem, vmem_limit_bytes=VMEM_LIMIT_V7X)


def _dot(a, b):
    return jnp.dot(a, b, preferred_element_type=F32)


def _dot_hi(a, b):
    return jnp.dot(a, b, precision=lax.Precision.HIGHEST, preferred_element_type=F32)


def _dot_nt(a, b, precision=None):
    return lax.dot_general(a, b, (((1,), (1,)), ((), ())), precision=precision, preferred_element_type=F32)


def _dot_tn(a, b):
    return lax.dot_general(a, b, (((0,), (0,)), ((), ())), preferred_element_type=F32)


def _sigmoid(x):
    return 1.0 / (1.0 + jnp.exp(-x))


def _rms(x, g):
    return x * lax.rsqrt(jnp.mean(x * x, axis=-1, keepdims=True) + EPS) * g


def _iota(shape, dim):
    return lax.broadcasted_iota(jnp.int32, shape, dim)


def _mod_row(n_ctx_blocks, blocks_per_lat, i):
    return jnp.where(i < n_ctx_blocks, 0, 1 + (i - n_ctx_blocks) // blocks_per_lat)


def _mod_kernel(c_ref, w_ref, b_ref, o_ref):
    c = c_ref[...]
    o_ref[0] = _dot_hi(c * _sigmoid(c), w_ref[0]) + b_ref[0]


def _modulation(cond, w_mod, b_mod):
    L = w_mod.shape[0]
    rows = cond.shape[0]
    tn = 1536
    return pl.pallas_call(
        _mod_kernel,
        grid=(L, 6 * D // tn),
        in_specs=[pl.BlockSpec((rows, D), lambda l, j: (0, 0)),
                  pl.BlockSpec((1, D, tn), lambda l, j: (l, 0, j)),
                  pl.BlockSpec((1, 1, tn), lambda l, j: (l, 0, j))],
        out_specs=pl.BlockSpec((1, rows, tn), lambda l, j: (l, 0, j)),
        out_shape=jax.ShapeDtypeStruct((L, rows, 6 * D), F32),
        compiler_params=_params("parallel", "parallel"),
        name="modulation",
    )(cond, w_mod, b_mod.reshape(L, 1, 6 * D))


def _inproj_kernel(x_ref, m_ref, g_ref, w_ref, of_ref, oqk_ref, ovr_ref, omla_ref, osm_ref, og_ref):
    y = _rms(x_ref[...], g_ref[...])
    h = (y * (1.0 + m_ref[0, :, D:2 * D]) + m_ref[0, :, 0:D]).astype(BF)
    of_ref[...] = _dot(h, w_ref[:, C_F[0]:C_F[1]]).astype(BF)
    oqk_ref[...] = _dot(h, w_ref[:, C_QK[0]:C_QK[1]])
    ovr_ref[...] = _dot(h, w_ref[:, C_VR[0]:C_VR[1]]).astype(BF)
    omla_ref[...] = _dot(h, w_ref[:, C_MLA[0]:C_MLA[1]])
    osm_ref[...] = _dot(h, w_ref[:, C_SM[0]:C_SM[1]])
    og_ref[...] = _dot(h, w_ref[:, C_G[0]:C_G[1]]).astype(BF)


def _in_projection(x, mods, g1, w_in_r, n_ctx_blocks, blocks_per_lat):
    T = x.shape[0]
    tb = TOKEN_BLOCK
    row = functools.partial(_mod_row, n_ctx_blocks, blocks_per_lat)
    widths = [(C_F, BF), (C_QK, F32), (C_VR, BF), (C_MLA, F32), (C_SM, F32), (C_G, BF)]
    return pl.pallas_call(
        _inproj_kernel,
        grid=(T // tb,),
        in_specs=[pl.BlockSpec((tb, D), lambda i: (i, 0)),
                  pl.BlockSpec((1, 1, 2 * D), lambda i: (row(i), 0, 0)),
                  pl.BlockSpec((1, D), lambda i: (0, 0)),
                  pl.BlockSpec((D, IN_COLS_R), lambda i: (0, 0))],
        out_specs=[pl.BlockSpec((tb, c[1] - c[0]), lambda i: (i, 0)) for c, _ in widths],
        out_shape=[jax.ShapeDtypeStruct((T, c[1] - c[0]), dt) for c, dt in widths],
        compiler_params=_params("parallel"),
        name="in_projection",
    )(x, mods, g1, w_in_r)


def _seq_call(kernel, name, n, nseq, blk_off, seq_ins, const_ins, out_widths, extra_outs=(), scratch=()):
    in_specs = [pl.BlockSpec((n, a.shape[1]), lambda i: (i + blk_off, 0)) for a in seq_ins]
    in_specs += [pl.BlockSpec(bs, im) for _, bs, im in const_ins]
    args = list(seq_ins) + [a for a, _, _ in const_ins]
    out_specs = [pl.BlockSpec((n, w), lambda i: (i, 0)) for w, _ in out_widths]
    out_shape = [jax.ShapeDtypeStruct((nseq * n, w), dt) for w, dt in out_widths]
    out_specs += [pl.BlockSpec(bs, im) for _, _, bs, im in extra_outs]
    out_shape += [jax.ShapeDtypeStruct(s, dt) for s, dt, _, _ in extra_outs]
    return pl.pallas_call(
        kernel, grid=(nseq,), in_specs=in_specs, out_specs=out_specs, out_shape=out_shape,
        scratch_shapes=list(scratch), compiler_params=_params("parallel"), name=name,
    )(*args)


def _fourier_kernel(u_ref, r_ref, lc_ref, ls_ref, o_ref):
    y = _dot(u_ref[...], r_ref[...])
    o_ref[...] = (_dot(lc_ref[...], y[:, :FN_W].astype(BF)) + _dot(ls_ref[...], y[:, FN_W:].astype(BF))).astype(BF)


def _dft_tables(n):
    k = np.arange(FN_GW)
    ang = 2.0 * np.pi * ((k[:, None] * k[None, :]) % FN_GW) / FN_GW
    eye = np.eye(FN_G)
    right = np.concatenate([np.kron(eye, np.cos(ang)), np.kron(eye, np.sin(ang))], axis=1)
    p = np.arange(n)
    angn = 2.0 * np.pi * ((p[:, None] * p[None, :]) % n) / n
    scale = 1.0 / np.sqrt(float(n * FN_GW))
    return (jnp.asarray(right, F32).astype(BF), jnp.asarray(np.cos(angn) * scale, F32).astype(BF),
            jnp.asarray(-np.sin(angn) * scale, F32).astype(BF))


def _fourier(zf, n, nseq, blk_off):
    right, lc, ls = _dft_tables(n)
    consts = [(right, (FN_W, 2 * FN_W), lambda i: (0, 0)), (lc, (n, n), lambda i: (0, 0)), (ls, (n, n), lambda i: (0, 0))]
    return _seq_call(_fourier_kernel, "fourier_mix", n, nseq, blk_off, [zf], consts, [(FN_W, BF)])


def _log_gate(z, w_ref, b_ref):
    pre = _dot_hi(z, w_ref[...]) + b_ref[...]
    return (jnp.minimum(pre, 0.0) - jnp.log1p(jnp.exp(-jnp.abs(pre)))) * (1.0 / G_TAU)


def _gla_kernel(has_state, n, *refs):
    if has_state:
        (zqk, zvr, zsm, wgf, bgf, wgb, bgb, gout, s0f, s0b, o_ref, sf_ref, sb_ref, oacc, lg, st) = refs
    else:
        (zqk, zvr, zsm, wgf, bgf, wgb, bgb, gout, o_ref, sf_ref, sb_ref, oacc, lg, st) = refs
        s0f = s0b = None
    C = G_CHUNK
    nc = n // C
    ri, ci = _iota((C, C), 0), _iota((C, C), 1)
    t_idx, lane = _iota((C, 128), 0), _iota((C, 128), 1)
    s_idx = lane & (C - 1)
    left = lane < GDK
    vleft = _iota((C, 2 * GDV), 1) < GDV
    blockdiag = (_iota((2 * GDV, 2 * GDK), 0) >> 7) == (_iota((2 * GDV, 2 * GDK), 1) >> 6)

    def load_state(s0_ref):
        for p in range(2):
            if s0_ref is None:
                st[p] = jnp.zeros((2 * GDV, 2 * GDK), F32)
            else:
                z = jnp.zeros((GDK, GDV), F32)
                blk = jnp.concatenate([jnp.concatenate([s0_ref[0, 2 * p], z], axis=1),
                                       jnp.concatenate([z, s0_ref[0, 2 * p + 1]], axis=1)], axis=0)
                st[p] = blk.T

    def store_state(out_ref):
        for p in range(2):
            blk = st[p].T
            out_ref[0, 2 * p] = blk[0:GDK, 0:GDV]
            out_ref[0, 2 * p + 1] = blk[GDK:2 * GDK, GDV:2 * GDV]

    def run(reverse):
        tri = (ci >= ri).astype(F32) if reverse else (ci <= ri).astype(F32)
        mask = (s_idx >= t_idx) if reverse else (s_idx <= t_idx)

        def chunk(step, carry):
            c = (nc - 1 - step) if reverse else step
            rows = pl.ds(pl.multiple_of(c * C, C), C)
            cum = _dot_hi(tri, lg[rows, :])
            tot = cum[0:1] if reverse else cum[C - 1:C]
            q = zqk[rows, 0:GH * GDK] * (GDK ** -0.5)
            k = zqk[rows, GH * GDK:2 * GH * GDK]
            qh = (q * jnp.exp(cum)).astype(BF)
            kh = (k * jnp.exp(-cum)).astype(BF)
            kb = (k * jnp.exp(tot - cum)).astype(BF)
            dec = jnp.exp(tot)
            for p in range(2):
                ls = slice(128 * p, 128 * p + 128)
                vs = slice(256 * p, 256 * p + 256)
                qp, kp = qh[:, ls], kh[:, ls]
                zk = jnp.zeros_like(kp)
                kblk = jnp.concatenate([jnp.where(left, kp, zk), jnp.where(left, zk, kp)], axis=0)
                sc = jnp.where(mask, _dot_nt(qp, kblk), 0.0).astype(BF)
                vp = zvr[rows, vs]
                zv = jnp.zeros_like(vp)
                vblk = jnp.concatenate([jnp.where(vleft, vp, zv), jnp.where(vleft, zv, vp)], axis=0)
                stp = st[p]
                o_p = _dot(sc, vblk) + _dot_nt(qp, stp.astype(BF))
                st[p] = dec[:, ls] * stp + jnp.where(blockdiag, _dot_tn(vp, kb[:, ls]), 0.0)
                if not reverse:
                    oacc[rows, vs] = o_p
                else:
                    o_t = oacc[rows, vs] + o_p
                    for hh in range(2):
                        hs = slice(256 * p + 128 * hh, 256 * p + 128 * hh + 128)
                        oh = o_t[:, 128 * hh:128 * hh + 128]
                        oh = oh * lax.rsqrt(jnp.mean(oh * oh, axis=-1, keepdims=True) + EPS) * gout[:, hs]
                        r = zvr[rows, GH * GDV + hs.start:GH * GDV + hs.stop].astype(F32)
                        o_ref[rows, hs] = (oh * (r * _sigmoid(r))).astype(BF)
            return carry

        lax.fori_loop(0, nc, chunk, 0)

    lg[...] = _log_gate(zsm[:, 0:G_RANK], wgf, bgf)
    load_state(s0f)
    run(False)
    store_state(sf_ref)
    lg[...] = _log_gate(zsm[:, G_RANK:2 * G_RANK], wgb, bgb)
    load_state(s0b)
    run(True)
    store_state(sb_ref)


def _gla(zqk, zvr, zsm, gate_w, n, nseq, blk_off, states):
    wgf, bgf, wgb, bgb, gout = gate_w
    c2 = lambda i: (0, 0)
    consts = [(wgf, wgf.shape, c2), (bgf, bgf.shape, c2), (wgb, wgb.shape, c2), (bgb, bgb.shape, c2), (gout, gout.shape, c2)]
    st_blk = (1, GH, GDK, GDV)
    st_map = lambda i: (i, 0, 0, 0)
    if states is not None:
        consts += [(s, st_blk, st_map) for s in states]
    extra = [((nseq, GH, GDK, GDV), F32, st_blk, st_map)] * 2
    scratch = [pltpu.VMEM((n, GH * GDV), F32), pltpu.VMEM((n, GH * GDK), F32), pltpu.VMEM((2, 2 * GDV, 2 * GDK), F32)]
    return _seq_call(functools.partial(_gla_kernel, states is not None, n), "gla_mixer", n, nseq, blk_off,
                     [zqk, zvr, zsm], consts, [(GH * GDV, BF)], extra_outs=extra, scratch=scratch)


def _mla_kernel(latent, n, past, *refs):
    if latent:
        (zmla, zsm, gq, wq, gkv, wkv, cckv, ckr, cosq, sinq, cosk, sink, o_ref, qs, kns, vs, krs) = refs
    else:
        (zmla, zsm, gq, wq, gkv, wkv, o_ref, ckv_ref, qs, kns, vs, krs) = refs
    sk = past + n
    scale = (M_NOPE + M_ROPE) ** -0.5
    nw, rw = MH * M_NOPE, MH * M_ROPE
    qa = _dot(_rms(zmla[:, 0:M_QL], gq[...]).astype(BF), wq[...])
    qr = qa[:, nw:nw + rw]
    if latent:
        qr = qr * cosq[...] + qa[:, nw + rw:nw + 2 * rw] * sinq[...]
    qs[:, 0:nw] = qa[:, 0:nw] * scale
    qs[:, nw:nw + rw] = qr * scale
    ckv = _rms(zmla[:, M_QL:M_QL + M_KVL], gkv[...])
    kv = _dot(ckv.astype(BF), wkv[...])
    kr = zsm[:, 32:64]
    if latent:
        kr = kr * cosk[...] + zsm[:, 64:96] * sink[...]
        kvc = _dot(cckv[0].astype(BF), wkv[...])
        kns[0:past, :] = kvc[:, 0:nw].astype(BF)
        vs[0:past, :] = kvc[:, nw:].astype(BF)
        krs[0:past, :] = jnp.concatenate([ckr[0]] * 4, axis=1).astype(BF)
    else:
        ckv_ref[...] = ckv
    kns[past:sk, :] = kv[:, 0:nw].astype(BF)
    vs[past:sk, :] = kv[:, nw:].astype(BF)
    krs[past:sk, :] = jnp.concatenate([kr] * 4, axis=1).astype(BF)

    qb = min(Q_BLOCK, n)
    lane = _iota((qb, 128), 1)

    def block(step, carry):
        rows = pl.ds(pl.multiple_of(step * qb, qb), qb)
        for p in range(MH // 2):
            ls = slice(128 * p, 128 * p + 128)
            qn = qs[rows, ls]
            quad = (2 * p) // 4
            qrp = qs[rows, nw + 128 * quad:nw + 128 * quad + 128]
            rhs = jnp.concatenate([kns[:, ls], krs[...]], axis=1)
            vp = vs[:, ls]
            o_pair = None
            for hh in range(2):
                j = (2 * p + hh) % 4
                qn_m = jnp.where((lane >> 6) == hh, qn, 0.0).astype(BF)
                qr_m = jnp.where((lane >> 5) == j, qrp, 0.0).astype(BF)
                s = _dot_nt(jnp.concatenate([qn_m, qr_m], axis=1), rhs)
                e = jnp.exp(s - jnp.max(s, axis=-1, keepdims=True))
                pv = _dot(e.astype(BF), vp) / jnp.sum(e, axis=-1, keepdims=True)
                o_pair = pv if hh == 0 else jnp.where(lane < M_V, o_pair, pv)
            o_ref[rows, ls] = o_pair.astype(BF)
        return carry

    lax.fori_loop(0, n // qb, block, 0)


def _rope_tables(n):
    half = M_ROPE // 2
    pos = jnp.arange(n)
    row = (pos // GRID_W).astype(F32)
    col = (pos % GRID_W).astype(F32)
    inv = ROPE_BASE ** (-jnp.arange(0, half, 2, dtype=F32) / half)
    ang = jnp.concatenate([row[:, None] * inv, col[:, None] * inv], axis=-1)
    cos = jnp.repeat(jnp.cos(ang), 2, axis=-1)
    sin = jnp.repeat(jnp.sin(ang), 2, axis=-1) * jnp.tile(jnp.asarray([-1.0, 1.0], F32), half)
    return jnp.tile(cos, (1, MH)), jnp.tile(sin, (1, MH)), cos, sin


def _mla(zmla, zsm, w, n, nseq, blk_off, cache):
    gq, wq, gkv, wkv = w
    c2 = lambda i: (0, 0)
    consts = [(gq, gq.shape, c2), (wq, wq.shape, c2), (gkv, gkv.shape, c2), (wkv, wkv.shape, c2)]
    past = 0
    extra = []
    if cache is not None:
        cckv, ckr = cache
        past = cckv.shape[1]
        c3 = lambda i: (i, 0, 0)
        consts += [(cckv, (1, past, M_KVL), c3), (ckr, (1, past, M_ROPE), c3)]
        consts += [(t, t.shape, c2) for t in _rope_tables(n)]
    else:
        extra = [((nseq * n, M_KVL), F32, (n, M_KVL), lambda i: (i, 0))]
    sk = past + n
    scratch = [pltpu.VMEM((n, MH * (M_NOPE + M_ROPE)), F32), pltpu.VMEM((sk, MH * M_NOPE), BF),
               pltpu.VMEM((sk, MH * M_V), BF), pltpu.VMEM((sk, 128), BF)]
    return _seq_call(functools.partial(_mla_kernel, cache is not None, n, past), "mla_mixer", n, nseq, blk_off,
                     [zmla, zsm], consts, [(MH * M_V, BF)], extra_outs=extra, scratch=scratch)


def _route(logits_t, bias):
    nt = logits_t.shape[1]
    gsz = N_EXP // N_GRP
    scores = _sigmoid(logits_t)
    sel = scores + bias
    neg = -jnp.inf
    sub = _iota((gsz, nt), 0)
    tops = []
    for g in range(N_GRP):
        blk = sel[gsz * g:gsz * g + gsz]
        m1 = jnp.max(blk, axis=0, keepdims=True)
        first = jnp.min(jnp.where(blk == m1, sub, gsz), axis=0, keepdims=True)
        m2 = jnp.max(jnp.where(sub == first, neg, blk), axis=0, keepdims=True)
        tops.append(m1 + m2)
    gs = jnp.concatenate(tops, axis=0)
    gidx = _iota((N_GRP, nt), 0)
    grank = jnp.zeros((N_GRP, nt), jnp.int32)
    for j in range(N_GRP):
        rj = gs[j:j + 1]
        grank += ((rj > gs) | ((rj == gs) & (gidx > j))).astype(jnp.int32)
    keep = grank < TOPK_GRP
    masked = jnp.concatenate(
        [jnp.where(jnp.broadcast_to(keep[g:g + 1], (gsz, nt)), sel[gsz * g:gsz * g + gsz], neg) for g in range(N_GRP)], axis=0)
    eidx = _iota((N_EXP, nt), 0)
    rank = jnp.zeros((N_EXP, nt), jnp.int32)
    for j in range(N_EXP):
        rj = masked[j:j + 1]
        rank += ((rj > masked) | ((rj == masked) & (eidx > j))).astype(jnp.int32)
    chosen = rank < TOP_K
    w = jnp.where(chosen, scores, 0.0)
    return chosen, w / jnp.sum(w, axis=0, keepdims=True) * ROUTED_SCALE


def _dispatch_meta(chosen, gates_t):
    tb = chosen.shape[1]
    sel = chosen.astype(F32)
    selb = sel.astype(BF)
    earlier = (_iota((tb, tb), 0) < _iota((tb, tb), 1)).astype(BF)
    rank = _dot(selb, earlier)
    cnt = jnp.sum(sel, axis=1, keepdims=True)
    padded = jnp.floor((cnt + (ROW_GRANULE - 1)) * (1.0 / ROW_GRANULE)) * ROW_GRANULE
    below = (_iota((N_EXP, N_EXP), 1) < _iota((N_EXP, N_EXP), 0)).astype(BF)
    start = _dot(below, jnp.broadcast_to(padded, (N_EXP, 128)).astype(BF))[:, 0:1]
    pos = start + rank
    kidx = _dot(below, selb)
    pos8, gate8 = [], []
    for k in range(TOP_K):
        hit = chosen & (kidx == float(k))
        pos8.append(jnp.sum(jnp.where(hit, pos, 0.0), axis=0, keepdims=True))
        gate8.append(jnp.sum(jnp.where(hit, gates_t, 0.0), axis=0, keepdims=True))
    return (jnp.concatenate(pos8, axis=0).astype(jnp.int32), jnp.concatenate(gate8, axis=0), cnt)


def _merge_kernel(n_ctx_blocks, x_ref, fc_ref, fl_ref, ogc_ref, ogl_ref, omc_ref, oml_ref, zg_ref, m_ref,
                  wbf, wbg, wbm, wout, gn2, wrt, brt, xm_ref, h2_ref, pos_ref, gate_ref, cnt_ref):
    is_ctx = pl.program_id(0) < n_ctx_blocks
    ya = _dot(jnp.where(is_ctx, fc_ref[...], fl_ref[...]), wbf[...])
    yb = _dot(jnp.where(is_ctx, ogc_ref[...], ogl_ref[...]), wbg[...])
    yc = _dot(jnp.where(is_ctx, omc_ref[...], oml_ref[...]), wbm[...])
    merged = (_sigmoid(zg_ref[:, 0:D].astype(F32)) * ya + _sigmoid(zg_ref[:, D:2 * D].astype(F32)) * yb
              + _sigmoid(zg_ref[:, 2 * D:3 * D].astype(F32)) * yc)
    xm = x_ref[...] + m_ref[0, :, 2 * D:3 * D] * _dot(merged.astype(BF), wout[...])
    xm_ref[...] = xm
    h2 = _rms(xm, gn2[...]) * (1.0 + m_ref[0, :, 4 * D:5 * D]) + m_ref[0, :, 3 * D:4 * D]
    h2_ref[...] = h2.astype(BF)
    chosen, gates_t = _route(_dot_nt(wrt[...], h2, precision=lax.Precision.HIGHEST), brt[...])
    for sb in range(gates_t.shape[1] // DISP_BLOCK):
        ls = slice(sb * DISP_BLOCK, (sb + 1) * DISP_BLOCK)
        pos8, gate8, cnt = _dispatch_meta(chosen[:, ls], gates_t[:, ls])
        pos_ref[:, ls] = pos8
        gate_ref[:, ls] = gate8
        cnt_ref[sb] = jnp.broadcast_to(cnt, (N_EXP, 128))


def _merge(x, mix_ctx, mix_lat, zg, mods, w, n_ctx_blocks, blocks_per_lat):
    T = x.shape[0]
    tb = TOKEN_BLOCK
    row = functools.partial(_mod_row, n_ctx_blocks, blocks_per_lat)
    rb = lambda wd: pl.BlockSpec((tb, wd), lambda i: (i, 0))
    cb = lambda a: pl.BlockSpec(a.shape, lambda i: (0, 0))
    ctx_b = lambda wd: pl.BlockSpec((tb, wd), lambda i: (jnp.minimum(i, n_ctx_blocks - 1), 0))
    lat_b = lambda wd: pl.BlockSpec((tb, wd), lambda i: (jnp.maximum(i - n_ctx_blocks, 0), 0))
    mix_specs, mix_args = [], []
    for a_c, a_l in zip(mix_ctx, mix_lat):
        mix_specs += [ctx_b(a_c.shape[1]), lat_b(a_l.shape[1])]
        mix_args += [a_c, a_l]
    return pl.pallas_call(
        functools.partial(_merge_kernel, n_ctx_blocks),
        grid=(T // tb,),
        in_specs=[rb(D)] + mix_specs + [rb(3 * D),
                  pl.BlockSpec((1, 1, 6 * D), lambda i: (row(i), 0, 0))] + [cb(a) for a in w],
        out_specs=[rb(D), rb(D), pl.BlockSpec((TOP_K, tb), lambda i: (0, i)), pl.BlockSpec((TOP_K, tb), lambda i: (0, i)),
                   pl.BlockSpec((tb // DISP_BLOCK, N_EXP, 128), lambda i: (i, 0, 0))],
        out_shape=[jax.ShapeDtypeStruct((T, D), F32), jax.ShapeDtypeStruct((T, D), BF),
                   jax.ShapeDtypeStruct((TOP_K, T), jnp.int32), jax.ShapeDtypeStruct((TOP_K, T), F32),
                   jax.ShapeDtypeStruct((T // DISP_BLOCK, N_EXP, 128), F32)],
        compiler_params=_params("parallel"),
        name="merge_route",
    )(x, *mix_args, zg, mods, *w)


def _silu_mul(a, b):
    return a * _sigmoid(a) * b


def _dispatch_kernel(n_blocks, h_ref, pos_ref, gate_ref, xs_ref, gt_ref):
    h = h_ref[...]
    live = (pl.program_id(0) < n_blocks).astype(F32)
    rc = 256
    for r0 in range(0, DISP_ROWS, rc):
        rows = _iota((rc, DISP_BLOCK), 0) + r0
        gt = jnp.zeros((rc, DISP_BLOCK), F32)
        for k in range(TOP_K):
            gt = jnp.where(rows == pos_ref[k:k + 1, :], gate_ref[k:k + 1, :] * live, gt)
        gt_ref[r0:r0 + rc, :] = gt.astype(BF)
        place = jnp.where(gt != 0.0, 1.0, 0.0).astype(BF)
        xs_ref[r0:r0 + rc, :] = _dot(place, h).astype(BF)


def _dispatch(h2, pos8, gate8):
    T = h2.shape[0]
    nblk = T // DISP_BLOCK
    last = lambda b: jnp.minimum(b, nblk - 1)
    return pl.pallas_call(
        functools.partial(_dispatch_kernel, nblk),
        grid=(nblk + 1,),
        in_specs=[pl.BlockSpec((DISP_BLOCK, D), lambda b: (last(b), 0)),
                  pl.BlockSpec((TOP_K, DISP_BLOCK), lambda b: (0, last(b))),
                  pl.BlockSpec((TOP_K, DISP_BLOCK), lambda b: (0, last(b)))],
        out_specs=[pl.BlockSpec((DISP_ROWS, D), lambda b: (b, 0)), pl.BlockSpec((DISP_ROWS, DISP_BLOCK), lambda b: (b, 0))],
        out_shape=[jax.ShapeDtypeStruct(((nblk + 1) * DISP_ROWS, D), BF),
                   jax.ShapeDtypeStruct(((nblk + 1) * DISP_ROWS, DISP_BLOCK), BF)],
        compiler_params=_params("parallel"),
        name="moe_dispatch",
    )(h2, pos8, gate8)


def _tile_tables(cnt, n_chunks_max, n_tiles_max):
    nblk = cnt.shape[0]
    nch = (cnt + (ROW_GRANULE - 1)) // ROW_GRANULE
    first = jnp.arange(nblk, dtype=jnp.int32)[:, None] * BLOCK_CHUNKS + jnp.cumsum(nch, axis=1) - nch
    tiles_e = (jnp.sum(nch, axis=0) + (CHUNKS_PER_TILE - 1)) // CHUNKS_PER_TILE
    pad_e = tiles_e * CHUNKS_PER_TILE - jnp.sum(nch, axis=0)
    seg_len = jnp.concatenate([nch.T, pad_e[:, None]], axis=1).reshape(-1)
    seg_first = jnp.concatenate([first.T, jnp.full((N_EXP, 1), -1, jnp.int32)], axis=1).reshape(-1)
    seg_start = jnp.cumsum(seg_len) - seg_len
    j = jnp.arange(n_chunks_max, dtype=jnp.int32)
    started = seg_start[None, :] <= j[:, None]
    inc = lambda v: v - jnp.concatenate([jnp.zeros((1,), v.dtype), v[:-1]])
    seg_first_j = jnp.sum(jnp.where(started, inc(seg_first)[None, :], 0), axis=1)
    seg_start_j = jnp.sum(jnp.where(started, inc(seg_start)[None, :], 0), axis=1)
    real = (seg_first_j >= 0) & (j < jnp.sum(seg_len))
    chunk = seg_first_j + (j - seg_start_j)
    pad_rank = jnp.cumsum(jnp.where(real, 0, 1)) - 1
    src = jnp.where(real, chunk, BLOCK_CHUNKS - 1).astype(jnp.int32)
    dst = jnp.where(real, chunk, nblk * BLOCK_CHUNKS + pad_rank % BLOCK_CHUNKS).astype(jnp.int32)
    tile_end = jnp.cumsum(tiles_e)
    i = jnp.arange(n_tiles_max, dtype=jnp.int32)
    tile_expert = jnp.minimum(jnp.sum((tile_end[None, :] <= i[:, None]).astype(jnp.int32), axis=1), N_EXP - 1)
    return src, dst, tile_expert.astype(jnp.int32), tile_end[-1:].astype(jnp.int32)


def _expert_kernel(src_ref, dst_ref, texp_ref, nused_ref, xs_hbm, wg_ref, wu_ref, wd_ref, ys_hbm,
                   xbuf, ybuf, wgu_bf, wd_bf, gsem, ssem):
    i = pl.program_id(0)
    n_used = nused_ref[0]
    slot = lax.rem(i, 2)
    g = ROW_GRANULE

    def chunk_copies(tile, slot_, to_buffer, do):
        for c in range(CHUNKS_PER_TILE):
            j = tile * CHUNKS_PER_TILE + c
            if to_buffer:
                rows = pl.ds(pl.multiple_of(src_ref[j] * g, g), g)
                cp = pltpu.make_async_copy(xs_hbm.at[rows], xbuf.at[slot_, pl.ds(c * g, g)], gsem.at[slot_])
            else:
                rows = pl.ds(pl.multiple_of(dst_ref[j] * g, g), g)
                cp = pltpu.make_async_copy(ybuf.at[slot_, pl.ds(c * g, g)], ys_hbm.at[rows], ssem.at[slot_])
            do(cp)

    start = lambda cp: cp.start()
    wait = lambda cp: cp.wait()

    @pl.when(i == 0)
    def _():
        chunk_copies(0, 0, True, start)

    @pl.when(i < n_used)
    def _():
        chunk_copies(i, slot, True, wait)

        @pl.when(i + 1 < n_used)
        def _():
            chunk_copies(i + 1, 1 - slot, True, start)

        @pl.when((i == 0) | (texp_ref[i] != texp_ref[jnp.maximum(i - 1, 0)]))
        def _():
            wgu_bf[:, 0:E_DIM] = wg_ref[0, 0].astype(BF)
            wgu_bf[:, E_DIM:2 * E_DIM] = wu_ref[0, 0].astype(BF)
            wd_bf[...] = wd_ref[0, 0].astype(BF)

        gu = _dot(xbuf[slot], wgu_bf[...])
        hid = _silu_mul(gu[:, 0:E_DIM], gu[:, E_DIM:2 * E_DIM])
        ybuf[slot] = _dot(hid.astype(BF), wd_bf[...]).astype(BF)
        chunk_copies(i, slot, False, start)

        @pl.when(i >= 1)
        def _():
            chunk_copies(i - 1, 1 - slot, False, wait)

        @pl.when(i == n_used - 1)
        def _():
            chunk_copies(i, slot, False, wait)


def _experts(layer, xs, tables, w_eg, w_eu, w_ed, n_tiles_max):
    src, dst, tile_expert, n_used = tables
    wmap = lambda i, src_, dst_, texp, nu: (layer, texp[i], 0, 0)
    grid_spec = pltpu.PrefetchScalarGridSpec(
        num_scalar_prefetch=4,
        grid=(n_tiles_max,),
        in_specs=[pl.BlockSpec(memory_space=pl.ANY),
                  pl.BlockSpec((1, 1, D, E_DIM), wmap),
                  pl.BlockSpec((1, 1, D, E_DIM), wmap),
                  pl.BlockSpec((1, 1, E_DIM, D), wmap)],
        out_specs=pl.BlockSpec(memory_space=pl.ANY),
        scratch_shapes=[pltpu.VMEM((2, EXP_TILE, D), BF), pltpu.VMEM((2, EXP_TILE, D), BF),
                        pltpu.VMEM((D, 2 * E_DIM), BF), pltpu.VMEM((E_DIM, D), BF),
                        pltpu.SemaphoreType.DMA((2,)), pltpu.SemaphoreType.DMA((2,))],
    )
    return pl.pallas_call(
        _expert_kernel,
        grid_spec=grid_spec,
        out_shape=jax.ShapeDtypeStruct(xs.shape, xs.dtype),
        input_output_aliases={4: 0},
        compiler_params=_params("arbitrary"),
        name="moe_experts",
    )(src, dst, tile_expert, n_used, xs, w_eg, w_eu, w_ed)


def _combine_kernel(final, ys_ref, gt_ref, h_ref, sg_ref, su_ref, sd_ref, x_ref, m_ref, gf_ref, o_ref):
    routed = _dot_tn(gt_ref[...], ys_ref[...])
    h = h_ref[...]
    sh = _silu_mul(_dot(h, sg_ref[0].astype(BF)), _dot(h, su_ref[0].astype(BF)))
    out = x_ref[...] + m_ref[0] * (routed + _dot(sh.astype(BF), sd_ref[0].astype(BF)))
    if final:
        out = _rms(out, gf_ref[...])
    o_ref[...] = out


def _combine(layer, final, ys, gt, h2, w_sg, w_su, w_sd, xm, mods, g_final, n_ctx_blocks, blocks_per_lat):
    T = h2.shape[0]
    tb = DISP_BLOCK
    row = functools.partial(_mod_row, n_ctx_blocks, blocks_per_lat)
    return pl.pallas_call(
        functools.partial(_combine_kernel, final),
        grid=(T // tb,),
        in_specs=[pl.BlockSpec((DISP_ROWS, D), lambda b: (b, 0)),
                  pl.BlockSpec((DISP_ROWS, tb), lambda b: (b, 0)),
                  pl.BlockSpec((tb, D), lambda b: (b, 0)),
                  pl.BlockSpec((1, D, E_DIM), lambda b: (layer, 0, 0)),
                  pl.BlockSpec((1, D, E_DIM), lambda b: (layer, 0, 0)),
                  pl.BlockSpec((1, E_DIM, D), lambda b: (layer, 0, 0)),
                  pl.BlockSpec((tb, D), lambda b: (b, 0)),
                  pl.BlockSpec((1, 1, D), lambda b: (row(b), 0, 5)),
                  pl.BlockSpec((1, D), lambda b: (0, 0))],
        out_specs=pl.BlockSpec((tb, D), lambda b: (b, 0)),
        out_shape=jax.ShapeDtypeStruct((T, D), F32),
        compiler_params=_params("parallel"),
        name="moe_combine",
    )(ys, gt, h2, w_sg, w_su, w_sd, xm, mods, g_final)


def _moe(layer, final, h2, pos8, gate8, cnt, w_eg, w_eu, w_ed, w_sg, w_su, w_sd, xm, mods, g_final, disp_blocks):
    T = h2.shape[0]
    nblk = T // DISP_BLOCK
    n_chunks_max = (TOP_K * T + N_EXP * nblk * (ROW_GRANULE - 1)) // ROW_GRANULE + N_EXP * (CHUNKS_PER_TILE - 1)
    n_tiles_max = -(-n_chunks_max // CHUNKS_PER_TILE)
    xs, gt = _dispatch(h2, pos8, gate8)
    tables = _tile_tables(cnt[:, :, 0].astype(jnp.int32), n_tiles_max * CHUNKS_PER_TILE, n_tiles_max)
    ys = _experts(layer, xs, tables, w_eg, w_eu, w_ed, n_tiles_max)
    return _combine(layer, final, ys, gt, h2, w_sg, w_su, w_sd, xm, mods, g_final, *disp_blocks)


def _reorder_w_in(w_in):
    kr = w_in[:, :, 2592:2624]
    kr_sw = kr.reshape(kr.shape[0], D, M_ROPE // 2, 2)[..., ::-1].reshape(kr.shape)
    pad = jnp.zeros(kr.shape, w_in.dtype)
    out = jnp.concatenate([w_in[:, :, :1920], w_in[:, :, 1952:2592], w_in[:, :, 1920:1952], kr, kr_sw, pad,
                           w_in[:, :, 2624:]], axis=2)
    return out.astype(BF)


def _reorder_w_q(w_q_up, with_swap):
    L = w_q_up.shape[0]
    w = w_q_up.reshape(L, M_QL, MH, M_NOPE + M_ROPE)
    nope = w[..., :M_NOPE].reshape(L, M_QL, MH * M_NOPE)
    rope = w[..., M_NOPE:]
    parts = [nope, rope.reshape(L, M_QL, MH * M_ROPE)]
    if with_swap:
        parts.append(rope.reshape(L, M_QL, MH, M_ROPE // 2, 2)[..., ::-1].reshape(L, M_QL, MH * M_ROPE))
    return jnp.concatenate(parts, axis=2).astype(BF)


def _reorder_w_kv(w_kv_up):
    L = w_kv_up.shape[0]
    w = w_kv_up.reshape(L, M_KVL, MH, M_NOPE + M_V)
    return jnp.concatenate([w[..., :M_NOPE].reshape(L, M_KVL, MH * M_NOPE),
                            w[..., M_NOPE:].reshape(L, M_KVL, MH * M_V)], axis=2).astype(BF)


def kernel(x_prompt, x_sample, state_gla_fwd, state_gla_bwd, cache_mla_ckv, cache_mla_krope, c, c_ctx, w_mod, b_mod, g_norm1, g_norm2, w_in, w_gla_gate_f, b_gla_gate_f, w_gla_gate_b, b_gla_gate_b, g_gla_out, g_q_a, w_q_up, g_kv_a, w_kv_up, w_br_fourier, w_br_gla, w_br_mla, w_out, w_router, b_router, w_exp_gate, w_exp_up, w_exp_down, w_sh_gate, w_sh_up, w_sh_down, g_final):
    nb, sl, _ = x_prompt.shape
    db, dl, _ = x_sample.shape
    L = w_mod.shape[0]
    t_ctx, t_lat = nb * sl, db * dl
    T = t_ctx + t_lat
    assert sl % G_CHUNK == 0 and dl % G_CHUNK == 0 and dl % GRID_W == 0
    assert t_ctx % dl == 0 and dl % TOKEN_BLOCK == 0 and TOKEN_BLOCK % DISP_BLOCK == 0
    assert t_ctx % TOKEN_BLOCK == 0 and dl % Q_BLOCK == 0 and 1 + db <= 8

    x = jnp.concatenate([x_prompt.reshape(t_ctx, D), x_sample.reshape(t_lat, D)], axis=0)
    cond = jnp.concatenate([c_ctx[None, :], c, jnp.zeros((7 - db, D), F32)], axis=0)
    mods_all = _modulation(cond, w_mod, b_mod)

    w_in_r = _reorder_w_in(w_in)
    wq_ctx = _reorder_w_q(w_q_up, False)
    wq_lat = _reorder_w_q(w_q_up, True)
    wkv_r = _reorder_w_kv(w_kv_up)
    lat_off = t_ctx // dl

    new_f, new_b, new_ckv, new_kr = [], [], [], []
    for l in range(L):
        mods = mods_all[l].reshape(8, 1, 6 * D)
        tok = (t_ctx // TOKEN_BLOCK, dl // TOKEN_BLOCK)
        zf, zqk, zvr, zmla, zsm, zg = _in_projection(x, mods, g_norm1[l][None, :], w_in_r[l], *tok)

        (f_c,) = _fourier(zf, sl, nb, 0)
        (f_l,) = _fourier(zf, dl, db, lat_off)

        gate_w = (w_gla_gate_f[l], b_gla_gate_f[l][None, :], w_gla_gate_b[l], b_gla_gate_b[l][None, :],
                  g_gla_out[l].reshape(1, GH * GDV))
        og_c, s_f, s_b = _gla(zqk, zvr, zsm, gate_w, sl, nb, 0, None)
        og_l, _, _ = _gla(zqk, zvr, zsm, gate_w, dl, db, lat_off, (state_gla_fwd[:, l], state_gla_bwd[:, l]))

        gq, gkv = g_q_a[l][None, :], g_kv_a[l][None, :]
        om_c, ckv = _mla(zmla, zsm, (gq, wq_ctx[l], gkv, wkv_r[l]), sl, nb, 0, None)
        (om_l,) = _mla(zmla, zsm, (gq, wq_lat[l], gkv, wkv_r[l]), dl, db, lat_off,
                       (cache_mla_ckv[:, l], cache_mla_krope[:, l]))

        mw = (w_br_fourier[l].astype(BF), w_br_gla[l].astype(BF), w_br_mla[l].astype(BF), w_out[l].astype(BF),
              g_norm2[l][None, :], w_router[l].T, b_router[l][:, None])
        xm, h2, pos8, gate8, cnt = _merge(x, (f_c, og_c, om_c), (f_l, og_l, om_l), zg, mods, mw, *tok)

        x = _moe(l, l == L - 1, h2, pos8, gate8, cnt, w_exp_gate, w_exp_up, w_exp_down, w_sh_gate, w_sh_up, w_sh_down,
                 xm, mods, g_final[None, :], (t_ctx // DISP_BLOCK, dl // DISP_BLOCK))

        new_f.append(s_f)
        new_b.append(s_b)
        new_ckv.append(ckv.reshape(nb, sl, M_KVL))
        new_kr.append(zsm[:t_ctx, 32:64].reshape(nb, sl, M_ROPE))

    y_prompt = x[:t_ctx].reshape(nb, sl, D)
    y_sample = x[t_ctx:].reshape(db, dl, D)
    return (y_prompt, y_sample, jnp.stack(new_f, axis=1), jnp.stack(new_b, axis=1),
            jnp.stack(new_ckv, axis=1), jnp.stack(new_kr, axis=1))
```

```python
import functools

import numpy as np
import jax
import jax.numpy as jnp
from jax import lax
from jax.experimental import pallas as pl
from jax.experimental.pallas import tpu as pltpu

F32 = jnp.float32
BF = jnp.bfloat16

D = 1024
GRID_W = 64
FN_G, FN_GW = 4, 96
FN_W = FN_G * FN_GW
GH, GDK, GDV = 4, 64, 128
G_RANK = 16
G_TAU = 16.0
G_CHUNK = 64
MH, M_NOPE, M_ROPE, M_V = 8, 64, 32, 64
M_QL, M_KVL = 384, 256
ROPE_BASE = 10000.0
N_EXP, TOP_K, N_GRP, TOPK_GRP = 64, 8, 8, 4
E_DIM = 256
ROUTED_SCALE = 2.5
EPS = 1e-6

C_F = (0, 384)
C_QK = (384, 896)
C_VR = (896, 1920)
C_MLA = (1920, 2560)
C_SM = (2560, 2688)
C_G = (2688, 5760)
IN_COLS_R = 5760

VMEM_LIMIT_V7X = 56 * 1024 * 1024
TOKEN_BLOCK = 512
Q_BLOCK = 256
DISP_BLOCK = 256
ROW_GRANULE = 16
DISP_ROWS = -(-(TOP_K * DISP_BLOCK + N_EXP * (ROW_GRANULE - 1)) // 256) * 256
EXP_TILE = 512
CHUNKS_PER_TILE = EXP_TILE // ROW_GRANULE
BLOCK_CHUNKS = DISP_ROWS // ROW_GRANULE


def _params(*sem):
    return pltpu.CompilerParams(dimension_semantics=sem, vmem_limit_bytes=VMEM_LIMIT_V7X)


def _dot(a, b):
    return jnp.dot(a, b, preferred_element_type=F32)


def _dot_hi(a, b):
    return jnp.dot(a, b, precision=lax.Precision.HIGHEST, preferred_element_type=F32)


def _dot_nt(a, b, precision=None):
    return lax.dot_general(a, b, (((1,), (1,)), ((), ())), precision=precision, preferred_element_type=F32)


def _dot_tn(a, b):
    return lax.dot_general(a, b, (((0,), (0,)), ((), ())), preferred_element_type=F32)


def _sigmoid(x):
    return 1.0 / (1.0 + jnp.exp(-x))


def _rms(x, g):
    return x * lax.rsqrt(jnp.mean(x * x, axis=-1, keepdims=True) + EPS) * g


def _iota(shape, dim):
    return lax.broadcasted_iota(jnp.int32, shape, dim)


def _mod_row(n_ctx_blocks, blocks_per_lat, i):
    return jnp.where(i < n_ctx_blocks, 0, 1 + (i - n_ctx_blocks) // blocks_per_lat)


def _mod_kernel(c_ref, w_ref, b_ref, o_ref):
    c = c_ref[...]
    o_ref[0] = _dot_hi(c * _sigmoid(c), w_ref[0]) + b_ref[0]


def _modulation(cond, w_mod, b_mod):
    L = w_mod.shape[0]
    rows = cond.shape[0]
    tn = 1536
    return pl.pallas_call(
        _mod_kernel,
        grid=(L, 6 * D // tn),
        in_specs=[pl.BlockSpec((rows, D), lambda l, j: (0, 0)),
                  pl.BlockSpec((1, D, tn), lambda l, j: (l, 0, j)),
                  pl.BlockSpec((1, 1, tn), lambda l, j: (l, 0, j))],
        out_specs=pl.BlockSpec((1, rows, tn), lambda l, j: (l, 0, j)),
        out_shape=jax.ShapeDtypeStruct((L, rows, 6 * D), F32),
        compiler_params=_params("parallel", "parallel"),
        name="modulation",
    )(cond, w_mod, b_mod.reshape(L, 1, 6 * D))


def _inproj_kernel(x_ref, m_ref, g_ref, w_ref, of_ref, oqk_ref, ovr_ref, omla_ref, osm_ref, og_ref):
    y = _rms(x_ref[...], g_ref[...])
    h = (y * (1.0 + m_ref[0, :, D:2 * D]) + m_ref[0, :, 0:D]).astype(BF)
    of_ref[...] = _dot(h, w_ref[:, C_F[0]:C_F[1]]).astype(BF)
    oqk_ref[...] = _dot(h, w_ref[:, C_QK[0]:C_QK[1]])
    ovr_ref[...] = _dot(h, w_ref[:, C_VR[0]:C_VR[1]]).astype(BF)
    omla_ref[...] = _dot(h, w_ref[:, C_MLA[0]:C_MLA[1]])
    osm_ref[...] = _dot(h, w_ref[:, C_SM[0]:C_SM[1]])
    og_ref[...] = _dot(h, w_ref[:, C_G[0]:C_G[1]]).astype(BF)


def _in_projection(x, mods, g1, w_in_r, n_ctx_blocks, blocks_per_lat):
    T = x.shape[0]
    tb = TOKEN_BLOCK
    row = functools.partial(_mod_row, n_ctx_blocks, blocks_per_lat)
    widths = [(C_F, BF), (C_QK, F32), (C_VR, BF), (C_MLA, F32), (C_SM, F32), (C_G, BF)]
    return pl.pallas_call(
        _inproj_kernel,
        grid=(T // tb,),
        in_specs=[pl.BlockSpec((tb, D), lambda i: (i, 0)),
                  pl.BlockSpec((1, 1, 2 * D), lambda i: (row(i), 0, 0)),
                  pl.BlockSpec((1, D), lambda i: (0, 0)),
                  pl.BlockSpec((D, IN_COLS_R), lambda i: (0, 0))],
        out_specs=[pl.BlockSpec((tb, c[1] - c[0]), lambda i: (i, 0)) for c, _ in widths],
        out_shape=[jax.ShapeDtypeStruct((T, c[1] - c[0]), dt) for c, dt in widths],
        compiler_params=_params("parallel"),
        name="in_projection",
    )(x, mods, g1, w_in_r)


def _seq_call(kernel, name, n, nseq, blk_off, seq_ins, const_ins, out_widths, extra_outs=(), scratch=()):
    in_specs = [pl.BlockSpec((n, a.shape[1]), lambda i: (i + blk_off, 0)) for a in seq_ins]
    in_specs += [pl.BlockSpec(bs, im) for _, bs, im in const_ins]
    args = list(seq_ins) + [a for a, _, _ in const_ins]
    out_specs = [pl.BlockSpec((n, w), lambda i: (i, 0)) for w, _ in out_widths]
    out_shape = [jax.ShapeDtypeStruct((nseq * n, w), dt) for w, dt in out_widths]
    out_specs += [pl.BlockSpec(bs, im) for _, _, bs, im in extra_outs]
    out_shape += [jax.ShapeDtypeStruct(s, dt) for s, dt, _, _ in extra_outs]
    return pl.pallas_call(
        kernel, grid=(nseq,), in_specs=in_specs, out_specs=out_specs, out_shape=out_shape,
        scratch_shapes=list(scratch), compiler_params=_params("parallel"), name=name,
    )(*args)


def _fourier_kernel(u_ref, r_ref, lc_ref, ls_ref, o_ref):
    y = _dot(u_ref[...], r_ref[...])
    o_ref[...] = (_dot(lc_ref[...], y[:, :FN_W].astype(BF)) + _dot(ls_ref[...], y[:, FN_W:].astype(BF))).astype(BF)


def _dft_tables(n):
    k = np.arange(FN_GW)
    ang = 2.0 * np.pi * ((k[:, None] * k[None, :]) % FN_GW) / FN_GW
    eye = np.eye(FN_G)
    right = np.concatenate([np.kron(eye, np.cos(ang)), np.kron(eye, np.sin(ang))], axis=1)
    p = np.arange(n)
    angn = 2.0 * np.pi * ((p[:, None] * p[None, :]) % n) / n
    scale = 1.0 / np.sqrt(float(n * FN_GW))
    return (jnp.asarray(right, F32).astype(BF), jnp.asarray(np.cos(angn) * scale, F32).astype(BF),
            jnp.asarray(-np.sin(angn) * scale, F32).astype(BF))


def _fourier(zf, n, nseq, blk_off):
    right, lc, ls = _dft_tables(n)
    consts = [(right, (FN_W, 2 * FN_W), lambda i: (0, 0)), (lc, (n, n), lambda i: (0, 0)), (ls, (n, n), lambda i: (0, 0))]
    return _seq_call(_fourier_kernel, "fourier_mix", n, nseq, blk_off, [zf], consts, [(FN_W, BF)])


def _log_gate(z, w_ref, b_ref):
    pre = _dot_hi(z, w_ref[...]) + b_ref[...]
    return (jnp.minimum(pre, 0.0) - jnp.log1p(jnp.exp(-jnp.abs(pre)))) * (1.0 / G_TAU)


def _gla_kernel(has_state, n, *refs):
    if has_state:
        (zqk, zvr, zsm, wgf, bgf, wgb, bgb, gout, s0f, s0b, o_ref, sf_ref, sb_ref,
         oacc_f, oacc_b, lg_f, lg_b, st_f, st_b) = refs
    else:
        (zqk, zvr, zsm, wgf, bgf, wgb, bgb, gout, o_ref, sf_ref, sb_ref,
         oacc_f, oacc_b, lg_f, lg_b, st_f, st_b) = refs
        s0f = s0b = None
    C = G_CHUNK
    nc = n // C
    ri, ci = _iota((C, C), 0), _iota((C, C), 1)
    t_idx, lane = _iota((C, 128), 0), _iota((C, 128), 1)
    s_idx = lane & (C - 1)
    left = lane < GDK
    vleft = _iota((C, 2 * GDV), 1) < GDV
    blockdiag = (_iota((2 * GDV, 2 * GDK), 0) >> 7) == (_iota((2 * GDV, 2 * GDK), 1) >> 6)

    def load_state(st, s0_ref):
        for p in range(2):
            if s0_ref is None:
                st[p] = jnp.zeros((2 * GDV, 2 * GDK), F32)
            else:
                z = jnp.zeros((GDK, GDV), F32)
                blk = jnp.concatenate([jnp.concatenate([s0_ref[0, 2 * p], z], axis=1),
                                       jnp.concatenate([z, s0_ref[0, 2 * p + 1]], axis=1)], axis=0)
                st[p] = blk.T

    def store_state(st, out_ref):
        for p in range(2):
            blk = st[p].T
            out_ref[0, 2 * p] = blk[0:GDK, 0:GDV]
            out_ref[0, 2 * p + 1] = blk[GDK:2 * GDK, GDV:2 * GDV]

    def chunk(c, reverse, lg, st, oacc):
        tri = (ci >= ri).astype(F32) if reverse else (ci <= ri).astype(F32)
        mask = (s_idx >= t_idx) if reverse else (s_idx <= t_idx)
        rows = pl.ds(pl.multiple_of(c * C, C), C)
        cum = _dot_hi(tri, lg[rows, :])
        tot = cum[0:1] if reverse else cum[C - 1:C]
        q = zqk[rows, 0:GH * GDK] * (GDK ** -0.5)
        k = zqk[rows, GH * GDK:2 * GH * GDK]
        qh = (q * jnp.exp(cum)).astype(BF)
        kh = (k * jnp.exp(-cum)).astype(BF)
        kb = (k * jnp.exp(tot - cum)).astype(BF)
        dec = jnp.exp(tot)
        for p in range(2):
            ls = slice(128 * p, 128 * p + 128)
            vs = slice(256 * p, 256 * p + 256)
            qp, kp = qh[:, ls], kh[:, ls]
            zk = jnp.zeros_like(kp)
            kblk = jnp.concatenate([jnp.where(left, kp, zk), jnp.where(left, zk, kp)], axis=0)
            sc = jnp.where(mask, _dot_nt(qp, kblk), 0.0).astype(BF)
            vp = zvr[rows, vs]
            zv = jnp.zeros_like(vp)
            vblk = jnp.concatenate([jnp.where(vleft, vp, zv), jnp.where(vleft, zv, vp)], axis=0)
            stp = st[p]
            oacc[rows, vs] = _dot(sc, vblk) + _dot_nt(qp, stp.astype(BF))
            st[p] = dec[:, ls] * stp + jnp.where(blockdiag, _dot_tn(vp, kb[:, ls]), 0.0)

    def both_directions(step, carry):
        chunk(step, False, lg_f, st_f, oacc_f)
        chunk(nc - 1 - step, True, lg_b, st_b, oacc_b)
        return carry

    lg_f[...] = _log_gate(zsm[:, 0:G_RANK], wgf, bgf)
    lg_b[...] = _log_gate(zsm[:, G_RANK:2 * G_RANK], wgb, bgb)
    load_state(st_f, s0f)
    load_state(st_b, s0b)
    lax.fori_loop(0, nc, both_directions, 0)
    store_state(st_f, sf_ref)
    store_state(st_b, sb_ref)

    rb = 128
    for r0 in range(0, n, rb):
        for h in range(GH):
            hs = slice(GDV * h, GDV * h + GDV)
            oh = oacc_f[r0:r0 + rb, hs] + oacc_b[r0:r0 + rb, hs]
            oh = oh * lax.rsqrt(jnp.mean(oh * oh, axis=-1, keepdims=True) + EPS) * gout[:, hs]
            r = zvr[r0:r0 + rb, GH * GDV + hs.start:GH * GDV + hs.stop].astype(F32)
            o_ref[r0:r0 + rb, hs] = (oh * (r * _sigmoid(r))).astype(BF)


def _gla(zqk, zvr, zsm, gate_w, n, nseq, blk_off, states):
    wgf, bgf, wgb, bgb, gout = gate_w
    c2 = lambda i: (0, 0)
    consts = [(wgf, wgf.shape, c2), (bgf, bgf.shape, c2), (wgb, wgb.shape, c2), (bgb, bgb.shape, c2), (gout, gout.shape, c2)]
    st_blk = (1, GH, GDK, GDV)
    st_map = lambda i: (i, 0, 0, 0)
    if states is not None:
        consts += [(s, st_blk, st_map) for s in states]
    extra = [((nseq, GH, GDK, GDV), F32, st_blk, st_map)] * 2
    scratch = ([pltpu.VMEM((n, GH * GDV), F32)] * 2 + [pltpu.VMEM((n, GH * GDK), F32)] * 2
               + [pltpu.VMEM((2, 2 * GDV, 2 * GDK), F32)] * 2)
    return _seq_call(functools.partial(_gla_kernel, states is not None, n), "gla_mixer", n, nseq, blk_off,
                     [zqk, zvr, zsm], consts, [(GH * GDV, BF)], extra_outs=extra, scratch=scratch)


def _mla_kernel(latent, n, past, *refs):
    if latent:
        (zmla, zsm, gq, wq, gkv, wkv, cckv, ckr, cosq, sinq, cosk, sink, o_ref, qs, kns, vs, krs) = refs
    else:
        (zmla, zsm, gq, wq, gkv, wkv, o_ref, ckv_ref, qs, kns, vs, krs) = refs
    sk = past + n
    scale = (M_NOPE + M_ROPE) ** -0.5
    nw, rw = MH * M_NOPE, MH * M_ROPE
    qa = _dot(_rms(zmla[:, 0:M_QL], gq[...]).astype(BF), wq[...])
    qr = qa[:, nw:nw + rw]
    if latent:
        qr = qr * cosq[...] + qa[:, nw + rw:nw + 2 * rw] * sinq[...]
    qs[:, 0:nw] = qa[:, 0:nw] * scale
    qs[:, nw:nw + rw] = qr * scale
    ckv = _rms(zmla[:, M_QL:M_QL + M_KVL], gkv[...])
    kv = _dot(ckv.astype(BF), wkv[...])
    kr = zsm[:, 32:64]
    if latent:
        kr = kr * cosk[...] + zsm[:, 64:96] * sink[...]
        kvc = _dot(cckv[0].astype(BF), wkv[...])
        kns[0:past, :] = kvc[:, 0:nw].astype(BF)
        vs[0:past, :] = kvc[:, nw:].astype(BF)
        krs[0:past, :] = jnp.concatenate([ckr[0]] * 4, axis=1).astype(BF)
    else:
        ckv_ref[...] = ckv
    kns[past:sk, :] = kv[:, 0:nw].astype(BF)
    vs[past:sk, :] = kv[:, nw:].astype(BF)
    krs[past:sk, :] = jnp.concatenate([kr] * 4, axis=1).astype(BF)

    qb = min(Q_BLOCK, n)
    lane = _iota((qb, 128), 1)

    def block(step, carry):
        rows = pl.ds(pl.multiple_of(step * qb, qb), qb)
        for p in range(MH // 2):
            ls = slice(128 * p, 128 * p + 128)
            qn = qs[rows, ls]
            quad = (2 * p) // 4
            qrp = qs[rows, nw + 128 * quad:nw + 128 * quad + 128]
            rhs = jnp.concatenate([kns[:, ls], krs[...]], axis=1)
            vp = vs[:, ls]
            o_pair = None
            for hh in range(2):
                j = (2 * p + hh) % 4
                qn_m = jnp.where((lane >> 6) == hh, qn, 0.0).astype(BF)
                qr_m = jnp.where((lane >> 5) == j, qrp, 0.0).astype(BF)
                s = _dot_nt(jnp.concatenate([qn_m, qr_m], axis=1), rhs)
                e = jnp.exp(s - jnp.max(s, axis=-1, keepdims=True))
                pv = _dot(e.astype(BF), vp) / jnp.sum(e, axis=-1, keepdims=True)
                o_pair = pv if hh == 0 else jnp.where(lane < M_V, o_pair, pv)
            o_ref[rows, ls] = o_pair.astype(BF)
        return carry

    lax.fori_loop(0, n // qb, block, 0)


def _rope_tables(n):
    half = M_ROPE // 2
    pos = jnp.arange(n)
    row = (pos // GRID_W).astype(F32)
    col = (pos % GRID_W).astype(F32)
    inv = ROPE_BASE ** (-jnp.arange(0, half, 2, dtype=F32) / half)
    ang = jnp.concatenate([row[:, None] * inv, col[:, None] * inv], axis=-1)
    cos = jnp.repeat(jnp.cos(ang), 2, axis=-1)
    sin = jnp.repeat(jnp.sin(ang), 2, axis=-1) * jnp.tile(jnp.asarray([-1.0, 1.0], F32), half)
    return jnp.tile(cos, (1, MH)), jnp.tile(sin, (1, MH)), cos, sin


def _mla(zmla, zsm, w, n, nseq, blk_off, cache):
    gq, wq, gkv, wkv = w
    c2 = lambda i: (0, 0)
    consts = [(gq, gq.shape, c2), (wq, wq.shape, c2), (gkv, gkv.shape, c2), (wkv, wkv.shape, c2)]
    past = 0
    extra = []
    if cache is not None:
        cckv, ckr = cache
        past = cckv.shape[1]
        c3 = lambda i: (i, 0, 0)
        consts += [(cckv, (1, past, M_KVL), c3), (ckr, (1, past, M_ROPE), c3)]
        consts += [(t, t.shape, c2) for t in _rope_tables(n)]
    else:
        extra = [((nseq * n, M_KVL), F32, (n, M_KVL), lambda i: (i, 0))]
    sk = past + n
    scratch = [pltpu.VMEM((n, MH * (M_NOPE + M_ROPE)), F32), pltpu.VMEM((sk, MH * M_NOPE), BF),
               pltpu.VMEM((sk, MH * M_V), BF), pltpu.VMEM((sk, 128), BF)]
    return _seq_call(functools.partial(_mla_kernel, cache is not None, n, past), "mla_mixer", n, nseq, blk_off,
                     [zmla, zsm], consts, [(MH * M_V, BF)], extra_outs=extra, scratch=scratch)


def _route(logits_t, bias):
    nt = logits_t.shape[1]
    gsz = N_EXP // N_GRP
    scores = _sigmoid(logits_t)
    sel = scores + bias
    neg = -jnp.inf
    sub = _iota((gsz, nt), 0)
    tops = []
    for g in range(N_GRP):
        blk = sel[gsz * g:gsz * g + gsz]
        m1 = jnp.max(blk, axis=0, keepdims=True)
        first = jnp.min(jnp.where(blk == m1, sub, gsz), axis=0, keepdims=True)
        m2 = jnp.max(jnp.where(sub == first, neg, blk), axis=0, keepdims=True)
        tops.append(m1 + m2)
    gs = jnp.concatenate(tops, axis=0)
    gidx = _iota((N_GRP, nt), 0)
    grank = jnp.zeros((N_GRP, nt), jnp.int32)
    for j in range(N_GRP):
        rj = gs[j:j + 1]
        grank += ((rj > gs) | ((rj == gs) & (gidx > j))).astype(jnp.int32)
    keep = grank < TOPK_GRP
    masked = jnp.concatenate(
        [jnp.where(jnp.broadcast_to(keep[g:g + 1], (gsz, nt)), sel[gsz * g:gsz * g + gsz], neg) for g in range(N_GRP)], axis=0)
    eidx = _iota((N_EXP, nt), 0)
    rank = jnp.zeros((N_EXP, nt), jnp.int32)
    for j in range(N_EXP):
        rj = masked[j:j + 1]
        rank += ((rj > masked) | ((rj == masked) & (eidx > j))).astype(jnp.int32)
    chosen = rank < TOP_K
    w = jnp.where(chosen, scores, 0.0)
    return chosen, w / jnp.sum(w, axis=0, keepdims=True) * ROUTED_SCALE


def _dispatch_meta(chosen, gates_t):
    tb = chosen.shape[1]
    sel = chosen.astype(F32)
    selb = sel.astype(BF)
    earlier = (_iota((tb, tb), 0) < _iota((tb, tb), 1)).astype(BF)
    rank = _dot(selb, earlier)
    cnt = jnp.sum(sel, axis=1, keepdims=True)
    padded = jnp.floor((cnt + (ROW_GRANULE - 1)) * (1.0 / ROW_GRANULE)) * ROW_GRANULE
    below = (_iota((N_EXP, N_EXP), 1) < _iota((N_EXP, N_EXP), 0)).astype(BF)
    start = _dot(below, jnp.broadcast_to(padded, (N_EXP, 128)).astype(BF))[:, 0:1]
    pos = start + rank
    kidx = _dot(below, selb)
    pos8, gate8 = [], []
    for k in range(TOP_K):
        hit = chosen & (kidx == float(k))
        pos8.append(jnp.sum(jnp.where(hit, pos, 0.0), axis=0, keepdims=True))
        gate8.append(jnp.sum(jnp.where(hit, gates_t, 0.0), axis=0, keepdims=True))
    return (jnp.concatenate(pos8, axis=0).astype(jnp.int32), jnp.concatenate(gate8, axis=0), cnt)


def _merge_kernel(n_ctx_blocks, x_ref, fc_ref, fl_ref, ogc_ref, ogl_ref, omc_ref, oml_ref, zg_ref, m_ref,
                  wbf, wbg, wbm, wout, gn2, wrt, brt, xm_ref, h2_ref, pos_ref, gate_ref, cnt_ref):
    is_ctx = pl.program_id(0) < n_ctx_blocks
    ya = _dot(jnp.where(is_ctx, fc_ref[...], fl_ref[...]), wbf[...])
    yb = _dot(jnp.where(is_ctx, ogc_ref[...], ogl_ref[...]), wbg[...])
    yc = _dot(jnp.where(is_ctx, omc_ref[...], oml_ref[...]), wbm[...])
    merged = (_sigmoid(zg_ref[:, 0:D].astype(F32)) * ya + _sigmoid(zg_ref[:, D:2 * D].astype(F32)) * yb
              + _sigmoid(zg_ref[:, 2 * D:3 * D].astype(F32)) * yc)
    xm = x_ref[...] + m_ref[0, :, 2 * D:3 * D] * _dot(merged.astype(BF), wout[...])
    xm_ref[...] = xm
    h2 = _rms(xm, gn2[...]) * (1.0 + m_ref[0, :, 4 * D:5 * D]) + m_ref[0, :, 3 * D:4 * D]
    h2_ref[...] = h2.astype(BF)
    chosen, gates_t = _route(_dot_nt(wrt[...], h2, precision=lax.Precision.HIGHEST), brt[...])
    for sb in range(gates_t.shape[1] // DISP_BLOCK):
        ls = slice(sb * DISP_BLOCK, (sb + 1) * DISP_BLOCK)
        pos8, gate8, cnt = _dispatch_meta(chosen[:, ls], gates_t[:, ls])
        pos_ref[:, ls] = pos8
        gate_ref[:, ls] = gate8
        cnt_ref[sb] = jnp.broadcast_to(cnt, (N_EXP, 128))


def _merge(x, mix_ctx, mix_lat, zg, mods, w, n_ctx_blocks, blocks_per_lat):
    T = x.shape[0]
    tb = TOKEN_BLOCK
    row = functools.partial(_mod_row, n_ctx_blocks, blocks_per_lat)
    rb = lambda wd: pl.BlockSpec((tb, wd), lambda i: (i, 0))
    cb = lambda a: pl.BlockSpec(a.shape, lambda i: (0, 0))
    ctx_b = lambda wd: pl.BlockSpec((tb, wd), lambda i: (jnp.minimum(i, n_ctx_blocks - 1), 0))
    lat_b = lambda wd: pl.BlockSpec((tb, wd), lambda i: (jnp.maximum(i - n_ctx_blocks, 0), 0))
    mix_specs, mix_args = [], []
    for a_c, a_l in zip(mix_ctx, mix_lat):
        mix_specs += [ctx_b(a_c.shape[1]), lat_b(a_l.shape[1])]
        mix_args += [a_c, a_l]
    return pl.pallas_call(
        functools.partial(_merge_kernel, n_ctx_blocks),
        grid=(T // tb,),
        in_specs=[rb(D)] + mix_specs + [rb(3 * D),
                  pl.BlockSpec((1, 1, 6 * D), lambda i: (row(i), 0, 0))] + [cb(a) for a in w],
        out_specs=[rb(D), rb(D), pl.BlockSpec((TOP_K, tb), lambda i: (0, i)), pl.BlockSpec((TOP_K, tb), lambda i: (0, i)),
                   pl.BlockSpec((tb // DISP_BLOCK, N_EXP, 128), lambda i: (i, 0, 0))],
        out_shape=[jax.ShapeDtypeStruct((T, D), F32), jax.ShapeDtypeStruct((T, D), BF),
                   jax.ShapeDtypeStruct((TOP_K, T), jnp.int32), jax.ShapeDtypeStruct((TOP_K, T), F32),
                   jax.ShapeDtypeStruct((T // DISP_BLOCK, N_EXP, 128), F32)],
        compiler_params=_params("parallel"),
        name="merge_route",
    )(x, *mix_args, zg, mods, *w)


def _silu_mul(a, b):
    return a * _sigmoid(a) * b


def _dispatch_kernel(n_blocks, h_ref, pos_ref, gate_ref, xs_ref, gt_ref):
    h = h_ref[...]
    live = (pl.program_id(0) < n_blocks).astype(F32)
    rc = 256
    for r0 in range(0, DISP_ROWS, rc):
        rows = _iota((rc, DISP_BLOCK), 0) + r0
        gt = jnp.zeros((rc, DISP_BLOCK), F32)
        for k in range(TOP_K):
            gt = jnp.where(rows == pos_ref[k:k + 1, :], gate_ref[k:k + 1, :] * live, gt)
        gt_ref[r0:r0 + rc, :] = gt.astype(BF)
        place = jnp.where(gt != 0.0, 1.0, 0.0).astype(BF)
        xs_ref[r0:r0 + rc, :] = _dot(place, h).astype(BF)


def _dispatch(h2, pos8, gate8):
    T = h2.shape[0]
    nblk = T // DISP_BLOCK
    last = lambda b: jnp.minimum(b, nblk - 1)
    return pl.pallas_call(
        functools.partial(_dispatch_kernel, nblk),
        grid=(nblk + 1,),
        in_specs=[pl.BlockSpec((DISP_BLOCK, D), lambda b: (last(b), 0)),
                  pl.BlockSpec((TOP_K, DISP_BLOCK), lambda b: (0, last(b))),
                  pl.BlockSpec((TOP_K, DISP_BLOCK), lambda b: (0, last(b)))],
        out_specs=[pl.BlockSpec((DISP_ROWS, D), lambda b: (b, 0)), pl.BlockSpec((DISP_ROWS, DISP_BLOCK), lambda b: (b, 0))],
        out_shape=[jax.ShapeDtypeStruct(((nblk + 1) * DISP_ROWS, D), BF),
                   jax.ShapeDtypeStruct(((nblk + 1) * DISP_ROWS, DISP_BLOCK), BF)],
        compiler_params=_params("parallel"),
        name="moe_dispatch",
    )(h2, pos8, gate8)


def _tile_tables(cnt, n_chunks_max, n_tiles_max):
    nblk = cnt.shape[0]
    nch = (cnt + (ROW_GRANULE - 1)) // ROW_GRANULE
    first = jnp.arange(nblk, dtype=jnp.int32)[:, None] * BLOCK_CHUNKS + jnp.cumsum(nch, axis=1) - nch
    tiles_e = (jnp.sum(nch, axis=0) + (CHUNKS_PER_TILE - 1)) // CHUNKS_PER_TILE
    pad_e = tiles_e * CHUNKS_PER_TILE - jnp.sum(nch, axis=0)
    seg_len = jnp.concatenate([nch.T, pad_e[:, None]], axis=1).reshape(-1)
    seg_first = jnp.concatenate([first.T, jnp.full((N_EXP, 1), -1, jnp.int32)], axis=1).reshape(-1)
    seg_start = jnp.cumsum(seg_len) - seg_len
    j = jnp.arange(n_chunks_max, dtype=jnp.int32)
    started = seg_start[None, :] <= j[:, None]
    inc = lambda v: v - jnp.concatenate([jnp.zeros((1,), v.dtype), v[:-1]])
    seg_first_j = jnp.sum(jnp.where(started, inc(seg_first)[None, :], 0), axis=1)
    seg_start_j = jnp.sum(jnp.where(started, inc(seg_start)[None, :], 0), axis=1)
    real = (seg_first_j >= 0) & (j < jnp.sum(seg_len))
    chunk = seg_first_j + (j - seg_start_j)
    pad_rank = jnp.cumsum(jnp.where(real, 0, 1)) - 1
    src = jnp.where(real, chunk, BLOCK_CHUNKS - 1).astype(jnp.int32)
    dst = jnp.where(real, chunk, nblk * BLOCK_CHUNKS + pad_rank % BLOCK_CHUNKS).astype(jnp.int32)
    tile_end = jnp.cumsum(tiles_e)
    i = jnp.arange(n_tiles_max, dtype=jnp.int32)
    tile_expert = jnp.minimum(jnp.sum((tile_end[None, :] <= i[:, None]).astype(jnp.int32), axis=1), N_EXP - 1)
    return src, dst, tile_expert.astype(jnp.int32), tile_end[-1:].astype(jnp.int32)


def _expert_kernel(src_ref, dst_ref, texp_ref, nused_ref, xs_hbm, wg_ref, wu_ref, wd_ref, ys_hbm,
                   xbuf, ybuf, wgu_bf, wd_bf, gsem, ssem):
    i = pl.program_id(0)
    n_used = nused_ref[0]
    slot = lax.rem(i, 2)
    g = ROW_GRANULE

    def chunk_copies(tile, slot_, to_buffer, do):
        for c in range(CHUNKS_PER_TILE):
            j = tile * CHUNKS_PER_TILE + c
            if to_buffer:
                rows = pl.ds(pl.multiple_of(src_ref[j] * g, g), g)
                cp = pltpu.make_async_copy(xs_hbm.at[rows], xbuf.at[slot_, pl.ds(c * g, g)], gsem.at[slot_])
            else:
                rows = pl.ds(pl.multiple_of(dst_ref[j] * g, g), g)
                cp = pltpu.make_async_copy(ybuf.at[slot_, pl.ds(c * g, g)], ys_hbm.at[rows], ssem.at[slot_])
            do(cp)

    start = lambda cp: cp.start()
    wait = lambda cp: cp.wait()

    @pl.when(i == 0)
    def _():
        chunk_copies(0, 0, True, start)

    @pl.when(i == n_used)
    def _():
        chunk_copies(i, slot, True, wait)

    @pl.when(i < n_used)
    def _():
        chunk_copies(i, slot, True, wait)
        chunk_copies(i + 1, 1 - slot, True, start)

        @pl.when((i == 0) | (texp_ref[i] != texp_ref[jnp.maximum(i - 1, 0)]))
        def _():
            wgu_bf[:, 0:E_DIM] = wg_ref[0, 0].astype(BF)
            wgu_bf[:, E_DIM:2 * E_DIM] = wu_ref[0, 0].astype(BF)
            wd_bf[...] = wd_ref[0, 0].astype(BF)

        gu = _dot(xbuf[slot], wgu_bf[...])
        hid = _silu_mul(gu[:, 0:E_DIM], gu[:, E_DIM:2 * E_DIM])
        ybuf[slot] = _dot(hid.astype(BF), wd_bf[...]).astype(BF)
        chunk_copies(i, slot, False, start)

        @pl.when(i >= 1)
        def _():
            chunk_copies(i - 1, 1 - slot, False, wait)

        @pl.when(i == n_used - 1)
        def _():
            chunk_copies(i, slot, False, wait)


def _experts(layer, xs, tables, w_eg, w_eu, w_ed, n_tiles_max):
    src, dst, tile_expert, n_used = tables
    wmap = lambda i, src_, dst_, texp, nu: (layer, texp[i], 0, 0)
    grid_spec = pltpu.PrefetchScalarGridSpec(
        num_scalar_prefetch=4,
        grid=(n_tiles_max + 1,),
        in_specs=[pl.BlockSpec(memory_space=pl.ANY),
                  pl.BlockSpec((1, 1, D, E_DIM), wmap),
                  pl.BlockSpec((1, 1, D, E_DIM), wmap),
                  pl.BlockSpec((1, 1, E_DIM, D), wmap)],
        out_specs=pl.BlockSpec(memory_space=pl.ANY),
        scratch_shapes=[pltpu.VMEM((2, EXP_TILE, D), BF), pltpu.VMEM((2, EXP_TILE, D), BF),
                        pltpu.VMEM((D, 2 * E_DIM), BF), pltpu.VMEM((E_DIM, D), BF),
                        pltpu.SemaphoreType.DMA((2,)), pltpu.SemaphoreType.DMA((2,))],
    )
    return pl.pallas_call(
        _expert_kernel,
        grid_spec=grid_spec,
        out_shape=jax.ShapeDtypeStruct(xs.shape, xs.dtype),
        input_output_aliases={4: 0},
        compiler_params=_params("arbitrary"),
        name="moe_experts",
    )(src, dst, tile_expert, n_used, xs, w_eg, w_eu, w_ed)


def _combine_kernel(final, ys_ref, gt_ref, h_ref, sg_ref, su_ref, sd_ref, x_ref, m_ref, gf_ref, o_ref):
    routed = _dot_tn(gt_ref[...], ys_ref[...])
    h = h_ref[...]
    sh = _silu_mul(_dot(h, sg_ref[0].astype(BF)), _dot(h, su_ref[0].astype(BF)))
    out = x_ref[...] + m_ref[0] * (routed + _dot(sh.astype(BF), sd_ref[0].astype(BF)))
    if final:
        out = _rms(out, gf_ref[...])
    o_ref[...] = out


def _combine(layer, final, ys, gt, h2, w_sg, w_su, w_sd, xm, mods, g_final, n_ctx_blocks, blocks_per_lat):
    T = h2.shape[0]
    tb = DISP_BLOCK
    row = functools.partial(_mod_row, n_ctx_blocks, blocks_per_lat)
    return pl.pallas_call(
        functools.partial(_combine_kernel, final),
        grid=(T // tb,),
        in_specs=[pl.BlockSpec((DISP_ROWS, D), lambda b: (b, 0)),
                  pl.BlockSpec((DISP_ROWS, tb), lambda b: (b, 0)),
                  pl.BlockSpec((tb, D), lambda b: (b, 0)),
                  pl.BlockSpec((1, D, E_DIM), lambda b: (layer, 0, 0)),
                  pl.BlockSpec((1, D, E_DIM), lambda b: (layer, 0, 0)),
                  pl.BlockSpec((1, E_DIM, D), lambda b: (layer, 0, 0)),
                  pl.BlockSpec((tb, D), lambda b: (b, 0)),
                  pl.BlockSpec((1, 1, D), lambda b: (row(b), 0, 5)),
                  pl.BlockSpec((1, D), lambda b: (0, 0))],
        out_specs=pl.BlockSpec((tb, D), lambda b: (b, 0)),
        out_shape=jax.ShapeDtypeStruct((T, D), F32),
        compiler_params=_params("parallel"),
        name="moe_combine",
    )(ys, gt, h2, w_sg, w_su, w_sd, xm, mods, g_final)


def _moe(layer, final, h2, pos8, gate8, cnt, w_eg, w_eu, w_ed, w_sg, w_su, w_sd, xm, mods, g_final, disp_blocks):
    T = h2.shape[0]
    nblk = T // DISP_BLOCK
    n_chunks_max = (TOP_K * T + N_EXP * nblk * (ROW_GRANULE - 1)) // ROW_GRANULE + N_EXP * (CHUNKS_PER_TILE - 1)
    n_tiles_max = -(-n_chunks_max // CHUNKS_PER_TILE)
    xs, gt = _dispatch(h2, pos8, gate8)
    tables = _tile_tables(cnt[:, :, 0].astype(jnp.int32), (n_tiles_max + 1) * CHUNKS_PER_TILE, n_tiles_max + 1)
    ys = _experts(layer, xs, tables, w_eg, w_eu, w_ed, n_tiles_max)
    return _combine(layer, final, ys, gt, h2, w_sg, w_su, w_sd, xm, mods, g_final, *disp_blocks)


def _reorder_w_in(w_in):
    kr = w_in[:, :, 2592:2624]
    kr_sw = kr.reshape(kr.shape[0], D, M_ROPE // 2, 2)[..., ::-1].reshape(kr.shape)
    pad = jnp.zeros(kr.shape, w_in.dtype)
    out = jnp.concatenate([w_in[:, :, :1920], w_in[:, :, 1952:2592], w_in[:, :, 1920:1952], kr, kr_sw, pad,
                           w_in[:, :, 2624:]], axis=2)
    return out.astype(BF)


def _reorder_w_q(w_q_up, with_swap):
    L = w_q_up.shape[0]
    w = w_q_up.reshape(L, M_QL, MH, M_NOPE + M_ROPE)
    nope = w[..., :M_NOPE].reshape(L, M_QL, MH * M_NOPE)
    rope = w[..., M_NOPE:]
    parts = [nope, rope.reshape(L, M_QL, MH * M_ROPE)]
    if with_swap:
        parts.append(rope.reshape(L, M_QL, MH, M_ROPE // 2, 2)[..., ::-1].reshape(L, M_QL, MH * M_ROPE))
    return jnp.concatenate(parts, axis=2).astype(BF)


def _reorder_w_kv(w_kv_up):
    L = w_kv_up.shape[0]
    w = w_kv_up.reshape(L, M_KVL, MH, M_NOPE + M_V)
    return jnp.concatenate([w[..., :M_NOPE].reshape(L, M_KVL, MH * M_NOPE),
                            w[..., M_NOPE:].reshape(L, M_KVL, MH * M_V)], axis=2).astype(BF)


def kernel(x_prompt, x_sample, state_gla_fwd, state_gla_bwd, cache_mla_ckv, cache_mla_krope, c, c_ctx, w_mod, b_mod, g_norm1, g_norm2, w_in, w_gla_gate_f, b_gla_gate_f, w_gla_gate_b, b_gla_gate_b, g_gla_out, g_q_a, w_q_up, g_kv_a, w_kv_up, w_br_fourier, w_br_gla, w_br_mla, w_out, w_router, b_router, w_exp_gate, w_exp_up, w_exp_down, w_sh_gate, w_sh_up, w_sh_down, g_final):
    nb, sl, _ = x_prompt.shape
    db, dl, _ = x_sample.shape
    L = w_mod.shape[0]
    t_ctx, t_lat = nb * sl, db * dl
    T = t_ctx + t_lat
    assert sl % G_CHUNK == 0 and dl % G_CHUNK == 0 and dl % GRID_W == 0
    assert t_ctx % dl == 0 and dl % TOKEN_BLOCK == 0 and TOKEN_BLOCK % DISP_BLOCK == 0
    assert t_ctx % TOKEN_BLOCK == 0 and dl % Q_BLOCK == 0 and 1 + db <= 8

    x = jnp.concatenate([x_prompt.reshape(t_ctx, D), x_sample.reshape(t_lat, D)], axis=0)
    cond = jnp.concatenate([c_ctx[None, :], c, jnp.zeros((7 - db, D), F32)], axis=0)
    mods_all = _modulation(cond, w_mod, b_mod)

    w_in_r = _reorder_w_in(w_in)
    wq_ctx = _reorder_w_q(w_q_up, False)
    wq_lat = _reorder_w_q(w_q_up, True)
    wkv_r = _reorder_w_kv(w_kv_up)
    lat_off = t_ctx // dl

    new_f, new_b, new_ckv, new_kr = [], [], [], []
    for l in range(L):
        mods = mods_all[l].reshape(8, 1, 6 * D)
        tok = (t_ctx // TOKEN_BLOCK, dl // TOKEN_BLOCK)
        zf, zqk, zvr, zmla, zsm, zg = _in_projection(x, mods, g_norm1[l][None, :], w_in_r[l], *tok)

        (f_c,) = _fourier(zf, sl, nb, 0)
        (f_l,) = _fourier(zf, dl, db, lat_off)

        gate_w = (w_gla_gate_f[l], b_gla_gate_f[l][None, :], w_gla_gate_b[l], b_gla_gate_b[l][None, :],
                  g_gla_out[l].reshape(1, GH * GDV))
        og_c, s_f, s_b = _gla(zqk, zvr, zsm, gate_w, sl, nb, 0, None)
        og_l, _, _ = _gla(zqk, zvr, zsm, gate_w, dl, db, lat_off, (state_gla_fwd[:, l], state_gla_bwd[:, l]))

        gq, gkv = g_q_a[l][None, :], g_kv_a[l][None, :]
        om_c, ckv = _mla(zmla, zsm, (gq, wq_ctx[l], gkv, wkv_r[l]), sl, nb, 0, None)
        (om_l,) = _mla(zmla, zsm, (gq, wq_lat[l], gkv, wkv_r[l]), dl, db, lat_off,
                       (cache_mla_ckv[:, l], cache_mla_krope[:, l]))

        mw = (w_br_fourier[l].astype(BF), w_br_gla[l].astype(BF), w_br_mla[l].astype(BF), w_out[l].astype(BF),
              g_norm2[l][None, :], w_router[l].T, b_router[l][:, None])
        xm, h2, pos8, gate8, cnt = _merge(x, (f_c, og_c, om_c), (f_l, og_l, om_l), zg, mods, mw, *tok)

        x = _moe(l, l == L - 1, h2, pos8, gate8, cnt, w_exp_gate, w_exp_up, w_exp_down, w_sh_gate, w_sh_up, w_sh_down,
                 xm, mods, g_final[None, :], (t_ctx // DISP_BLOCK, dl // DISP_BLOCK))

        new_f.append(s_f)
        new_b.append(s_b)
        new_ckv.append(ckv.reshape(nb, sl, M_KVL))
        new_kr.append(zsm[:t_ctx, 32:64].reshape(nb, sl, M_ROPE))

    y_prompt = x[:t_ctx].reshape(nb, sl, D)
    y_sample = x[t_ctx:].reshape(db, dl, D)
    return (y_prompt, y_sample, jnp.stack(new_f, axis=1), jnp.stack(new_b, axis=1),
            jnp.stack(new_ckv, axis=1), jnp.stack(new_kr, axis=1))
```

```python
import functools

import numpy as np
import jax
import jax.numpy as jnp
from jax import lax
from jax.experimental import pallas as pl
from jax.experimental.pallas import tpu as pltpu

F32 = jnp.float32
BF = jnp.bfloat16

D = 1024
GRID_W = 64
FN_G, FN_GW = 4, 96
FN_W = FN_G * FN_GW
GH, GDK, GDV = 4, 64, 128
G_RANK = 16
G_TAU = 16.0
G_CHUNK = 64
MH, M_NOPE, M_ROPE, M_V = 8, 64, 32, 64
M_QL, M_KVL = 384, 256
ROPE_BASE = 10000.0
N_EXP, TOP_K, N_GRP, TOPK_GRP = 64, 8, 8, 4
E_DIM = 256
ROUTED_SCALE = 2.5
EPS = 1e-6

C_F = (0, 384)
C_QK = (384, 896)
C_VR = (896, 1920)
C_MLA = (1920, 2560)
C_SM = (2560, 2688)
C_G = (2688, 5760)
IN_COLS_R = 5760

VMEM_LIMIT_V7X = 56 * 1024 * 1024
TOKEN_BLOCK = 512
Q_BLOCK = 256
DISP_BLOCK = 256
ROW_GRANULE = 16
DISP_ROWS = -(-(TOP_K * DISP_BLOCK + N_EXP * (ROW_GRANULE - 1)) // 256) * 256
EXP_TILE = 512
CHUNKS_PER_TILE = EXP_TILE // ROW_GRANULE
BLOCK_CHUNKS = DISP_ROWS // ROW_GRANULE


def _params(*sem):
    return pltpu.CompilerParams(dimension_semantics=sem, vmem_limit_bytes=VMEM_LIMIT_V7X)


def _dot(a, b):
    return jnp.dot(a, b, preferred_element_type=F32)


def _dot_hi(a, b):
    return jnp.dot(a, b, precision=lax.Precision.HIGHEST, preferred_element_type=F32)


def _dot_nt(a, b, precision=None):
    return lax.dot_general(a, b, (((1,), (1,)), ((), ())), precision=precision, preferred_element_type=F32)


def _dot_tn(a, b):
    return lax.dot_general(a, b, (((0,), (0,)), ((), ())), preferred_element_type=F32)


def _sigmoid(x):
    return 1.0 / (1.0 + jnp.exp(-x))


def _rms(x, g):
    return x * lax.rsqrt(jnp.mean(x * x, axis=-1, keepdims=True) + EPS) * g


def _iota(shape, dim):
    return lax.broadcasted_iota(jnp.int32, shape, dim)


def _mod_row(n_ctx_blocks, blocks_per_lat, i):
    return jnp.where(i < n_ctx_blocks, 0, 1 + (i - n_ctx_blocks) // blocks_per_lat)


def _mod_kernel(c_ref, w_ref, b_ref, o_ref):
    c = c_ref[...]
    o_ref[0] = _dot_hi(c * _sigmoid(c), w_ref[0]) + b_ref[0]


def _modulation(cond, w_mod, b_mod):
    L = w_mod.shape[0]
    rows = cond.shape[0]
    tn = 1536
    return pl.pallas_call(
        _mod_kernel,
        grid=(L, 6 * D // tn),
        in_specs=[pl.BlockSpec((rows, D), lambda l, j: (0, 0)),
                  pl.BlockSpec((1, D, tn), lambda l, j: (l, 0, j)),
                  pl.BlockSpec((1, 1, tn), lambda l, j: (l, 0, j))],
        out_specs=pl.BlockSpec((1, rows, tn), lambda l, j: (l, 0, j)),
        out_shape=jax.ShapeDtypeStruct((L, rows, 6 * D), F32),
        compiler_params=_params("parallel", "parallel"),
        name="modulation",
    )(cond, w_mod, b_mod.reshape(L, 1, 6 * D))


def _inproj_kernel(n_ctx_blocks, xc_ref, xl_ref, m_ref, g_ref, w_ref, of_ref, oqk_ref, ovr_ref, omla_ref, osm_ref, og_ref):
    y = _rms(jnp.where(pl.program_id(0) < n_ctx_blocks, xc_ref[...], xl_ref[...]), g_ref[...])
    h = (y * (1.0 + m_ref[0, :, D:2 * D]) + m_ref[0, :, 0:D]).astype(BF)
    of_ref[...] = _dot(h, w_ref[:, C_F[0]:C_F[1]]).astype(BF)
    oqk_ref[...] = _dot(h, w_ref[:, C_QK[0]:C_QK[1]])
    ovr_ref[...] = _dot(h, w_ref[:, C_VR[0]:C_VR[1]]).astype(BF)
    omla_ref[...] = _dot(h, w_ref[:, C_MLA[0]:C_MLA[1]])
    osm_ref[...] = _dot(h, w_ref[:, C_SM[0]:C_SM[1]])
    og_ref[...] = _dot(h, w_ref[:, C_G[0]:C_G[1]]).astype(BF)


def _x_pair_specs(x_pair, tb, n_ctx_blocks):
    _, _, lat_off = x_pair
    return [pl.BlockSpec((tb, D), lambda i: (jnp.minimum(i, n_ctx_blocks - 1), 0)),
            pl.BlockSpec((tb, D), lambda i: (jnp.maximum(i - n_ctx_blocks, 0) + lat_off, 0))]


def _in_projection(x_pair, T, mods, g1, w_in_r, n_ctx_blocks, blocks_per_lat):
    tb = TOKEN_BLOCK
    row = functools.partial(_mod_row, n_ctx_blocks, blocks_per_lat)
    widths = [(C_F, BF), (C_QK, F32), (C_VR, BF), (C_MLA, F32), (C_SM, F32), (C_G, BF)]
    return pl.pallas_call(
        functools.partial(_inproj_kernel, n_ctx_blocks),
        grid=(T // tb,),
        in_specs=_x_pair_specs(x_pair, tb, n_ctx_blocks) + [
                  pl.BlockSpec((1, 1, 2 * D), lambda i: (row(i), 0, 0)),
                  pl.BlockSpec((1, D), lambda i: (0, 0)),
                  pl.BlockSpec((D, IN_COLS_R), lambda i: (0, 0))],
        out_specs=[pl.BlockSpec((tb, c[1] - c[0]), lambda i: (i, 0)) for c, _ in widths],
        out_shape=[jax.ShapeDtypeStruct((T, c[1] - c[0]), dt) for c, dt in widths],
        compiler_params=_params("parallel"),
        name="in_projection",
    )(x_pair[0], x_pair[1], mods, g1, w_in_r)


def _seq_call(kernel, name, n, nseq, blk_off, seq_ins, const_ins, out_widths, extra_outs=(), scratch=()):
    in_specs = [pl.BlockSpec((n, a.shape[1]), lambda i: (i + blk_off, 0)) for a in seq_ins]
    in_specs += [pl.BlockSpec(bs, im) for _, bs, im in const_ins]
    args = list(seq_ins) + [a for a, _, _ in const_ins]
    out_specs = [pl.BlockSpec((n, w), lambda i: (i, 0)) for w, _ in out_widths]
    out_shape = [jax.ShapeDtypeStruct((nseq * n, w), dt) for w, dt in out_widths]
    out_specs += [pl.BlockSpec(bs, im) for _, _, bs, im in extra_outs]
    out_shape += [jax.ShapeDtypeStruct(s, dt) for s, dt, _, _ in extra_outs]
    return pl.pallas_call(
        kernel, grid=(nseq,), in_specs=in_specs, out_specs=out_specs, out_shape=out_shape,
        scratch_shapes=list(scratch), compiler_params=_params("parallel"), name=name,
    )(*args)


def _fourier_kernel(u_ref, r_ref, lc_ref, ls_ref, o_ref):
    y = _dot(u_ref[...], r_ref[...])
    o_ref[...] = (_dot(lc_ref[...], y[:, :FN_W].astype(BF)) + _dot(ls_ref[...], y[:, FN_W:].astype(BF))).astype(BF)


def _dft_tables(n):
    k = np.arange(FN_GW)
    ang = 2.0 * np.pi * ((k[:, None] * k[None, :]) % FN_GW) / FN_GW
    eye = np.eye(FN_G)
    right = np.concatenate([np.kron(eye, np.cos(ang)), np.kron(eye, np.sin(ang))], axis=1)
    p = np.arange(n)
    angn = 2.0 * np.pi * ((p[:, None] * p[None, :]) % n) / n
    scale = 1.0 / np.sqrt(float(n * FN_GW))
    return (jnp.asarray(right, F32).astype(BF), jnp.asarray(np.cos(angn) * scale, F32).astype(BF),
            jnp.asarray(-np.sin(angn) * scale, F32).astype(BF))


def _fourier(zf, n, nseq, blk_off):
    right, lc, ls = _dft_tables(n)
    consts = [(right, (FN_W, 2 * FN_W), lambda i: (0, 0)), (lc, (n, n), lambda i: (0, 0)), (ls, (n, n), lambda i: (0, 0))]
    return _seq_call(_fourier_kernel, "fourier_mix", n, nseq, blk_off, [zf], consts, [(FN_W, BF)])


def _bf_parts(x, n):
    parts, rest = [], x
    for _ in range(n):
        p = rest.astype(BF)
        parts.append(p)
        rest = rest - p.astype(F32)
    return parts


def _dot_f32(a, b):
    a1, a2 = _bf_parts(a, 2)
    b1, b2 = _bf_parts(b, 2)
    return (_dot(a1, b2) + _dot(a2, b1)) + _dot(a1, b1)


def _cumulate(tri, g):
    g1, g2, g3 = _bf_parts(g, 3)
    return (_dot(tri, g3) + _dot(tri, g2)) + _dot(tri, g1)


def _log_gate(z, w_ref, b_ref):
    pre = _dot_f32(z, w_ref[...]) + b_ref[...]
    return (jnp.minimum(pre, 0.0) - jnp.log1p(jnp.exp(-jnp.abs(pre)))) * (1.0 / G_TAU)


def _gla_kernel(has_state, n, *refs):
    if has_state:
        (zqk, zvr, zsm, wgf, bgf, wgb, bgb, gout, s0f, s0b, o_ref, sf_ref, sb_ref,
         oacc_f, oacc_b, lg_f, lg_b, st_f, st_b) = refs
    else:
        (zqk, zvr, zsm, wgf, bgf, wgb, bgb, gout, o_ref, sf_ref, sb_ref,
         oacc_f, oacc_b, lg_f, lg_b, st_f, st_b) = refs
        s0f = s0b = None
    C = G_CHUNK
    nc = n // C
    ri, ci = _iota((C, C), 0), _iota((C, C), 1)
    t_idx, lane = _iota((C, 128), 0), _iota((C, 128), 1)
    s_idx = lane & (C - 1)
    left = lane < GDK
    vleft = _iota((C, 2 * GDV), 1) < GDV
    blockdiag = (_iota((2 * GDV, 2 * GDK), 0) >> 7) == (_iota((2 * GDV, 2 * GDK), 1) >> 6)

    def load_state(st, s0_ref):
        for p in range(2):
            if s0_ref is None:
                st[p] = jnp.zeros((2 * GDV, 2 * GDK), F32)
            else:
                z = jnp.zeros((GDK, GDV), F32)
                blk = jnp.concatenate([jnp.concatenate([s0_ref[0, 2 * p], z], axis=1),
                                       jnp.concatenate([z, s0_ref[0, 2 * p + 1]], axis=1)], axis=0)
                st[p] = blk.T

    def store_state(st, out_ref):
        for p in range(2):
            blk = st[p].T
            out_ref[0, 2 * p] = blk[0:GDK, 0:GDV]
            out_ref[0, 2 * p + 1] = blk[GDK:2 * GDK, GDV:2 * GDV]

    def chunk(c, reverse, lg, st, oacc):
        tri = (ci >= ri).astype(BF) if reverse else (ci <= ri).astype(BF)
        mask = (s_idx >= t_idx) if reverse else (s_idx <= t_idx)
        rows = pl.ds(pl.multiple_of(c * C, C), C)
        cum = _cumulate(tri, lg[rows, :])
        tot = cum[0:1] if reverse else cum[C - 1:C]
        q = zqk[rows, 0:GH * GDK] * (GDK ** -0.5)
        k = zqk[rows, GH * GDK:2 * GH * GDK]
        qh = (q * jnp.exp(cum)).astype(BF)
        kh = (k * jnp.exp(-cum)).astype(BF)
        kb = (k * jnp.exp(tot - cum)).astype(BF)
        dec = jnp.exp(tot)
        for p in range(2):
            ls = slice(128 * p, 128 * p + 128)
            vs = slice(256 * p, 256 * p + 256)
            qp, kp = qh[:, ls], kh[:, ls]
            zk = jnp.zeros_like(kp)
            kblk = jnp.concatenate([jnp.where(left, kp, zk), jnp.where(left, zk, kp)], axis=0)
            sc = jnp.where(mask, _dot_nt(qp, kblk), 0.0).astype(BF)
            vp = zvr[rows, vs]
            zv = jnp.zeros_like(vp)
            vblk = jnp.concatenate([jnp.where(vleft, vp, zv), jnp.where(vleft, zv, vp)], axis=0)
            stp = st[p]
            oacc[rows, vs] = _dot(sc, vblk) + _dot_nt(qp, stp.astype(BF))
            st[p] = dec[:, ls] * stp + jnp.where(blockdiag, _dot_tn(vp, kb[:, ls]), 0.0)

    def both_directions(step, carry):
        chunk(step, False, lg_f, st_f, oacc_f)
        chunk(nc - 1 - step, True, lg_b, st_b, oacc_b)
        return carry

    lg_f[...] = _log_gate(zsm[:, 0:G_RANK], wgf, bgf)
    lg_b[...] = _log_gate(zsm[:, G_RANK:2 * G_RANK], wgb, bgb)
    load_state(st_f, s0f)
    load_state(st_b, s0b)
    lax.fori_loop(0, nc, both_directions, 0, unroll=min(nc, 4))
    store_state(st_f, sf_ref)
    store_state(st_b, sb_ref)

    rb = 128
    for r0 in range(0, n, rb):
        for h in range(GH):
            hs = slice(GDV * h, GDV * h + GDV)
            oh = oacc_f[r0:r0 + rb, hs] + oacc_b[r0:r0 + rb, hs]
            oh = oh * lax.rsqrt(jnp.mean(oh * oh, axis=-1, keepdims=True) + EPS) * gout[:, hs]
            r = zvr[r0:r0 + rb, GH * GDV + hs.start:GH * GDV + hs.stop].astype(F32)
            o_ref[r0:r0 + rb, hs] = (oh * (r * _sigmoid(r))).astype(BF)


def _gla(zqk, zvr, zsm, gate_w, n, nseq, blk_off, states):
    wgf, bgf, wgb, bgb, gout = gate_w
    c2 = lambda i: (0, 0)
    consts = [(wgf, wgf.shape, c2), (bgf, bgf.shape, c2), (wgb, wgb.shape, c2), (bgb, bgb.shape, c2), (gout, gout.shape, c2)]
    st_blk = (1, GH, GDK, GDV)
    st_map = lambda i: (i, 0, 0, 0)
    if states is not None:
        consts += [(s, st_blk, st_map) for s in states]
    extra = [((nseq, GH, GDK, GDV), F32, st_blk, st_map)] * 2
    scratch = ([pltpu.VMEM((n, GH * GDV), F32)] * 2 + [pltpu.VMEM((n, GH * GDK), F32)] * 2
               + [pltpu.VMEM((2, 2 * GDV, 2 * GDK), F32)] * 2)
    return _seq_call(functools.partial(_gla_kernel, states is not None, n), "gla_mixer", n, nseq, blk_off,
                     [zqk, zvr, zsm], consts, [(GH * GDV, BF)], extra_outs=extra, scratch=scratch)


def _mla_kernel(latent, n, past, *refs):
    if latent:
        (zmla, zsm, gq, wq, gkv, wkv, cckv, ckr, cosq, sinq, cosk, sink, o_ref, qs, kns, vs, krs) = refs
    else:
        (zmla, zsm, gq, wq, gkv, wkv, o_ref, ckv_ref, qs, kns, vs, krs) = refs
    sk = past + n
    scale = (M_NOPE + M_ROPE) ** -0.5
    nw, rw = MH * M_NOPE, MH * M_ROPE
    qa = _dot(_rms(zmla[:, 0:M_QL], gq[...]).astype(BF), wq[...])
    qr = qa[:, nw:nw + rw]
    if latent:
        qr = qr * cosq[...] + qa[:, nw + rw:nw + 2 * rw] * sinq[...]
    qs[:, 0:nw] = qa[:, 0:nw] * scale
    qs[:, nw:nw + rw] = qr * scale
    ckv = _rms(zmla[:, M_QL:M_QL + M_KVL], gkv[...])
    kv = _dot(ckv.astype(BF), wkv[...])
    kr = zsm[:, 32:64]
    if latent:
        kr = kr * cosk[...] + zsm[:, 64:96] * sink[...]
        kvc = _dot(cckv[0].astype(BF), wkv[...])
        kns[0:past, :] = kvc[:, 0:nw].astype(BF)
        vs[0:past, :] = kvc[:, nw:].astype(BF)
        krs[0:past, :] = jnp.concatenate([ckr[0]] * 4, axis=1).astype(BF)
    else:
        ckv_ref[...] = ckv
    kns[past:sk, :] = kv[:, 0:nw].astype(BF)
    vs[past:sk, :] = kv[:, nw:].astype(BF)
    krs[past:sk, :] = jnp.concatenate([kr] * 4, axis=1).astype(BF)

    qb = min(Q_BLOCK, n)
    lane = _iota((qb, 128), 1)

    def block(step, carry):
        rows = pl.ds(pl.multiple_of(step * qb, qb), qb)
        for p in range(MH // 2):
            ls = slice(128 * p, 128 * p + 128)
            qn = qs[rows, ls]
            quad = (2 * p) // 4
            qrp = qs[rows, nw + 128 * quad:nw + 128 * quad + 128]
            rhs = jnp.concatenate([kns[:, ls], krs[...]], axis=1)
            vp = vs[:, ls]
            o_pair = None
            for hh in range(2):
                j = (2 * p + hh) % 4
                qn_m = jnp.where((lane >> 6) == hh, qn, 0.0).astype(BF)
                qr_m = jnp.where((lane >> 5) == j, qrp, 0.0).astype(BF)
                s = _dot_nt(jnp.concatenate([qn_m, qr_m], axis=1), rhs)
                e = jnp.exp(s - jnp.max(s, axis=-1, keepdims=True))
                pv = _dot(e.astype(BF), vp) / jnp.sum(e, axis=-1, keepdims=True)
                o_pair = pv if hh == 0 else jnp.where(lane < M_V, o_pair, pv)
            o_ref[rows, ls] = o_pair.astype(BF)
        return carry

    lax.fori_loop(0, n // qb, block, 0)


def _rope_tables(n):
    half = M_ROPE // 2
    pos = jnp.arange(n)
    row = (pos // GRID_W).astype(F32)
    col = (pos % GRID_W).astype(F32)
    inv = ROPE_BASE ** (-jnp.arange(0, half, 2, dtype=F32) / half)
    ang = jnp.concatenate([row[:, None] * inv, col[:, None] * inv], axis=-1)
    cos = jnp.repeat(jnp.cos(ang), 2, axis=-1)
    sin = jnp.repeat(jnp.sin(ang), 2, axis=-1) * jnp.tile(jnp.asarray([-1.0, 1.0], F32), half)
    return jnp.tile(cos, (1, MH)), jnp.tile(sin, (1, MH)), cos, sin


def _mla(zmla, zsm, w, n, nseq, blk_off, cache):
    gq, wq, gkv, wkv = w
    c2 = lambda i: (0, 0)
    consts = [(gq, gq.shape, c2), (wq, wq.shape, c2), (gkv, gkv.shape, c2), (wkv, wkv.shape, c2)]
    past = 0
    extra = []
    if cache is not None:
        cckv, ckr = cache
        past = cckv.shape[1]
        c3 = lambda i: (i, 0, 0)
        consts += [(cckv, (1, past, M_KVL), c3), (ckr, (1, past, M_ROPE), c3)]
        consts += [(t, t.shape, c2) for t in _rope_tables(n)]
    else:
        extra = [((nseq * n, M_KVL), F32, (n, M_KVL), lambda i: (i, 0))]
    sk = past + n
    scratch = [pltpu.VMEM((n, MH * (M_NOPE + M_ROPE)), F32), pltpu.VMEM((sk, MH * M_NOPE), BF),
               pltpu.VMEM((sk, MH * M_V), BF), pltpu.VMEM((sk, 128), BF)]
    return _seq_call(functools.partial(_mla_kernel, cache is not None, n, past), "mla_mixer", n, nseq, blk_off,
                     [zmla, zsm], consts, [(MH * M_V, BF)], extra_outs=extra, scratch=scratch)


def _route(logits_t, bias):
    nt = logits_t.shape[1]
    gsz = N_EXP // N_GRP
    scores = _sigmoid(logits_t)
    sel = scores + bias
    neg = -jnp.inf
    sub = _iota((gsz, nt), 0)
    tops = []
    for g in range(N_GRP):
        blk = sel[gsz * g:gsz * g + gsz]
        m1 = jnp.max(blk, axis=0, keepdims=True)
        first = jnp.min(jnp.where(blk == m1, sub, gsz), axis=0, keepdims=True)
        m2 = jnp.max(jnp.where(sub == first, neg, blk), axis=0, keepdims=True)
        tops.append(m1 + m2)
    gs = jnp.concatenate(tops, axis=0)
    gidx = _iota((N_GRP, nt), 0)
    grank = jnp.zeros((N_GRP, nt), jnp.int32)
    for j in range(N_GRP):
        rj = gs[j:j + 1]
        grank += ((rj > gs) | ((rj == gs) & (gidx > j))).astype(jnp.int32)
    keep = grank < TOPK_GRP
    masked = jnp.concatenate(
        [jnp.where(jnp.broadcast_to(keep[g:g + 1], (gsz, nt)), sel[gsz * g:gsz * g + gsz], neg) for g in range(N_GRP)], axis=0)
    eidx = _iota((N_EXP, nt), 0)
    chosen = eidx < 0
    work = masked
    for _ in range(TOP_K):
        top = jnp.max(work, axis=0, keepdims=True)
        first = jnp.min(jnp.where(work == top, eidx, N_EXP), axis=0, keepdims=True)
        hit = eidx == first
        chosen = chosen | hit
        work = jnp.where(hit, neg, work)
    w = jnp.where(chosen, scores, 0.0)
    return chosen, w / jnp.sum(w, axis=0, keepdims=True) * ROUTED_SCALE


def _dispatch_meta(chosen, gates_t):
    tb = chosen.shape[1]
    sel = chosen.astype(F32)
    selb = sel.astype(BF)
    earlier = (_iota((tb, tb), 0) < _iota((tb, tb), 1)).astype(BF)
    rank = _dot(selb, earlier)
    cnt = jnp.sum(sel, axis=1, keepdims=True)
    padded = jnp.floor((cnt + (ROW_GRANULE - 1)) * (1.0 / ROW_GRANULE)) * ROW_GRANULE
    below = (_iota((N_EXP, N_EXP), 1) < _iota((N_EXP, N_EXP), 0)).astype(BF)
    start = _dot(below, jnp.broadcast_to(padded, (N_EXP, 128)).astype(BF))[:, 0:1]
    pos = start + rank
    kidx = _dot(below, selb)
    pos8, gate8 = [], []
    for k in range(TOP_K):
        hit = chosen & (kidx == float(k))
        pos8.append(jnp.sum(jnp.where(hit, pos, 0.0), axis=0, keepdims=True))
        gate8.append(jnp.sum(jnp.where(hit, gates_t, 0.0), axis=0, keepdims=True))
    return (jnp.concatenate(pos8, axis=0).astype(jnp.int32), jnp.concatenate(gate8, axis=0), cnt)


def _merge_kernel(n_ctx_blocks, xc_ref, xl_ref, fc_ref, fl_ref, ogc_ref, ogl_ref, omc_ref, oml_ref, zg_ref, m_ref,
                  wbf, wbg, wbm, wout, gn2, wrt, brt, xm_ref, h2_ref, pos_ref, gate_ref, cnt_ref):
    is_ctx = pl.program_id(0) < n_ctx_blocks
    ya = _dot(jnp.where(is_ctx, fc_ref[...], fl_ref[...]), wbf[...])
    yb = _dot(jnp.where(is_ctx, ogc_ref[...], ogl_ref[...]), wbg[...])
    yc = _dot(jnp.where(is_ctx, omc_ref[...], oml_ref[...]), wbm[...])
    merged = (_sigmoid(zg_ref[:, 0:D]) * ya.astype(BF) + _sigmoid(zg_ref[:, D:2 * D]) * yb.astype(BF)
              + _sigmoid(zg_ref[:, 2 * D:3 * D]) * yc.astype(BF))
    xm = jnp.where(is_ctx, xc_ref[...], xl_ref[...]) + m_ref[0, :, 2 * D:3 * D] * _dot(merged, wout[...])
    xm_ref[...] = xm
    h2 = _rms(xm, gn2[...]) * (1.0 + m_ref[0, :, 4 * D:5 * D]) + m_ref[0, :, 3 * D:4 * D]
    h2_ref[...] = h2.astype(BF)
    chosen, gates_t = _route(_dot_nt(wrt[...], h2, precision=lax.Precision.HIGHEST), brt[...])
    for sb in range(gates_t.shape[1] // DISP_BLOCK):
        ls = slice(sb * DISP_BLOCK, (sb + 1) * DISP_BLOCK)
        pos8, gate8, cnt = _dispatch_meta(chosen[:, ls], gates_t[:, ls])
        pos_ref[:, ls] = pos8
        gate_ref[:, ls] = gate8
        cnt_ref[sb] = jnp.broadcast_to(cnt, (N_EXP, 128))


def _merge(x_pair, T, mix_ctx, mix_lat, zg, mods, w, n_ctx_blocks, blocks_per_lat):
    tb = TOKEN_BLOCK
    row = functools.partial(_mod_row, n_ctx_blocks, blocks_per_lat)
    rb = lambda wd: pl.BlockSpec((tb, wd), lambda i: (i, 0))
    cb = lambda a: pl.BlockSpec(a.shape, lambda i: (0, 0))
    ctx_b = lambda wd: pl.BlockSpec((tb, wd), lambda i: (jnp.minimum(i, n_ctx_blocks - 1), 0))
    lat_b = lambda wd: pl.BlockSpec((tb, wd), lambda i: (jnp.maximum(i - n_ctx_blocks, 0), 0))
    mix_specs, mix_args = [], []
    for a_c, a_l in zip(mix_ctx, mix_lat):
        mix_specs += [ctx_b(a_c.shape[1]), lat_b(a_l.shape[1])]
        mix_args += [a_c, a_l]
    return pl.pallas_call(
        functools.partial(_merge_kernel, n_ctx_blocks),
        grid=(T // tb,),
        in_specs=_x_pair_specs(x_pair, tb, n_ctx_blocks) + mix_specs + [rb(3 * D),
                  pl.BlockSpec((1, 1, 6 * D), lambda i: (row(i), 0, 0))] + [cb(a) for a in w],
        out_specs=[rb(D), rb(D), pl.BlockSpec((TOP_K, tb), lambda i: (0, i)), pl.BlockSpec((TOP_K, tb), lambda i: (0, i)),
                   pl.BlockSpec((tb // DISP_BLOCK, N_EXP, 128), lambda i: (i, 0, 0))],
        out_shape=[jax.ShapeDtypeStruct((T, D), F32), jax.ShapeDtypeStruct((T, D), BF),
                   jax.ShapeDtypeStruct((TOP_K, T), jnp.int32), jax.ShapeDtypeStruct((TOP_K, T), F32),
                   jax.ShapeDtypeStruct((T // DISP_BLOCK, N_EXP, 128), F32)],
        compiler_params=_params("parallel"),
        name="merge_route",
    )(x_pair[0], x_pair[1], *mix_args, zg, mods, *w)


def _silu_mul(a, b):
    return a * _sigmoid(a) * b


def _dispatch_kernel(n_blocks, h_ref, pos_ref, gate_ref, xs_ref, gt_ref):
    h = h_ref[...]
    live = (pl.program_id(0) < n_blocks).astype(F32)
    rc = 256
    for r0 in range(0, DISP_ROWS, rc):
        rows = _iota((rc, DISP_BLOCK), 0) + r0
        gt = jnp.zeros((rc, DISP_BLOCK), F32)
        for k in range(TOP_K):
            gt = jnp.where(rows == pos_ref[k:k + 1, :], gate_ref[k:k + 1, :] * live, gt)
        gt_ref[r0:r0 + rc, :] = gt.astype(BF)
        place = jnp.where(gt != 0.0, 1.0, 0.0).astype(BF)
        xs_ref[r0:r0 + rc, :] = _dot(place, h).astype(BF)


def _dispatch(h2, pos8, gate8):
    T = h2.shape[0]
    nblk = T // DISP_BLOCK
    last = lambda b: jnp.minimum(b, nblk - 1)
    return pl.pallas_call(
        functools.partial(_dispatch_kernel, nblk),
        grid=(nblk + 1,),
        in_specs=[pl.BlockSpec((DISP_BLOCK, D), lambda b: (last(b), 0)),
                  pl.BlockSpec((TOP_K, DISP_BLOCK), lambda b: (0, last(b))),
                  pl.BlockSpec((TOP_K, DISP_BLOCK), lambda b: (0, last(b)))],
        out_specs=[pl.BlockSpec((DISP_ROWS, D), lambda b: (b, 0)), pl.BlockSpec((DISP_ROWS, DISP_BLOCK), lambda b: (b, 0))],
        out_shape=[jax.ShapeDtypeStruct(((nblk + 1) * DISP_ROWS, D), BF),
                   jax.ShapeDtypeStruct(((nblk + 1) * DISP_ROWS, DISP_BLOCK), BF)],
        compiler_params=_params("parallel"),
        name="moe_dispatch",
    )(h2, pos8, gate8)


def _tile_tables(cnt, n_chunks_max, n_tiles_max):
    nblk = cnt.shape[0]
    nch = (cnt + (ROW_GRANULE - 1)) // ROW_GRANULE
    first = jnp.arange(nblk, dtype=jnp.int32)[:, None] * BLOCK_CHUNKS + jnp.cumsum(nch, axis=1) - nch
    tiles_e = (jnp.sum(nch, axis=0) + (CHUNKS_PER_TILE - 1)) // CHUNKS_PER_TILE
    span_e = tiles_e * CHUNKS_PER_TILE
    exp_start = jnp.cumsum(span_e) - span_e
    j = jnp.arange(n_chunks_max, dtype=jnp.int32)
    e_j = jnp.sum((exp_start[None, :] <= j[:, None]).astype(jnp.int32), axis=1) - 1
    onehot = (e_j[:, None] == jnp.arange(N_EXP, dtype=jnp.int32)[None, :]).astype(F32)
    rows_of = lambda tab: jnp.dot(onehot, tab.astype(F32), precision=lax.Precision.HIGHEST).astype(jnp.int32)
    local = j - rows_of(exp_start[:, None])[:, 0]
    blk_len = rows_of(nch.T)
    blk_start = rows_of(jnp.cumsum(nch.T, axis=1) - nch.T)
    inside = (blk_start <= local[:, None]) & (local[:, None] < blk_start + blk_len)
    real = jnp.any(inside, axis=1)
    chunk = jnp.sum(jnp.where(inside, rows_of(first.T) + (local[:, None] - blk_start), 0), axis=1)
    pad_rank = jnp.cumsum(jnp.where(real, 0, 1)) - 1
    src = jnp.where(real, chunk, BLOCK_CHUNKS - 1).astype(jnp.int32)
    dst = jnp.where(real, chunk, nblk * BLOCK_CHUNKS + pad_rank % BLOCK_CHUNKS).astype(jnp.int32)
    tile_end = jnp.cumsum(tiles_e)
    i = jnp.arange(n_tiles_max, dtype=jnp.int32)
    tile_expert = jnp.minimum(jnp.sum((tile_end[None, :] <= i[:, None]).astype(jnp.int32), axis=1), N_EXP - 1)
    return src, dst, tile_expert.astype(jnp.int32), tile_end[-1:].astype(jnp.int32)


def _expert_kernel(src_ref, dst_ref, texp_ref, nused_ref, xs_hbm, wg_ref, wu_ref, wd_ref, ys_hbm,
                   xbuf, ybuf, wgu_bf, wd_bf, gsem, ssem):
    i = pl.program_id(0)
    n_used = nused_ref[0]
    slot = lax.rem(i, 2)

    def chunk_copies(tile, slot_, to_buffer, do):
        for c in range(CHUNKS_PER_TILE):
            j = tile * CHUNKS_PER_TILE + c
            if to_buffer:
                cp = pltpu.make_async_copy(xs_hbm.at[src_ref[j]], xbuf.at[slot_, c], gsem.at[slot_])
            else:
                cp = pltpu.make_async_copy(ybuf.at[slot_, c], ys_hbm.at[dst_ref[j]], ssem.at[slot_])
            do(cp)

    start = lambda cp: cp.start()
    wait = lambda cp: cp.wait()

    @pl.when(i == 0)
    def _():
        chunk_copies(0, 0, True, start)

    @pl.when(i == n_used)
    def _():
        chunk_copies(i, slot, True, wait)

    @pl.when(i < n_used)
    def _():
        chunk_copies(i, slot, True, wait)
        chunk_copies(i + 1, 1 - slot, True, start)

        @pl.when((i == 0) | (texp_ref[i] != texp_ref[jnp.maximum(i - 1, 0)]))
        def _():
            wgu_bf[:, 0:E_DIM] = wg_ref[0, 0].astype(BF)
            wgu_bf[:, E_DIM:2 * E_DIM] = wu_ref[0, 0].astype(BF)
            wd_bf[...] = wd_ref[0, 0].astype(BF)

        gu = _dot(xbuf[slot].reshape(EXP_TILE, D), wgu_bf[...])
        hid = _silu_mul(gu[:, 0:E_DIM], gu[:, E_DIM:2 * E_DIM])
        ybuf[slot] = _dot(hid.astype(BF), wd_bf[...]).astype(BF).reshape(CHUNKS_PER_TILE, ROW_GRANULE, D)
        chunk_copies(i, slot, False, start)

        @pl.when(i >= 1)
        def _():
            chunk_copies(i - 1, 1 - slot, False, wait)

        @pl.when(i == n_used - 1)
        def _():
            chunk_copies(i, slot, False, wait)


def _experts(layer, xs, tables, w_eg, w_eu, w_ed, n_tiles_max):
    src, dst, tile_expert, n_used = tables
    chunks = xs.reshape(-1, ROW_GRANULE, D)
    wmap =lambda i, src_, dst_, texp, nu: (layer, texp[i], 0, 0)
    grid_spec = pltpu.PrefetchScalarGridSpec(
        num_scalar_prefetch=4,
        grid=(n_tiles_max + 1,),
        in_specs=[pl.BlockSpec(memory_space=pl.ANY),
                  pl.BlockSpec((1, 1, D, E_DIM), wmap),
                  pl.BlockSpec((1, 1, D, E_DIM), wmap),
                  pl.BlockSpec((1, 1, E_DIM, D), wmap)],
        out_specs=pl.BlockSpec(memory_space=pl.ANY),
        scratch_shapes=[pltpu.VMEM((2, CHUNKS_PER_TILE, ROW_GRANULE, D), BF),
                        pltpu.VMEM((2, CHUNKS_PER_TILE, ROW_GRANULE, D), BF),
                        pltpu.VMEM((D, 2 * E_DIM), BF), pltpu.VMEM((E_DIM, D), BF),
                        pltpu.SemaphoreType.DMA((2,)), pltpu.SemaphoreType.DMA((2,))],
    )
    return pl.pallas_call(
        _expert_kernel,
        grid_spec=grid_spec,
        out_shape=jax.ShapeDtypeStruct(chunks.shape, chunks.dtype),
        input_output_aliases={4: 0},
        compiler_params=_params("arbitrary"),
        name="moe_experts",
    )(src, dst, tile_expert, n_used, chunks, w_eg, w_eu, w_ed).reshape(xs.shape)


def _combine_kernel(final, ys_ref, gt_ref, h_ref, sg_ref, su_ref, sd_ref, x_ref, m_ref, gf_ref, o_ref):
    routed = _dot_tn(gt_ref[...], ys_ref[...])
    h = h_ref[...]
    sh = _silu_mul(_dot(h, sg_ref[0].astype(BF)), _dot(h, su_ref[0].astype(BF)))
    out = x_ref[...] + m_ref[0] * (routed + _dot(sh.astype(BF), sd_ref[0].astype(BF)))
    if final:
        out = _rms(out, gf_ref[...])
    o_ref[...] = out


def _combine(layer, final, ys, gt, h2, w_sg, w_su, w_sd, xm, mods, g_final, n_ctx_blocks, blocks_per_lat):
    T = h2.shape[0]
    tb = DISP_BLOCK
    row = functools.partial(_mod_row, n_ctx_blocks, blocks_per_lat)
    return pl.pallas_call(
        functools.partial(_combine_kernel, final),
        grid=(T // tb,),
        in_specs=[pl.BlockSpec((DISP_ROWS, D), lambda b: (b, 0)),
                  pl.BlockSpec((DISP_ROWS, tb), lambda b: (b, 0)),
                  pl.BlockSpec((tb, D), lambda b: (b, 0)),
                  pl.BlockSpec((1, D, E_DIM), lambda b: (layer, 0, 0)),
                  pl.BlockSpec((1, D, E_DIM), lambda b: (layer, 0, 0)),
                  pl.BlockSpec((1, E_DIM, D), lambda b: (layer, 0, 0)),
                  pl.BlockSpec((tb, D), lambda b: (b, 0)),
                  pl.BlockSpec((1, 1, D), lambda b: (row(b), 0, 5)),
                  pl.BlockSpec((1, D), lambda b: (0, 0))],
        out_specs=pl.BlockSpec((tb, D), lambda b: (b, 0)),
        out_shape=jax.ShapeDtypeStruct((T, D), F32),
        compiler_params=_params("parallel"),
        name="moe_combine",
    )(ys, gt, h2, w_sg, w_su, w_sd, xm, mods, g_final)


def _moe(layer, final, h2, pos8, gate8, cnt, w_eg, w_eu, w_ed, w_sg, w_su, w_sd, xm, mods, g_final, disp_blocks):
    T = h2.shape[0]
    nblk = T // DISP_BLOCK
    n_chunks_max = (TOP_K * T + N_EXP * nblk * (ROW_GRANULE - 1)) // ROW_GRANULE + N_EXP * (CHUNKS_PER_TILE - 1)
    n_tiles_max = -(-n_chunks_max // CHUNKS_PER_TILE)
    xs, gt = _dispatch(h2, pos8, gate8)
    tables = _tile_tables(cnt[:, :, 0].astype(jnp.int32), (n_tiles_max + 1) * CHUNKS_PER_TILE, n_tiles_max + 1)
    ys = _experts(layer, xs, tables, w_eg, w_eu, w_ed, n_tiles_max)
    return _combine(layer, final, ys, gt, h2, w_sg, w_su, w_sd, xm, mods, g_final, *disp_blocks)


def _reorder_w_in(w_in):
    kr = w_in[:, :, 2592:2624]
    kr_sw = kr.reshape(kr.shape[0], D, M_ROPE // 2, 2)[..., ::-1].reshape(kr.shape)
    pad = jnp.zeros(kr.shape, w_in.dtype)
    out = jnp.concatenate([w_in[:, :, :1920], w_in[:, :, 1952:2592], w_in[:, :, 1920:1952], kr, kr_sw, pad,
                           w_in[:, :, 2624:]], axis=2)
    return out.astype(BF)


def _reorder_w_q(w_q_up, with_swap):
    L = w_q_up.shape[0]
    w = w_q_up.reshape(L, M_QL, MH, M_NOPE + M_ROPE)
    nope = w[..., :M_NOPE].reshape(L, M_QL, MH * M_NOPE)
    rope = w[..., M_NOPE:]
    parts = [nope, rope.reshape(L, M_QL, MH * M_ROPE)]
    if with_swap:
        parts.append(rope.reshape(L, M_QL, MH, M_ROPE // 2, 2)[..., ::-1].reshape(L, M_QL, MH * M_ROPE))
    return jnp.concatenate(parts, axis=2).astype(BF)


def _reorder_w_kv(w_kv_up):
    L = w_kv_up.shape[0]
    w = w_kv_up.reshape(L, M_KVL, MH, M_NOPE + M_V)
    return jnp.concatenate([w[..., :M_NOPE].reshape(L, M_KVL, MH * M_NOPE),
                            w[..., M_NOPE:].reshape(L, M_KVL, MH * M_V)], axis=2).astype(BF)


def kernel(x_prompt, x_sample, state_gla_fwd, state_gla_bwd, cache_mla_ckv, cache_mla_krope, c, c_ctx, w_mod, b_mod, g_norm1, g_norm2, w_in, w_gla_gate_f, b_gla_gate_f, w_gla_gate_b, b_gla_gate_b, g_gla_out, g_q_a, w_q_up, g_kv_a, w_kv_up, w_br_fourier, w_br_gla, w_br_mla, w_out, w_router, b_router, w_exp_gate, w_exp_up, w_exp_down, w_sh_gate, w_sh_up, w_sh_down, g_final):
    nb, sl, _ = x_prompt.shape
    db, dl, _ = x_sample.shape
    L = w_mod.shape[0]
    t_ctx, t_lat = nb * sl, db * dl
    T = t_ctx + t_lat
    assert sl % G_CHUNK == 0 and dl % G_CHUNK == 0 and dl % GRID_W == 0
    assert t_ctx % dl == 0 and dl % TOKEN_BLOCK == 0 and TOKEN_BLOCK % DISP_BLOCK == 0
    assert t_ctx % TOKEN_BLOCK == 0 and dl % Q_BLOCK == 0 and 1 + db <= 8

    x_pair = (x_prompt.reshape(t_ctx, D), x_sample.reshape(t_lat, D), 0)
    cond = jnp.concatenate([c_ctx[None, :], c, jnp.zeros((7 - db, D), F32)], axis=0)
    mods_all = _modulation(cond, w_mod, b_mod)

    w_in_r = _reorder_w_in(w_in)
    wq_ctx = _reorder_w_q(w_q_up, False)
    wq_lat = _reorder_w_q(w_q_up, True)
    wkv_r = _reorder_w_kv(w_kv_up)
    lat_off = t_ctx // dl

    new_f, new_b, new_ckv, new_kr = [], [], [], []
    for l in range(L):
        mods = mods_all[l].reshape(8, 1, 6 * D)
        tok = (t_ctx // TOKEN_BLOCK, dl // TOKEN_BLOCK)
        zf, zqk, zvr, zmla, zsm, zg = _in_projection(x_pair, T, mods, g_norm1[l][None, :], w_in_r[l], *tok)

        (f_c,) = _fourier(zf, sl, nb, 0)
        (f_l,) = _fourier(zf, dl, db, lat_off)

        gate_w = (w_gla_gate_f[l], b_gla_gate_f[l][None, :], w_gla_gate_b[l], b_gla_gate_b[l][None, :],
                  g_gla_out[l].reshape(1, GH * GDV))
        og_c, s_f, s_b = _gla(zqk, zvr, zsm, gate_w, sl, nb, 0, None)
        og_l, _, _ = _gla(zqk, zvr, zsm, gate_w, dl, db, lat_off, (state_gla_fwd[:, l], state_gla_bwd[:, l]))

        gq, gkv = g_q_a[l][None, :], g_kv_a[l][None, :]
        om_c, ckv = _mla(zmla, zsm, (gq, wq_ctx[l], gkv, wkv_r[l]), sl, nb, 0, None)
        (om_l,) = _mla(zmla, zsm, (gq, wq_lat[l], gkv, wkv_r[l]), dl, db, lat_off,
                       (cache_mla_ckv[:, l], cache_mla_krope[:, l]))

        mw = (w_br_fourier[l].astype(BF), w_br_gla[l].astype(BF), w_br_mla[l].astype(BF), w_out[l].astype(BF),
              g_norm2[l][None, :], w_router[l].T, b_router[l][:, None])
        xm, h2, pos8, gate8, cnt = _merge(x_pair, T, (f_c, og_c, om_c), (f_l, og_l, om_l), zg, mods, mw, *tok)

        x = _moe(l, l == L - 1, h2, pos8, gate8, cnt, w_exp_gate, w_exp_up, w_exp_down, w_sh_gate, w_sh_up, w_sh_down,
                 xm, mods, g_final[None, :], (t_ctx // DISP_BLOCK, dl // DISP_BLOCK))
        x_pair = (x, x, t_ctx // TOKEN_BLOCK)

        new_f.append(s_f)
        new_b.append(s_b)
        new_ckv.append(ckv.reshape(nb, sl, M_KVL))
        new_kr.append(zsm[:t_ctx, 32:64].reshape(nb, sl, M_ROPE))

    y_prompt = x[:t_ctx].reshape(nb, sl, D)
    y_sample = x[t_ctx:].reshape(db, dl, D)
    return (y_prompt, y_sample, jnp.stack(new_f, axis=1), jnp.stack(new_b, axis=1),
            jnp.stack(new_ckv, axis=1), jnp.stack(new_kr, axis=1))
```

```python
import functools

import numpy as np
import jax
import jax.numpy as jnp
from jax import lax
from jax.experimental import pallas as pl
from jax.experimental.pallas import tpu as pltpu

F32 = jnp.float32
BF = jnp.bfloat16

D = 1024
GRID_W = 64
FN_G, FN_GW = 4, 96
FN_W = FN_G * FN_GW
GH, GDK, GDV = 4, 64, 128
G_RANK = 16
G_TAU = 16.0
G_CHUNK = 64
MH, M_NOPE, M_ROPE, M_V = 8, 64, 32, 64
M_QL, M_KVL = 384, 256
ROPE_BASE = 10000.0
N_EXP, TOP_K, N_GRP, TOPK_GRP = 64, 8, 8, 4
E_DIM = 256
ROUTED_SCALE = 2.5
EPS = 1e-6

C_F = (0, 384)
C_QK = (384, 896)
C_VR = (896, 1920)
C_MLA = (1920, 2560)
C_SM = (2560, 2688)
C_G = (2688, 5760)
IN_COLS_R = 5760

VMEM_LIMIT_V7X = 56 * 1024 * 1024
TOKEN_BLOCK = 512
Q_BLOCK = 256
DISP_BLOCK = 256
ROW_GRANULE = 16
DISP_ROWS = -(-(TOP_K * DISP_BLOCK + N_EXP * (ROW_GRANULE - 1)) // 256) * 256
EXP_TILE = 512
CHUNKS_PER_TILE = EXP_TILE // ROW_GRANULE
BLOCK_CHUNKS = DISP_ROWS // ROW_GRANULE


def _params(*sem):
    return pltpu.CompilerParams(dimension_semantics=sem, vmem_limit_bytes=VMEM_LIMIT_V7X)


def _dot(a, b):
    return jnp.dot(a, b, preferred_element_type=F32)


def _dot_hi(a, b):
    return jnp.dot(a, b, precision=lax.Precision.HIGHEST, preferred_element_type=F32)


def _dot_nt(a, b, precision=None):
    return lax.dot_general(a, b, (((1,), (1,)), ((), ())), precision=precision, preferred_element_type=F32)


def _dot_tn(a, b):
    return lax.dot_general(a, b, (((0,), (0,)), ((), ())), preferred_element_type=F32)


def _sigmoid(x):
    return 1.0 / (1.0 + jnp.exp(-x))


def _rms(x, g):
    return x * lax.rsqrt(jnp.mean(x * x, axis=-1, keepdims=True) + EPS) * g


def _iota(shape, dim):
    return lax.broadcasted_iota(jnp.int32, shape, dim)


def _mod_row(n_ctx_blocks, blocks_per_lat, i):
    return jnp.where(i < n_ctx_blocks, 0, 1 + (i - n_ctx_blocks) // blocks_per_lat)


def _mod_kernel(c_ref, w_ref, b_ref, o_ref):
    c = c_ref[...]
    o_ref[0] = _dot_hi(c * _sigmoid(c), w_ref[0]) + b_ref[0]


def _modulation(cond, w_mod, b_mod):
    L = w_mod.shape[0]
    rows = cond.shape[0]
    tn = 1536
    return pl.pallas_call(
        _mod_kernel,
        grid=(L, 6 * D // tn),
        in_specs=[pl.BlockSpec((rows, D), lambda l, j: (0, 0)),
                  pl.BlockSpec((1, D, tn), lambda l, j: (l, 0, j)),
                  pl.BlockSpec((1, 1, tn), lambda l, j: (l, 0, j))],
        out_specs=pl.BlockSpec((1, rows, tn), lambda l, j: (l, 0, j)),
        out_shape=jax.ShapeDtypeStruct((L, rows, 6 * D), F32),
        compiler_params=_params("parallel", "parallel"),
        name="modulation",
    )(cond, w_mod, b_mod.reshape(L, 1, 6 * D))


def _inproj_kernel(n_ctx_blocks, xc_ref, xl_ref, m_ref, g_ref, w_ref, of_ref, oqk_ref, ovr_ref, omla_ref, osm_ref, og_ref):
    y = _rms(jnp.where(pl.program_id(0) < n_ctx_blocks, xc_ref[...], xl_ref[...]), g_ref[...])
    h = (y * (1.0 + m_ref[0, :, D:2 * D]) + m_ref[0, :, 0:D]).astype(BF)
    of_ref[...] = _dot(h, w_ref[:, C_F[0]:C_F[1]]).astype(BF)
    oqk_ref[...] = _dot(h, w_ref[:, C_QK[0]:C_QK[1]])
    ovr_ref[...] = _dot(h, w_ref[:, C_VR[0]:C_VR[1]]).astype(BF)
    omla_ref[...] = _dot(h, w_ref[:, C_MLA[0]:C_MLA[1]])
    osm_ref[...] = _dot(h, w_ref[:, C_SM[0]:C_SM[1]])
    og_ref[...] = _dot(h, w_ref[:, C_G[0]:C_G[1]]).astype(BF)


def _x_pair_specs(x_pair, tb, n_ctx_blocks):
    _, _, lat_off = x_pair
    return [pl.BlockSpec((tb, D), lambda i: (jnp.minimum(i, n_ctx_blocks - 1), 0)),
            pl.BlockSpec((tb, D), lambda i: (jnp.maximum(i - n_ctx_blocks, 0) + lat_off, 0))]


def _in_projection(x_pair, T, mods, g1, w_in_r, n_ctx_blocks, blocks_per_lat):
    tb = TOKEN_BLOCK
    row = functools.partial(_mod_row, n_ctx_blocks, blocks_per_lat)
    widths = [(C_F, BF), (C_QK, F32), (C_VR, BF), (C_MLA, F32), (C_SM, F32), (C_G, BF)]
    return pl.pallas_call(
        functools.partial(_inproj_kernel, n_ctx_blocks),
        grid=(T // tb,),
        in_specs=_x_pair_specs(x_pair, tb, n_ctx_blocks) + [
                  pl.BlockSpec((1, 1, 2 * D), lambda i: (row(i), 0, 0)),
                  pl.BlockSpec((1, D), lambda i: (0, 0)),
                  pl.BlockSpec((D, IN_COLS_R), lambda i: (0, 0))],
        out_specs=[pl.BlockSpec((tb, c[1] - c[0]), lambda i: (i, 0)) for c, _ in widths],
        out_shape=[jax.ShapeDtypeStruct((T, c[1] - c[0]), dt) for c, dt in widths],
        compiler_params=_params("parallel"),
        name="in_projection",
    )(x_pair[0], x_pair[1], mods, g1, w_in_r)


def _seq_call(kernel, name, n, nseq, blk_off, seq_ins, const_ins, out_widths, extra_outs=(), scratch=()):
    in_specs = [pl.BlockSpec((n, a.shape[1]), lambda i: (i + blk_off, 0)) for a in seq_ins]
    in_specs += [pl.BlockSpec(bs, im) for _, bs, im in const_ins]
    args = list(seq_ins) + [a for a, _, _ in const_ins]
    out_specs = [pl.BlockSpec((n, w), lambda i: (i, 0)) for w, _ in out_widths]
    out_shape = [jax.ShapeDtypeStruct((nseq * n, w), dt) for w, dt in out_widths]
    out_specs += [pl.BlockSpec(bs, im) for _, _, bs, im in extra_outs]
    out_shape += [jax.ShapeDtypeStruct(s, dt) for s, dt, _, _ in extra_outs]
    return pl.pallas_call(
        kernel, grid=(nseq,), in_specs=in_specs, out_specs=out_specs, out_shape=out_shape,
        scratch_shapes=list(scratch), compiler_params=_params("parallel"), name=name,
    )(*args)


def _fourier_kernel(u_ref, r_ref, lc_ref, ls_ref, o_ref):
    y = _dot(u_ref[...], r_ref[...])
    o_ref[...] = (_dot(lc_ref[...], y[:, :FN_W].astype(BF)) + _dot(ls_ref[...], y[:, FN_W:].astype(BF))).astype(BF)


def _dft_tables(n):
    k = np.arange(FN_GW)
    ang = 2.0 * np.pi * ((k[:, None] * k[None, :]) % FN_GW) / FN_GW
    eye = np.eye(FN_G)
    right = np.concatenate([np.kron(eye, np.cos(ang)), np.kron(eye, np.sin(ang))], axis=1)
    p = np.arange(n)
    angn = 2.0 * np.pi * ((p[:, None] * p[None, :]) % n) / n
    scale = 1.0 / np.sqrt(float(n * FN_GW))
    return (jnp.asarray(right, F32).astype(BF), jnp.asarray(np.cos(angn) * scale, F32).astype(BF),
            jnp.asarray(-np.sin(angn) * scale, F32).astype(BF))


def _fourier(zf, n, nseq, blk_off):
    right, lc, ls = _dft_tables(n)
    consts = [(right, (FN_W, 2 * FN_W), lambda i: (0, 0)), (lc, (n, n), lambda i: (0, 0)), (ls, (n, n), lambda i: (0, 0))]
    return _seq_call(_fourier_kernel, "fourier_mix", n, nseq, blk_off, [zf], consts, [(FN_W, BF)])


def _bf_parts(x, n):
    parts, rest = [], x
    for _ in range(n):
        p = rest.astype(BF)
        parts.append(p)
        rest = rest - p.astype(F32)
    return parts


def _dot_f32(a, b):
    a1, a2 = _bf_parts(a, 2)
    b1, b2 = _bf_parts(b, 2)
    return (_dot(a1, b2) + _dot(a2, b1)) + _dot(a1, b1)


def _cumulate(tri, g):
    g1, g2, g3 = _bf_parts(g, 3)
    return (_dot(tri, g3) + _dot(tri, g2)) + _dot(tri, g1)


def _log_gate(z, w_ref, b_ref):
    pre = _dot_f32(z, w_ref[...]) + b_ref[...]
    return (jnp.minimum(pre, 0.0) - jnp.log1p(jnp.exp(-jnp.abs(pre)))) * (1.0 / G_TAU)


def _gla_kernel(has_state, n, *refs):
    if has_state:
        (zqk, zvr, zsm, wgf, bgf, wgb, bgb, gout, s0f, s0b, o_ref, sf_ref, sb_ref,
         oacc_f, oacc_b, lg_f, lg_b, st_f, st_b) = refs
    else:
        (zqk, zvr, zsm, wgf, bgf, wgb, bgb, gout, o_ref, sf_ref, sb_ref,
         oacc_f, oacc_b, lg_f, lg_b, st_f, st_b) = refs
        s0f = s0b = None
    C = G_CHUNK
    nc = n // C
    ri, ci = _iota((C, C), 0), _iota((C, C), 1)
    t_idx, lane = _iota((C, 128), 0), _iota((C, 128), 1)
    s_idx = lane & (C - 1)
    left = lane < GDK
    vleft = _iota((C, 2 * GDV), 1) < GDV
    blockdiag = (_iota((2 * GDV, 2 * GDK), 0) >> 7) == (_iota((2 * GDV, 2 * GDK), 1) >> 6)

    def load_state(st, s0_ref):
        for p in range(2):
            if s0_ref is None:
                st[p] = jnp.zeros((2 * GDV, 2 * GDK), F32)
            else:
                z = jnp.zeros((GDK, GDV), F32)
                blk = jnp.concatenate([jnp.concatenate([s0_ref[0, 2 * p], z], axis=1),
                                       jnp.concatenate([z, s0_ref[0, 2 * p + 1]], axis=1)], axis=0)
                st[p] = blk.T

    def store_state(st, out_ref):
        for p in range(2):
            blk = st[p].T
            out_ref[0, 2 * p] = blk[0:GDK, 0:GDV]
            out_ref[0, 2 * p + 1] = blk[GDK:2 * GDK, GDV:2 * GDV]

    def chunk(c, reverse, lg, st, oacc):
        tri = (ci >= ri).astype(BF) if reverse else (ci <= ri).astype(BF)
        mask = (s_idx >= t_idx) if reverse else (s_idx <= t_idx)
        rows = pl.ds(pl.multiple_of(c * C, C), C)
        cum = _cumulate(tri, lg[rows, :])
        tot = cum[0:1] if reverse else cum[C - 1:C]
        q = zqk[rows, 0:GH * GDK] * (GDK ** -0.5)
        k = zqk[rows, GH * GDK:2 * GH * GDK]
        qh = (q * jnp.exp(cum)).astype(BF)
        kh = (k * jnp.exp(-cum)).astype(BF)
        kb = (k * jnp.exp(tot - cum)).astype(BF)
        dec = jnp.exp(tot)
        for p in range(2):
            ls = slice(128 * p, 128 * p + 128)
            vs = slice(256 * p, 256 * p + 256)
            qp, kp = qh[:, ls], kh[:, ls]
            zk = jnp.zeros_like(kp)
            kblk = jnp.concatenate([jnp.where(left, kp, zk), jnp.where(left, zk, kp)], axis=0)
            sc = jnp.where(mask, _dot_nt(qp, kblk), 0.0).astype(BF)
            vp = zvr[rows, vs]
            zv = jnp.zeros_like(vp)
            vblk = jnp.concatenate([jnp.where(vleft, vp, zv), jnp.where(vleft, zv, vp)], axis=0)
            stp = st[p]
            oacc[rows, vs] = _dot(sc, vblk) + _dot_nt(qp, stp.astype(BF))
            st[p] = dec[:, ls] * stp + jnp.where(blockdiag, _dot_tn(vp, kb[:, ls]), 0.0)

    def both_directions(step, carry):
        chunk(step, False, lg_f, st_f, oacc_f)
        chunk(nc - 1 - step, True, lg_b, st_b, oacc_b)
        return carry

    lg_f[...] = _log_gate(zsm[:, 0:G_RANK], wgf, bgf)
    lg_b[...] = _log_gate(zsm[:, G_RANK:2 * G_RANK], wgb, bgb)
    load_state(st_f, s0f)
    load_state(st_b, s0b)
    lax.fori_loop(0, nc, both_directions, 0, unroll=min(nc, 4))
    store_state(st_f, sf_ref)
    store_state(st_b, sb_ref)

    rb = 128
    for r0 in range(0, n, rb):
        for h in range(GH):
            hs = slice(GDV * h, GDV * h + GDV)
            oh = oacc_f[r0:r0 + rb, hs] + oacc_b[r0:r0 + rb, hs]
            oh = oh * lax.rsqrt(jnp.mean(oh * oh, axis=-1, keepdims=True) + EPS) * gout[:, hs]
            r = zvr[r0:r0 + rb, GH * GDV + hs.start:GH * GDV + hs.stop].astype(F32)
            o_ref[r0:r0 + rb, hs] = (oh * (r * _sigmoid(r))).astype(BF)


def _gla(zqk, zvr, zsm, gate_w, n, nseq, blk_off, states):
    wgf, bgf, wgb, bgb, gout = gate_w
    c2 = lambda i: (0, 0)
    consts = [(wgf, wgf.shape, c2), (bgf, bgf.shape, c2), (wgb, wgb.shape, c2), (bgb, bgb.shape, c2), (gout, gout.shape, c2)]
    st_blk = (1, GH, GDK, GDV)
    st_map = lambda i: (i, 0, 0, 0)
    if states is not None:
        consts += [(s, st_blk, st_map) for s in states]
    extra = [((nseq, GH, GDK, GDV), F32, st_blk, st_map)] * 2
    scratch = ([pltpu.VMEM((n, GH * GDV), F32)] * 2 + [pltpu.VMEM((n, GH * GDK), F32)] * 2
               + [pltpu.VMEM((2, 2 * GDV, 2 * GDK), F32)] * 2)
    return _seq_call(functools.partial(_gla_kernel, states is not None, n), "gla_mixer", n, nseq, blk_off,
                     [zqk, zvr, zsm], consts, [(GH * GDV, BF)], extra_outs=extra, scratch=scratch)


def _mla_kernel(latent, n, past, *refs):
    if latent:
        (zmla, zsm, gq, wq, gkv, wkv, cckv, ckr, cosq, sinq, cosk, sink, o_ref, qs, kns, vs, krs) = refs
    else:
        (zmla, zsm, gq, wq, gkv, wkv, o_ref, ckv_ref, qs, kns, vs, krs) = refs
    sk = past + n
    scale = (M_NOPE + M_ROPE) ** -0.5
    nw, rw = MH * M_NOPE, MH * M_ROPE
    qa = _dot(_rms(zmla[:, 0:M_QL], gq[...]).astype(BF), wq[...])
    qr = qa[:, nw:nw + rw]
    if latent:
        qr = qr * cosq[...] + qa[:, nw + rw:nw + 2 * rw] * sinq[...]
    qs[:, 0:nw] = qa[:, 0:nw] * scale
    qs[:, nw:nw + rw] = qr * scale
    ckv = _rms(zmla[:, M_QL:M_QL + M_KVL], gkv[...])
    kv = _dot(ckv.astype(BF), wkv[...])
    kr = zsm[:, 32:64]
    if latent:
        kr = kr * cosk[...] + zsm[:, 64:96] * sink[...]
        kvc = _dot(cckv[0].astype(BF), wkv[...])
        kns[0:past, :] = kvc[:, 0:nw].astype(BF)
        vs[0:past, :] = kvc[:, nw:].astype(BF)
        krs[0:past, :] = jnp.concatenate([ckr[0]] * 4, axis=1).astype(BF)
    else:
        ckv_ref[...] = ckv
    kns[past:sk, :] = kv[:, 0:nw].astype(BF)
    vs[past:sk, :] = kv[:, nw:].astype(BF)
    krs[past:sk, :] = jnp.concatenate([kr] * 4, axis=1).astype(BF)

    qb = min(Q_BLOCK, n)
    lane = _iota((qb, 128), 1)

    def block(step, carry):
        rows = pl.ds(pl.multiple_of(step * qb, qb), qb)
        for p in range(MH // 2):
            ls = slice(128 * p, 128 * p + 128)
            qn = qs[rows, ls]
            quad = (2 * p) // 4
            qrp = qs[rows, nw + 128 * quad:nw + 128 * quad + 128]
            rhs = jnp.concatenate([kns[:, ls], krs[...]], axis=1)
            vp = vs[:, ls]
            o_pair = None
            for hh in range(2):
                j = (2 * p + hh) % 4
                qn_m = jnp.where((lane >> 6) == hh, qn, 0.0).astype(BF)
                qr_m = jnp.where((lane >> 5) == j, qrp, 0.0).astype(BF)
                s = _dot_nt(jnp.concatenate([qn_m, qr_m], axis=1), rhs)
                e = jnp.exp(s - jnp.max(s, axis=-1, keepdims=True))
                pv = _dot(e.astype(BF), vp) / jnp.sum(e, axis=-1, keepdims=True)
                o_pair = pv if hh == 0 else jnp.where(lane < M_V, o_pair, pv)
            o_ref[rows, ls] = o_pair.astype(BF)
        return carry

    lax.fori_loop(0, n // qb, block, 0)


def _rope_tables(n):
    half = M_ROPE // 2
    pos = jnp.arange(n)
    row = (pos // GRID_W).astype(F32)
    col = (pos % GRID_W).astype(F32)
    inv = ROPE_BASE ** (-jnp.arange(0, half, 2, dtype=F32) / half)
    ang = jnp.concatenate([row[:, None] * inv, col[:, None] * inv], axis=-1)
    cos = jnp.repeat(jnp.cos(ang), 2, axis=-1)
    sin = jnp.repeat(jnp.sin(ang), 2, axis=-1) * jnp.tile(jnp.asarray([-1.0, 1.0], F32), half)
    return jnp.tile(cos, (1, MH)), jnp.tile(sin, (1, MH)), cos, sin


def _mla(zmla, zsm, w, n, nseq, blk_off, cache):
    gq, wq, gkv, wkv = w
    c2 = lambda i: (0, 0)
    consts = [(gq, gq.shape, c2), (wq, wq.shape, c2), (gkv, gkv.shape, c2), (wkv, wkv.shape, c2)]
    past = 0
    extra = []
    if cache is not None:
        cckv, ckr = cache
        past = cckv.shape[1]
        c3 = lambda i: (i, 0, 0)
        consts += [(cckv, (1, past, M_KVL), c3), (ckr, (1, past, M_ROPE), c3)]
        consts += [(t, t.shape, c2) for t in _rope_tables(n)]
    else:
        extra = [((nseq * n, M_KVL), F32, (n, M_KVL), lambda i: (i, 0))]
    sk = past + n
    scratch = [pltpu.VMEM((n, MH * (M_NOPE + M_ROPE)), F32), pltpu.VMEM((sk, MH * M_NOPE), BF),
               pltpu.VMEM((sk, MH * M_V), BF), pltpu.VMEM((sk, 128), BF)]
    return _seq_call(functools.partial(_mla_kernel, cache is not None, n, past), "mla_mixer", n, nseq, blk_off,
                     [zmla, zsm], consts, [(MH * M_V, BF)], extra_outs=extra, scratch=scratch)


def _route(logits_t, bias):
    nt = logits_t.shape[1]
    gsz = N_EXP // N_GRP
    scores = _sigmoid(logits_t)
    sel = scores + bias
    neg = -jnp.inf
    sub = _iota((gsz, nt), 0)
    tops = []
    for g in range(N_GRP):
        blk = sel[gsz * g:gsz * g + gsz]
        m1 = jnp.max(blk, axis=0, keepdims=True)
        first = jnp.min(jnp.where(blk == m1, sub, gsz), axis=0, keepdims=True)
        m2 = jnp.max(jnp.where(sub == first, neg, blk), axis=0, keepdims=True)
        tops.append(m1 + m2)
    gs = jnp.concatenate(tops, axis=0)
    gidx = _iota((N_GRP, nt), 0)
    grank = jnp.zeros((N_GRP, nt), jnp.int32)
    for j in range(N_GRP):
        rj = gs[j:j + 1]
        grank += ((rj > gs) | ((rj == gs) & (gidx > j))).astype(jnp.int32)
    keep = grank < TOPK_GRP
    masked = jnp.concatenate(
        [jnp.where(jnp.broadcast_to(keep[g:g + 1], (gsz, nt)), sel[gsz * g:gsz * g + gsz], neg) for g in range(N_GRP)], axis=0)
    eidx = _iota((N_EXP, nt), 0)
    chosen = eidx < 0
    work = masked
    for _ in range(TOP_K):
        top = jnp.max(work, axis=0, keepdims=True)
        first = jnp.min(jnp.where(work == top, eidx, N_EXP), axis=0, keepdims=True)
        hit = eidx == first
        chosen = chosen | hit
        work = jnp.where(hit, neg, work)
    w = jnp.where(chosen, scores, 0.0)
    return chosen, w / jnp.sum(w, axis=0, keepdims=True) * ROUTED_SCALE


def _dispatch_meta(chosen, gates_t):
    tb = chosen.shape[1]
    sel = chosen.astype(F32)
    selb = sel.astype(BF)
    earlier = (_iota((tb, tb), 0) < _iota((tb, tb), 1)).astype(BF)
    rank = _dot(selb, earlier)
    cnt = jnp.sum(sel, axis=1, keepdims=True)
    padded = jnp.floor((cnt + (ROW_GRANULE - 1)) * (1.0 / ROW_GRANULE)) * ROW_GRANULE
    below = (_iota((N_EXP, N_EXP), 1) < _iota((N_EXP, N_EXP), 0)).astype(BF)
    start = _dot(below, jnp.broadcast_to(padded, (N_EXP, 128)).astype(BF))[:, 0:1]
    pos = start + rank
    kidx = _dot(below, selb)
    pos8, gate8 = [], []
    for k in range(TOP_K):
        hit = chosen & (kidx == float(k))
        pos8.append(jnp.sum(jnp.where(hit, pos, 0.0), axis=0, keepdims=True))
        gate8.append(jnp.sum(jnp.where(hit, gates_t, 0.0), axis=0, keepdims=True))
    return (jnp.concatenate(pos8, axis=0).astype(jnp.int32), jnp.concatenate(gate8, axis=0), cnt)


def _merge_kernel(n_ctx_blocks, xc_ref, xl_ref, fc_ref, fl_ref, ogc_ref, ogl_ref, omc_ref, oml_ref, zg_ref, m_ref,
                  wbf, wbg, wbm, wout, gn2, wrt, brt, xm_ref, h2_ref, pos_ref, gate_ref, cnt_ref):
    is_ctx = pl.program_id(0) < n_ctx_blocks
    ya = _dot(jnp.where(is_ctx, fc_ref[...], fl_ref[...]), wbf[...])
    yb = _dot(jnp.where(is_ctx, ogc_ref[...], ogl_ref[...]), wbg[...])
    yc = _dot(jnp.where(is_ctx, omc_ref[...], oml_ref[...]), wbm[...])
    merged = (_sigmoid(zg_ref[:, 0:D]) * ya.astype(BF) + _sigmoid(zg_ref[:, D:2 * D]) * yb.astype(BF)
              + _sigmoid(zg_ref[:, 2 * D:3 * D]) * yc.astype(BF))
    xm = jnp.where(is_ctx, xc_ref[...], xl_ref[...]) + m_ref[0, :, 2 * D:3 * D] * _dot(merged, wout[...])
    xm_ref[...] = xm
    h2 = _rms(xm, gn2[...]) * (1.0 + m_ref[0, :, 4 * D:5 * D]) + m_ref[0, :, 3 * D:4 * D]
    h2_ref[...] = h2.astype(BF)
    chosen, gates_t = _route(_dot_nt(wrt[...], h2, precision=lax.Precision.HIGHEST), brt[...])
    for sb in range(gates_t.shape[1] // DISP_BLOCK):
        ls = slice(sb * DISP_BLOCK, (sb + 1) * DISP_BLOCK)
        pos8, gate8, cnt = _dispatch_meta(chosen[:, ls], gates_t[:, ls])
        pos_ref[:, ls] = pos8
        gate_ref[:, ls] = gate8
        cnt_ref[sb] = jnp.broadcast_to(cnt, (N_EXP, 128))


def _merge(x_pair, T, mix_ctx, mix_lat, zg, mods, w, n_ctx_blocks, blocks_per_lat):
    tb = TOKEN_BLOCK
    row = functools.partial(_mod_row, n_ctx_blocks, blocks_per_lat)
    rb = lambda wd: pl.BlockSpec((tb, wd), lambda i: (i, 0))
    cb = lambda a: pl.BlockSpec(a.shape, lambda i: (0, 0))
    ctx_b = lambda wd: pl.BlockSpec((tb, wd), lambda i: (jnp.minimum(i, n_ctx_blocks - 1), 0))
    lat_b = lambda wd: pl.BlockSpec((tb, wd), lambda i: (jnp.maximum(i - n_ctx_blocks, 0), 0))
    mix_specs, mix_args = [], []
    for a_c, a_l in zip(mix_ctx, mix_lat):
        mix_specs += [ctx_b(a_c.shape[1]), lat_b(a_l.shape[1])]
        mix_args += [a_c, a_l]
    return pl.pallas_call(
        functools.partial(_merge_kernel, n_ctx_blocks),
        grid=(T // tb,),
        in_specs=_x_pair_specs(x_pair, tb, n_ctx_blocks) + mix_specs + [rb(3 * D),
                  pl.BlockSpec((1, 1, 6 * D), lambda i: (row(i), 0, 0))] + [cb(a) for a in w],
        out_specs=[rb(D), rb(D), pl.BlockSpec((TOP_K, tb), lambda i: (0, i)), pl.BlockSpec((TOP_K, tb), lambda i: (0, i)),
                   pl.BlockSpec((tb // DISP_BLOCK, N_EXP, 128), lambda i: (i, 0, 0))],
        out_shape=[jax.ShapeDtypeStruct((T, D), F32), jax.ShapeDtypeStruct((T, D), BF),
                   jax.ShapeDtypeStruct((TOP_K, T), jnp.int32), jax.ShapeDtypeStruct((TOP_K, T), F32),
                   jax.ShapeDtypeStruct((T // DISP_BLOCK, N_EXP, 128), F32)],
        compiler_params=_params("parallel"),
        name="merge_route",
    )(x_pair[0], x_pair[1], *mix_args, zg, mods, *w)


def _silu_mul(a, b):
    return a * _sigmoid(a) * b


DISP_ROW_CHUNK = 256


def _placement(pos_ref, weight_ref, r0):
    rows = _iota((DISP_ROW_CHUNK, DISP_BLOCK), 0) + r0
    p = jnp.zeros((DISP_ROW_CHUNK, DISP_BLOCK), F32)
    for k in range(TOP_K):
        w = 1.0 if weight_ref is None else weight_ref[k:k + 1, :]
        p = jnp.where(rows == pos_ref[k:k + 1, :], w, p)
    return p.astype(BF)


def _dispatch_kernel(n_blocks, h_ref, pos_ref, xs_ref):
    h = jnp.where(pl.program_id(0) < n_blocks, h_ref[...], jnp.zeros_like(h_ref))
    for r0 in range(0, DISP_ROWS, DISP_ROW_CHUNK):
        xs_ref[r0:r0 + DISP_ROW_CHUNK, :] = _dot(_placement(pos_ref, None, r0), h).astype(BF)


def _dispatch(h2, pos8):
    T = h2.shape[0]
    nblk = T // DISP_BLOCK
    last = lambda b: jnp.minimum(b, nblk - 1)
    return pl.pallas_call(
        functools.partial(_dispatch_kernel, nblk),
        grid=(nblk + 1,),
        in_specs=[pl.BlockSpec((DISP_BLOCK, D), lambda b: (last(b), 0)),
                  pl.BlockSpec((TOP_K, DISP_BLOCK), lambda b: (0, last(b)))],
        out_specs=pl.BlockSpec((DISP_ROWS, D), lambda b: (b, 0)),
        out_shape=jax.ShapeDtypeStruct(((nblk + 1) * DISP_ROWS, D), BF),
        compiler_params=_params("parallel"),
        name="moe_dispatch",
    )(h2, pos8)


def _tile_tables(cnt, n_chunks_max, n_tiles_max):
    nblk = cnt.shape[0]
    nch = (cnt + (ROW_GRANULE - 1)) // ROW_GRANULE
    first = jnp.arange(nblk, dtype=jnp.int32)[:, None] * BLOCK_CHUNKS + jnp.cumsum(nch, axis=1) - nch
    tiles_e = (jnp.sum(nch, axis=0) + (CHUNKS_PER_TILE - 1)) // CHUNKS_PER_TILE
    span_e = tiles_e * CHUNKS_PER_TILE
    exp_start = jnp.cumsum(span_e) - span_e
    j = jnp.arange(n_chunks_max, dtype=jnp.int32)
    e_j = jnp.sum((exp_start[None, :] <= j[:, None]).astype(jnp.int32), axis=1) - 1
    onehot = (e_j[:, None] == jnp.arange(N_EXP, dtype=jnp.int32)[None, :]).astype(F32)
    rows_of = lambda tab: jnp.dot(onehot, tab.astype(F32), precision=lax.Precision.HIGHEST).astype(jnp.int32)
    local = j - rows_of(exp_start[:, None])[:, 0]
    blk_len = rows_of(nch.T)
    blk_start = rows_of(jnp.cumsum(nch.T, axis=1) - nch.T)
    inside = (blk_start <= local[:, None]) & (local[:, None] < blk_start + blk_len)
    real = jnp.any(inside, axis=1)
    chunk = jnp.sum(jnp.where(inside, rows_of(first.T) + (local[:, None] - blk_start), 0), axis=1)
    pad_rank = jnp.cumsum(jnp.where(real, 0, 1)) - 1
    src = jnp.where(real, chunk, BLOCK_CHUNKS - 1).astype(jnp.int32)
    dst = jnp.where(real, chunk, nblk * BLOCK_CHUNKS + pad_rank % BLOCK_CHUNKS).astype(jnp.int32)
    tile_end = jnp.cumsum(tiles_e)
    i = jnp.arange(n_tiles_max, dtype=jnp.int32)
    tile_expert = jnp.minimum(jnp.sum((tile_end[None, :] <= i[:, None]).astype(jnp.int32), axis=1), N_EXP - 1)
    return src, dst, tile_expert.astype(jnp.int32), tile_end[-1:].astype(jnp.int32)


def _expert_kernel(src_ref, dst_ref, texp_ref, nused_ref, xs_hbm, wg_ref, wu_ref, wd_ref, ys_hbm,
                   xbuf, ybuf, wgu_bf, wd_bf, gsem, ssem):
    i = pl.program_id(0)
    n_used = nused_ref[0]
    slot = lax.rem(i, 2)

    def chunk_copies(tile, slot_, to_buffer, do):
        for c in range(CHUNKS_PER_TILE):
            j = tile * CHUNKS_PER_TILE + c
            if to_buffer:
                cp = pltpu.make_async_copy(xs_hbm.at[src_ref[j]], xbuf.at[slot_, c], gsem.at[slot_])
            else:
                cp = pltpu.make_async_copy(ybuf.at[slot_, c], ys_hbm.at[dst_ref[j]], ssem.at[slot_])
            do(cp, c)

    start = lambda cp, c: cp.start(priority=c % 2)
    wait = lambda cp, c: cp.wait()

    @pl.when(i == 0)
    def _():
        chunk_copies(0, 0, True, start)

    @pl.when(i == n_used)
    def _():
        chunk_copies(i, slot, True, wait)

    @pl.when(i < n_used)
    def _():
        chunk_copies(i, slot, True, wait)
        chunk_copies(i + 1, 1 - slot, True, start)

        @pl.when((i == 0) | (texp_ref[i] != texp_ref[jnp.maximum(i - 1, 0)]))
        def _():
            wgu_bf[:, 0:E_DIM] = wg_ref[0, 0].astype(BF)
            wgu_bf[:, E_DIM:2 * E_DIM] = wu_ref[0, 0].astype(BF)
            wd_bf[...] = wd_ref[0, 0].astype(BF)

        gu = _dot(xbuf[slot].reshape(EXP_TILE, D), wgu_bf[...])
        hid = _silu_mul(gu[:, 0:E_DIM], gu[:, E_DIM:2 * E_DIM])
        ybuf[slot] = _dot(hid.astype(BF), wd_bf[...]).astype(BF).reshape(CHUNKS_PER_TILE, ROW_GRANULE, D)
        chunk_copies(i, slot, False, start)

        @pl.when(i >= 1)
        def _():
            chunk_copies(i - 1, 1 - slot, False, wait)

        @pl.when(i == n_used - 1)
        def _():
            chunk_copies(i, slot, False, wait)


def _experts(layer, xs, tables, w_eg, w_eu, w_ed, n_tiles_max):
    src, dst, tile_expert, n_used = tables
    chunks = xs.reshape(-1, ROW_GRANULE, D)
    wmap =lambda i, src_, dst_, texp, nu: (layer, texp[i], 0, 0)
    grid_spec = pltpu.PrefetchScalarGridSpec(
        num_scalar_prefetch=4,
        grid=(n_tiles_max + 1,),
        in_specs=[pl.BlockSpec(memory_space=pl.ANY),
                  pl.BlockSpec((1, 1, D, E_DIM), wmap),
                  pl.BlockSpec((1, 1, D, E_DIM), wmap),
                  pl.BlockSpec((1, 1, E_DIM, D), wmap)],
        out_specs=pl.BlockSpec(memory_space=pl.ANY),
        scratch_shapes=[pltpu.VMEM((2, CHUNKS_PER_TILE, ROW_GRANULE, D), BF),
                        pltpu.VMEM((2, CHUNKS_PER_TILE, ROW_GRANULE, D), BF),
                        pltpu.VMEM((D, 2 * E_DIM), BF), pltpu.VMEM((E_DIM, D), BF),
                        pltpu.SemaphoreType.DMA((2,)), pltpu.SemaphoreType.DMA((2,))],
    )
    return pl.pallas_call(
        _expert_kernel,
        grid_spec=grid_spec,
        out_shape=jax.ShapeDtypeStruct(chunks.shape, chunks.dtype),
        input_output_aliases={4: 0},
        compiler_params=_params("arbitrary"),
        name="moe_experts",
    )(src, dst, tile_expert, n_used, chunks, w_eg, w_eu, w_ed).reshape(xs.shape)


def _combine_kernel(final, ys_ref, pos_ref, gate_ref, h_ref, sg_ref, su_ref, sd_ref, x_ref, m_ref, gf_ref, o_ref):
    routed = jnp.zeros((DISP_BLOCK, D), F32)
    for r0 in range(0, DISP_ROWS, DISP_ROW_CHUNK):
        routed = routed + _dot_tn(_placement(pos_ref, gate_ref, r0), ys_ref[r0:r0 + DISP_ROW_CHUNK, :])
    h = h_ref[...]
    sh = _silu_mul(_dot(h, sg_ref[0].astype(BF)), _dot(h, su_ref[0].astype(BF)))
    out = x_ref[...] + m_ref[0] * (routed + _dot(sh.astype(BF), sd_ref[0].astype(BF)))
    if final:
        out = _rms(out, gf_ref[...])
    o_ref[...] = out


def _combine(layer, final, ys, pos8, gate8, h2, w_sg, w_su, w_sd, xm, mods, g_final, n_ctx_blocks, blocks_per_lat):
    T = h2.shape[0]
    tb = DISP_BLOCK
    row = functools.partial(_mod_row, n_ctx_blocks, blocks_per_lat)
    return pl.pallas_call(
        functools.partial(_combine_kernel, final),
        grid=(T // tb,),
        in_specs=[pl.BlockSpec((DISP_ROWS, D), lambda b: (b, 0)),
                  pl.BlockSpec((TOP_K, tb), lambda b: (0, b)),
                  pl.BlockSpec((TOP_K, tb), lambda b: (0, b)),
                  pl.BlockSpec((tb, D), lambda b: (b, 0)),
                  pl.BlockSpec((1, D, E_DIM), lambda b: (layer, 0, 0)),
                  pl.BlockSpec((1, D, E_DIM), lambda b: (layer, 0, 0)),
                  pl.BlockSpec((1, E_DIM, D), lambda b: (layer, 0, 0)),
                  pl.BlockSpec((tb, D), lambda b: (b, 0)),
                  pl.BlockSpec((1, 1, D), lambda b: (row(b), 0, 5)),
                  pl.BlockSpec((1, D), lambda b: (0, 0))],
        out_specs=pl.BlockSpec((tb, D), lambda b: (b, 0)),
        out_shape=jax.ShapeDtypeStruct((T, D), F32),
        compiler_params=_params("parallel"),
        name="moe_combine",
    )(ys, pos8, gate8, h2, w_sg, w_su, w_sd, xm, mods, g_final)


def _moe(layer, final, h2, pos8, gate8, cnt, w_eg, w_eu, w_ed, w_sg, w_su, w_sd, xm, mods, g_final, disp_blocks):
    T = h2.shape[0]
    nblk = T // DISP_BLOCK
    n_chunks_max = (TOP_K * T + N_EXP * nblk * (ROW_GRANULE - 1)) // ROW_GRANULE + N_EXP * (CHUNKS_PER_TILE - 1)
    n_tiles_max = -(-n_chunks_max // CHUNKS_PER_TILE)
    xs = _dispatch(h2, pos8)
    tables = _tile_tables(cnt[:, :, 0].astype(jnp.int32), (n_tiles_max + 1) * CHUNKS_PER_TILE, n_tiles_max + 1)
    ys = _experts(layer, xs, tables, w_eg, w_eu, w_ed, n_tiles_max)
    return _combine(layer, final, ys, pos8, gate8, h2, w_sg, w_su, w_sd, xm, mods, g_final, *disp_blocks)


def _reorder_w_in(w_in):
    kr = w_in[:, :, 2592:2624]
    kr_sw = kr.reshape(kr.shape[0], D, M_ROPE // 2, 2)[..., ::-1].reshape(kr.shape)
    pad = jnp.zeros(kr.shape, w_in.dtype)
    out = jnp.concatenate([w_in[:, :, :1920], w_in[:, :, 1952:2592], w_in[:, :, 1920:1952], kr, kr_sw, pad,
                           w_in[:, :, 2624:]], axis=2)
    return out.astype(BF)


def _reorder_w_q(w_q_up, with_swap):
    L = w_q_up.shape[0]
    w = w_q_up.reshape(L, M_QL, MH, M_NOPE + M_ROPE)
    nope = w[..., :M_NOPE].reshape(L, M_QL, MH * M_NOPE)
    rope = w[..., M_NOPE:]
    parts = [nope, rope.reshape(L, M_QL, MH * M_ROPE)]
    if with_swap:
        parts.append(rope.reshape(L, M_QL, MH, M_ROPE // 2, 2)[..., ::-1].reshape(L, M_QL, MH * M_ROPE))
    return jnp.concatenate(parts, axis=2).astype(BF)


def _reorder_w_kv(w_kv_up):
    L = w_kv_up.shape[0]
    w = w_kv_up.reshape(L, M_KVL, MH, M_NOPE + M_V)
    return jnp.concatenate([w[..., :M_NOPE].reshape(L, M_KVL, MH * M_NOPE),
                            w[..., M_NOPE:].reshape(L, M_KVL, MH * M_V)], axis=2).astype(BF)


def kernel(x_prompt, x_sample, state_gla_fwd, state_gla_bwd, cache_mla_ckv, cache_mla_krope, c, c_ctx, w_mod, b_mod, g_norm1, g_norm2, w_in, w_gla_gate_f, b_gla_gate_f, w_gla_gate_b, b_gla_gate_b, g_gla_out, g_q_a, w_q_up, g_kv_a, w_kv_up, w_br_fourier, w_br_gla, w_br_mla, w_out, w_router, b_router, w_exp_gate, w_exp_up, w_exp_down, w_sh_gate, w_sh_up, w_sh_down, g_final):
    nb, sl, _ = x_prompt.shape
    db, dl, _ = x_sample.shape
    L = w_mod.shape[0]
    t_ctx, t_lat = nb * sl, db * dl
    T = t_ctx + t_lat
    assert sl % G_CHUNK == 0 and dl % G_CHUNK == 0 and dl % GRID_W == 0
    assert t_ctx % dl == 0 and dl % TOKEN_BLOCK == 0 and TOKEN_BLOCK % DISP_BLOCK == 0
    assert t_ctx % TOKEN_BLOCK == 0 and dl % Q_BLOCK == 0 and 1 + db <= 8

    x_pair = (x_prompt.reshape(t_ctx, D), x_sample.reshape(t_lat, D), 0)
    cond = jnp.concatenate([c_ctx[None, :], c, jnp.zeros((7 - db, D), F32)], axis=0)
    mods_all = _modulation(cond, w_mod, b_mod)

    w_in_r = _reorder_w_in(w_in)
    wq_ctx = _reorder_w_q(w_q_up, False)
    wq_lat = _reorder_w_q(w_q_up, True)
    wkv_r = _reorder_w_kv(w_kv_up)
    lat_off = t_ctx // dl

    new_f, new_b, new_ckv, new_kr = [], [], [], []
    for l in range(L):
        mods = mods_all[l].reshape(8, 1, 6 * D)
        tok = (t_ctx // TOKEN_BLOCK, dl // TOKEN_BLOCK)
        zf, zqk, zvr, zmla, zsm, zg = _in_projection(x_pair, T, mods, g_norm1[l][None, :], w_in_r[l], *tok)

        (f_c,) = _fourier(zf, sl, nb, 0)
        (f_l,) = _fourier(zf, dl, db, lat_off)

        gate_w = (w_gla_gate_f[l], b_gla_gate_f[l][None, :], w_gla_gate_b[l], b_gla_gate_b[l][None, :],
                  g_gla_out[l].reshape(1, GH * GDV))
        og_c, s_f, s_b = _gla(zqk, zvr, zsm, gate_w, sl, nb, 0, None)
        og_l, _, _ = _gla(zqk, zvr, zsm, gate_w, dl, db, lat_off, (state_gla_fwd[:, l], state_gla_bwd[:, l]))

        gq, gkv = g_q_a[l][None, :], g_kv_a[l][None, :]
        om_c, ckv = _mla(zmla, zsm, (gq, wq_ctx[l], gkv, wkv_r[l]), sl, nb, 0, None)
        (om_l,) = _mla(zmla, zsm, (gq, wq_lat[l], gkv, wkv_r[l]), dl, db, lat_off,
                       (cache_mla_ckv[:, l], cache_mla_krope[:, l]))

        mw = (w_br_fourier[l].astype(BF), w_br_gla[l].astype(BF), w_br_mla[l].astype(BF), w_out[l].astype(BF),
              g_norm2[l][None, :], w_router[l].T, b_router[l][:, None])
        xm, h2, pos8, gate8, cnt = _merge(x_pair, T, (f_c, og_c, om_c), (f_l, og_l, om_l), zg, mods, mw, *tok)

        x = _moe(l, l == L - 1, h2, pos8, gate8, cnt, w_exp_gate, w_exp_up, w_exp_down, w_sh_gate, w_sh_up, w_sh_down,
                 xm, mods, g_final[None, :], (t_ctx // DISP_BLOCK, dl // DISP_BLOCK))
        x_pair = (x, x, t_ctx // TOKEN_BLOCK)

        new_f.append(s_f)
        new_b.append(s_b)
        new_ckv.append(ckv.reshape(nb, sl, M_KVL))
        new_kr.append(zsm[:t_ctx, 32:64].reshape(nb, sl, M_ROPE))

    y_prompt = x[:t_ctx].reshape(nb, sl, D)
    y_sample = x[t_ctx:].reshape(db, dl, D)
    return (y_prompt, y_sample, jnp.stack(new_f, axis=1), jnp.stack(new_b, axis=1),
            jnp.stack(new_ckv, axis=1), jnp.stack(new_kr, axis=1))
```

```python
import functools

import numpy as np
import jax
import jax.numpy as jnp
from jax import lax
from jax.experimental import pallas as pl
from jax.experimental.pallas import tpu as pltpu

F32 = jnp.float32
BF = jnp.bfloat16

D = 1024
GRID_W = 64
FN_G, FN_GW = 4, 96
FN_W = FN_G * FN_GW
GH, GDK, GDV = 4, 64, 128
G_RANK = 16
G_TAU = 16.0
G_CHUNK = 64
MH, M_NOPE, M_ROPE, M_V = 8, 64, 32, 64
M_QL, M_KVL = 384, 256
ROPE_BASE = 10000.0
N_EXP, TOP_K, N_GRP, TOPK_GRP = 64, 8, 8, 4
E_DIM = 256
ROUTED_SCALE = 2.5
EPS = 1e-6

C_F = (0, 384)
C_QK = (384, 896)
C_VR = (896, 1920)
C_MLA = (1920, 2560)
C_SM = (2560, 2688)
C_G = (2688, 5760)
IN_COLS_R = 5760

VMEM_LIMIT_V7X = 56 * 1024 * 1024
TOKEN_BLOCK = 512
Q_BLOCK = 256
DISP_BLOCK = 256
ROW_GRANULE = 16
DISP_ROWS = -(-(TOP_K * DISP_BLOCK + N_EXP * (ROW_GRANULE - 1)) // 256) * 256
EXP_TILE = 512
CHUNKS_PER_TILE = EXP_TILE // ROW_GRANULE
BLOCK_CHUNKS = DISP_ROWS // ROW_GRANULE


def _params(*sem):
    return pltpu.CompilerParams(dimension_semantics=sem, vmem_limit_bytes=VMEM_LIMIT_V7X)


def _dot(a, b):
    return jnp.dot(a, b, preferred_element_type=F32)


def _dot_hi(a, b):
    return jnp.dot(a, b, precision=lax.Precision.HIGHEST, preferred_element_type=F32)


def _dot_nt(a, b, precision=None):
    return lax.dot_general(a, b, (((1,), (1,)), ((), ())), precision=precision, preferred_element_type=F32)


def _dot_tn(a, b):
    return lax.dot_general(a, b, (((0,), (0,)), ((), ())), preferred_element_type=F32)


def _sigmoid(x):
    return 1.0 / (1.0 + jnp.exp(-x))


def _rms(x, g):
    return x * lax.rsqrt(jnp.mean(x * x, axis=-1, keepdims=True) + EPS) * g


def _iota(shape, dim):
    return lax.broadcasted_iota(jnp.int32, shape, dim)


def _mod_row(n_ctx_blocks, blocks_per_lat, i):
    return jnp.where(i < n_ctx_blocks, 0, 1 + (i - n_ctx_blocks) // blocks_per_lat)


def _mod_kernel(c_ref, w_ref, b_ref, o_ref):
    c = c_ref[...]
    o_ref[0] = _dot_hi(c * _sigmoid(c), w_ref[0]) + b_ref[0]


def _modulation(cond, w_mod, b_mod):
    L = w_mod.shape[0]
    rows = cond.shape[0]
    tn = 1536
    return pl.pallas_call(
        _mod_kernel,
        grid=(L, 6 * D // tn),
        in_specs=[pl.BlockSpec((rows, D), lambda l, j: (0, 0)),
                  pl.BlockSpec((1, D, tn), lambda l, j: (l, 0, j)),
                  pl.BlockSpec((1, 1, tn), lambda l, j: (l, 0, j))],
        out_specs=pl.BlockSpec((1, rows, tn), lambda l, j: (l, 0, j)),
        out_shape=jax.ShapeDtypeStruct((L, rows, 6 * D), F32),
        compiler_params=_params("parallel", "parallel"),
        name="modulation",
    )(cond, w_mod, b_mod.reshape(L, 1, 6 * D))


def _inproj_kernel(n_ctx_blocks, xc_ref, xl_ref, m_ref, g_ref, w_ref, of_ref, oqk_ref, ovr_ref, omla_ref, osm_ref, og_ref):
    y = _rms(jnp.where(pl.program_id(0) < n_ctx_blocks, xc_ref[...], xl_ref[...]), g_ref[...])
    h = (y * (1.0 + m_ref[0, :, D:2 * D]) + m_ref[0, :, 0:D]).astype(BF)
    of_ref[...] = _dot(h, w_ref[:, C_F[0]:C_F[1]]).astype(BF)
    oqk_ref[...] = _dot(h, w_ref[:, C_QK[0]:C_QK[1]])
    ovr_ref[...] = _dot(h, w_ref[:, C_VR[0]:C_VR[1]]).astype(BF)
    omla_ref[...] = _dot(h, w_ref[:, C_MLA[0]:C_MLA[1]])
    osm_ref[...] = _dot(h, w_ref[:, C_SM[0]:C_SM[1]])
    og_ref[...] = _dot(h, w_ref[:, C_G[0]:C_G[1]]).astype(BF)


def _x_pair_specs(x_pair, tb, n_ctx_blocks):
    _, _, lat_off = x_pair
    return [pl.BlockSpec((tb, D), lambda i: (jnp.minimum(i, n_ctx_blocks - 1), 0)),
            pl.BlockSpec((tb, D), lambda i: (jnp.maximum(i - n_ctx_blocks, 0) + lat_off, 0))]


def _in_projection(x_pair, T, mods, g1, w_in_r, n_ctx_blocks, blocks_per_lat):
    tb = TOKEN_BLOCK
    row = functools.partial(_mod_row, n_ctx_blocks, blocks_per_lat)
    widths = [(C_F, BF), (C_QK, F32), (C_VR, BF), (C_MLA, F32), (C_SM, F32), (C_G, BF)]
    return pl.pallas_call(
        functools.partial(_inproj_kernel, n_ctx_blocks),
        grid=(T // tb,),
        in_specs=_x_pair_specs(x_pair, tb, n_ctx_blocks) + [
                  pl.BlockSpec((1, 1, 2 * D), lambda i: (row(i), 0, 0)),
                  pl.BlockSpec((1, D), lambda i: (0, 0)),
                  pl.BlockSpec((D, IN_COLS_R), lambda i: (0, 0))],
        out_specs=[pl.BlockSpec((tb, c[1] - c[0]), lambda i: (i, 0)) for c, _ in widths],
        out_shape=[jax.ShapeDtypeStruct((T, c[1] - c[0]), dt) for c, dt in widths],
        compiler_params=_params("parallel"),
        name="in_projection",
    )(x_pair[0], x_pair[1], mods, g1, w_in_r)


def _seq_call(kernel, name, n, nseq, blk_off, seq_ins, const_ins, out_widths, extra_outs=(), scratch=()):
    in_specs = [pl.BlockSpec((n, a.shape[1]), lambda i: (i + blk_off, 0)) for a in seq_ins]
    in_specs += [pl.BlockSpec(bs, im) for _, bs, im in const_ins]
    args = list(seq_ins) + [a for a, _, _ in const_ins]
    out_specs = [pl.BlockSpec((n, w), lambda i: (i, 0)) for w, _ in out_widths]
    out_shape = [jax.ShapeDtypeStruct((nseq * n, w), dt) for w, dt in out_widths]
    out_specs += [pl.BlockSpec(bs, im) for _, _, bs, im in extra_outs]
    out_shape += [jax.ShapeDtypeStruct(s, dt) for s, dt, _, _ in extra_outs]
    return pl.pallas_call(
        kernel, grid=(nseq,), in_specs=in_specs, out_specs=out_specs, out_shape=out_shape,
        scratch_shapes=list(scratch), compiler_params=_params("parallel"), name=name,
    )(*args)


def _fourier_kernel(u_ref, r_ref, lc_ref, ls_ref, o_ref):
    y = _dot(u_ref[...], r_ref[...])
    o_ref[...] = (_dot(lc_ref[...], y[:, :FN_W].astype(BF)) + _dot(ls_ref[...], y[:, FN_W:].astype(BF))).astype(BF)


def _dft_tables(n):
    k = np.arange(FN_GW)
    ang = 2.0 * np.pi * ((k[:, None] * k[None, :]) % FN_GW) / FN_GW
    eye = np.eye(FN_G)
    right = np.concatenate([np.kron(eye, np.cos(ang)), np.kron(eye, np.sin(ang))], axis=1)
    p = np.arange(n)
    angn = 2.0 * np.pi * ((p[:, None] * p[None, :]) % n) / n
    scale = 1.0 / np.sqrt(float(n * FN_GW))
    return (jnp.asarray(right, F32).astype(BF), jnp.asarray(np.cos(angn) * scale, F32).astype(BF),
            jnp.asarray(-np.sin(angn) * scale, F32).astype(BF))


def _fourier(zf, n, nseq, blk_off):
    right, lc, ls = _dft_tables(n)
    consts = [(right, (FN_W, 2 * FN_W), lambda i: (0, 0)), (lc, (n, n), lambda i: (0, 0)), (ls, (n, n), lambda i: (0, 0))]
    return _seq_call(_fourier_kernel, "fourier_mix", n, nseq, blk_off, [zf], consts, [(FN_W, BF)])


def _bf_parts(x, n):
    parts, rest = [], x
    for _ in range(n):
        p = rest.astype(BF)
        parts.append(p)
        rest = rest - p.astype(F32)
    return parts


def _dot_f32(a, b):
    a1, a2 = _bf_parts(a, 2)
    b1, b2 = _bf_parts(b, 2)
    return (_dot(a1, b2) + _dot(a2, b1)) + _dot(a1, b1)


def _cumulate(tri, g):
    g1, g2, g3 = _bf_parts(g, 3)
    return (_dot(tri, g3) + _dot(tri, g2)) + _dot(tri, g1)


def _log_gate(z, w_ref, b_ref):
    pre = _dot_f32(z, w_ref[...]) + b_ref[...]
    return (jnp.minimum(pre, 0.0) - jnp.log1p(jnp.exp(-jnp.abs(pre)))) * (1.0 / G_TAU)


def _gla_kernel(has_state, n, *refs):
    if has_state:
        (zqk, zvr, zsm, wgf, bgf, wgb, bgb, gout, s0f, s0b, o_ref, sf_ref, sb_ref,
         oacc_f, oacc_b, lg_f, lg_b, st_f, st_b) = refs
    else:
        (zqk, zvr, zsm, wgf, bgf, wgb, bgb, gout, o_ref, sf_ref, sb_ref,
         oacc_f, oacc_b, lg_f, lg_b, st_f, st_b) = refs
        s0f = s0b = None
    C = G_CHUNK
    nc = n // C
    ri, ci = _iota((C, C), 0), _iota((C, C), 1)
    t_idx, lane = _iota((C, 128), 0), _iota((C, 128), 1)
    s_idx = lane & (C - 1)
    left = lane < GDK
    vleft = _iota((C, 2 * GDV), 1) < GDV
    blockdiag = (_iota((2 * GDV, 2 * GDK), 0) >> 7) == (_iota((2 * GDV, 2 * GDK), 1) >> 6)

    def load_state(st, s0_ref):
        for p in range(2):
            if s0_ref is None:
                st[p] = jnp.zeros((2 * GDV, 2 * GDK), F32)
            else:
                z = jnp.zeros((GDK, GDV), F32)
                blk = jnp.concatenate([jnp.concatenate([s0_ref[0, 2 * p], z], axis=1),
                                       jnp.concatenate([z, s0_ref[0, 2 * p + 1]], axis=1)], axis=0)
                st[p] = blk.T

    def store_state(st, out_ref):
        for p in range(2):
            blk = st[p].T
            out_ref[0, 2 * p] = blk[0:GDK, 0:GDV]
            out_ref[0, 2 * p + 1] = blk[GDK:2 * GDK, GDV:2 * GDV]

    def chunk(c, reverse, lg, st, oacc):
        tri = (ci >= ri).astype(BF) if reverse else (ci <= ri).astype(BF)
        mask = (s_idx >= t_idx) if reverse else (s_idx <= t_idx)
        rows = pl.ds(pl.multiple_of(c * C, C), C)
        cum = _cumulate(tri, lg[rows, :])
        tot = cum[0:1] if reverse else cum[C - 1:C]
        q = zqk[rows, 0:GH * GDK] * (GDK ** -0.5)
        k = zqk[rows, GH * GDK:2 * GH * GDK]
        qh = (q * jnp.exp(cum)).astype(BF)
        kh = (k * jnp.exp(-cum)).astype(BF)
        kb = (k * jnp.exp(tot - cum)).astype(BF)
        dec = jnp.exp(tot)
        for p in range(2):
            ls = slice(128 * p, 128 * p + 128)
            vs = slice(256 * p, 256 * p + 256)
            qp, kp = qh[:, ls], kh[:, ls]
            zk = jnp.zeros_like(kp)
            kblk = jnp.concatenate([jnp.where(left, kp, zk), jnp.where(left, zk, kp)], axis=0)
            sc = jnp.where(mask, _dot_nt(qp, kblk), 0.0).astype(BF)
            vp = zvr[rows, vs]
            zv = jnp.zeros_like(vp)
            vblk = jnp.concatenate([jnp.where(vleft, vp, zv), jnp.where(vleft, zv, vp)], axis=0)
            stp = st[p]
            oacc[rows, vs] = _dot(sc, vblk) + _dot_nt(qp, stp.astype(BF))
            st[p] = dec[:, ls] * stp + jnp.where(blockdiag, _dot_tn(vp, kb[:, ls]), 0.0)

    def both_directions(step, carry):
        chunk(step, False, lg_f, st_f, oacc_f)
        chunk(nc - 1 - step, True, lg_b, st_b, oacc_b)
        return carry

    lg_f[...] = _log_gate(zsm[:, 0:G_RANK], wgf, bgf)
    lg_b[...] = _log_gate(zsm[:, G_RANK:2 * G_RANK], wgb, bgb)
    load_state(st_f, s0f)
    load_state(st_b, s0b)
    lax.fori_loop(0, nc, both_directions, 0, unroll=min(nc, 4))
    store_state(st_f, sf_ref)
    store_state(st_b, sb_ref)

    rb = 128
    for r0 in range(0, n, rb):
        for h in range(GH):
            hs = slice(GDV * h, GDV * h + GDV)
            oh = oacc_f[r0:r0 + rb, hs] + oacc_b[r0:r0 + rb, hs]
            oh = oh * lax.rsqrt(jnp.mean(oh * oh, axis=-1, keepdims=True) + EPS) * gout[:, hs]
            r = zvr[r0:r0 + rb, GH * GDV + hs.start:GH * GDV + hs.stop].astype(F32)
            o_ref[r0:r0 + rb, hs] = (oh * (r * _sigmoid(r))).astype(BF)


def _gla(zqk, zvr, zsm, gate_w, n, nseq, blk_off, states):
    wgf, bgf, wgb, bgb, gout = gate_w
    c2 = lambda i: (0, 0)
    consts = [(wgf, wgf.shape, c2), (bgf, bgf.shape, c2), (wgb, wgb.shape, c2), (bgb, bgb.shape, c2), (gout, gout.shape, c2)]
    st_blk = (1, GH, GDK, GDV)
    st_map = lambda i: (i, 0, 0, 0)
    if states is not None:
        consts += [(s, st_blk, st_map) for s in states]
    extra = [((nseq, GH, GDK, GDV), F32, st_blk, st_map)] * 2
    scratch = ([pltpu.VMEM((n, GH * GDV), F32)] * 2 + [pltpu.VMEM((n, GH * GDK), F32)] * 2
               + [pltpu.VMEM((2, 2 * GDV, 2 * GDK), F32)] * 2)
    return _seq_call(functools.partial(_gla_kernel, states is not None, n), "gla_mixer", n, nseq, blk_off,
                     [zqk, zvr, zsm], consts, [(GH * GDV, BF)], extra_outs=extra, scratch=scratch)


def _mla_kernel(latent, n, past, *refs):
    if latent:
        (zmla, zsm, gq, wq, gkv, wkv, cckv, ckr, cosq, sinq, cosk, sink, o_ref, qs, kns, vs, krs) = refs
    else:
        (zmla, zsm, gq, wq, gkv, wkv, o_ref, ckv_ref, qs, kns, vs, krs) = refs
    sk = past + n
    scale = (M_NOPE + M_ROPE) ** -0.5
    nw, rw = MH * M_NOPE, MH * M_ROPE
    qa = _dot(_rms(zmla[:, 0:M_QL], gq[...]).astype(BF), wq[...])
    qr = qa[:, nw:nw + rw]
    if latent:
        qr = qr * cosq[...] + qa[:, nw + rw:nw + 2 * rw] * sinq[...]
    qs[:, 0:nw] = qa[:, 0:nw] * scale
    qs[:, nw:nw + rw] = qr * scale
    ckv = _rms(zmla[:, M_QL:M_QL + M_KVL], gkv[...])
    kv = _dot(ckv.astype(BF), wkv[...])
    kr = zsm[:, 32:64]
    if latent:
        kr = kr * cosk[...] + zsm[:, 64:96] * sink[...]
        kvc = _dot(cckv[0].astype(BF), wkv[...])
        kns[0:past, :] = kvc[:, 0:nw].astype(BF)
        vs[0:past, :] = kvc[:, nw:].astype(BF)
        krs[0:past, :] = jnp.concatenate([ckr[0]] * 4, axis=1).astype(BF)
    else:
        ckv_ref[...] = ckv
    kns[past:sk, :] = kv[:, 0:nw].astype(BF)
    vs[past:sk, :] = kv[:, nw:].astype(BF)
    krs[past:sk, :] = jnp.concatenate([kr] * 4, axis=1).astype(BF)

    qb = min(Q_BLOCK, n)
    lane = _iota((qb, 128), 1)

    def block(step, carry):
        rows = pl.ds(pl.multiple_of(step * qb, qb), qb)
        for p in range(MH // 2):
            ls = slice(128 * p, 128 * p + 128)
            qn = qs[rows, ls]
            quad = (2 * p) // 4
            qrp = qs[rows, nw + 128 * quad:nw + 128 * quad + 128]
            rhs = jnp.concatenate([kns[:, ls], krs[...]], axis=1)
            vp = vs[:, ls]
            o_pair = None
            for hh in range(2):
                j = (2 * p + hh) % 4
                qn_m = jnp.where((lane >> 6) == hh, qn, 0.0).astype(BF)
                qr_m = jnp.where((lane >> 5) == j, qrp, 0.0).astype(BF)
                s = _dot_nt(jnp.concatenate([qn_m, qr_m], axis=1), rhs)
                e = jnp.exp(s - jnp.max(s, axis=-1, keepdims=True))
                pv = _dot(e.astype(BF), vp) / jnp.sum(e, axis=-1, keepdims=True)
                o_pair = pv if hh == 0 else jnp.where(lane < M_V, o_pair, pv)
            o_ref[rows, ls] = o_pair.astype(BF)
        return carry

    lax.fori_loop(0, n // qb, block, 0)


def _rope_tables(n):
    half = M_ROPE // 2
    pos = jnp.arange(n)
    row = (pos // GRID_W).astype(F32)
    col = (pos % GRID_W).astype(F32)
    inv = ROPE_BASE ** (-jnp.arange(0, half, 2, dtype=F32) / half)
    ang = jnp.concatenate([row[:, None] * inv, col[:, None] * inv], axis=-1)
    cos = jnp.repeat(jnp.cos(ang), 2, axis=-1)
    sin = jnp.repeat(jnp.sin(ang), 2, axis=-1) * jnp.tile(jnp.asarray([-1.0, 1.0], F32), half)
    return jnp.tile(cos, (1, MH)), jnp.tile(sin, (1, MH)), cos, sin


def _mla(zmla, zsm, w, n, nseq, blk_off, cache):
    gq, wq, gkv, wkv = w
    c2 = lambda i: (0, 0)
    consts = [(gq, gq.shape, c2), (wq, wq.shape, c2), (gkv, gkv.shape, c2), (wkv, wkv.shape, c2)]
    past = 0
    extra = []
    if cache is not None:
        cckv, ckr = cache
        past = cckv.shape[1]
        c3 = lambda i: (i, 0, 0)
        consts += [(cckv, (1, past, M_KVL), c3), (ckr, (1, past, M_ROPE), c3)]
        consts += [(t, t.shape, c2) for t in _rope_tables(n)]
    else:
        extra = [((nseq * n, M_KVL), F32, (n, M_KVL), lambda i: (i, 0))]
    sk = past + n
    scratch = [pltpu.VMEM((n, MH * (M_NOPE + M_ROPE)), F32), pltpu.VMEM((sk, MH * M_NOPE), BF),
               pltpu.VMEM((sk, MH * M_V), BF), pltpu.VMEM((sk, 128), BF)]
    return _seq_call(functools.partial(_mla_kernel, cache is not None, n, past), "mla_mixer", n, nseq, blk_off,
                     [zmla, zsm], consts, [(MH * M_V, BF)], extra_outs=extra, scratch=scratch)


def _route(logits_t, bias):
    nt = logits_t.shape[1]
    gsz = N_EXP // N_GRP
    scores = _sigmoid(logits_t)
    sel = scores + bias
    neg = -jnp.inf
    sub = _iota((gsz, nt), 0)
    tops = []
    for g in range(N_GRP):
        blk = sel[gsz * g:gsz * g + gsz]
        m1 = jnp.max(blk, axis=0, keepdims=True)
        first = jnp.min(jnp.where(blk == m1, sub, gsz), axis=0, keepdims=True)
        m2 = jnp.max(jnp.where(sub == first, neg, blk), axis=0, keepdims=True)
        tops.append(m1 + m2)
    gs = jnp.concatenate(tops, axis=0)
    gidx = _iota((N_GRP, nt), 0)
    grank = jnp.zeros((N_GRP, nt), jnp.int32)
    for j in range(N_GRP):
        rj = gs[j:j + 1]
        grank += ((rj > gs) | ((rj == gs) & (gidx > j))).astype(jnp.int32)
    keep = grank < TOPK_GRP
    masked = jnp.concatenate(
        [jnp.where(jnp.broadcast_to(keep[g:g + 1], (gsz, nt)), sel[gsz * g:gsz * g + gsz], neg) for g in range(N_GRP)], axis=0)
    eidx = _iota((N_EXP, nt), 0)
    chosen = eidx < 0
    work = masked
    for _ in range(TOP_K):
        top = jnp.max(work, axis=0, keepdims=True)
        first = jnp.min(jnp.where(work == top, eidx, N_EXP), axis=0, keepdims=True)
        hit = eidx == first
        chosen = chosen | hit
        work = jnp.where(hit, neg, work)
    w = jnp.where(chosen, scores, 0.0)
    return chosen, w / jnp.sum(w, axis=0, keepdims=True) * ROUTED_SCALE


def _dispatch_meta(chosen, gates_t):
    tb = chosen.shape[1]
    sel = chosen.astype(F32)
    selb = sel.astype(BF)
    earlier = (_iota((tb, tb), 0) < _iota((tb, tb), 1)).astype(BF)
    rank = _dot(selb, earlier)
    cnt = jnp.sum(sel, axis=1, keepdims=True)
    padded = jnp.floor((cnt + (ROW_GRANULE - 1)) * (1.0 / ROW_GRANULE)) * ROW_GRANULE
    below = (_iota((N_EXP, N_EXP), 1) < _iota((N_EXP, N_EXP), 0)).astype(BF)
    start = _dot(below, jnp.broadcast_to(padded, (N_EXP, 128)).astype(BF))[:, 0:1]
    pos = start + rank
    kidx = _dot(below, selb)
    pos8, gate8 = [], []
    for k in range(TOP_K):
        hit = chosen & (kidx == float(k))
        pos8.append(jnp.sum(jnp.where(hit, pos, 0.0), axis=0, keepdims=True))
        gate8.append(jnp.sum(jnp.where(hit, gates_t, 0.0), axis=0, keepdims=True))
    return (jnp.concatenate(pos8, axis=0).astype(jnp.int32), jnp.concatenate(gate8, axis=0), cnt)


def _merge_kernel(n_ctx_blocks, xc_ref, xl_ref, fc_ref, fl_ref, ogc_ref, ogl_ref, omc_ref, oml_ref, zg_ref, m_ref,
                  wbf, wbg, wbm, wout, gn2, wrt, brt, xm_ref, h2_ref, pos_ref, gate_ref, cnt_ref):
    is_ctx = pl.program_id(0) < n_ctx_blocks
    ya = _dot(jnp.where(is_ctx, fc_ref[...], fl_ref[...]), wbf[...])
    yb = _dot(jnp.where(is_ctx, ogc_ref[...], ogl_ref[...]), wbg[...])
    yc = _dot(jnp.where(is_ctx, omc_ref[...], oml_ref[...]), wbm[...])
    merged = (_sigmoid(zg_ref[:, 0:D]) * ya.astype(BF) + _sigmoid(zg_ref[:, D:2 * D]) * yb.astype(BF)
              + _sigmoid(zg_ref[:, 2 * D:3 * D]) * yc.astype(BF))
    xm = jnp.where(is_ctx, xc_ref[...], xl_ref[...]) + m_ref[0, :, 2 * D:3 * D] * _dot(merged, wout[...])
    xm_ref[...] = xm
    h2 = _rms(xm, gn2[...]) * (1.0 + m_ref[0, :, 4 * D:5 * D]) + m_ref[0, :, 3 * D:4 * D]
    h2_ref[...] = h2.astype(BF)
    chosen, gates_t = _route(_dot_nt(wrt[...], h2, precision=lax.Precision.HIGHEST), brt[...])
    for sb in range(gates_t.shape[1] // DISP_BLOCK):
        ls = slice(sb * DISP_BLOCK, (sb + 1) * DISP_BLOCK)
        pos8, gate8, cnt = _dispatch_meta(chosen[:, ls], gates_t[:, ls])
        pos_ref[:, ls] = pos8
        gate_ref[:, ls] = gate8
        cnt_ref[sb] = jnp.broadcast_to(cnt, (N_EXP, 128))


def _merge(x_pair, T, mix_ctx, mix_lat, zg, mods, w, n_ctx_blocks, blocks_per_lat):
    tb = TOKEN_BLOCK
    row = functools.partial(_mod_row, n_ctx_blocks, blocks_per_lat)
    rb = lambda wd: pl.BlockSpec((tb, wd), lambda i: (i, 0))
    cb = lambda a: pl.BlockSpec(a.shape, lambda i: (0, 0))
    ctx_b = lambda wd: pl.BlockSpec((tb, wd), lambda i: (jnp.minimum(i, n_ctx_blocks - 1), 0))
    lat_b = lambda wd: pl.BlockSpec((tb, wd), lambda i: (jnp.maximum(i - n_ctx_blocks, 0), 0))
    mix_specs, mix_args = [], []
    for a_c, a_l in zip(mix_ctx, mix_lat):
        mix_specs += [ctx_b(a_c.shape[1]), lat_b(a_l.shape[1])]
        mix_args += [a_c, a_l]
    return pl.pallas_call(
        functools.partial(_merge_kernel, n_ctx_blocks),
        grid=(T // tb,),
        in_specs=_x_pair_specs(x_pair, tb, n_ctx_blocks) + mix_specs + [rb(3 * D),
                  pl.BlockSpec((1, 1, 6 * D), lambda i: (row(i), 0, 0))] + [cb(a) for a in w],
        out_specs=[rb(D), rb(D), pl.BlockSpec((TOP_K, tb), lambda i: (0, i)), pl.BlockSpec((TOP_K, tb), lambda i: (0, i)),
                   pl.BlockSpec((tb // DISP_BLOCK, N_EXP, 128), lambda i: (i, 0, 0))],
        out_shape=[jax.ShapeDtypeStruct((T, D), F32), jax.ShapeDtypeStruct((T, D), BF),
                   jax.ShapeDtypeStruct((TOP_K, T), jnp.int32), jax.ShapeDtypeStruct((TOP_K, T), F32),
                   jax.ShapeDtypeStruct((T // DISP_BLOCK, N_EXP, 128), F32)],
        compiler_params=_params("parallel"),
        name="merge_route",
    )(x_pair[0], x_pair[1], *mix_args, zg, mods, *w)


def _silu_mul(a, b):
    return a * _sigmoid(a) * b


DISP_ROW_CHUNK = 256


def _placement(pos_ref, weight_ref, r0):
    rows = _iota((DISP_ROW_CHUNK, DISP_BLOCK), 0) + r0
    p = jnp.zeros((DISP_ROW_CHUNK, DISP_BLOCK), F32)
    for k in range(TOP_K):
        w = 1.0 if weight_ref is None else weight_ref[k:k + 1, :]
        p = jnp.where(rows == pos_ref[k:k + 1, :], w, p)
    return p.astype(BF)


def _dispatch_kernel(n_blocks, h_ref, pos_ref, xs_ref):
    h = jnp.where(pl.program_id(0) < n_blocks, h_ref[...], jnp.zeros_like(h_ref))
    for r0 in range(0, DISP_ROWS, DISP_ROW_CHUNK):
        xs_ref[r0:r0 + DISP_ROW_CHUNK, :] = _dot(_placement(pos_ref, None, r0), h).astype(BF)


def _dispatch(h2, pos8):
    T = h2.shape[0]
    nblk = T // DISP_BLOCK
    last = lambda b: jnp.minimum(b, nblk - 1)
    return pl.pallas_call(
        functools.partial(_dispatch_kernel, nblk),
        grid=(nblk + 1,),
        in_specs=[pl.BlockSpec((DISP_BLOCK, D), lambda b: (last(b), 0)),
                  pl.BlockSpec((TOP_K, DISP_BLOCK), lambda b: (0, last(b)))],
        out_specs=pl.BlockSpec((DISP_ROWS, D), lambda b: (b, 0)),
        out_shape=jax.ShapeDtypeStruct(((nblk + 1) * DISP_ROWS, D), BF),
        compiler_params=_params("parallel"),
        name="moe_dispatch",
    )(h2, pos8)


def _tile_tables(cnt, n_chunks_max, n_tiles_max):
    nblk = cnt.shape[0]
    nch = (cnt + (ROW_GRANULE - 1)) // ROW_GRANULE
    first = jnp.arange(nblk, dtype=jnp.int32)[:, None] * BLOCK_CHUNKS + jnp.cumsum(nch, axis=1) - nch
    tiles_e = (jnp.sum(nch, axis=0) + (CHUNKS_PER_TILE - 1)) // CHUNKS_PER_TILE
    span_e = tiles_e * CHUNKS_PER_TILE
    exp_start = jnp.cumsum(span_e) - span_e
    j = jnp.arange(n_chunks_max, dtype=jnp.int32)
    e_j = jnp.sum((exp_start[None, :] <= j[:, None]).astype(jnp.int32), axis=1) - 1
    onehot = (e_j[:, None] == jnp.arange(N_EXP, dtype=jnp.int32)[None, :]).astype(F32)
    rows_of = lambda tab: jnp.dot(onehot, tab.astype(F32), precision=lax.Precision.HIGHEST).astype(jnp.int32)
    local = j - rows_of(exp_start[:, None])[:, 0]
    blk_len = rows_of(nch.T)
    blk_start = rows_of(jnp.cumsum(nch.T, axis=1) - nch.T)
    inside = (blk_start <= local[:, None]) & (local[:, None] < blk_start + blk_len)
    real = jnp.any(inside, axis=1)
    chunk = jnp.sum(jnp.where(inside, rows_of(first.T) + (local[:, None] - blk_start), 0), axis=1)
    pad_rank = jnp.cumsum(jnp.where(real, 0, 1)) - 1
    src = jnp.where(real, chunk, BLOCK_CHUNKS - 1).astype(jnp.int32)
    dst = jnp.where(real, chunk, nblk * BLOCK_CHUNKS + pad_rank % BLOCK_CHUNKS).astype(jnp.int32)
    tile_end = jnp.cumsum(tiles_e)
    i = jnp.arange(n_tiles_max, dtype=jnp.int32)
    tile_expert = jnp.minimum(jnp.sum((tile_end[None, :] <= i[:, None]).astype(jnp.int32), axis=1), N_EXP - 1)
    ids = jnp.arange(N_EXP, dtype=jnp.int32)
    has_tiles = tiles_e > 0
    later = (ids[None, :] > ids[:, None]) & has_tiles[None, :]
    next_e = jnp.min(jnp.where(later, ids[None, :], N_EXP), axis=1)
    run_parity = (jnp.cumsum(has_tiles.astype(jnp.int32)) - 1) % 2
    of_tile = lambda v: jnp.sum(jnp.where(tile_expert[:, None] == ids[None, :], v[None, :], 0), axis=1)
    opens = (i == 0) | (tile_expert != jnp.concatenate([tile_expert[:1], tile_expert[:-1]]))
    wctl = jnp.concatenate([opens.astype(jnp.int32), jnp.minimum(of_tile(next_e), N_EXP - 1), of_tile(run_parity),
                            (of_tile(next_e) < N_EXP).astype(jnp.int32)]).astype(jnp.int32)
    return src, dst, tile_expert.astype(jnp.int32), wctl, tile_end[-1:].astype(jnp.int32)


def _expert_kernel(layer, src_ref, dst_ref, texp_ref, wctl_ref, nused_ref, xs_hbm, wg_hbm, wu_hbm, wd_hbm, ys_hbm,
                   xbuf, ybuf, wg_f, wu_f, wd_f, wgu_bf, wd_bf, gsem, ssem, wsem):
    i = pl.program_id(0)
    nt = pl.num_programs(0)
    n_used = nused_ref[0]
    slot = lax.rem(i, 2)

    def weight_copies(e, buf, do):
        for w_hbm, w_f in ((wg_hbm, wg_f), (wu_hbm, wu_f), (wd_hbm, wd_f)):
            do(pltpu.make_async_copy(w_hbm.at[layer, e], w_f.at[buf], wsem.at[buf]))

    def chunk_copies(tile, slot_, to_buffer, do):
        for c in range(CHUNKS_PER_TILE):
            j = tile * CHUNKS_PER_TILE + c
            if to_buffer:
                cp = pltpu.make_async_copy(xs_hbm.at[src_ref[j]], xbuf.at[slot_, c], gsem.at[slot_])
            else:
                cp = pltpu.make_async_copy(ybuf.at[slot_, c], ys_hbm.at[dst_ref[j]], ssem.at[slot_])
            do(cp, c)

    start = lambda cp, c: cp.start(priority=c % 2)
    wait = lambda cp, c: cp.wait()

    @pl.when(i == 0)
    def _():
        chunk_copies(0, 0, True, start)
        weight_copies(texp_ref[0], 0, lambda cp: cp.start())

    @pl.when(i == n_used)
    def _():
        chunk_copies(i, slot, True, wait)

    @pl.when(i < n_used)
    def _():
        chunk_copies(i, slot, True, wait)
        chunk_copies(i + 1, 1 - slot, True, start)

        @pl.when(wctl_ref[i] == 1)
        def _():
            buf = wctl_ref[2 * nt + i]
            weight_copies(texp_ref[i], buf, lambda cp: cp.wait())

            @pl.when(wctl_ref[3 * nt + i] == 1)
            def _():
                weight_copies(wctl_ref[nt + i], 1 - buf, lambda cp: cp.start())

            wgu_bf[:, 0:E_DIM] = wg_f[buf].astype(BF)
            wgu_bf[:, E_DIM:2 * E_DIM] = wu_f[buf].astype(BF)
            wd_bf[...] = wd_f[buf].astype(BF)

        gu = _dot(xbuf[slot].reshape(EXP_TILE, D), wgu_bf[...])
        hid = _silu_mul(gu[:, 0:E_DIM], gu[:, E_DIM:2 * E_DIM])
        ybuf[slot] = _dot(hid.astype(BF), wd_bf[...]).astype(BF).reshape(CHUNKS_PER_TILE, ROW_GRANULE, D)
        chunk_copies(i, slot, False, start)

        @pl.when(i >= 1)
        def _():
            chunk_copies(i - 1, 1 - slot, False, wait)

        @pl.when(i == n_used - 1)
        def _():
            chunk_copies(i, slot, False, wait)


def _experts(layer, xs, tables, w_eg, w_eu, w_ed, n_tiles_max):
    src, dst, tile_expert, wctl, n_used = tables
    chunks = xs.reshape(-1, ROW_GRANULE, D)
    anywhere = pl.BlockSpec(memory_space=pl.ANY)
    grid_spec = pltpu.PrefetchScalarGridSpec(
        num_scalar_prefetch=5,
        grid=(n_tiles_max + 1,),
        in_specs=[anywhere, anywhere, anywhere, anywhere],
        out_specs=anywhere,
        scratch_shapes=[pltpu.VMEM((2, CHUNKS_PER_TILE, ROW_GRANULE, D), BF),
                        pltpu.VMEM((2, CHUNKS_PER_TILE, ROW_GRANULE, D), BF),
                        pltpu.VMEM((2, D, E_DIM), F32), pltpu.VMEM((2, D, E_DIM), F32), pltpu.VMEM((2, E_DIM, D), F32),
                        pltpu.VMEM((D, 2 * E_DIM), BF), pltpu.VMEM((E_DIM, D), BF),
                        pltpu.SemaphoreType.DMA((2,)), pltpu.SemaphoreType.DMA((2,)), pltpu.SemaphoreType.DMA((2,))],
    )
    return pl.pallas_call(
        functools.partial(_expert_kernel, layer),
        grid_spec=grid_spec,
        out_shape=jax.ShapeDtypeStruct(chunks.shape, chunks.dtype),
        input_output_aliases={5: 0},
        compiler_params=_params("arbitrary"),
        name="moe_experts",
    )(src, dst, tile_expert, wctl, n_used, chunks, w_eg, w_eu, w_ed).reshape(xs.shape)


def _combine_kernel(final, ys_ref, pos_ref, gate_ref, h_ref, sg_ref, su_ref, sd_ref, x_ref, m_ref, gf_ref, o_ref):
    routed = jnp.zeros((DISP_BLOCK, D), F32)
    for r0 in range(0, DISP_ROWS, DISP_ROW_CHUNK):
        routed = routed + _dot_tn(_placement(pos_ref, gate_ref, r0), ys_ref[r0:r0 + DISP_ROW_CHUNK, :])
    h = h_ref[...]
    sh = _silu_mul(_dot(h, sg_ref[0].astype(BF)), _dot(h, su_ref[0].astype(BF)))
    out = x_ref[...] + m_ref[0] * (routed + _dot(sh.astype(BF), sd_ref[0].astype(BF)))
    if final:
        out = _rms(out, gf_ref[...])
    o_ref[...] = out


def _combine(layer, final, ys, pos8, gate8, h2, w_sg, w_su, w_sd, xm, mods, g_final, n_ctx_blocks, blocks_per_lat):
    T = h2.shape[0]
    tb = DISP_BLOCK
    row = functools.partial(_mod_row, n_ctx_blocks, blocks_per_lat)
    return pl.pallas_call(
        functools.partial(_combine_kernel, final),
        grid=(T // tb,),
        in_specs=[pl.BlockSpec((DISP_ROWS, D), lambda b: (b, 0)),
                  pl.BlockSpec((TOP_K, tb), lambda b: (0, b)),
                  pl.BlockSpec((TOP_K, tb), lambda b: (0, b)),
                  pl.BlockSpec((tb, D), lambda b: (b, 0)),
                  pl.BlockSpec((1, D, E_DIM), lambda b: (layer, 0, 0)),
                  pl.BlockSpec((1, D, E_DIM), lambda b: (layer, 0, 0)),
                  pl.BlockSpec((1, E_DIM, D), lambda b: (layer, 0, 0)),
                  pl.BlockSpec((tb, D), lambda b: (b, 0)),
                  pl.BlockSpec((1, 1, D), lambda b: (row(b), 0, 5)),
                  pl.BlockSpec((1, D), lambda b: (0, 0))],
        out_specs=pl.BlockSpec((tb, D), lambda b: (b, 0)),
        out_shape=jax.ShapeDtypeStruct((T, D), F32),
        compiler_params=_params("parallel"),
        name="moe_combine",
    )(ys, pos8, gate8, h2, w_sg, w_su, w_sd, xm, mods, g_final)


def _moe(layer, final, h2, pos8, gate8, cnt, w_eg, w_eu, w_ed, w_sg, w_su, w_sd, xm, mods, g_final, disp_blocks):
    T = h2.shape[0]
    nblk = T // DISP_BLOCK
    n_chunks_max = (TOP_K * T + N_EXP * nblk * (ROW_GRANULE - 1)) // ROW_GRANULE + N_EXP * (CHUNKS_PER_TILE - 1)
    n_tiles_max = -(-n_chunks_max // CHUNKS_PER_TILE)
    xs = _dispatch(h2, pos8)
    tables = _tile_tables(cnt[:, :, 0].astype(jnp.int32), (n_tiles_max + 1) * CHUNKS_PER_TILE, n_tiles_max + 1)
    ys = _experts(layer, xs, tables, w_eg, w_eu, w_ed, n_tiles_max)
    return _combine(layer, final, ys, pos8, gate8, h2, w_sg, w_su, w_sd, xm, mods, g_final, *disp_blocks)


def _reorder_w_in(w_in):
    kr = w_in[:, :, 2592:2624]
    kr_sw = kr.reshape(kr.shape[0], D, M_ROPE // 2, 2)[..., ::-1].reshape(kr.shape)
    pad = jnp.zeros(kr.shape, w_in.dtype)
    out = jnp.concatenate([w_in[:, :, :1920], w_in[:, :, 1952:2592], w_in[:, :, 1920:1952], kr, kr_sw, pad,
                           w_in[:, :, 2624:]], axis=2)
    return out.astype(BF)


def _reorder_w_q(w_q_up, with_swap):
    L = w_q_up.shape[0]
    w = w_q_up.reshape(L, M_QL, MH, M_NOPE + M_ROPE)
    nope = w[..., :M_NOPE].reshape(L, M_QL, MH * M_NOPE)
    rope = w[..., M_NOPE:]
    parts = [nope, rope.reshape(L, M_QL, MH * M_ROPE)]
    if with_swap:
        parts.append(rope.reshape(L, M_QL, MH, M_ROPE // 2, 2)[..., ::-1].reshape(L, M_QL, MH * M_ROPE))
    return jnp.concatenate(parts, axis=2).astype(BF)


def _reorder_w_kv(w_kv_up):
    L = w_kv_up.shape[0]
    w = w_kv_up.reshape(L, M_KVL, MH, M_NOPE + M_V)
    return jnp.concatenate([w[..., :M_NOPE].reshape(L, M_KVL, MH * M_NOPE),
                            w[..., M_NOPE:].reshape(L, M_KVL, MH * M_V)], axis=2).astype(BF)


def kernel(x_prompt, x_sample, state_gla_fwd, state_gla_bwd, cache_mla_ckv, cache_mla_krope, c, c_ctx, w_mod, b_mod, g_norm1, g_norm2, w_in, w_gla_gate_f, b_gla_gate_f, w_gla_gate_b, b_gla_gate_b, g_gla_out, g_q_a, w_q_up, g_kv_a, w_kv_up, w_br_fourier, w_br_gla, w_br_mla, w_out, w_router, b_router, w_exp_gate, w_exp_up, w_exp_down, w_sh_gate, w_sh_up, w_sh_down, g_final):
    nb, sl, _ = x_prompt.shape
    db, dl, _ = x_sample.shape
    L = w_mod.shape[0]
    t_ctx, t_lat = nb * sl, db * dl
    T = t_ctx + t_lat
    assert sl % G_CHUNK == 0 and dl % G_CHUNK == 0 and dl % GRID_W == 0
    assert t_ctx % dl == 0 and dl % TOKEN_BLOCK == 0 and TOKEN_BLOCK % DISP_BLOCK == 0
    assert t_ctx % TOKEN_BLOCK == 0 and dl % Q_BLOCK == 0 and 1 + db <= 8

    x_pair = (x_prompt.reshape(t_ctx, D), x_sample.reshape(t_lat, D), 0)
    cond = jnp.concatenate([c_ctx[None, :], c, jnp.zeros((7 - db, D), F32)], axis=0)
    mods_all = _modulation(cond, w_mod, b_mod)

    w_in_r = _reorder_w_in(w_in)
    wq_ctx = _reorder_w_q(w_q_up, False)
    wq_lat = _reorder_w_q(w_q_up, True)
    wkv_r = _reorder_w_kv(w_kv_up)
    lat_off = t_ctx // dl

    new_f, new_b, new_ckv, new_kr = [], [], [], []
    for l in range(L):
        mods = mods_all[l].reshape(8, 1, 6 * D)
        tok = (t_ctx // TOKEN_BLOCK, dl // TOKEN_BLOCK)
        zf, zqk, zvr, zmla, zsm, zg = _in_projection(x_pair, T, mods, g_norm1[l][None, :], w_in_r[l], *tok)

        (f_c,) = _fourier(zf, sl, nb, 0)
        (f_l,) = _fourier(zf, dl, db, lat_off)

        gate_w = (w_gla_gate_f[l], b_gla_gate_f[l][None, :], w_gla_gate_b[l], b_gla_gate_b[l][None, :],
                  g_gla_out[l].reshape(1, GH * GDV))
        og_c, s_f, s_b = _gla(zqk, zvr, zsm, gate_w, sl, nb, 0, None)
        og_l, _, _ = _gla(zqk, zvr, zsm, gate_w, dl, db, lat_off, (state_gla_fwd[:, l], state_gla_bwd[:, l]))

        gq, gkv = g_q_a[l][None, :], g_kv_a[l][None, :]
        om_c, ckv = _mla(zmla, zsm, (gq, wq_ctx[l], gkv, wkv_r[l]), sl, nb, 0, None)
        (om_l,) = _mla(zmla, zsm, (gq, wq_lat[l], gkv, wkv_r[l]), dl, db, lat_off,
                       (cache_mla_ckv[:, l], cache_mla_krope[:, l]))

        mw = (w_br_fourier[l].astype(BF), w_br_gla[l].astype(BF), w_br_mla[l].astype(BF), w_out[l].astype(BF),
              g_norm2[l][None, :], w_router[l].T, b_router[l][:, None])
        xm, h2, pos8, gate8, cnt = _merge(x_pair, T, (f_c, og_c, om_c), (f_l, og_l, om_l), zg, mods, mw, *tok)

        x = _moe(l, l == L - 1, h2, pos8, gate8, cnt, w_exp_gate, w_exp_up, w_exp_down, w_sh_gate, w_sh_up, w_sh_down,
                 xm, mods, g_final[None, :], (t_ctx // DISP_BLOCK, dl // DISP_BLOCK))
        x_pair = (x, x, t_ctx // TOKEN_BLOCK)

        new_f.append(s_f)
        new_b.append(s_b)
        new_ckv.append(ckv.reshape(nb, sl, M_KVL))
        new_kr.append(zsm[:t_ctx, 32:64].reshape(nb, sl, M_ROPE))

    y_prompt = x[:t_ctx].reshape(nb, sl, D)
    y_sample = x[t_ctx:].reshape(db, dl, D)
    return (y_prompt, y_sample, jnp.stack(new_f, axis=1), jnp.stack(new_b, axis=1),
            jnp.stack(new_ckv, axis=1), jnp.stack(new_kr, axis=1))
```

```python
import functools

import numpy as np
import jax
import jax.numpy as jnp
from jax import lax
from jax.experimental import pallas as pl
from jax.experimental.pallas import tpu as pltpu

F32 = jnp.float32
BF = jnp.bfloat16

D = 1024
GRID_W = 64
FN_G, FN_GW = 4, 96
FN_W = FN_G * FN_GW
GH, GDK, GDV = 4, 64, 128
G_RANK = 16
G_TAU = 16.0
G_CHUNK = 64
MH, M_NOPE, M_ROPE, M_V = 8, 64, 32, 64
M_QL, M_KVL = 384, 256
ROPE_BASE = 10000.0
N_EXP, TOP_K, N_GRP, TOPK_GRP = 64, 8, 8, 4
E_DIM = 256
ROUTED_SCALE = 2.5
EPS = 1e-6

C_F = (0, 384)
C_QK = (384, 896)
C_VR = (896, 1920)
C_MLA = (1920, 2560)
C_SM = (2560, 2688)
C_G = (2688, 5760)
IN_COLS_R = 5760

VMEM_LIMIT_V7X = 56 * 1024 * 1024
TOKEN_BLOCK = 512
Q_BLOCK = 256
DISP_BLOCK = 256
ROW_GRANULE = 16
DISP_ROWS = -(-(TOP_K * DISP_BLOCK + N_EXP * (ROW_GRANULE - 1)) // 256) * 256
EXP_TILE = 512
CHUNKS_PER_TILE = EXP_TILE // ROW_GRANULE
BLOCK_CHUNKS = DISP_ROWS // ROW_GRANULE


def _params(*sem):
    return pltpu.CompilerParams(dimension_semantics=sem, vmem_limit_bytes=VMEM_LIMIT_V7X)


def _dot(a, b):
    return jnp.dot(a, b, preferred_element_type=F32)


def _dot_hi(a, b):
    return jnp.dot(a, b, precision=lax.Precision.HIGHEST, preferred_element_type=F32)


def _dot_nt(a, b, precision=None):
    return lax.dot_general(a, b, (((1,), (1,)), ((), ())), precision=precision, preferred_element_type=F32)


def _dot_tn(a, b):
    return lax.dot_general(a, b, (((0,), (0,)), ((), ())), preferred_element_type=F32)


def _sigmoid(x):
    return 1.0 / (1.0 + jnp.exp(-x))


def _rms(x, g):
    return x * lax.rsqrt(jnp.mean(x * x, axis=-1, keepdims=True) + EPS) * g


def _iota(shape, dim):
    return lax.broadcasted_iota(jnp.int32, shape, dim)


def _mod_row(n_ctx_blocks, blocks_per_lat, i):
    return jnp.where(i < n_ctx_blocks, 0, 1 + (i - n_ctx_blocks) // blocks_per_lat)


def _mod_kernel(c_ref, w_ref, b_ref, o_ref):
    c = c_ref[...]
    o_ref[0] = _dot_hi(c * _sigmoid(c), w_ref[0]) + b_ref[0]


def _modulation(cond, w_mod, b_mod):
    L = w_mod.shape[0]
    rows = cond.shape[0]
    tn = 1536
    return pl.pallas_call(
        _mod_kernel,
        grid=(L, 6 * D // tn),
        in_specs=[pl.BlockSpec((rows, D), lambda l, j: (0, 0)),
                  pl.BlockSpec((1, D, tn), lambda l, j: (l, 0, j)),
                  pl.BlockSpec((1, 1, tn), lambda l, j: (l, 0, j))],
        out_specs=pl.BlockSpec((1, rows, tn), lambda l, j: (l, 0, j)),
        out_shape=jax.ShapeDtypeStruct((L, rows, 6 * D), F32),
        compiler_params=_params("parallel", "parallel"),
        name="modulation",
    )(cond, w_mod, b_mod.reshape(L, 1, 6 * D))


def _inproj_kernel(n_ctx_blocks, xc_ref, xl_ref, m_ref, g_ref, w_ref, of_ref, oqk_ref, ovr_ref, omla_ref, osm_ref, og_ref):
    y = _rms(jnp.where(pl.program_id(0) < n_ctx_blocks, xc_ref[...], xl_ref[...]), g_ref[...])
    h = (y * (1.0 + m_ref[0, :, D:2 * D]) + m_ref[0, :, 0:D]).astype(BF)
    of_ref[...] = _dot(h, w_ref[:, C_F[0]:C_F[1]]).astype(BF)
    oqk_ref[...] = _dot(h, w_ref[:, C_QK[0]:C_QK[1]])
    ovr_ref[...] = _dot(h, w_ref[:, C_VR[0]:C_VR[1]]).astype(BF)
    omla_ref[...] = _dot(h, w_ref[:, C_MLA[0]:C_MLA[1]])
    osm_ref[...] = _dot(h, w_ref[:, C_SM[0]:C_SM[1]])
    og_ref[...] = _dot(h, w_ref[:, C_G[0]:C_G[1]]).astype(BF)


def _x_pair_specs(x_pair, tb, n_ctx_blocks):
    _, _, lat_off = x_pair
    return [pl.BlockSpec((tb, D), lambda i: (jnp.minimum(i, n_ctx_blocks - 1), 0)),
            pl.BlockSpec((tb, D), lambda i: (jnp.maximum(i - n_ctx_blocks, 0) + lat_off, 0))]


def _in_projection(x_pair, T, mods, g1, w_in_r, n_ctx_blocks, blocks_per_lat):
    tb = TOKEN_BLOCK
    row = functools.partial(_mod_row, n_ctx_blocks, blocks_per_lat)
    widths = [(C_F, BF), (C_QK, F32), (C_VR, BF), (C_MLA, F32), (C_SM, F32), (C_G, BF)]
    return pl.pallas_call(
        functools.partial(_inproj_kernel, n_ctx_blocks),
        grid=(T // tb,),
        in_specs=_x_pair_specs(x_pair, tb, n_ctx_blocks) + [
                  pl.BlockSpec((1, 1, 2 * D), lambda i: (row(i), 0, 0)),
                  pl.BlockSpec((1, D), lambda i: (0, 0)),
                  pl.BlockSpec((D, IN_COLS_R), lambda i: (0, 0))],
        out_specs=[pl.BlockSpec((tb, c[1] - c[0]), lambda i: (i, 0)) for c, _ in widths],
        out_shape=[jax.ShapeDtypeStruct((T, c[1] - c[0]), dt) for c, dt in widths],
        compiler_params=_params("parallel"),
        name="in_projection",
    )(x_pair[0], x_pair[1], mods, g1, w_in_r)


def _seq_call(kernel, name, n, nseq, blk_off, seq_ins, const_ins, out_widths, extra_outs=(), scratch=()):
    in_specs = [pl.BlockSpec((n, a.shape[1]), lambda i: (i + blk_off, 0)) for a in seq_ins]
    in_specs += [pl.BlockSpec(bs, im) for _, bs, im in const_ins]
    args = list(seq_ins) + [a for a, _, _ in const_ins]
    out_specs = [pl.BlockSpec((n, w), lambda i: (i, 0)) for w, _ in out_widths]
    out_shape = [jax.ShapeDtypeStruct((nseq * n, w), dt) for w, dt in out_widths]
    out_specs += [pl.BlockSpec(bs, im) for _, _, bs, im in extra_outs]
    out_shape += [jax.ShapeDtypeStruct(s, dt) for s, dt, _, _ in extra_outs]
    return pl.pallas_call(
        kernel, grid=(nseq,), in_specs=in_specs, out_specs=out_specs, out_shape=out_shape,
        scratch_shapes=list(scratch), compiler_params=_params("parallel"), name=name,
    )(*args)


def _fourier_kernel(u_ref, r_ref, lc_ref, ls_ref, o_ref):
    y = _dot(u_ref[...], r_ref[...])
    o_ref[...] = (_dot(lc_ref[...], y[:, :FN_W].astype(BF)) + _dot(ls_ref[...], y[:, FN_W:].astype(BF))).astype(BF)


def _dft_tables(n):
    k = np.arange(FN_GW)
    ang = 2.0 * np.pi * ((k[:, None] * k[None, :]) % FN_GW) / FN_GW
    eye = np.eye(FN_G)
    right = np.concatenate([np.kron(eye, np.cos(ang)), np.kron(eye, np.sin(ang))], axis=1)
    p = np.arange(n)
    angn = 2.0 * np.pi * ((p[:, None] * p[None, :]) % n) / n
    scale = 1.0 / np.sqrt(float(n * FN_GW))
    return (jnp.asarray(right, F32).astype(BF), jnp.asarray(np.cos(angn) * scale, F32).astype(BF),
            jnp.asarray(-np.sin(angn) * scale, F32).astype(BF))


def _fourier(zf, n, nseq, blk_off):
    right, lc, ls = _dft_tables(n)
    consts = [(right, (FN_W, 2 * FN_W), lambda i: (0, 0)), (lc, (n, n), lambda i: (0, 0)), (ls, (n, n), lambda i: (0, 0))]
    return _seq_call(_fourier_kernel, "fourier_mix", n, nseq, blk_off, [zf], consts, [(FN_W, BF)])


def _bf_parts(x, n):
    parts, rest = [], x
    for _ in range(n):
        p = rest.astype(BF)
        parts.append(p)
        rest = rest - p.astype(F32)
    return parts


def _dot_f32(a, b):
    a1, a2 = _bf_parts(a, 2)
    b1, b2 = _bf_parts(b, 2)
    return (_dot(a1, b2) + _dot(a2, b1)) + _dot(a1, b1)


def _cumulate(tri, g):
    g1, g2, g3 = _bf_parts(g, 3)
    return (_dot(tri, g3) + _dot(tri, g2)) + _dot(tri, g1)


def _log_gate(z, w_ref, b_ref):
    pre = _dot_f32(z, w_ref[...]) + b_ref[...]
    return (jnp.minimum(pre, 0.0) - jnp.log1p(jnp.exp(-jnp.abs(pre)))) * (1.0 / G_TAU)


def _gla_kernel(has_state, n, *refs):
    if has_state:
        (zqk, zvr, zsm, wgf, bgf, wgb, bgb, gout, s0f, s0b, o_ref, sf_ref, sb_ref,
         oacc_f, oacc_b, lg_f, lg_b, st_f, st_b) = refs
    else:
        (zqk, zvr, zsm, wgf, bgf, wgb, bgb, gout, o_ref, sf_ref, sb_ref,
         oacc_f, oacc_b, lg_f, lg_b, st_f, st_b) = refs
        s0f = s0b = None
    C = G_CHUNK
    nc = n // C
    ri, ci = _iota((C, C), 0), _iota((C, C), 1)
    t_idx, lane = _iota((C, 128), 0), _iota((C, 128), 1)
    s_idx = lane & (C - 1)
    left = lane < GDK
    vleft = _iota((C, 2 * GDV), 1) < GDV
    blockdiag = (_iota((2 * GDV, 2 * GDK), 0) >> 7) == (_iota((2 * GDV, 2 * GDK), 1) >> 6)
    row_w = _iota((C, GH * GDK), 0)

    def block_reference(cum, blk, off):
        if blk >= 8:
            return jnp.concatenate([jnp.broadcast_to(cum[j * blk + off:j * blk + off + 1], (blk, cum.shape[1]))
                                    for j in range(C // blk)], axis=0)
        out = cum
        for m in range(blk):
            if m != off:
                out = jnp.where((row_w & (blk - 1)) == m, pltpu.roll(cum, (m - off) % C, axis=0), out)
        return out

    def pair_blocks(x, ls):
        xp = x[:, ls]
        z = jnp.zeros_like(xp)
        return jnp.concatenate([jnp.where(left, xp, z), jnp.where(left, z, xp)], axis=0)

    def intra_scores(q, k, cum, reverse):
        levels = []
        blk = C
        while blk >= 2:
            half = blk // 2
            ref = block_reference(cum, blk, half if reverse else half - 1)
            qs = (q * jnp.exp(jnp.minimum(cum - ref, 0.0))).astype(BF)
            ks = (k * jnp.exp(jnp.minimum(ref - cum, 0.0))).astype(BF)
            t_in, s_in = t_idx & (blk - 1), s_idx & (blk - 1)
            same = (t_idx & -blk) == (s_idx & -blk)
            split = ((s_in >= half) & (t_in < half)) if reverse else ((t_in >= half) & (s_in < half))
            levels.append((qs, ks, same & split))
            blk = half
        levels.append((q.astype(BF), k.astype(BF), t_idx == s_idx))
        out = []
        for p in range(2):
            ls = slice(128 * p, 128 * p + 128)
            sc = jnp.zeros((C, 128), F32)
            for qs, ks, m in levels:
                sc = jnp.where(m, _dot_nt(qs[:, ls], pair_blocks(ks, ls)), sc)
            out.append(sc.astype(BF))
        return out

    def load_state(st, s0_ref):
        for p in range(2):
            if s0_ref is None:
                st[p] = jnp.zeros((2 * GDV, 2 * GDK), F32)
            else:
                z = jnp.zeros((GDK, GDV), F32)
                blk = jnp.concatenate([jnp.concatenate([s0_ref[0, 2 * p], z], axis=1),
                                       jnp.concatenate([z, s0_ref[0, 2 * p + 1]], axis=1)], axis=0)
                st[p] = blk.T

    def store_state(st, out_ref):
        for p in range(2):
            blk = st[p].T
            out_ref[0, 2 * p] = blk[0:GDK, 0:GDV]
            out_ref[0, 2 * p + 1] = blk[GDK:2 * GDK, GDV:2 * GDV]

    def chunk(c, reverse, lg, st, oacc):
        tri = (ci >= ri).astype(BF) if reverse else (ci <= ri).astype(BF)
        rows = pl.ds(pl.multiple_of(c * C, C), C)
        cum = _cumulate(tri, lg[rows, :])
        tot = cum[0:1] if reverse else cum[C - 1:C]
        q = zqk[rows, 0:GH * GDK] * (GDK ** -0.5)
        k = zqk[rows, GH * GDK:2 * GH * GDK]
        qh = (q * jnp.exp(cum)).astype(BF)
        kb = (k * jnp.exp(tot - cum)).astype(BF)
        dec = jnp.exp(tot)
        scores = intra_scores(q, k, cum, reverse)
        for p in range(2):
            ls = slice(128 * p, 128 * p + 128)
            vs = slice(256 * p, 256 * p + 256)
            vp = zvr[rows, vs]
            zv = jnp.zeros_like(vp)
            vblk = jnp.concatenate([jnp.where(vleft, vp, zv), jnp.where(vleft, zv, vp)], axis=0)
            stp = st[p]
            oacc[rows, vs] = _dot(scores[p], vblk) + _dot_nt(qh[:, ls], stp.astype(BF))
            st[p] = dec[:, ls] * stp + jnp.where(blockdiag, _dot_tn(vp, kb[:, ls]), 0.0)

    def both_directions(step, carry):
        chunk(step, False, lg_f, st_f, oacc_f)
        chunk(nc - 1 - step, True, lg_b, st_b, oacc_b)
        return carry

    lg_f[...] = _log_gate(zsm[:, 0:G_RANK], wgf, bgf)
    lg_b[...] = _log_gate(zsm[:, G_RANK:2 * G_RANK], wgb, bgb)
    load_state(st_f, s0f)
    load_state(st_b, s0b)
    lax.fori_loop(0, nc, both_directions, 0, unroll=min(nc, 4))
    store_state(st_f, sf_ref)
    store_state(st_b, sb_ref)

    rb = 128
    for r0 in range(0, n, rb):
        for h in range(GH):
            hs = slice(GDV * h, GDV * h + GDV)
            oh = oacc_f[r0:r0 + rb, hs] + oacc_b[r0:r0 + rb, hs]
            oh = oh * lax.rsqrt(jnp.mean(oh * oh, axis=-1, keepdims=True) + EPS) * gout[:, hs]
            r = zvr[r0:r0 + rb, GH * GDV + hs.start:GH * GDV + hs.stop].astype(F32)
            o_ref[r0:r0 + rb, hs] = (oh * (r * _sigmoid(r))).astype(BF)


def _gla(zqk, zvr, zsm, gate_w, n, nseq, blk_off, states):
    wgf, bgf, wgb, bgb, gout = gate_w
    c2 = lambda i: (0, 0)
    consts = [(wgf, wgf.shape, c2), (bgf, bgf.shape, c2), (wgb, wgb.shape, c2), (bgb, bgb.shape, c2), (gout, gout.shape, c2)]
    st_blk = (1, GH, GDK, GDV)
    st_map = lambda i: (i, 0, 0, 0)
    if states is not None:
        consts += [(s, st_blk, st_map) for s in states]
    extra = [((nseq, GH, GDK, GDV), F32, st_blk, st_map)] * 2
    scratch = ([pltpu.VMEM((n, GH * GDV), F32)] * 2 + [pltpu.VMEM((n, GH * GDK), F32)] * 2
               + [pltpu.VMEM((2, 2 * GDV, 2 * GDK), F32)] * 2)
    return _seq_call(functools.partial(_gla_kernel, states is not None, n), "gla_mixer", n, nseq, blk_off,
                     [zqk, zvr, zsm], consts, [(GH * GDV, BF)], extra_outs=extra, scratch=scratch)


def _mla_kernel(latent, n, past, *refs):
    if latent:
        (zmla, zsm, gq, wq, gkv, wkv, cckv, ckr, cosq, sinq, cosk, sink, o_ref, qs, kns, vs, krs) = refs
    else:
        (zmla, zsm, gq, wq, gkv, wkv, o_ref, ckv_ref, qs, kns, vs, krs) = refs
    sk = past + n
    scale = (M_NOPE + M_ROPE) ** -0.5
    nw, rw = MH * M_NOPE, MH * M_ROPE
    qa = _dot(_rms(zmla[:, 0:M_QL], gq[...]).astype(BF), wq[...])
    qr = qa[:, nw:nw + rw]
    if latent:
        qr = qr * cosq[...] + qa[:, nw + rw:nw + 2 * rw] * sinq[...]
    qs[:, 0:nw] = qa[:, 0:nw] * scale
    qs[:, nw:nw + rw] = qr * scale
    ckv = _rms(zmla[:, M_QL:M_QL + M_KVL], gkv[...])
    kv = _dot(ckv.astype(BF), wkv[...])
    kr = zsm[:, 32:64]
    if latent:
        kr = kr * cosk[...] + zsm[:, 64:96] * sink[...]
        kvc = _dot(cckv[0].astype(BF), wkv[...])
        kns[0:past, :] = kvc[:, 0:nw].astype(BF)
        vs[0:past, :] = kvc[:, nw:].astype(BF)
        krs[0:past, :] = jnp.concatenate([ckr[0]] * 4, axis=1).astype(BF)
    else:
        ckv_ref[...] = ckv
    kns[past:sk, :] = kv[:, 0:nw].astype(BF)
    vs[past:sk, :] = kv[:, nw:].astype(BF)
    krs[past:sk, :] = jnp.concatenate([kr] * 4, axis=1).astype(BF)

    qb = min(Q_BLOCK, n)
    lane = _iota((qb, 128), 1)

    def block(step, carry):
        rows = pl.ds(pl.multiple_of(step * qb, qb), qb)
        for p in range(MH // 2):
            ls = slice(128 * p, 128 * p + 128)
            qn = qs[rows, ls]
            quad = (2 * p) // 4
            qrp = qs[rows, nw + 128 * quad:nw + 128 * quad + 128]
            rhs = jnp.concatenate([kns[:, ls], krs[...]], axis=1)
            vp = vs[:, ls]
            o_pair = None
            for hh in range(2):
                j = (2 * p + hh) % 4
                qn_m = jnp.where((lane >> 6) == hh, qn, 0.0).astype(BF)
                qr_m = jnp.where((lane >> 5) == j, qrp, 0.0).astype(BF)
                s = _dot_nt(jnp.concatenate([qn_m, qr_m], axis=1), rhs)
                e = jnp.exp(s - jnp.max(s, axis=-1, keepdims=True))
                pv = _dot(e.astype(BF), vp) / jnp.sum(e, axis=-1, keepdims=True)
                o_pair = pv if hh == 0 else jnp.where(lane < M_V, o_pair, pv)
            o_ref[rows, ls] = o_pair.astype(BF)
        return carry

    lax.fori_loop(0, n // qb, block, 0)


def _rope_tables(n):
    half = M_ROPE // 2
    pos = jnp.arange(n)
    row = (pos // GRID_W).astype(F32)
    col = (pos % GRID_W).astype(F32)
    inv = ROPE_BASE ** (-jnp.arange(0, half, 2, dtype=F32) / half)
    ang = jnp.concatenate([row[:, None] * inv, col[:, None] * inv], axis=-1)
    cos = jnp.repeat(jnp.cos(ang), 2, axis=-1)
    sin = jnp.repeat(jnp.sin(ang), 2, axis=-1) * jnp.tile(jnp.asarray([-1.0, 1.0], F32), half)
    return jnp.tile(cos, (1, MH)), jnp.tile(sin, (1, MH)), cos, sin


def _mla(zmla, zsm, w, n, nseq, blk_off, cache):
    gq, wq, gkv, wkv = w
    c2 = lambda i: (0, 0)
    consts = [(gq, gq.shape, c2), (wq, wq.shape, c2), (gkv, gkv.shape, c2), (wkv, wkv.shape, c2)]
    past = 0
    extra = []
    if cache is not None:
        cckv, ckr = cache
        past = cckv.shape[1]
        c3 = lambda i: (i, 0, 0)
        consts += [(cckv, (1, past, M_KVL), c3), (ckr, (1, past, M_ROPE), c3)]
        consts += [(t, t.shape, c2) for t in _rope_tables(n)]
    else:
        extra = [((nseq * n, M_KVL), F32, (n, M_KVL), lambda i: (i, 0))]
    sk = past + n
    scratch = [pltpu.VMEM((n, MH * (M_NOPE + M_ROPE)), F32), pltpu.VMEM((sk, MH * M_NOPE), BF),
               pltpu.VMEM((sk, MH * M_V), BF), pltpu.VMEM((sk, 128), BF)]
    return _seq_call(functools.partial(_mla_kernel, cache is not None, n, past), "mla_mixer", n, nseq, blk_off,
                     [zmla, zsm], consts, [(MH * M_V, BF)], extra_outs=extra, scratch=scratch)


def _route(logits_t, bias):
    nt = logits_t.shape[1]
    gsz = N_EXP // N_GRP
    scores = _sigmoid(logits_t)
    sel = scores + bias
    neg = -jnp.inf
    sub = _iota((gsz, nt), 0)
    tops = []
    for g in range(N_GRP):
        blk = sel[gsz * g:gsz * g + gsz]
        m1 = jnp.max(blk, axis=0, keepdims=True)
        first = jnp.min(jnp.where(blk == m1, sub, gsz), axis=0, keepdims=True)
        m2 = jnp.max(jnp.where(sub == first, neg, blk), axis=0, keepdims=True)
        tops.append(m1 + m2)
    gs = jnp.concatenate(tops, axis=0)
    gidx = _iota((N_GRP, nt), 0)
    grank = jnp.zeros((N_GRP, nt), jnp.int32)
    for j in range(N_GRP):
        rj = gs[j:j + 1]
        grank += ((rj > gs) | ((rj == gs) & (gidx > j))).astype(jnp.int32)
    keep = grank < TOPK_GRP
    masked = jnp.concatenate(
        [jnp.where(jnp.broadcast_to(keep[g:g + 1], (gsz, nt)), sel[gsz * g:gsz * g + gsz], neg) for g in range(N_GRP)], axis=0)
    eidx = _iota((N_EXP, nt), 0)
    chosen = eidx < 0
    work = masked
    for _ in range(TOP_K):
        top = jnp.max(work, axis=0, keepdims=True)
        first = jnp.min(jnp.where(work == top, eidx, N_EXP), axis=0, keepdims=True)
        hit = eidx == first
        chosen = chosen | hit
        work = jnp.where(hit, neg, work)
    w = jnp.where(chosen, scores, 0.0)
    return chosen, w / jnp.sum(w, axis=0, keepdims=True) * ROUTED_SCALE


def _dispatch_meta(chosen, gates_t):
    tb = chosen.shape[1]
    sel = chosen.astype(F32)
    selb = sel.astype(BF)
    earlier = (_iota((tb, tb), 0) < _iota((tb, tb), 1)).astype(BF)
    rank = _dot(selb, earlier)
    cnt = jnp.sum(sel, axis=1, keepdims=True)
    padded = jnp.floor((cnt + (ROW_GRANULE - 1)) * (1.0 / ROW_GRANULE)) * ROW_GRANULE
    below = (_iota((N_EXP, N_EXP), 1) < _iota((N_EXP, N_EXP), 0)).astype(BF)
    start = _dot(below, jnp.broadcast_to(padded, (N_EXP, 128)).astype(BF))[:, 0:1]
    pos = start + rank
    kidx = _dot(below, selb)
    pos8, gate8 = [], []
    for k in range(TOP_K):
        hit = chosen & (kidx == float(k))
        pos8.append(jnp.sum(jnp.where(hit, pos, 0.0), axis=0, keepdims=True))
        gate8.append(jnp.sum(jnp.where(hit, gates_t, 0.0), axis=0, keepdims=True))
    return (jnp.concatenate(pos8, axis=0).astype(jnp.int32), jnp.concatenate(gate8, axis=0), cnt)


def _merge_kernel(n_ctx_blocks, xc_ref, xl_ref, fc_ref, fl_ref, ogc_ref, ogl_ref, omc_ref, oml_ref, zg_ref, m_ref,
                  wbf, wbg, wbm, wout, gn2, wrt, brt, xm_ref, h2_ref, pos_ref, gate_ref, cnt_ref):
    is_ctx = pl.program_id(0) < n_ctx_blocks
    ya = _dot(jnp.where(is_ctx, fc_ref[...], fl_ref[...]), wbf[...])
    yb = _dot(jnp.where(is_ctx, ogc_ref[...], ogl_ref[...]), wbg[...])
    yc = _dot(jnp.where(is_ctx, omc_ref[...], oml_ref[...]), wbm[...])
    merged = (_sigmoid(zg_ref[:, 0:D]) * ya.astype(BF) + _sigmoid(zg_ref[:, D:2 * D]) * yb.astype(BF)
              + _sigmoid(zg_ref[:, 2 * D:3 * D]) * yc.astype(BF))
    xm = jnp.where(is_ctx, xc_ref[...], xl_ref[...]) + m_ref[0, :, 2 * D:3 * D] * _dot(merged, wout[...])
    xm_ref[...] = xm
    h2 = _rms(xm, gn2[...]) * (1.0 + m_ref[0, :, 4 * D:5 * D]) + m_ref[0, :, 3 * D:4 * D]
    h2_ref[...] = h2.astype(BF)
    chosen, gates_t = _route(_dot_nt(wrt[...], h2, precision=lax.Precision.HIGHEST), brt[...])
    for sb in range(gates_t.shape[1] // DISP_BLOCK):
        ls = slice(sb * DISP_BLOCK, (sb + 1) * DISP_BLOCK)
        pos8, gate8, cnt = _dispatch_meta(chosen[:, ls], gates_t[:, ls])
        pos_ref[:, ls] = pos8
        gate_ref[:, ls] = gate8
        cnt_ref[sb] = jnp.broadcast_to(cnt, (N_EXP, 128))


def _merge(x_pair, T, mix_ctx, mix_lat, zg, mods, w, n_ctx_blocks, blocks_per_lat):
    tb = TOKEN_BLOCK
    row = functools.partial(_mod_row, n_ctx_blocks, blocks_per_lat)
    rb = lambda wd: pl.BlockSpec((tb, wd), lambda i: (i, 0))
    cb = lambda a: pl.BlockSpec(a.shape, lambda i: (0, 0))
    ctx_b = lambda wd: pl.BlockSpec((tb, wd), lambda i: (jnp.minimum(i, n_ctx_blocks - 1), 0))
    lat_b = lambda wd: pl.BlockSpec((tb, wd), lambda i: (jnp.maximum(i - n_ctx_blocks, 0), 0))
    mix_specs, mix_args = [], []
    for a_c, a_l in zip(mix_ctx, mix_lat):
        mix_specs += [ctx_b(a_c.shape[1]), lat_b(a_l.shape[1])]
        mix_args += [a_c, a_l]
    return pl.pallas_call(
        functools.partial(_merge_kernel, n_ctx_blocks),
        grid=(T // tb,),
        in_specs=_x_pair_specs(x_pair, tb, n_ctx_blocks) + mix_specs + [rb(3 * D),
                  pl.BlockSpec((1, 1, 6 * D), lambda i: (row(i), 0, 0))] + [cb(a) for a in w],
        out_specs=[rb(D), rb(D), pl.BlockSpec((TOP_K, tb), lambda i: (0, i)), pl.BlockSpec((TOP_K, tb), lambda i: (0, i)),
                   pl.BlockSpec((tb // DISP_BLOCK, N_EXP, 128), lambda i: (i, 0, 0))],
        out_shape=[jax.ShapeDtypeStruct((T, D), F32), jax.ShapeDtypeStruct((T, D), BF),
                   jax.ShapeDtypeStruct((TOP_K, T), jnp.int32), jax.ShapeDtypeStruct((TOP_K, T), F32),
                   jax.ShapeDtypeStruct((T // DISP_BLOCK, N_EXP, 128), F32)],
        compiler_params=_params("parallel"),
        name="merge_route",
    )(x_pair[0], x_pair[1], *mix_args, zg, mods, *w)


def _silu_mul(a, b):
    return a * _sigmoid(a) * b


DISP_ROW_CHUNK = 256


def _placement(pos_ref, weight_ref, r0):
    rows = _iota((DISP_ROW_CHUNK, DISP_BLOCK), 0) + r0
    p = jnp.zeros((DISP_ROW_CHUNK, DISP_BLOCK), F32)
    for k in range(TOP_K):
        w = 1.0 if weight_ref is None else weight_ref[k:k + 1, :]
        p = jnp.where(rows == pos_ref[k:k + 1, :], w, p)
    return p.astype(BF)


def _dispatch_kernel(n_blocks, h_ref, pos_ref, xs_ref):
    h = jnp.where(pl.program_id(0) < n_blocks, h_ref[...], jnp.zeros_like(h_ref))
    for r0 in range(0, DISP_ROWS, DISP_ROW_CHUNK):
        xs_ref[r0:r0 + DISP_ROW_CHUNK, :] = _dot(_placement(pos_ref, None, r0), h).astype(BF)


def _dispatch(h2, pos8):
    T = h2.shape[0]
    nblk = T // DISP_BLOCK
    last = lambda b: jnp.minimum(b, nblk - 1)
    return pl.pallas_call(
        functools.partial(_dispatch_kernel, nblk),
        grid=(nblk + 1,),
        in_specs=[pl.BlockSpec((DISP_BLOCK, D), lambda b: (last(b), 0)),
                  pl.BlockSpec((TOP_K, DISP_BLOCK), lambda b: (0, last(b)))],
        out_specs=pl.BlockSpec((DISP_ROWS, D), lambda b: (b, 0)),
        out_shape=jax.ShapeDtypeStruct(((nblk + 1) * DISP_ROWS, D), BF),
        compiler_params=_params("parallel"),
        name="moe_dispatch",
    )(h2, pos8)


def _tile_tables(cnt, n_chunks_max, n_tiles_max):
    nblk = cnt.shape[0]
    nch = (cnt + (ROW_GRANULE - 1)) // ROW_GRANULE
    first = jnp.arange(nblk, dtype=jnp.int32)[:, None] * BLOCK_CHUNKS + jnp.cumsum(nch, axis=1) - nch
    tiles_e = (jnp.sum(nch, axis=0) + (CHUNKS_PER_TILE - 1)) // CHUNKS_PER_TILE
    span_e = tiles_e * CHUNKS_PER_TILE
    exp_start = jnp.cumsum(span_e) - span_e
    j = jnp.arange(n_chunks_max, dtype=jnp.int32)
    e_j = jnp.sum((exp_start[None, :] <= j[:, None]).astype(jnp.int32), axis=1) - 1
    onehot = (e_j[:, None] == jnp.arange(N_EXP, dtype=jnp.int32)[None, :]).astype(F32)
    rows_of = lambda tab: jnp.dot(onehot, tab.astype(F32), precision=lax.Precision.HIGHEST).astype(jnp.int32)
    local = j - rows_of(exp_start[:, None])[:, 0]
    blk_len = rows_of(nch.T)
    blk_start = rows_of(jnp.cumsum(nch.T, axis=1) - nch.T)
    inside = (blk_start <= local[:, None]) & (local[:, None] < blk_start + blk_len)
    real = jnp.any(inside, axis=1)
    chunk = jnp.sum(jnp.where(inside, rows_of(first.T) + (local[:, None] - blk_start), 0), axis=1)
    pad_rank = jnp.cumsum(jnp.where(real, 0, 1)) - 1
    src = jnp.where(real, chunk, BLOCK_CHUNKS - 1).astype(jnp.int32)
    dst = jnp.where(real, chunk, nblk * BLOCK_CHUNKS + pad_rank % BLOCK_CHUNKS).astype(jnp.int32)
    tile_end = jnp.cumsum(tiles_e)
    i = jnp.arange(n_tiles_max, dtype=jnp.int32)
    tile_expert = jnp.minimum(jnp.sum((tile_end[None, :] <= i[:, None]).astype(jnp.int32), axis=1), N_EXP - 1)
    return src, dst, tile_expert.astype(jnp.int32), tile_end[-1:].astype(jnp.int32)


def _expert_kernel(src_ref, dst_ref, texp_ref, nused_ref, xs_hbm, wg_ref, wu_ref, wd_ref, ys_hbm,
                   xbuf, ybuf, wgu_bf, wd_bf, gsem, ssem):
    i = pl.program_id(0)
    n_used = nused_ref[0]
    slot = lax.rem(i, 2)

    def chunk_copies(tile, slot_, to_buffer, do):
        for c in range(CHUNKS_PER_TILE):
            j = tile * CHUNKS_PER_TILE + c
            if to_buffer:
                cp = pltpu.make_async_copy(xs_hbm.at[src_ref[j]], xbuf.at[slot_, c], gsem.at[slot_])
            else:
                cp = pltpu.make_async_copy(ybuf.at[slot_, c], ys_hbm.at[dst_ref[j]], ssem.at[slot_])
            do(cp, c)

    start = lambda cp, c: cp.start(priority=c % 2)
    wait = lambda cp, c: cp.wait()

    @pl.when(i == 0)
    def _():
        chunk_copies(0, 0, True, start)

    @pl.when(i == n_used)
    def _():
        chunk_copies(i, slot, True, wait)

    @pl.when(i < n_used)
    def _():
        chunk_copies(i, slot, True, wait)
        chunk_copies(i + 1, 1 - slot, True, start)

        @pl.when((i == 0) | (texp_ref[i] != texp_ref[jnp.maximum(i - 1, 0)]))
        def _():
            wgu_bf[:, 0:E_DIM] = wg_ref[0, 0].astype(BF)
            wgu_bf[:, E_DIM:2 * E_DIM] = wu_ref[0, 0].astype(BF)
            wd_bf[...] = wd_ref[0, 0].astype(BF)

        gu = _dot(xbuf[slot].reshape(EXP_TILE, D), wgu_bf[...])
        hid = _silu_mul(gu[:, 0:E_DIM], gu[:, E_DIM:2 * E_DIM])
        ybuf[slot] = _dot(hid.astype(BF), wd_bf[...]).astype(BF).reshape(CHUNKS_PER_TILE, ROW_GRANULE, D)
        chunk_copies(i, slot, False, start)

        @pl.when(i >= 1)
        def _():
            chunk_copies(i - 1, 1 - slot, False, wait)

        @pl.when(i == n_used - 1)
        def _():
            chunk_copies(i, slot, False, wait)


def _experts(layer, xs, tables, w_eg, w_eu, w_ed, n_tiles_max):
    src, dst, tile_expert, n_used = tables
    chunks = xs.reshape(-1, ROW_GRANULE, D)
    wmap = lambda i, src_, dst_, texp, nu: (layer, texp[i], 0, 0)
    grid_spec = pltpu.PrefetchScalarGridSpec(
        num_scalar_prefetch=4,
        grid=(n_tiles_max + 1,),
        in_specs=[pl.BlockSpec(memory_space=pl.ANY),
                  pl.BlockSpec((1, 1, D, E_DIM), wmap),
                  pl.BlockSpec((1, 1, D, E_DIM), wmap),
                  pl.BlockSpec((1, 1, E_DIM, D), wmap)],
        out_specs=pl.BlockSpec(memory_space=pl.ANY),
        scratch_shapes=[pltpu.VMEM((2, CHUNKS_PER_TILE, ROW_GRANULE, D), BF),
                        pltpu.VMEM((2, CHUNKS_PER_TILE, ROW_GRANULE, D), BF),
                        pltpu.VMEM((D, 2 * E_DIM), BF), pltpu.VMEM((E_DIM, D), BF),
                        pltpu.SemaphoreType.DMA((2,)), pltpu.SemaphoreType.DMA((2,))],
    )
    return pl.pallas_call(
        _expert_kernel,
        grid_spec=grid_spec,
        out_shape=jax.ShapeDtypeStruct(chunks.shape, chunks.dtype),
        input_output_aliases={4: 0},
        compiler_params=_params("arbitrary"),
        name="moe_experts",
    )(src, dst, tile_expert, n_used, chunks, w_eg, w_eu, w_ed).reshape(xs.shape)


def _combine_kernel(final, ys_ref, pos_ref, gate_ref, h_ref, sg_ref, su_ref, sd_ref, x_ref, m_ref, gf_ref, o_ref):
    routed = jnp.zeros((DISP_BLOCK, D), F32)
    for r0 in range(0, DISP_ROWS, DISP_ROW_CHUNK):
        routed = routed + _dot_tn(_placement(pos_ref, gate_ref, r0), ys_ref[r0:r0 + DISP_ROW_CHUNK, :])
    h = h_ref[...]
    sh = _silu_mul(_dot(h, sg_ref[0].astype(BF)), _dot(h, su_ref[0].astype(BF)))
    out = x_ref[...] + m_ref[0] * (routed + _dot(sh.astype(BF), sd_ref[0].astype(BF)))
    if final:
        out = _rms(out, gf_ref[...])
    o_ref[...] = out


def _combine(layer, final, ys, pos8, gate8, h2, w_sg, w_su, w_sd, xm, mods, g_final, n_ctx_blocks, blocks_per_lat):
    T = h2.shape[0]
    tb = DISP_BLOCK
    row = functools.partial(_mod_row, n_ctx_blocks, blocks_per_lat)
    return pl.pallas_call(
        functools.partial(_combine_kernel, final),
        grid=(T // tb,),
        in_specs=[pl.BlockSpec((DISP_ROWS, D), lambda b: (b, 0)),
                  pl.BlockSpec((TOP_K, tb), lambda b: (0, b)),
                  pl.BlockSpec((TOP_K, tb), lambda b: (0, b)),
                  pl.BlockSpec((tb, D), lambda b: (b, 0)),
                  pl.BlockSpec((1, D, E_DIM), lambda b: (layer, 0, 0)),
                  pl.BlockSpec((1, D, E_DIM), lambda b: (layer, 0, 0)),
                  pl.BlockSpec((1, E_DIM, D), lambda b: (layer, 0, 0)),
                  pl.BlockSpec((tb, D), lambda b: (b, 0)),
                  pl.BlockSpec((1, 1, D), lambda b: (row(b), 0, 5)),
                  pl.BlockSpec((1, D), lambda b: (0, 0))],
        out_specs=pl.BlockSpec((tb, D), lambda b: (b, 0)),
        out_shape=jax.ShapeDtypeStruct((T, D), F32),
        compiler_params=_params("parallel"),
        name="moe_combine",
    )(ys, pos8, gate8, h2, w_sg, w_su, w_sd, xm, mods, g_final)


def _moe(layer, final, h2, pos8, gate8, cnt, w_eg, w_eu, w_ed, w_sg, w_su, w_sd, xm, mods, g_final, disp_blocks):
    T = h2.shape[0]
    nblk = T // DISP_BLOCK
    n_chunks_max = (TOP_K * T + N_EXP * nblk * (ROW_GRANULE - 1)) // ROW_GRANULE + N_EXP * (CHUNKS_PER_TILE - 1)
    n_tiles_max = -(-n_chunks_max // CHUNKS_PER_TILE)
    xs = _dispatch(h2, pos8)
    tables = _tile_tables(cnt[:, :, 0].astype(jnp.int32), (n_tiles_max + 1) * CHUNKS_PER_TILE, n_tiles_max + 1)
    ys = _experts(layer, xs, tables, w_eg, w_eu, w_ed, n_tiles_max)
    return _combine(layer, final, ys, pos8, gate8, h2, w_sg, w_su, w_sd, xm, mods, g_final, *disp_blocks)


def _reorder_w_in(w_in):
    kr = w_in[:, :, 2592:2624]
    kr_sw = kr.reshape(kr.shape[0], D, M_ROPE // 2, 2)[..., ::-1].reshape(kr.shape)
    pad = jnp.zeros(kr.shape, w_in.dtype)
    out = jnp.concatenate([w_in[:, :, :1920], w_in[:, :, 1952:2592], w_in[:, :, 1920:1952], kr, kr_sw, pad,
                           w_in[:, :, 2624:]], axis=2)
    return out.astype(BF)


def _reorder_w_q(w_q_up, with_swap):
    L = w_q_up.shape[0]
    w = w_q_up.reshape(L, M_QL, MH, M_NOPE + M_ROPE)
    nope = w[..., :M_NOPE].reshape(L, M_QL, MH * M_NOPE)
    rope = w[..., M_NOPE:]
    parts = [nope, rope.reshape(L, M_QL, MH * M_ROPE)]
    if with_swap:
        parts.append(rope.reshape(L, M_QL, MH, M_ROPE // 2, 2)[..., ::-1].reshape(L, M_QL, MH * M_ROPE))
    return jnp.concatenate(parts, axis=2).astype(BF)


def _reorder_w_kv(w_kv_up):
    L = w_kv_up.shape[0]
    w = w_kv_up.reshape(L, M_KVL, MH, M_NOPE + M_V)
    return jnp.concatenate([w[..., :M_NOPE].reshape(L, M_KVL, MH * M_NOPE),
                            w[..., M_NOPE:].reshape(L, M_KVL, MH * M_V)], axis=2).astype(BF)


def kernel(x_prompt, x_sample, state_gla_fwd, state_gla_bwd, cache_mla_ckv, cache_mla_krope, c, c_ctx, w_mod, b_mod, g_norm1, g_norm2, w_in, w_gla_gate_f, b_gla_gate_f, w_gla_gate_b, b_gla_gate_b, g_gla_out, g_q_a, w_q_up, g_kv_a, w_kv_up, w_br_fourier, w_br_gla, w_br_mla, w_out, w_router, b_router, w_exp_gate, w_exp_up, w_exp_down, w_sh_gate, w_sh_up, w_sh_down, g_final):
    nb, sl, _ = x_prompt.shape
    db, dl, _ = x_sample.shape
    L = w_mod.shape[0]
    t_ctx, t_lat = nb * sl, db * dl
    T = t_ctx + t_lat
    assert sl % G_CHUNK == 0 and dl % G_CHUNK == 0 and dl % GRID_W == 0
    assert t_ctx % dl == 0 and dl % TOKEN_BLOCK == 0 and TOKEN_BLOCK % DISP_BLOCK == 0
    assert t_ctx % TOKEN_BLOCK == 0 and dl % Q_BLOCK == 0 and 1 + db <= 8

    x_pair = (x_prompt.reshape(t_ctx, D), x_sample.reshape(t_lat, D), 0)
    cond = jnp.concatenate([c_ctx[None, :], c, jnp.zeros((7 - db, D), F32)], axis=0)
    mods_all = _modulation(cond, w_mod, b_mod)

    w_in_r = _reorder_w_in(w_in)
    wq_ctx = _reorder_w_q(w_q_up, False)
    wq_lat = _reorder_w_q(w_q_up, True)
    wkv_r = _reorder_w_kv(w_kv_up)
    lat_off = t_ctx // dl

    new_f, new_b, new_ckv, new_kr = [], [], [], []
    for l in range(L):
        mods = mods_all[l].reshape(8, 1, 6 * D)
        tok = (t_ctx // TOKEN_BLOCK, dl // TOKEN_BLOCK)
        zf, zqk, zvr, zmla, zsm, zg = _in_projection(x_pair, T, mods, g_norm1[l][None, :], w_in_r[l], *tok)

        (f_c,) = _fourier(zf, sl, nb, 0)
        (f_l,) = _fourier(zf, dl, db, lat_off)

        gate_w = (w_gla_gate_f[l], b_gla_gate_f[l][None, :], w_gla_gate_b[l], b_gla_gate_b[l][None, :],
                  g_gla_out[l].reshape(1, GH * GDV))
        og_c, s_f, s_b = _gla(zqk, zvr, zsm, gate_w, sl, nb, 0, None)
        og_l, _, _ = _gla(zqk, zvr, zsm, gate_w, dl, db, lat_off, (state_gla_fwd[:, l], state_gla_bwd[:, l]))

        gq, gkv = g_q_a[l][None, :], g_kv_a[l][None, :]
        om_c, ckv = _mla(zmla, zsm, (gq, wq_ctx[l], gkv, wkv_r[l]), sl, nb, 0, None)
        (om_l,) = _mla(zmla, zsm, (gq, wq_lat[l], gkv, wkv_r[l]), dl, db, lat_off,
                       (cache_mla_ckv[:, l], cache_mla_krope[:, l]))

        mw = (w_br_fourier[l].astype(BF), w_br_gla[l].astype(BF), w_br_mla[l].astype(BF), w_out[l].astype(BF),
              g_norm2[l][None, :], w_router[l].T, b_router[l][:, None])
        xm, h2, pos8, gate8, cnt = _merge(x_pair, T, (f_c, og_c, om_c), (f_l, og_l, om_l), zg, mods, mw, *tok)

        x = _moe(l, l == L - 1, h2, pos8, gate8, cnt, w_exp_gate, w_exp_up, w_exp_down, w_sh_gate, w_sh_up, w_sh_down,
                 xm, mods, g_final[None, :], (t_ctx // DISP_BLOCK, dl // DISP_BLOCK))
        x_pair = (x, x, t_ctx // TOKEN_BLOCK)

        new_f.append(s_f)
        new_b.append(s_b)
        new_ckv.append(ckv.reshape(nb, sl, M_KVL))
        new_kr.append(zsm[:t_ctx, 32:64].reshape(nb, sl, M_ROPE))

    y_prompt = x[:t_ctx].reshape(nb, sl, D)
    y_sample = x[t_ctx:].reshape(db, dl, D)
    return (y_prompt, y_sample, jnp.stack(new_f, axis=1), jnp.stack(new_b, axis=1),
            jnp.stack(new_ckv, axis=1), jnp.stack(new_kr, axis=1))
```

```python
import functools

import numpy as np
import jax
import jax.numpy as jnp
from jax import lax
from jax.experimental import pallas as pl
from jax.experimental.pallas import tpu as pltpu

F32 = jnp.float32
BF = jnp.bfloat16

D = 1024
GRID_W = 64
FN_G, FN_GW = 4, 96
FN_W = FN_G * FN_GW
GH, GDK, GDV = 4, 64, 128
G_RANK = 16
G_TAU = 16.0
G_CHUNK = 64
MH, M_NOPE, M_ROPE, M_V = 8, 64, 32, 64
M_QL, M_KVL = 384, 256
ROPE_BASE = 10000.0
N_EXP, TOP_K, N_GRP, TOPK_GRP = 64, 8, 8, 4
E_DIM = 256
ROUTED_SCALE = 2.5
EPS = 1e-6

C_F = (0, 384)
C_QK = (384, 896)
C_VR = (896, 1920)
C_MLA = (1920, 2560)
C_SM = (2560, 2688)
C_G = (2688, 5760)
IN_COLS_R = 5760

VMEM_LIMIT_V7X = 56 * 1024 * 1024
TOKEN_BLOCK = 512
Q_BLOCK = 256
DISP_BLOCK = 256
ROW_GRANULE = 16
DISP_ROWS = -(-(TOP_K * DISP_BLOCK + N_EXP * (ROW_GRANULE - 1)) // 256) * 256
EXP_TILE = 512
CHUNKS_PER_TILE = EXP_TILE // ROW_GRANULE
BLOCK_CHUNKS = DISP_ROWS // ROW_GRANULE


def _params(*sem):
    return pltpu.CompilerParams(dimension_semantics=sem, vmem_limit_bytes=VMEM_LIMIT_V7X)


def _dot(a, b):
    return jnp.dot(a, b, preferred_element_type=F32)


def _dot_hi(a, b):
    return jnp.dot(a, b, precision=lax.Precision.HIGHEST, preferred_element_type=F32)


def _dot_nt(a, b, precision=None):
    return lax.dot_general(a, b, (((1,), (1,)), ((), ())), precision=precision, preferred_element_type=F32)


def _dot_tn(a, b):
    return lax.dot_general(a, b, (((0,), (0,)), ((), ())), preferred_element_type=F32)


def _sigmoid(x):
    return 1.0 / (1.0 + jnp.exp(-x))


def _rms(x, g):
    return x * lax.rsqrt(jnp.mean(x * x, axis=-1, keepdims=True) + EPS) * g


def _iota(shape, dim):
    return lax.broadcasted_iota(jnp.int32, shape, dim)


def _mod_row(n_ctx_blocks, blocks_per_lat, i):
    return jnp.where(i < n_ctx_blocks, 0, 1 + (i - n_ctx_blocks) // blocks_per_lat)


def _mod_kernel(c_ref, w_ref, b_ref, o_ref):
    c = c_ref[...]
    o_ref[0] = _dot_hi(c * _sigmoid(c), w_ref[0]) + b_ref[0]


def _modulation(cond, w_mod, b_mod):
    L = w_mod.shape[0]
    rows = cond.shape[0]
    tn = 1536
    return pl.pallas_call(
        _mod_kernel,
        grid=(L, 6 * D // tn),
        in_specs=[pl.BlockSpec((rows, D), lambda l, j: (0, 0)),
                  pl.BlockSpec((1, D, tn), lambda l, j: (l, 0, j)),
                  pl.BlockSpec((1, 1, tn), lambda l, j: (l, 0, j))],
        out_specs=pl.BlockSpec((1, rows, tn), lambda l, j: (l, 0, j)),
        out_shape=jax.ShapeDtypeStruct((L, rows, 6 * D), F32),
        compiler_params=_params("parallel", "parallel"),
        name="modulation",
    )(cond, w_mod, b_mod.reshape(L, 1, 6 * D))


def _inproj_kernel(n_ctx_blocks, xc_ref, xl_ref, m_ref, g_ref, w_ref, of_ref, oqk_ref, ovr_ref, omla_ref, osm_ref, og_ref):
    y = _rms(jnp.where(pl.program_id(0) < n_ctx_blocks, xc_ref[...], xl_ref[...]), g_ref[...])
    h = (y * (1.0 + m_ref[0, :, D:2 * D]) + m_ref[0, :, 0:D]).astype(BF)
    of_ref[...] = _dot(h, w_ref[:, C_F[0]:C_F[1]]).astype(BF)
    oqk_ref[...] = _dot(h, w_ref[:, C_QK[0]:C_QK[1]])
    ovr_ref[...] = _dot(h, w_ref[:, C_VR[0]:C_VR[1]]).astype(BF)
    omla_ref[...] = _dot(h, w_ref[:, C_MLA[0]:C_MLA[1]])
    osm_ref[...] = _dot(h, w_ref[:, C_SM[0]:C_SM[1]])
    og_ref[...] = _dot(h, w_ref[:, C_G[0]:C_G[1]]).astype(BF)


def _x_pair_specs(x_pair, tb, n_ctx_blocks):
    _, _, lat_off = x_pair
    return [pl.BlockSpec((tb, D), lambda i: (jnp.minimum(i, n_ctx_blocks - 1), 0)),
            pl.BlockSpec((tb, D), lambda i: (jnp.maximum(i - n_ctx_blocks, 0) + lat_off, 0))]


def _in_projection(x_pair, T, mods, g1, w_in_r, n_ctx_blocks, blocks_per_lat):
    tb = TOKEN_BLOCK
    row = functools.partial(_mod_row, n_ctx_blocks, blocks_per_lat)
    widths = [(C_F, BF), (C_QK, F32), (C_VR, BF), (C_MLA, F32), (C_SM, F32), (C_G, BF)]
    return pl.pallas_call(
        functools.partial(_inproj_kernel, n_ctx_blocks),
        grid=(T // tb,),
        in_specs=_x_pair_specs(x_pair, tb, n_ctx_blocks) + [
                  pl.BlockSpec((1, 1, 2 * D), lambda i: (row(i), 0, 0)),
                  pl.BlockSpec((1, D), lambda i: (0, 0)),
                  pl.BlockSpec((D, IN_COLS_R), lambda i: (0, 0))],
        out_specs=[pl.BlockSpec((tb, c[1] - c[0]), lambda i: (i, 0)) for c, _ in widths],
        out_shape=[jax.ShapeDtypeStruct((T, c[1] - c[0]), dt) for c, dt in widths],
        compiler_params=_params("parallel"),
        name="in_projection",
    )(x_pair[0], x_pair[1], mods, g1, w_in_r)


def _seq_call(kernel, name, n, nseq, blk_off, seq_ins, const_ins, out_widths, extra_outs=(), scratch=()):
    in_specs = [pl.BlockSpec((n, a.shape[1]), lambda i: (i + blk_off, 0)) for a in seq_ins]
    in_specs += [pl.BlockSpec(bs, im) for _, bs, im in const_ins]
    args = list(seq_ins) + [a for a, _, _ in const_ins]
    out_specs = [pl.BlockSpec((n, w), lambda i: (i, 0)) for w, _ in out_widths]
    out_shape = [jax.ShapeDtypeStruct((nseq * n, w), dt) for w, dt in out_widths]
    out_specs += [pl.BlockSpec(bs, im) for _, _, bs, im in extra_outs]
    out_shape += [jax.ShapeDtypeStruct(s, dt) for s, dt, _, _ in extra_outs]
    return pl.pallas_call(
        kernel, grid=(nseq,), in_specs=in_specs, out_specs=out_specs, out_shape=out_shape,
        scratch_shapes=list(scratch), compiler_params=_params("parallel"), name=name,
    )(*args)


def _fourier_kernel(u_ref, r_ref, lc_ref, ls_ref, o_ref):
    y = _dot(u_ref[...], r_ref[...])
    o_ref[...] = (_dot(lc_ref[...], y[:, :FN_W].astype(BF)) + _dot(ls_ref[...], y[:, FN_W:].astype(BF))).astype(BF)


def _dft_tables(n):
    k = np.arange(FN_GW)
    ang = 2.0 * np.pi * ((k[:, None] * k[None, :]) % FN_GW) / FN_GW
    eye = np.eye(FN_G)
    right = np.concatenate([np.kron(eye, np.cos(ang)), np.kron(eye, np.sin(ang))], axis=1)
    p = np.arange(n)
    angn = 2.0 * np.pi * ((p[:, None] * p[None, :]) % n) / n
    scale = 1.0 / np.sqrt(float(n * FN_GW))
    return (jnp.asarray(right, F32).astype(BF), jnp.asarray(np.cos(angn) * scale, F32).astype(BF),
            jnp.asarray(-np.sin(angn) * scale, F32).astype(BF))


def _fourier(zf, n, nseq, blk_off):
    right, lc, ls = _dft_tables(n)
    consts = [(right, (FN_W, 2 * FN_W), lambda i: (0, 0)), (lc, (n, n), lambda i: (0, 0)), (ls, (n, n), lambda i: (0, 0))]
    return _seq_call(_fourier_kernel, "fourier_mix", n, nseq, blk_off, [zf], consts, [(FN_W, BF)])


def _bf_parts(x, n):
    parts, rest = [], x
    for _ in range(n):
        p = rest.astype(BF)
        parts.append(p)
        rest = rest - p.astype(F32)
    return parts


def _dot_f32(a, b):
    a1, a2 = _bf_parts(a, 2)
    b1, b2 = _bf_parts(b, 2)
    return (_dot(a1, b2) + _dot(a2, b1)) + _dot(a1, b1)


def _cumulate(tri, g):
    g1, g2, g3 = _bf_parts(g, 3)
    return (_dot(tri, g3) + _dot(tri, g2)) + _dot(tri, g1)


def _log_gate(z, w_ref, b_ref):
    pre = _dot_f32(z, w_ref[...]) + b_ref[...]
    return (jnp.minimum(pre, 0.0) - jnp.log1p(jnp.exp(-jnp.abs(pre)))) * (1.0 / G_TAU)


def _gla_kernel(has_state, n, *refs):
    if has_state:
        (zqk, zvr, zsm, wgf, bgf, wgb, bgb, gout, s0f, s0b, o_ref, sf_ref, sb_ref,
         oacc_f, oacc_b, lg_f, lg_b, st_f, st_b) = refs
    else:
        (zqk, zvr, zsm, wgf, bgf, wgb, bgb, gout, o_ref, sf_ref, sb_ref,
         oacc_f, oacc_b, lg_f, lg_b, st_f, st_b) = refs
        s0f = s0b = None
    C = G_CHUNK
    nc = n // C
    ri, ci = _iota((C, C), 0), _iota((C, C), 1)
    t_idx, lane = _iota((C, 128), 0), _iota((C, 128), 1)
    s_idx = lane & (C - 1)
    left = lane < GDK
    vleft = _iota((C, 2 * GDV), 1) < GDV
    blockdiag = (_iota((2 * GDV, 2 * GDK), 0) >> 7) == (_iota((2 * GDV, 2 * GDK), 1) >> 6)
    row_w = _iota((C, GH * GDK), 0)

    def block_reference(cum, blk, off):
        if blk >= 8:
            return jnp.concatenate([jnp.broadcast_to(cum[j * blk + off:j * blk + off + 1], (blk, cum.shape[1]))
                                    for j in range(C // blk)], axis=0)
        out = cum
        for m in range(blk):
            if m != off:
                out = jnp.where((row_w & (blk - 1)) == m, pltpu.roll(cum, (m - off) % C, axis=0), out)
        return out

    def pair_blocks(x, ls):
        xp = x[:, ls]
        z = jnp.zeros_like(xp)
        return jnp.concatenate([jnp.where(left, xp, z), jnp.where(left, z, xp)], axis=0)

    def intra_scores(q, k, cum, reverse):
        levels = []
        blk = C
        while blk >= 2:
            half = blk // 2
            ref = block_reference(cum, blk, half if reverse else half - 1)
            d = cum - ref
            w = jnp.exp(jnp.minimum(d, -d))
            qs = (q * w).astype(BF)
            ks = (k * w).astype(BF)
            t_in, s_in = t_idx & (blk - 1), s_idx & (blk - 1)
            same = (t_idx & -blk) == (s_idx & -blk)
            split = ((s_in >= half) & (t_in < half)) if reverse else ((t_in >= half) & (s_in < half))
            levels.append((qs, ks, same & split))
            blk = half
        levels.append((q.astype(BF), k.astype(BF), t_idx == s_idx))
        out = []
        for p in range(2):
            ls = slice(128 * p, 128 * p + 128)
            sc = jnp.zeros((C, 128), F32)
            for qs, ks, m in levels:
                sc = jnp.where(m, _dot_nt(qs[:, ls], pair_blocks(ks, ls)), sc)
            out.append(sc.astype(BF))
        return out

    def load_state(st, s0_ref):
        for p in range(2):
            if s0_ref is None:
                st[p] = jnp.zeros((2 * GDV, 2 * GDK), F32)
            else:
                z = jnp.zeros((GDK, GDV), F32)
                blk = jnp.concatenate([jnp.concatenate([s0_ref[0, 2 * p], z], axis=1),
                                       jnp.concatenate([z, s0_ref[0, 2 * p + 1]], axis=1)], axis=0)
                st[p] = blk.T

    def store_state(st, out_ref):
        for p in range(2):
            blk = st[p].T
            out_ref[0, 2 * p] = blk[0:GDK, 0:GDV]
            out_ref[0, 2 * p + 1] = blk[GDK:2 * GDK, GDV:2 * GDV]

    def chunk(c, reverse, lg, st, oacc):
        tri = (ci >= ri).astype(BF) if reverse else (ci <= ri).astype(BF)
        rows = pl.ds(pl.multiple_of(c * C, C), C)
        cum = _cumulate(tri, lg[rows, :])
        tot = cum[0:1] if reverse else cum[C - 1:C]
        q = zqk[rows, 0:GH * GDK] * (GDK ** -0.5)
        k = zqk[rows, GH * GDK:2 * GH * GDK]
        qh = (q * jnp.exp(cum)).astype(BF)
        kb = (k * jnp.exp(tot - cum)).astype(BF)
        dec = jnp.exp(tot)
        scores = intra_scores(q, k, cum, reverse)
        for p in range(2):
            ls = slice(128 * p, 128 * p + 128)
            vs = slice(256 * p, 256 * p + 256)
            vp = zvr[rows, vs]
            zv = jnp.zeros_like(vp)
            vblk = jnp.concatenate([jnp.where(vleft, vp, zv), jnp.where(vleft, zv, vp)], axis=0)
            stp = st[p]
            oacc[rows, vs] = _dot(scores[p], vblk) + _dot_nt(qh[:, ls], stp.astype(BF))
            st[p] = dec[:, ls] * stp + jnp.where(blockdiag, _dot_tn(vp, kb[:, ls]), 0.0)

    def both_directions(step, carry):
        chunk(step, False, lg_f, st_f, oacc_f)
        chunk(nc - 1 - step, True, lg_b, st_b, oacc_b)
        return carry

    lg_f[...] = _log_gate(zsm[:, 0:G_RANK], wgf, bgf)
    lg_b[...] = _log_gate(zsm[:, G_RANK:2 * G_RANK], wgb, bgb)
    load_state(st_f, s0f)
    load_state(st_b, s0b)
    lax.fori_loop(0, nc, both_directions, 0, unroll=min(nc, 4))
    store_state(st_f, sf_ref)
    store_state(st_b, sb_ref)

    rb = 128
    for r0 in range(0, n, rb):
        for h in range(GH):
            hs = slice(GDV * h, GDV * h + GDV)
            oh = oacc_f[r0:r0 + rb, hs] + oacc_b[r0:r0 + rb, hs]
            oh = oh * lax.rsqrt(jnp.mean(oh * oh, axis=-1, keepdims=True) + EPS) * gout[:, hs]
            r = zvr[r0:r0 + rb, GH * GDV + hs.start:GH * GDV + hs.stop].astype(F32)
            o_ref[r0:r0 + rb, hs] = (oh * (r * _sigmoid(r))).astype(BF)


def _gla(zqk, zvr, zsm, gate_w, n, nseq, blk_off, states):
    wgf, bgf, wgb, bgb, gout = gate_w
    c2 = lambda i: (0, 0)
    consts = [(wgf, wgf.shape, c2), (bgf, bgf.shape, c2), (wgb, wgb.shape, c2), (bgb, bgb.shape, c2), (gout, gout.shape, c2)]
    st_blk = (1, GH, GDK, GDV)
    st_map = lambda i: (i, 0, 0, 0)
    if states is not None:
        consts += [(s, st_blk, st_map) for s in states]
    extra = [((nseq, GH, GDK, GDV), F32, st_blk, st_map)] * 2
    scratch = ([pltpu.VMEM((n, GH * GDV), F32)] * 2 + [pltpu.VMEM((n, GH * GDK), F32)] * 2
               + [pltpu.VMEM((2, 2 * GDV, 2 * GDK), F32)] * 2)
    return _seq_call(functools.partial(_gla_kernel, states is not None, n), "gla_mixer", n, nseq, blk_off,
                     [zqk, zvr, zsm], consts, [(GH * GDV, BF)], extra_outs=extra, scratch=scratch)


def _mla_kernel(latent, n, past, *refs):
    if latent:
        (zmla, zsm, gq, wq, gkv, wkv, cckv, ckr, cosq, sinq, cosk, sink, o_ref, qs, kns, vs, krs) = refs
    else:
        (zmla, zsm, gq, wq, gkv, wkv, o_ref, ckv_ref, qs, kns, vs, krs) = refs
    sk = past + n
    scale = (M_NOPE + M_ROPE) ** -0.5
    nw, rw = MH * M_NOPE, MH * M_ROPE
    qa = _dot(_rms(zmla[:, 0:M_QL], gq[...]).astype(BF), wq[...])
    qr = qa[:, nw:nw + rw]
    if latent:
        qr = qr * cosq[...] + qa[:, nw + rw:nw + 2 * rw] * sinq[...]
    qs[:, 0:nw] = qa[:, 0:nw] * scale
    qs[:, nw:nw + rw] = qr * scale
    ckv = _rms(zmla[:, M_QL:M_QL + M_KVL], gkv[...])
    kv = _dot(ckv.astype(BF), wkv[...])
    kr = zsm[:, 32:64]
    if latent:
        kr = kr * cosk[...] + zsm[:, 64:96] * sink[...]
        kvc = _dot(cckv[0].astype(BF), wkv[...])
        kns[0:past, :] = kvc[:, 0:nw].astype(BF)
        vs[0:past, :] = kvc[:, nw:].astype(BF)
        krs[0:past, :] = jnp.concatenate([ckr[0]] * 4, axis=1).astype(BF)
    else:
        ckv_ref[...] = ckv
    kns[past:sk, :] = kv[:, 0:nw].astype(BF)
    vs[past:sk, :] = kv[:, nw:].astype(BF)
    krs[past:sk, :] = jnp.concatenate([kr] * 4, axis=1).astype(BF)

    qb = min(Q_BLOCK, n)
    lane = _iota((qb, 128), 1)

    def block(step, carry):
        rows = pl.ds(pl.multiple_of(step * qb, qb), qb)
        for p in range(MH // 2):
            ls = slice(128 * p, 128 * p + 128)
            qn = qs[rows, ls]
            quad = (2 * p) // 4
            qrp = qs[rows, nw + 128 * quad:nw + 128 * quad + 128]
            rhs = jnp.concatenate([kns[:, ls], krs[...]], axis=1)
            vp = vs[:, ls]
            o_pair = None
            for hh in range(2):
                j = (2 * p + hh) % 4
                qn_m = jnp.where((lane >> 6) == hh, qn, 0.0).astype(BF)
                qr_m = jnp.where((lane >> 5) == j, qrp, 0.0).astype(BF)
                s = _dot_nt(jnp.concatenate([qn_m, qr_m], axis=1), rhs)
                e = jnp.exp(s - jnp.max(s, axis=-1, keepdims=True))
                pv = _dot(e.astype(BF), vp) / jnp.sum(e, axis=-1, keepdims=True)
                o_pair = pv if hh == 0 else jnp.where(lane < M_V, o_pair, pv)
            o_ref[rows, ls] = o_pair.astype(BF)
        return carry

    lax.fori_loop(0, n // qb, block, 0)


def _rope_tables(n):
    half = M_ROPE // 2
    pos = jnp.arange(n)
    row = (pos // GRID_W).astype(F32)
    col = (pos % GRID_W).astype(F32)
    inv = ROPE_BASE ** (-jnp.arange(0, half, 2, dtype=F32) / half)
    ang = jnp.concatenate([row[:, None] * inv, col[:, None] * inv], axis=-1)
    cos = jnp.repeat(jnp.cos(ang), 2, axis=-1)
    sin = jnp.repeat(jnp.sin(ang), 2, axis=-1) * jnp.tile(jnp.asarray([-1.0, 1.0], F32), half)
    return jnp.tile(cos, (1, MH)), jnp.tile(sin, (1, MH)), cos, sin


def _mla(zmla, zsm, w, n, nseq, blk_off, cache):
    gq, wq, gkv, wkv = w
    c2 = lambda i: (0, 0)
    consts = [(gq, gq.shape, c2), (wq, wq.shape, c2), (gkv, gkv.shape, c2), (wkv, wkv.shape, c2)]
    past = 0
    extra = []
    if cache is not None:
        cckv, ckr = cache
        past = cckv.shape[1]
        c3 = lambda i: (i, 0, 0)
        consts += [(cckv, (1, past, M_KVL), c3), (ckr, (1, past, M_ROPE), c3)]
        consts += [(t, t.shape, c2) for t in _rope_tables(n)]
    else:
        extra = [((nseq * n, M_KVL), F32, (n, M_KVL), lambda i: (i, 0))]
    sk = past + n
    scratch = [pltpu.VMEM((n, MH * (M_NOPE + M_ROPE)), F32), pltpu.VMEM((sk, MH * M_NOPE), BF),
               pltpu.VMEM((sk, MH * M_V), BF), pltpu.VMEM((sk, 128), BF)]
    return _seq_call(functools.partial(_mla_kernel, cache is not None, n, past), "mla_mixer", n, nseq, blk_off,
                     [zmla, zsm], consts, [(MH * M_V, BF)], extra_outs=extra, scratch=scratch)


def _route(logits_t, bias):
    nt = logits_t.shape[1]
    gsz = N_EXP // N_GRP
    scores = _sigmoid(logits_t)
    sel = scores + bias
    neg = -jnp.inf
    sub = _iota((gsz, nt), 0)
    tops = []
    for g in range(N_GRP):
        blk = sel[gsz * g:gsz * g + gsz]
        m1 = jnp.max(blk, axis=0, keepdims=True)
        first = jnp.min(jnp.where(blk == m1, sub, gsz), axis=0, keepdims=True)
        m2 = jnp.max(jnp.where(sub == first, neg, blk), axis=0, keepdims=True)
        tops.append(m1 + m2)
    gs = jnp.concatenate(tops, axis=0)
    gidx = _iota((N_GRP, nt), 0)
    grank = jnp.zeros((N_GRP, nt), jnp.int32)
    for j in range(N_GRP):
        rj = gs[j:j + 1]
        grank += ((rj > gs) | ((rj == gs) & (gidx > j))).astype(jnp.int32)
    keep = grank < TOPK_GRP
    masked = jnp.concatenate(
        [jnp.where(jnp.broadcast_to(keep[g:g + 1], (gsz, nt)), sel[gsz * g:gsz * g + gsz], neg) for g in range(N_GRP)], axis=0)
    eidx = _iota((N_EXP, nt), 0)
    chosen = eidx < 0
    work = masked
    for _ in range(TOP_K):
        top = jnp.max(work, axis=0, keepdims=True)
        first = jnp.min(jnp.where(work == top, eidx, N_EXP), axis=0, keepdims=True)
        hit = eidx == first
        chosen = chosen | hit
        work = jnp.where(hit, neg, work)
    w = jnp.where(chosen, scores, 0.0)
    return chosen, w / jnp.sum(w, axis=0, keepdims=True) * ROUTED_SCALE


def _dispatch_meta(chosen, gates_t):
    tb = chosen.shape[1]
    sel = chosen.astype(F32)
    selb = sel.astype(BF)
    earlier = (_iota((tb, tb), 0) < _iota((tb, tb), 1)).astype(BF)
    rank = _dot(selb, earlier)
    cnt = jnp.sum(sel, axis=1, keepdims=True)
    padded = jnp.floor((cnt + (ROW_GRANULE - 1)) * (1.0 / ROW_GRANULE)) * ROW_GRANULE
    below = (_iota((N_EXP, N_EXP), 1) < _iota((N_EXP, N_EXP), 0)).astype(BF)
    start = _dot(below, jnp.broadcast_to(padded, (N_EXP, 128)).astype(BF))[:, 0:1]
    pos = start + rank
    kidx = _dot(below, selb)
    pos8, gate8 = [], []
    for k in range(TOP_K):
        hit = chosen & (kidx == float(k))
        pos8.append(jnp.sum(jnp.where(hit, pos, 0.0), axis=0, keepdims=True))
        gate8.append(jnp.sum(jnp.where(hit, gates_t, 0.0), axis=0, keepdims=True))
    return (jnp.concatenate(pos8, axis=0).astype(jnp.int32), jnp.concatenate(gate8, axis=0), cnt)


def _merge_kernel(n_ctx_blocks, xc_ref, xl_ref, fc_ref, fl_ref, ogc_ref, ogl_ref, omc_ref, oml_ref, zg_ref, m_ref,
                  wbf, wbg, wbm, wout, gn2, wrt, brt, xm_ref, h2_ref, pos_ref, gate_ref, cnt_ref):
    is_ctx = pl.program_id(0) < n_ctx_blocks
    ya = _dot(jnp.where(is_ctx, fc_ref[...], fl_ref[...]), wbf[...])
    yb = _dot(jnp.where(is_ctx, ogc_ref[...], ogl_ref[...]), wbg[...])
    yc = _dot(jnp.where(is_ctx, omc_ref[...], oml_ref[...]), wbm[...])
    merged = (_sigmoid(zg_ref[:, 0:D]) * ya.astype(BF) + _sigmoid(zg_ref[:, D:2 * D]) * yb.astype(BF)
              + _sigmoid(zg_ref[:, 2 * D:3 * D]) * yc.astype(BF))
    xm = jnp.where(is_ctx, xc_ref[...], xl_ref[...]) + m_ref[0, :, 2 * D:3 * D] * _dot(merged, wout[...])
    xm_ref[...] = xm
    h2 = _rms(xm, gn2[...]) * (1.0 + m_ref[0, :, 4 * D:5 * D]) + m_ref[0, :, 3 * D:4 * D]
    h2_ref[...] = h2.astype(BF)
    chosen, gates_t = _route(_dot_nt(wrt[...], h2, precision=lax.Precision.HIGHEST), brt[...])
    for sb in range(gates_t.shape[1] // DISP_BLOCK):
        ls = slice(sb * DISP_BLOCK, (sb + 1) * DISP_BLOCK)
        pos8, gate8, cnt = _dispatch_meta(chosen[:, ls], gates_t[:, ls])
        pos_ref[:, ls] = pos8
        gate_ref[:, ls] = gate8
        cnt_ref[sb] = jnp.broadcast_to(cnt, (N_EXP, 128))


def _merge(x_pair, T, mix_ctx, mix_lat, zg, mods, w, n_ctx_blocks, blocks_per_lat):
    tb = TOKEN_BLOCK
    row = functools.partial(_mod_row, n_ctx_blocks, blocks_per_lat)
    rb = lambda wd: pl.BlockSpec((tb, wd), lambda i: (i, 0))
    cb = lambda a: pl.BlockSpec(a.shape, lambda i: (0, 0))
    ctx_b = lambda wd: pl.BlockSpec((tb, wd), lambda i: (jnp.minimum(i, n_ctx_blocks - 1), 0))
    lat_b = lambda wd: pl.BlockSpec((tb, wd), lambda i: (jnp.maximum(i - n_ctx_blocks, 0), 0))
    mix_specs, mix_args = [], []
    for a_c, a_l in zip(mix_ctx, mix_lat):
        mix_specs += [ctx_b(a_c.shape[1]), lat_b(a_l.shape[1])]
        mix_args += [a_c, a_l]
    return pl.pallas_call(
        functools.partial(_merge_kernel, n_ctx_blocks),
        grid=(T // tb,),
        in_specs=_x_pair_specs(x_pair, tb, n_ctx_blocks) + mix_specs + [rb(3 * D),
                  pl.BlockSpec((1, 1, 6 * D), lambda i: (row(i), 0, 0))] + [cb(a) for a in w],
        out_specs=[rb(D), rb(D), pl.BlockSpec((TOP_K, tb), lambda i: (0, i)), pl.BlockSpec((TOP_K, tb), lambda i: (0, i)),
                   pl.BlockSpec((tb // DISP_BLOCK, N_EXP, 128), lambda i: (i, 0, 0))],
        out_shape=[jax.ShapeDtypeStruct((T, D), F32), jax.ShapeDtypeStruct((T, D), BF),
                   jax.ShapeDtypeStruct((TOP_K, T), jnp.int32), jax.ShapeDtypeStruct((TOP_K, T), F32),
                   jax.ShapeDtypeStruct((T // DISP_BLOCK, N_EXP, 128), F32)],
        compiler_params=_params("parallel"),
        name="merge_route",
    )(x_pair[0], x_pair[1], *mix_args, zg, mods, *w)


def _silu_mul(a, b):
    return a * _sigmoid(a) * b


DISP_ROW_CHUNK = 256


def _placement(pos_ref, weight_ref, r0):
    rows = _iota((DISP_ROW_CHUNK, DISP_BLOCK), 0) + r0
    p = jnp.zeros((DISP_ROW_CHUNK, DISP_BLOCK), F32)
    for k in range(TOP_K):
        w = 1.0 if weight_ref is None else weight_ref[k:k + 1, :]
        p = jnp.where(rows == pos_ref[k:k + 1, :], w, p)
    return p.astype(BF)


def _dispatch_kernel(n_blocks, h_ref, pos_ref, xs_ref):
    h = jnp.where(pl.program_id(0) < n_blocks, h_ref[...], jnp.zeros_like(h_ref))
    for r0 in range(0, DISP_ROWS, DISP_ROW_CHUNK):
        xs_ref[r0:r0 + DISP_ROW_CHUNK, :] = _dot(_placement(pos_ref, None, r0), h).astype(BF)


def _dispatch(h2, pos8):
    T = h2.shape[0]
    nblk = T // DISP_BLOCK
    last = lambda b: jnp.minimum(b, nblk - 1)
    return pl.pallas_call(
        functools.partial(_dispatch_kernel, nblk),
        grid=(nblk + 1,),
        in_specs=[pl.BlockSpec((DISP_BLOCK, D), lambda b: (last(b), 0)),
                  pl.BlockSpec((TOP_K, DISP_BLOCK), lambda b: (0, last(b)))],
        out_specs=pl.BlockSpec((DISP_ROWS, D), lambda b: (b, 0)),
        out_shape=jax.ShapeDtypeStruct(((nblk + 1) * DISP_ROWS, D), BF),
        compiler_params=_params("parallel"),
        name="moe_dispatch",
    )(h2, pos8)


def _tile_tables(cnt, n_chunks_max, n_tiles_max):
    nblk = cnt.shape[0]
    nch = (cnt + (ROW_GRANULE - 1)) // ROW_GRANULE
    first = jnp.arange(nblk, dtype=jnp.int32)[:, None] * BLOCK_CHUNKS + jnp.cumsum(nch, axis=1) - nch
    tiles_e = (jnp.sum(nch, axis=0) + (CHUNKS_PER_TILE - 1)) // CHUNKS_PER_TILE
    span_e = tiles_e * CHUNKS_PER_TILE
    exp_start = jnp.cumsum(span_e) - span_e
    j = jnp.arange(n_chunks_max, dtype=jnp.int32)
    e_j = jnp.sum((exp_start[None, :] <= j[:, None]).astype(jnp.int32), axis=1) - 1
    onehot = (e_j[:, None] == jnp.arange(N_EXP, dtype=jnp.int32)[None, :]).astype(F32)
    rows_of = lambda tab: jnp.dot(onehot, tab.astype(F32), precision=lax.Precision.HIGHEST).astype(jnp.int32)
    local = j - rows_of(exp_start[:, None])[:, 0]
    blk_len = rows_of(nch.T)
    blk_start = rows_of(jnp.cumsum(nch.T, axis=1) - nch.T)
    inside = (blk_start <= local[:, None]) & (local[:, None] < blk_start + blk_len)
    real = jnp.any(inside, axis=1)
    chunk = jnp.sum(jnp.where(inside, rows_of(first.T) + (local[:, None] - blk_start), 0), axis=1)
    pad_rank = jnp.cumsum(jnp.where(real, 0, 1)) - 1
    src = jnp.where(real, chunk, BLOCK_CHUNKS - 1).astype(jnp.int32)
    dst = jnp.where(real, chunk, nblk * BLOCK_CHUNKS + pad_rank % BLOCK_CHUNKS).astype(jnp.int32)
    tile_end = jnp.cumsum(tiles_e)
    i = jnp.arange(n_tiles_max, dtype=jnp.int32)
    tile_expert = jnp.minimum(jnp.sum((tile_end[None, :] <= i[:, None]).astype(jnp.int32), axis=1), N_EXP - 1)
    return src, dst, tile_expert.astype(jnp.int32), tile_end[-1:].astype(jnp.int32)


def _expert_kernel(src_ref, dst_ref, texp_ref, nused_ref, xs_hbm, wg_ref, wu_ref, wd_ref, ys_hbm,
                   xbuf, ybuf, wgu_bf, wd_bf, gsem, ssem):
    i = pl.program_id(0)
    n_used = nused_ref[0]
    slot = lax.rem(i, 2)

    def chunk_copies(tile, slot_, to_buffer, do):
        for c in range(CHUNKS_PER_TILE):
            j = tile * CHUNKS_PER_TILE + c
            if to_buffer:
                cp = pltpu.make_async_copy(xs_hbm.at[src_ref[j]], xbuf.at[slot_, c], gsem.at[slot_])
            else:
                cp = pltpu.make_async_copy(ybuf.at[slot_, c], ys_hbm.at[dst_ref[j]], ssem.at[slot_])
            do(cp, c)

    start = lambda cp, c: cp.start(priority=c % 2)
    wait = lambda cp, c: cp.wait()

    @pl.when(i == 0)
    def _():
        chunk_copies(0, 0, True, start)

    @pl.when(i == n_used)
    def _():
        chunk_copies(i, slot, True, wait)

    @pl.when(i < n_used)
    def _():
        chunk_copies(i, slot, True, wait)
        chunk_copies(i + 1, 1 - slot, True, start)

        @pl.when((i == 0) | (texp_ref[i] != texp_ref[jnp.maximum(i - 1, 0)]))
        def _():
            wgu_bf[:, 0:E_DIM] = wg_ref[0, 0].astype(BF)
            wgu_bf[:, E_DIM:2 * E_DIM] = wu_ref[0, 0].astype(BF)
            wd_bf[...] = wd_ref[0, 0].astype(BF)

        gu = _dot(xbuf[slot].reshape(EXP_TILE, D), wgu_bf[...])
        hid = _silu_mul(gu[:, 0:E_DIM], gu[:, E_DIM:2 * E_DIM])
        ybuf[slot] = _dot(hid.astype(BF), wd_bf[...]).astype(BF).reshape(CHUNKS_PER_TILE, ROW_GRANULE, D)
        chunk_copies(i, slot, False, start)

        @pl.when(i >= 1)
        def _():
            chunk_copies(i - 1, 1 - slot, False, wait)

        @pl.when(i == n_used - 1)
        def _():
            chunk_copies(i, slot, False, wait)


def _experts(layer, xs, tables, w_eg, w_eu, w_ed, n_tiles_max):
    src, dst, tile_expert, n_used = tables
    chunks = xs.reshape(-1, ROW_GRANULE, D)
    wmap = lambda i, src_, dst_, texp, nu: (layer, texp[i], 0, 0)
    grid_spec = pltpu.PrefetchScalarGridSpec(
        num_scalar_prefetch=4,
        grid=(n_tiles_max + 1,),
        in_specs=[pl.BlockSpec(memory_space=pl.ANY),
                  pl.BlockSpec((1, 1, D, E_DIM), wmap),
                  pl.BlockSpec((1, 1, D, E_DIM), wmap),
                  pl.BlockSpec((1, 1, E_DIM, D), wmap)],
        out_specs=pl.BlockSpec(memory_space=pl.ANY),
        scratch_shapes=[pltpu.VMEM((2, CHUNKS_PER_TILE, ROW_GRANULE, D), BF),
                        pltpu.VMEM((2, CHUNKS_PER_TILE, ROW_GRANULE, D), BF),
                        pltpu.VMEM((D, 2 * E_DIM), BF), pltpu.VMEM((E_DIM, D), BF),
                        pltpu.SemaphoreType.DMA((2,)), pltpu.SemaphoreType.DMA((2,))],
    )
    return pl.pallas_call(
        _expert_kernel,
        grid_spec=grid_spec,
        out_shape=jax.ShapeDtypeStruct(chunks.shape, chunks.dtype),
        input_output_aliases={4: 0},
        compiler_params=_params("arbitrary"),
        name="moe_experts",
    )(src, dst, tile_expert, n_used, chunks, w_eg, w_eu, w_ed).reshape(xs.shape)


def _combine_kernel(final, ys_ref, pos_ref, gate_ref, h_ref, sg_ref, su_ref, sd_ref, x_ref, m_ref, gf_ref, o_ref):
    routed = jnp.zeros((DISP_BLOCK, D), F32)
    for r0 in range(0, DISP_ROWS, DISP_ROW_CHUNK):
        routed = routed + _dot_tn(_placement(pos_ref, gate_ref, r0), ys_ref[r0:r0 + DISP_ROW_CHUNK, :])
    h = h_ref[...]
    sh = _silu_mul(_dot(h, sg_ref[0].astype(BF)), _dot(h, su_ref[0].astype(BF)))
    out = x_ref[...] + m_ref[0] * (routed + _dot(sh.astype(BF), sd_ref[0].astype(BF)))
    if final:
        out = _rms(out, gf_ref[...])
    o_ref[...] = out


def _combine(layer, final, ys, pos8, gate8, h2, w_sg, w_su, w_sd, xm, mods, g_final, n_ctx_blocks, blocks_per_lat,
             first_block, n_blocks):
    tb = DISP_BLOCK
    row = functools.partial(_mod_row, n_ctx_blocks, blocks_per_lat)
    at = lambda b: b + first_block
    return pl.pallas_call(
        functools.partial(_combine_kernel, final),
        grid=(n_blocks,),
        in_specs=[pl.BlockSpec((DISP_ROWS, D), lambda b: (at(b), 0)),
                  pl.BlockSpec((TOP_K, tb), lambda b: (0, at(b))),
                  pl.BlockSpec((TOP_K, tb), lambda b: (0, at(b))),
                  pl.BlockSpec((tb, D), lambda b: (at(b), 0)),
                  pl.BlockSpec((1, D, E_DIM), lambda b: (layer, 0, 0)),
                  pl.BlockSpec((1, D, E_DIM), lambda b: (layer, 0, 0)),
                  pl.BlockSpec((1, E_DIM, D), lambda b: (layer, 0, 0)),
                  pl.BlockSpec((tb, D), lambda b: (at(b), 0)),
                  pl.BlockSpec((1, 1, D), lambda b: (row(at(b)), 0, 5)),
                  pl.BlockSpec((1, D), lambda b: (0, 0))],
        out_specs=pl.BlockSpec((tb, D), lambda b: (b, 0)),
        out_shape=jax.ShapeDtypeStruct((n_blocks * tb, D), F32),
        compiler_params=_params("parallel"),
        name="moe_combine",
    )(ys, pos8, gate8, h2, w_sg, w_su, w_sd, xm, mods, g_final)


def _moe(layer, final, h2, pos8, gate8, cnt, w_eg, w_eu, w_ed, w_sg, w_su, w_sd, xm, mods, g_final, disp_blocks):
    T = h2.shape[0]
    nblk = T // DISP_BLOCK
    n_chunks_max = (TOP_K * T + N_EXP * nblk * (ROW_GRANULE - 1)) // ROW_GRANULE + N_EXP * (CHUNKS_PER_TILE - 1)
    n_tiles_max = -(-n_chunks_max // CHUNKS_PER_TILE)
    xs = _dispatch(h2, pos8)
    tables = _tile_tables(cnt[:, :, 0].astype(jnp.int32), (n_tiles_max + 1) * CHUNKS_PER_TILE, n_tiles_max + 1)
    ys = _experts(layer, xs, tables, w_eg, w_eu, w_ed, n_tiles_max)
    comb = functools.partial(_combine, layer, final, ys, pos8, gate8, h2, w_sg, w_su, w_sd, xm, mods, g_final,
                             *disp_blocks)
    if not final:
        return comb(0, nblk)
    n_ctx = disp_blocks[0]
    return comb(0, n_ctx), comb(n_ctx, nblk - n_ctx)


def _reorder_w_in(w_in):
    kr = w_in[:, :, 2592:2624]
    kr_sw = kr.reshape(kr.shape[0], D, M_ROPE // 2, 2)[..., ::-1].reshape(kr.shape)
    pad = jnp.zeros(kr.shape, w_in.dtype)
    out = jnp.concatenate([w_in[:, :, :1920], w_in[:, :, 1952:2592], w_in[:, :, 1920:1952], kr, kr_sw, pad,
                           w_in[:, :, 2624:]], axis=2)
    return out.astype(BF)


def _reorder_w_q(w_q_up, with_swap):
    L = w_q_up.shape[0]
    w = w_q_up.reshape(L, M_QL, MH, M_NOPE + M_ROPE)
    nope = w[..., :M_NOPE].reshape(L, M_QL, MH * M_NOPE)
    rope = w[..., M_NOPE:]
    parts = [nope, rope.reshape(L, M_QL, MH * M_ROPE)]
    if with_swap:
        parts.append(rope.reshape(L, M_QL, MH, M_ROPE // 2, 2)[..., ::-1].reshape(L, M_QL, MH * M_ROPE))
    return jnp.concatenate(parts, axis=2).astype(BF)


def _reorder_w_kv(w_kv_up):
    L = w_kv_up.shape[0]
    w = w_kv_up.reshape(L, M_KVL, MH, M_NOPE + M_V)
    return jnp.concatenate([w[..., :M_NOPE].reshape(L, M_KVL, MH * M_NOPE),
                            w[..., M_NOPE:].reshape(L, M_KVL, MH * M_V)], axis=2).astype(BF)


def kernel(x_prompt, x_sample, state_gla_fwd, state_gla_bwd, cache_mla_ckv, cache_mla_krope, c, c_ctx, w_mod, b_mod, g_norm1, g_norm2, w_in, w_gla_gate_f, b_gla_gate_f, w_gla_gate_b, b_gla_gate_b, g_gla_out, g_q_a, w_q_up, g_kv_a, w_kv_up, w_br_fourier, w_br_gla, w_br_mla, w_out, w_router, b_router, w_exp_gate, w_exp_up, w_exp_down, w_sh_gate, w_sh_up, w_sh_down, g_final):
    nb, sl, _ = x_prompt.shape
    db, dl, _ = x_sample.shape
    L = w_mod.shape[0]
    t_ctx, t_lat = nb * sl, db * dl
    T = t_ctx + t_lat
    assert sl % G_CHUNK == 0 and dl % G_CHUNK == 0 and dl % GRID_W == 0
    assert t_ctx % dl == 0 and dl % TOKEN_BLOCK == 0 and TOKEN_BLOCK % DISP_BLOCK == 0
    assert t_ctx % TOKEN_BLOCK == 0 and dl % Q_BLOCK == 0 and 1 + db <= 8

    x_pair = (x_prompt.reshape(t_ctx, D), x_sample.reshape(t_lat, D), 0)
    cond = jnp.concatenate([c_ctx[None, :], c, jnp.zeros((7 - db, D), F32)], axis=0)
    mods_all = _modulation(cond, w_mod, b_mod)

    w_in_r = _reorder_w_in(w_in)
    wq_ctx = _reorder_w_q(w_q_up, False)
    wq_lat = _reorder_w_q(w_q_up, True)
    wkv_r = _reorder_w_kv(w_kv_up)
    lat_off = t_ctx // dl

    new_f, new_b, new_ckv, new_kr = [], [], [], []
    for l in range(L):
        mods = mods_all[l].reshape(8, 1, 6 * D)
        tok = (t_ctx // TOKEN_BLOCK, dl // TOKEN_BLOCK)
        zf, zqk, zvr, zmla, zsm, zg = _in_projection(x_pair, T, mods, g_norm1[l][None, :], w_in_r[l], *tok)

        (f_c,) = _fourier(zf, sl, nb, 0)
        (f_l,) = _fourier(zf, dl, db, lat_off)

        gate_w = (w_gla_gate_f[l], b_gla_gate_f[l][None, :], w_gla_gate_b[l], b_gla_gate_b[l][None, :],
                  g_gla_out[l].reshape(1, GH * GDV))
        og_c, s_f, s_b = _gla(zqk, zvr, zsm, gate_w, sl, nb, 0, None)
        og_l, _, _ = _gla(zqk, zvr, zsm, gate_w, dl, db, lat_off, (state_gla_fwd[:, l], state_gla_bwd[:, l]))

        gq, gkv = g_q_a[l][None, :], g_kv_a[l][None, :]
        om_c, ckv = _mla(zmla, zsm, (gq, wq_ctx[l], gkv, wkv_r[l]), sl, nb, 0, None)
        (om_l,) = _mla(zmla, zsm, (gq, wq_lat[l], gkv, wkv_r[l]), dl, db, lat_off,
                       (cache_mla_ckv[:, l], cache_mla_krope[:, l]))

        mw = (w_br_fourier[l].astype(BF), w_br_gla[l].astype(BF), w_br_mla[l].astype(BF), w_out[l].astype(BF),
              g_norm2[l][None, :], w_router[l].T, b_router[l][:, None])
        xm, h2, pos8, gate8, cnt = _merge(x_pair, T, (f_c, og_c, om_c), (f_l, og_l, om_l), zg, mods, mw, *tok)

        x = _moe(l, l == L - 1, h2, pos8, gate8, cnt, w_exp_gate, w_exp_up, w_exp_down, w_sh_gate, w_sh_up, w_sh_down,
                 xm, mods, g_final[None, :], (t_ctx // DISP_BLOCK, dl // DISP_BLOCK))
        if l < L - 1:
            x_pair = (x, x, t_ctx // TOKEN_BLOCK)

        new_f.append(s_f)
        new_b.append(s_b)
        new_ckv.append(ckv.reshape(nb, sl, M_KVL))
        new_kr.append(zsm[:t_ctx, 32:64].reshape(nb, sl, M_ROPE))

    y_prompt = x[0].reshape(nb, sl, D)
    y_sample = x[1].reshape(db, dl, D)
    return (y_prompt, y_sample, jnp.stack(new_f, axis=1), jnp.stack(new_b, axis=1),
            jnp.stack(new_ckv, axis=1), jnp.stack(new_kr, axis=1))
```

```python
import functools

import numpy as np
import jax
import jax.numpy as jnp
from jax import lax
from jax.experimental import pallas as pl
from jax.experimental.pallas import tpu as pltpu

F32 = jnp.float32
BF = jnp.bfloat16

D = 1024
GRID_W = 64
FN_G, FN_GW = 4, 96
FN_W = FN_G * FN_GW
GH, GDK, GDV = 4, 64, 128
G_RANK = 16
G_TAU = 16.0
G_CHUNK = 64
MH, M_NOPE, M_ROPE, M_V = 8, 64, 32, 64
M_QL, M_KVL = 384, 256
ROPE_BASE = 10000.0
N_EXP, TOP_K, N_GRP, TOPK_GRP = 64, 8, 8, 4
E_DIM = 256
ROUTED_SCALE = 2.5
EPS = 1e-6

C_F = (0, 384)
C_QK = (384, 896)
C_VR = (896, 1920)
C_MLA = (1920, 2560)
C_SM = (2560, 2688)
C_G = (2688, 5760)
IN_COLS_R = 5760

VMEM_LIMIT_V7X = 56 * 1024 * 1024
TOKEN_BLOCK = 512
Q_BLOCK = 256
DISP_BLOCK = 256
ROW_GRANULE = 16
DISP_ROWS = -(-(TOP_K * DISP_BLOCK + N_EXP * (ROW_GRANULE - 1)) // 256) * 256
EXP_TILE = 512
CHUNKS_PER_TILE = EXP_TILE // ROW_GRANULE
BLOCK_CHUNKS = DISP_ROWS // ROW_GRANULE


def _params(*sem):
    return pltpu.CompilerParams(dimension_semantics=sem, vmem_limit_bytes=VMEM_LIMIT_V7X)


def _dot(a, b):
    return jnp.dot(a, b, preferred_element_type=F32)


def _dot_hi(a, b):
    return jnp.dot(a, b, precision=lax.Precision.HIGHEST, preferred_element_type=F32)


def _dot_nt(a, b, precision=None):
    return lax.dot_general(a, b, (((1,), (1,)), ((), ())), precision=precision, preferred_element_type=F32)


def _dot_tn(a, b):
    return lax.dot_general(a, b, (((0,), (0,)), ((), ())), preferred_element_type=F32)


def _sigmoid(x):
    return 1.0 / (1.0 + jnp.exp(-x))


def _rms(x, g):
    return x * lax.rsqrt(jnp.mean(x * x, axis=-1, keepdims=True) + EPS) * g


def _iota(shape, dim):
    return lax.broadcasted_iota(jnp.int32, shape, dim)


def _mod_row(n_ctx_blocks, blocks_per_lat, i):
    return jnp.where(i < n_ctx_blocks, 0, 1 + (i - n_ctx_blocks) // blocks_per_lat)


def _mod_kernel(c_ref, w_ref, b_ref, o_ref):
    c = c_ref[...]
    o_ref[0] = _dot_hi(c * _sigmoid(c), w_ref[0]) + b_ref[0]


def _modulation(cond, w_mod, b_mod):
    L = w_mod.shape[0]
    rows = cond.shape[0]
    tn = 1536
    return pl.pallas_call(
        _mod_kernel,
        grid=(L, 6 * D // tn),
        in_specs=[pl.BlockSpec((rows, D), lambda l, j: (0, 0)),
                  pl.BlockSpec((1, D, tn), lambda l, j: (l, 0, j)),
                  pl.BlockSpec((1, 1, tn), lambda l, j: (l, 0, j))],
        out_specs=pl.BlockSpec((1, rows, tn), lambda l, j: (l, 0, j)),
        out_shape=jax.ShapeDtypeStruct((L, rows, 6 * D), F32),
        compiler_params=_params("parallel", "parallel"),
        name="modulation",
    )(cond, w_mod, b_mod.reshape(L, 1, 6 * D))


def _inproj_kernel(n_ctx_blocks, xc_ref, xl_ref, m_ref, g_ref, w_ref, of_ref, oqk_ref, ovr_ref, omla_ref, osm_ref, og_ref):
    y = _rms(jnp.where(pl.program_id(0) < n_ctx_blocks, xc_ref[...], xl_ref[...]), g_ref[...])
    h = (y * (1.0 + m_ref[0, :, D:2 * D]) + m_ref[0, :, 0:D]).astype(BF)
    of_ref[...] = _dot(h, w_ref[:, C_F[0]:C_F[1]]).astype(BF)
    oqk_ref[...] = _dot(h, w_ref[:, C_QK[0]:C_QK[1]])
    ovr_ref[...] = _dot(h, w_ref[:, C_VR[0]:C_VR[1]]).astype(BF)
    omla_ref[...] = _dot(h, w_ref[:, C_MLA[0]:C_MLA[1]])
    osm_ref[...] = _dot(h, w_ref[:, C_SM[0]:C_SM[1]])
    og_ref[...] = _dot(h, w_ref[:, C_G[0]:C_G[1]]).astype(BF)


def _x_pair_specs(x_pair, tb, n_ctx_blocks):
    _, _, lat_off = x_pair
    return [pl.BlockSpec((tb, D), lambda i: (jnp.minimum(i, n_ctx_blocks - 1), 0)),
            pl.BlockSpec((tb, D), lambda i: (jnp.maximum(i - n_ctx_blocks, 0) + lat_off, 0))]


def _in_projection(x_pair, T, mods, g1, w_in_r, n_ctx_blocks, blocks_per_lat):
    tb = TOKEN_BLOCK
    row = functools.partial(_mod_row, n_ctx_blocks, blocks_per_lat)
    widths = [(C_F, BF), (C_QK, F32), (C_VR, BF), (C_MLA, F32), (C_SM, F32), (C_G, BF)]
    return pl.pallas_call(
        functools.partial(_inproj_kernel, n_ctx_blocks),
        grid=(T // tb,),
        in_specs=_x_pair_specs(x_pair, tb, n_ctx_blocks) + [
                  pl.BlockSpec((1, 1, 2 * D), lambda i: (row(i), 0, 0)),
                  pl.BlockSpec((1, D), lambda i: (0, 0)),
                  pl.BlockSpec((D, IN_COLS_R), lambda i: (0, 0))],
        out_specs=[pl.BlockSpec((tb, c[1] - c[0]), lambda i: (i, 0)) for c, _ in widths],
        out_shape=[jax.ShapeDtypeStruct((T, c[1] - c[0]), dt) for c, dt in widths],
        compiler_params=_params("parallel"),
        name="in_projection",
    )(x_pair[0], x_pair[1], mods, g1, w_in_r)


def _seq_call(kernel, name, n, nseq, blk_off, seq_ins, const_ins, out_widths, extra_outs=(), scratch=()):
    in_specs = [pl.BlockSpec((n, a.shape[1]), lambda i: (i + blk_off, 0)) for a in seq_ins]
    in_specs += [pl.BlockSpec(bs, im) for _, bs, im in const_ins]
    args = list(seq_ins) + [a for a, _, _ in const_ins]
    out_specs = [pl.BlockSpec((n, w), lambda i: (i, 0)) for w, _ in out_widths]
    out_shape = [jax.ShapeDtypeStruct((nseq * n, w), dt) for w, dt in out_widths]
    out_specs += [pl.BlockSpec(bs, im) for _, _, bs, im in extra_outs]
    out_shape += [jax.ShapeDtypeStruct(s, dt) for s, dt, _, _ in extra_outs]
    return pl.pallas_call(
        kernel, grid=(nseq,), in_specs=in_specs, out_specs=out_specs, out_shape=out_shape,
        scratch_shapes=list(scratch), compiler_params=_params("parallel"), name=name,
    )(*args)


def _fourier_kernel(u_ref, r_ref, lc_ref, ls_ref, o_ref):
    y = _dot(u_ref[...], r_ref[...])
    o_ref[...] = (_dot(lc_ref[...], y[:, :FN_W].astype(BF)) + _dot(ls_ref[...], y[:, FN_W:].astype(BF))).astype(BF)


def _dft_tables(n):
    k = np.arange(FN_GW)
    ang = 2.0 * np.pi * ((k[:, None] * k[None, :]) % FN_GW) / FN_GW
    eye = np.eye(FN_G)
    right = np.concatenate([np.kron(eye, np.cos(ang)), np.kron(eye, np.sin(ang))], axis=1)
    p = np.arange(n)
    angn = 2.0 * np.pi * ((p[:, None] * p[None, :]) % n) / n
    scale = 1.0 / np.sqrt(float(n * FN_GW))
    return (jnp.asarray(right, F32).astype(BF), jnp.asarray(np.cos(angn) * scale, F32).astype(BF),
            jnp.asarray(-np.sin(angn) * scale, F32).astype(BF))


def _fourier(zf, n, nseq, blk_off):
    right, lc, ls = _dft_tables(n)
    consts = [(right, (FN_W, 2 * FN_W), lambda i: (0, 0)), (lc, (n, n), lambda i: (0, 0)), (ls, (n, n), lambda i: (0, 0))]
    return _seq_call(_fourier_kernel, "fourier_mix", n, nseq, blk_off, [zf], consts, [(FN_W, BF)])


def _bf_parts(x, n):
    parts, rest = [], x
    for _ in range(n):
        p = rest.astype(BF)
        parts.append(p)
        rest = rest - p.astype(F32)
    return parts


def _dot_f32(a, b):
    a1, a2 = _bf_parts(a, 2)
    b1, b2 = _bf_parts(b, 2)
    return (_dot(a1, b2) + _dot(a2, b1)) + _dot(a1, b1)


def _cumulate(tri, g):
    g1, g2, g3 = _bf_parts(g, 3)
    return (_dot(tri, g3) + _dot(tri, g2)) + _dot(tri, g1)


def _log_gate(z, w_ref, b_ref):
    pre = _dot_f32(z, w_ref[...]) + b_ref[...]
    return (jnp.minimum(pre, 0.0) - jnp.log1p(jnp.exp(-jnp.abs(pre)))) * (1.0 / G_TAU)


def _gla_kernel(has_state, n, *refs):
    if has_state:
        (zqk, zvr, zsm, wgf, bgf, wgb, bgb, gout, s0f, s0b, o_ref, sf_ref, sb_ref,
         oacc_f, oacc_b, lg_f, lg_b, st_f, st_b) = refs
    else:
        (zqk, zvr, zsm, wgf, bgf, wgb, bgb, gout, o_ref, sf_ref, sb_ref,
         oacc_f, oacc_b, lg_f, lg_b, st_f, st_b) = refs
        s0f = s0b = None
    C = G_CHUNK
    nc = n // C
    ri, ci = _iota((C, C), 0), _iota((C, C), 1)
    t_idx, lane = _iota((C, 128), 0), _iota((C, 128), 1)
    s_idx = lane & (C - 1)
    left = lane < GDK
    vleft = _iota((C, 2 * GDV), 1) < GDV
    blockdiag = (_iota((2 * GDV, 2 * GDK), 0) >> 7) == (_iota((2 * GDV, 2 * GDK), 1) >> 6)
    row_w = _iota((C, GH * GDK), 0)

    def block_reference(cum, blk, off):
        if blk >= 8:
            return jnp.concatenate([jnp.broadcast_to(cum[j * blk + off:j * blk + off + 1], (blk, cum.shape[1]))
                                    for j in range(C // blk)], axis=0)
        out = cum
        for m in range(blk):
            if m != off:
                out = jnp.where((row_w & (blk - 1)) == m, pltpu.roll(cum, (m - off) % C, axis=0), out)
        return out

    def pair_blocks(x, ls):
        xp = x[:, ls]
        z = jnp.zeros_like(xp)
        return jnp.concatenate([jnp.where(left, xp, z), jnp.where(left, z, xp)], axis=0)

    def intra_scores(q, k, cum, reverse):
        levels = []
        blk = C
        while blk >= 2:
            half = blk // 2
            ref = block_reference(cum, blk, half if reverse else half - 1)
            d = cum - ref
            w = jnp.exp(jnp.minimum(d, -d))
            qs = (q * w).astype(BF)
            ks = (k * w).astype(BF)
            t_in, s_in = t_idx & (blk - 1), s_idx & (blk - 1)
            same = (t_idx & -blk) == (s_idx & -blk)
            split = ((s_in >= half) & (t_in < half)) if reverse else ((t_in >= half) & (s_in < half))
            levels.append((qs, ks, same & split))
            blk = half
        levels.append((q.astype(BF), k.astype(BF), t_idx == s_idx))
        out = []
        for p in range(2):
            ls = slice(128 * p, 128 * p + 128)
            sc = jnp.zeros((C, 128), F32)
            for qs, ks, m in levels:
                sc = jnp.where(m, _dot_nt(qs[:, ls], pair_blocks(ks, ls)), sc)
            out.append(sc.astype(BF))
        return out

    def load_state(st, s0_ref):
        for p in range(2):
            if s0_ref is None:
                st[p] = jnp.zeros((2 * GDV, 2 * GDK), F32)
            else:
                z = jnp.zeros((GDK, GDV), F32)
                blk = jnp.concatenate([jnp.concatenate([s0_ref[0, 2 * p], z], axis=1),
                                       jnp.concatenate([z, s0_ref[0, 2 * p + 1]], axis=1)], axis=0)
                st[p] = blk.T

    def store_state(st, out_ref):
        for p in range(2):
            blk = st[p].T
            out_ref[0, 2 * p] = blk[0:GDK, 0:GDV]
            out_ref[0, 2 * p + 1] = blk[GDK:2 * GDK, GDV:2 * GDV]

    def chunk(c, reverse, lg, st, oacc):
        tri = (ci >= ri).astype(BF) if reverse else (ci <= ri).astype(BF)
        rows = pl.ds(pl.multiple_of(c * C, C), C)
        cum = _cumulate(tri, lg[rows, :])
        tot = cum[0:1] if reverse else cum[C - 1:C]
        q = zqk[rows, 0:GH * GDK] * (GDK ** -0.5)
        k = zqk[rows, GH * GDK:2 * GH * GDK]
        qh = (q * jnp.exp(cum)).astype(BF)
        kb = (k * jnp.exp(tot - cum)).astype(BF)
        dec = jnp.exp(tot)
        scores = intra_scores(q, k, cum, reverse)
        for p in range(2):
            ls = slice(128 * p, 128 * p + 128)
            vs = slice(256 * p, 256 * p + 256)
            vp = zvr[rows, vs]
            zv = jnp.zeros_like(vp)
            vblk = jnp.concatenate([jnp.where(vleft, vp, zv), jnp.where(vleft, zv, vp)], axis=0)
            stp = st[p]
            oacc[rows, vs] = _dot(scores[p], vblk) + _dot_nt(qh[:, ls], stp.astype(BF))
            st[p] = dec[:, ls] * stp + jnp.where(blockdiag, _dot_tn(vp, kb[:, ls]), 0.0)

    def both_directions(step, carry):
        chunk(step, False, lg_f, st_f, oacc_f)
        chunk(nc - 1 - step, True, lg_b, st_b, oacc_b)
        return carry

    lg_f[...] = _log_gate(zsm[:, 0:G_RANK], wgf, bgf)
    lg_b[...] = _log_gate(zsm[:, G_RANK:2 * G_RANK], wgb, bgb)
    load_state(st_f, s0f)
    load_state(st_b, s0b)
    lax.fori_loop(0, nc, both_directions, 0, unroll=min(nc, 4))
    store_state(st_f, sf_ref)
    store_state(st_b, sb_ref)

    rb = 128
    for r0 in range(0, n, rb):
        for h in range(GH):
            hs = slice(GDV * h, GDV * h + GDV)
            oh = oacc_f[r0:r0 + rb, hs] + oacc_b[r0:r0 + rb, hs]
            oh = oh * lax.rsqrt(jnp.mean(oh * oh, axis=-1, keepdims=True) + EPS) * gout[:, hs]
            r = zvr[r0:r0 + rb, GH * GDV + hs.start:GH * GDV + hs.stop].astype(F32)
            o_ref[r0:r0 + rb, hs] = (oh * (r * _sigmoid(r))).astype(BF)


def _gla(zqk, zvr, zsm, gate_w, n, nseq, blk_off, states):
    wgf, bgf, wgb, bgb, gout = gate_w
    c2 = lambda i: (0, 0)
    consts = [(wgf, wgf.shape, c2), (bgf, bgf.shape, c2), (wgb, wgb.shape, c2), (bgb, bgb.shape, c2), (gout, gout.shape, c2)]
    st_blk = (1, GH, GDK, GDV)
    st_map = lambda i: (i, 0, 0, 0)
    if states is not None:
        consts += [(s, st_blk, st_map) for s in states]
    extra = [((nseq, GH, GDK, GDV), F32, st_blk, st_map)] * 2
    scratch = ([pltpu.VMEM((n, GH * GDV), F32)] * 2 + [pltpu.VMEM((n, GH * GDK), F32)] * 2
               + [pltpu.VMEM((2, 2 * GDV, 2 * GDK), F32)] * 2)
    return _seq_call(functools.partial(_gla_kernel, states is not None, n), "gla_mixer", n, nseq, blk_off,
                     [zqk, zvr, zsm], consts, [(GH * GDV, BF)], extra_outs=extra, scratch=scratch)


def _mla_kernel(latent, n, past, *refs):
    if latent:
        (zmla, zsm, gq, wq, gkv, wkv, cckv, ckr, cosq, sinq, cosk, sink, o_ref, qs, kns, vs, krs) = refs
    else:
        (zmla, zsm, gq, wq, gkv, wkv, o_ref, ckv_ref, qs, kns, vs, krs) = refs
    sk = past + n
    scale = (M_NOPE + M_ROPE) ** -0.5
    nw, rw = MH * M_NOPE, MH * M_ROPE
    qa = _dot(_rms(zmla[:, 0:M_QL], gq[...]).astype(BF), wq[...])
    qr = qa[:, nw:nw + rw]
    if latent:
        qr = qr * cosq[...] + qa[:, nw + rw:nw + 2 * rw] * sinq[...]
    qs[:, 0:nw] = qa[:, 0:nw] * scale
    qs[:, nw:nw + rw] = qr * scale
    ckv = _rms(zmla[:, M_QL:M_QL + M_KVL], gkv[...])
    kv = _dot(ckv.astype(BF), wkv[...])
    kr = zsm[:, 32:64]
    if latent:
        kr = kr * cosk[...] + zsm[:, 64:96] * sink[...]
        kvc = _dot(cckv[0].astype(BF), wkv[...])
        kns[0:past, :] = kvc[:, 0:nw].astype(BF)
        vs[0:past, :] = kvc[:, nw:].astype(BF)
        krs[0:past, :] = jnp.concatenate([ckr[0]] * 4, axis=1).astype(BF)
    else:
        ckv_ref[...] = ckv
    kns[past:sk, :] = kv[:, 0:nw].astype(BF)
    vs[past:sk, :] = kv[:, nw:].astype(BF)
    krs[past:sk, :] = jnp.concatenate([kr] * 4, axis=1).astype(BF)

    qb = min(Q_BLOCK, n)
    lane = _iota((qb, 128), 1)

    def block(step, carry):
        rows = pl.ds(pl.multiple_of(step * qb, qb), qb)
        for p in range(MH // 2):
            ls = slice(128 * p, 128 * p + 128)
            qn = qs[rows, ls]
            quad = (2 * p) // 4
            qrp = qs[rows, nw + 128 * quad:nw + 128 * quad + 128]
            rhs = jnp.concatenate([kns[:, ls], krs[...]], axis=1)
            vp = vs[:, ls]
            o_pair = None
            for hh in range(2):
                j = (2 * p + hh) % 4
                qn_m = jnp.where((lane >> 6) == hh, qn, 0.0).astype(BF)
                qr_m = jnp.where((lane >> 5) == j, qrp, 0.0).astype(BF)
                s = _dot_nt(jnp.concatenate([qn_m, qr_m], axis=1), rhs)
                e = jnp.exp(s - jnp.max(s, axis=-1, keepdims=True))
                pv = _dot(e.astype(BF), vp) / jnp.sum(e, axis=-1, keepdims=True)
                o_pair = pv if hh == 0 else jnp.where(lane < M_V, o_pair, pv)
            o_ref[rows, ls] = o_pair.astype(BF)
        return carry

    lax.fori_loop(0, n // qb, block, 0)


def _rope_tables(n):
    half = M_ROPE // 2
    pos = jnp.arange(n)
    row = (pos // GRID_W).astype(F32)
    col = (pos % GRID_W).astype(F32)
    inv = ROPE_BASE ** (-jnp.arange(0, half, 2, dtype=F32) / half)
    ang = jnp.concatenate([row[:, None] * inv, col[:, None] * inv], axis=-1)
    cos = jnp.repeat(jnp.cos(ang), 2, axis=-1)
    sin = jnp.repeat(jnp.sin(ang), 2, axis=-1) * jnp.tile(jnp.asarray([-1.0, 1.0], F32), half)
    return jnp.tile(cos, (1, MH)), jnp.tile(sin, (1, MH)), cos, sin


def _mla(zmla, zsm, w, n, nseq, blk_off, cache):
    gq, wq, gkv, wkv = w
    c2 = lambda i: (0, 0)
    consts = [(gq, gq.shape, c2), (wq, wq.shape, c2), (gkv, gkv.shape, c2), (wkv, wkv.shape, c2)]
    past = 0
    extra = []
    if cache is not None:
        cckv, ckr = cache
        past = cckv.shape[1]
        c3 = lambda i: (i, 0, 0)
        consts += [(cckv, (1, past, M_KVL), c3), (ckr, (1, past, M_ROPE), c3)]
        consts += [(t, t.shape, c2) for t in _rope_tables(n)]
    else:
        extra = [((nseq * n, M_KVL), F32, (n, M_KVL), lambda i: (i, 0))]
    sk = past + n
    scratch = [pltpu.VMEM((n, MH * (M_NOPE + M_ROPE)), F32), pltpu.VMEM((sk, MH * M_NOPE), BF),
               pltpu.VMEM((sk, MH * M_V), BF), pltpu.VMEM((sk, 128), BF)]
    return _seq_call(functools.partial(_mla_kernel, cache is not None, n, past), "mla_mixer", n, nseq, blk_off,
                     [zmla, zsm], consts, [(MH * M_V, BF)], extra_outs=extra, scratch=scratch)


def _route(logits_t, bias):
    nt = logits_t.shape[1]
    gsz = N_EXP // N_GRP
    scores = _sigmoid(logits_t)
    sel = scores + bias
    neg = -jnp.inf
    sub = _iota((gsz, nt), 0)
    tops = []
    for g in range(N_GRP):
        blk = sel[gsz * g:gsz * g + gsz]
        m1 = jnp.max(blk, axis=0, keepdims=True)
        first = jnp.min(jnp.where(blk == m1, sub, gsz), axis=0, keepdims=True)
        m2 = jnp.max(jnp.where(sub == first, neg, blk), axis=0, keepdims=True)
        tops.append(m1 + m2)
    gs = jnp.concatenate(tops, axis=0)
    gidx = _iota((N_GRP, nt), 0)
    grank = jnp.zeros((N_GRP, nt), jnp.int32)
    for j in range(N_GRP):
        rj = gs[j:j + 1]
        grank += ((rj > gs) | ((rj == gs) & (gidx > j))).astype(jnp.int32)
    keep = grank < TOPK_GRP
    masked = jnp.concatenate(
        [jnp.where(jnp.broadcast_to(keep[g:g + 1], (gsz, nt)), sel[gsz * g:gsz * g + gsz], neg) for g in range(N_GRP)], axis=0)
    eidx = _iota((N_EXP, nt), 0)
    chosen = eidx < 0
    work = masked
    for _ in range(TOP_K):
        top = jnp.max(work, axis=0, keepdims=True)
        first = jnp.min(jnp.where(work == top, eidx, N_EXP), axis=0, keepdims=True)
        hit = eidx == first
        chosen = chosen | hit
        work = jnp.where(hit, neg, work)
    w = jnp.where(chosen, scores, 0.0)
    return chosen, w / jnp.sum(w, axis=0, keepdims=True) * ROUTED_SCALE


def _dispatch_meta(chosen, gates_t):
    tb = chosen.shape[1]
    sel = chosen.astype(F32)
    selb = sel.astype(BF)
    earlier = (_iota((tb, tb), 0) < _iota((tb, tb), 1)).astype(BF)
    rank = _dot(selb, earlier)
    cnt = jnp.sum(sel, axis=1, keepdims=True)
    padded = jnp.floor((cnt + (ROW_GRANULE - 1)) * (1.0 / ROW_GRANULE)) * ROW_GRANULE
    below = (_iota((N_EXP, N_EXP), 1) < _iota((N_EXP, N_EXP), 0)).astype(BF)
    start = _dot(below, jnp.broadcast_to(padded, (N_EXP, 128)).astype(BF))[:, 0:1]
    pos = start + rank
    kidx = _dot(below, selb)
    pos8, gate8 = [], []
    for k in range(TOP_K):
        hit = chosen & (kidx == float(k))
        pos8.append(jnp.sum(jnp.where(hit, pos, 0.0), axis=0, keepdims=True))
        gate8.append(jnp.sum(jnp.where(hit, gates_t, 0.0), axis=0, keepdims=True))
    return (jnp.concatenate(pos8, axis=0).astype(jnp.int32), jnp.concatenate(gate8, axis=0), cnt)


def _merge_kernel(n_ctx_blocks, xc_ref, xl_ref, fc_ref, fl_ref, ogc_ref, ogl_ref, omc_ref, oml_ref, zg_ref, m_ref,
                  wbf, wbg, wbm, wout, gn2, wrt, brt, xm_ref, h2_ref, pos_ref, gate_ref, cnt_ref):
    is_ctx = pl.program_id(0) < n_ctx_blocks
    pick = lambda c_ref, l_ref, rs: jnp.where(is_ctx, c_ref[rs, :], l_ref[rs, :])
    for sb in range(xm_ref.shape[0] // DISP_BLOCK):
        rs = slice(sb * DISP_BLOCK, (sb + 1) * DISP_BLOCK)
        ya = _dot(pick(fc_ref, fl_ref, rs), wbf[...])
        yb = _dot(pick(ogc_ref, ogl_ref, rs), wbg[...])
        yc = _dot(pick(omc_ref, oml_ref, rs), wbm[...])
        merged = (_sigmoid(zg_ref[rs, 0:D]) * ya.astype(BF) + _sigmoid(zg_ref[rs, D:2 * D]) * yb.astype(BF)
                  + _sigmoid(zg_ref[rs, 2 * D:3 * D]) * yc.astype(BF))
        xm = pick(xc_ref, xl_ref, rs) + m_ref[0, :, 2 * D:3 * D] * _dot(merged, wout[...])
        xm_ref[rs, :] = xm
        h2 = _rms(xm, gn2[...]) * (1.0 + m_ref[0, :, 4 * D:5 * D]) + m_ref[0, :, 3 * D:4 * D]
        h2_ref[rs, :] = h2.astype(BF)
        chosen, gates_t = _route(_dot_nt(wrt[...], h2, precision=lax.Precision.HIGHEST), brt[...])
        pos8, gate8, cnt = _dispatch_meta(chosen, gates_t)
        pos_ref[:, rs] = pos8
        gate_ref[:, rs] = gate8
        cnt_ref[sb] = jnp.broadcast_to(cnt, (N_EXP, 128))


def _merge(x_pair, T, mix_ctx, mix_lat, zg, mods, w, n_ctx_blocks, blocks_per_lat):
    tb = TOKEN_BLOCK
    row = functools.partial(_mod_row, n_ctx_blocks, blocks_per_lat)
    rb = lambda wd: pl.BlockSpec((tb, wd), lambda i: (i, 0))
    cb = lambda a: pl.BlockSpec(a.shape, lambda i: (0, 0))
    ctx_b = lambda wd: pl.BlockSpec((tb, wd), lambda i: (jnp.minimum(i, n_ctx_blocks - 1), 0))
    lat_b = lambda wd: pl.BlockSpec((tb, wd), lambda i: (jnp.maximum(i - n_ctx_blocks, 0), 0))
    mix_specs, mix_args = [], []
    for a_c, a_l in zip(mix_ctx, mix_lat):
        mix_specs += [ctx_b(a_c.shape[1]), lat_b(a_l.shape[1])]
        mix_args += [a_c, a_l]
    return pl.pallas_call(
        functools.partial(_merge_kernel, n_ctx_blocks),
        grid=(T // tb,),
        in_specs=_x_pair_specs(x_pair, tb, n_ctx_blocks) + mix_specs + [rb(3 * D),
                  pl.BlockSpec((1, 1, 6 * D), lambda i: (row(i), 0, 0))] + [cb(a) for a in w],
        out_specs=[rb(D), rb(D), pl.BlockSpec((TOP_K, tb), lambda i: (0, i)), pl.BlockSpec((TOP_K, tb), lambda i: (0, i)),
                   pl.BlockSpec((tb // DISP_BLOCK, N_EXP, 128), lambda i: (i, 0, 0))],
        out_shape=[jax.ShapeDtypeStruct((T, D), F32), jax.ShapeDtypeStruct((T, D), BF),
                   jax.ShapeDtypeStruct((TOP_K, T), jnp.int32), jax.ShapeDtypeStruct((TOP_K, T), F32),
                   jax.ShapeDtypeStruct((T // DISP_BLOCK, N_EXP, 128), F32)],
        compiler_params=_params("parallel"),
        name="merge_route",
    )(x_pair[0], x_pair[1], *mix_args, zg, mods, *w)


def _silu_mul(a, b):
    return a * _sigmoid(a) * b


DISP_ROW_CHUNK = 256


def _placement(pos_ref, weight_ref, r0):
    rows = _iota((DISP_ROW_CHUNK, DISP_BLOCK), 0) + r0
    p = jnp.zeros((DISP_ROW_CHUNK, DISP_BLOCK), F32)
    for k in range(TOP_K):
        w = 1.0 if weight_ref is None else weight_ref[k:k + 1, :]
        p = jnp.where(rows == pos_ref[k:k + 1, :], w, p)
    return p.astype(BF)


def _dispatch_kernel(n_blocks, h_ref, pos_ref, xs_ref):
    h = jnp.where(pl.program_id(0) < n_blocks, h_ref[...], jnp.zeros_like(h_ref))
    for r0 in range(0, DISP_ROWS, DISP_ROW_CHUNK):
        xs_ref[r0:r0 + DISP_ROW_CHUNK, :] = _dot(_placement(pos_ref, None, r0), h).astype(BF)


def _dispatch(h2, pos8):
    T = h2.shape[0]
    nblk = T // DISP_BLOCK
    last = lambda b: jnp.minimum(b, nblk - 1)
    return pl.pallas_call(
        functools.partial(_dispatch_kernel, nblk),
        grid=(nblk + 1,),
        in_specs=[pl.BlockSpec((DISP_BLOCK, D), lambda b: (last(b), 0)),
                  pl.BlockSpec((TOP_K, DISP_BLOCK), lambda b: (0, last(b)))],
        out_specs=pl.BlockSpec((DISP_ROWS, D), lambda b: (b, 0)),
        out_shape=jax.ShapeDtypeStruct(((nblk + 1) * DISP_ROWS, D), BF),
        compiler_params=_params("parallel"),
        name="moe_dispatch",
    )(h2, pos8)


def _tile_tables(cnt, n_chunks_max, n_tiles_max):
    nblk = cnt.shape[0]
    nch = (cnt + (ROW_GRANULE - 1)) // ROW_GRANULE
    first = jnp.arange(nblk, dtype=jnp.int32)[:, None] * BLOCK_CHUNKS + jnp.cumsum(nch, axis=1) - nch
    tiles_e = (jnp.sum(nch, axis=0) + (CHUNKS_PER_TILE - 1)) // CHUNKS_PER_TILE
    span_e = tiles_e * CHUNKS_PER_TILE
    exp_start = jnp.cumsum(span_e) - span_e
    j = jnp.arange(n_chunks_max, dtype=jnp.int32)
    e_j = jnp.sum((exp_start[None, :] <= j[:, None]).astype(jnp.int32), axis=1) - 1
    onehot = (e_j[:, None] == jnp.arange(N_EXP, dtype=jnp.int32)[None, :]).astype(F32)
    rows_of = lambda tab: jnp.dot(onehot, tab.astype(F32), precision=lax.Precision.HIGHEST).astype(jnp.int32)
    local = j - rows_of(exp_start[:, None])[:, 0]
    blk_len = rows_of(nch.T)
    blk_start = rows_of(jnp.cumsum(nch.T, axis=1) - nch.T)
    inside = (blk_start <= local[:, None]) & (local[:, None] < blk_start + blk_len)
    real = jnp.any(inside, axis=1)
    chunk = jnp.sum(jnp.where(inside, rows_of(first.T) + (local[:, None] - blk_start), 0), axis=1)
    pad_rank = jnp.cumsum(jnp.where(real, 0, 1)) - 1
    src = jnp.where(real, chunk, BLOCK_CHUNKS - 1).astype(jnp.int32)
    dst = jnp.where(real, chunk, nblk * BLOCK_CHUNKS + pad_rank % BLOCK_CHUNKS).astype(jnp.int32)
    tile_end = jnp.cumsum(tiles_e)
    i = jnp.arange(n_tiles_max, dtype=jnp.int32)
    tile_expert = jnp.minimum(jnp.sum((tile_end[None, :] <= i[:, None]).astype(jnp.int32), axis=1), N_EXP - 1)
    return src, dst, tile_expert.astype(jnp.int32), tile_end[-1:].astype(jnp.int32)


def _expert_kernel(src_ref, dst_ref, texp_ref, nused_ref, xs_hbm, wg_ref, wu_ref, wd_ref, ys_hbm,
                   xbuf, ybuf, wgu_bf, wd_bf, gsem, ssem):
    i = pl.program_id(0)
    n_used = nused_ref[0]
    slot = lax.rem(i, 2)

    def chunk_copies(tile, slot_, to_buffer, do):
        for c in range(CHUNKS_PER_TILE):
            j = tile * CHUNKS_PER_TILE + c
            if to_buffer:
                cp = pltpu.make_async_copy(xs_hbm.at[src_ref[j]], xbuf.at[slot_, c], gsem.at[slot_])
            else:
                cp = pltpu.make_async_copy(ybuf.at[slot_, c], ys_hbm.at[dst_ref[j]], ssem.at[slot_])
            do(cp, c)

    start = lambda cp, c: cp.start(priority=c % 2)
    wait = lambda cp, c: cp.wait()

    @pl.when(i == 0)
    def _():
        chunk_copies(0, 0, True, start)

    @pl.when(i == n_used)
    def _():
        chunk_copies(i, slot, True, wait)

    @pl.when(i < n_used)
    def _():
        chunk_copies(i, slot, True, wait)
        chunk_copies(i + 1, 1 - slot, True, start)

        @pl.when((i == 0) | (texp_ref[i] != texp_ref[jnp.maximum(i - 1, 0)]))
        def _():
            wgu_bf[:, 0:E_DIM] = wg_ref[0, 0].astype(BF)
            wgu_bf[:, E_DIM:2 * E_DIM] = wu_ref[0, 0].astype(BF)
            wd_bf[...] = wd_ref[0, 0].astype(BF)

        gu = _dot(xbuf[slot].reshape(EXP_TILE, D), wgu_bf[...])
        hid = _silu_mul(gu[:, 0:E_DIM], gu[:, E_DIM:2 * E_DIM])
        ybuf[slot] = _dot(hid.astype(BF), wd_bf[...]).astype(BF).reshape(CHUNKS_PER_TILE, ROW_GRANULE, D)
        chunk_copies(i, slot, False, start)

        @pl.when(i >= 1)
        def _():
            chunk_copies(i - 1, 1 - slot, False, wait)

        @pl.when(i == n_used - 1)
        def _():
            chunk_copies(i, slot, False, wait)


def _experts(layer, xs, tables, w_eg, w_eu, w_ed, n_tiles_max):
    src, dst, tile_expert, n_used = tables
    chunks = xs.reshape(-1, ROW_GRANULE, D)
    wmap = lambda i, src_, dst_, texp, nu: (layer, texp[i], 0, 0)
    grid_spec = pltpu.PrefetchScalarGridSpec(
        num_scalar_prefetch=4,
        grid=(n_tiles_max + 1,),
        in_specs=[pl.BlockSpec(memory_space=pl.ANY),
                  pl.BlockSpec((1, 1, D, E_DIM), wmap),
                  pl.BlockSpec((1, 1, D, E_DIM), wmap),
                  pl.BlockSpec((1, 1, E_DIM, D), wmap)],
        out_specs=pl.BlockSpec(memory_space=pl.ANY),
        scratch_shapes=[pltpu.VMEM((2, CHUNKS_PER_TILE, ROW_GRANULE, D), BF),
                        pltpu.VMEM((2, CHUNKS_PER_TILE, ROW_GRANULE, D), BF),
                        pltpu.VMEM((D, 2 * E_DIM), BF), pltpu.VMEM((E_DIM, D), BF),
                        pltpu.SemaphoreType.DMA((2,)), pltpu.SemaphoreType.DMA((2,))],
    )
    return pl.pallas_call(
        _expert_kernel,
        grid_spec=grid_spec,
        out_shape=jax.ShapeDtypeStruct(chunks.shape, chunks.dtype),
        input_output_aliases={4: 0},
        compiler_params=_params("arbitrary"),
        name="moe_experts",
    )(src, dst, tile_expert, n_used, chunks, w_eg, w_eu, w_ed).reshape(xs.shape)


def _combine_kernel(final, ys_ref, pos_ref, gate_ref, h_ref, sg_ref, su_ref, sd_ref, x_ref, m_ref, gf_ref, o_ref):
    routed = jnp.zeros((DISP_BLOCK, D), F32)
    for r0 in range(0, DISP_ROWS, DISP_ROW_CHUNK):
        routed = routed + _dot_tn(_placement(pos_ref, gate_ref, r0), ys_ref[r0:r0 + DISP_ROW_CHUNK, :])
    h = h_ref[...]
    sh = _silu_mul(_dot(h, sg_ref[0].astype(BF)), _dot(h, su_ref[0].astype(BF)))
    out = x_ref[...] + m_ref[0] * (routed + _dot(sh.astype(BF), sd_ref[0].astype(BF)))
    if final:
        out = _rms(out, gf_ref[...])
    o_ref[...] = out


def _combine(layer, final, ys, pos8, gate8, h2, w_sg, w_su, w_sd, xm, mods, g_final, n_ctx_blocks, blocks_per_lat,
             first_block, n_blocks):
    tb = DISP_BLOCK
    row = functools.partial(_mod_row, n_ctx_blocks, blocks_per_lat)
    at = lambda b: b + first_block
    return pl.pallas_call(
        functools.partial(_combine_kernel, final),
        grid=(n_blocks,),
        in_specs=[pl.BlockSpec((DISP_ROWS, D), lambda b: (at(b), 0)),
                  pl.BlockSpec((TOP_K, tb), lambda b: (0, at(b))),
                  pl.BlockSpec((TOP_K, tb), lambda b: (0, at(b))),
                  pl.BlockSpec((tb, D), lambda b: (at(b), 0)),
                  pl.BlockSpec((1, D, E_DIM), lambda b: (layer, 0, 0)),
                  pl.BlockSpec((1, D, E_DIM), lambda b: (layer, 0, 0)),
                  pl.BlockSpec((1, E_DIM, D), lambda b: (layer, 0, 0)),
                  pl.BlockSpec((tb, D), lambda b: (at(b), 0)),
                  pl.BlockSpec((1, 1, D), lambda b: (row(at(b)), 0, 5)),
                  pl.BlockSpec((1, D), lambda b: (0, 0))],
        out_specs=pl.BlockSpec((tb, D), lambda b: (b, 0)),
        out_shape=jax.ShapeDtypeStruct((n_blocks * tb, D), F32),
        compiler_params=_params("parallel"),
        name="moe_combine",
    )(ys, pos8, gate8, h2, w_sg, w_su, w_sd, xm, mods, g_final)


def _moe(layer, final, h2, pos8, gate8, cnt, w_eg, w_eu, w_ed, w_sg, w_su, w_sd, xm, mods, g_final, disp_blocks):
    T = h2.shape[0]
    nblk = T // DISP_BLOCK
    n_chunks_max = (TOP_K * T + N_EXP * nblk * (ROW_GRANULE - 1)) // ROW_GRANULE + N_EXP * (CHUNKS_PER_TILE - 1)
    n_tiles_max = -(-n_chunks_max // CHUNKS_PER_TILE)
    xs = _dispatch(h2, pos8)
    tables = _tile_tables(cnt[:, :, 0].astype(jnp.int32), (n_tiles_max + 1) * CHUNKS_PER_TILE, n_tiles_max + 1)
    ys = _experts(layer, xs, tables, w_eg, w_eu, w_ed, n_tiles_max)
    comb = functools.partial(_combine, layer, final, ys, pos8, gate8, h2, w_sg, w_su, w_sd, xm, mods, g_final,
                             *disp_blocks)
    if not final:
        return comb(0, nblk)
    n_ctx = disp_blocks[0]
    return comb(0, n_ctx), comb(n_ctx, nblk - n_ctx)


def _reorder_w_in(w):
    kr = w[:, 2592:2624]
    kr_sw = kr.reshape(D, M_ROPE // 2, 2)[..., ::-1].reshape(kr.shape)
    pad = jnp.zeros(kr.shape, w.dtype)
    parts = [w[:, :1920], w[:, 1952:2592], w[:, 1920:1952], kr, kr_sw, pad, w[:, 2624:]]
    return jnp.concatenate([p.astype(BF) for p in parts], axis=1)


def _reorder_w_q(w_q_up, with_swap):
    L = w_q_up.shape[0]
    w = w_q_up.reshape(L, M_QL, MH, M_NOPE + M_ROPE)
    nope = w[..., :M_NOPE].reshape(L, M_QL, MH * M_NOPE)
    rope = w[..., M_NOPE:]
    parts = [nope, rope.reshape(L, M_QL, MH * M_ROPE)]
    if with_swap:
        parts.append(rope.reshape(L, M_QL, MH, M_ROPE // 2, 2)[..., ::-1].reshape(L, M_QL, MH * M_ROPE))
    return jnp.concatenate(parts, axis=2).astype(BF)


def _reorder_w_kv(w_kv_up):
    L = w_kv_up.shape[0]
    w = w_kv_up.reshape(L, M_KVL, MH, M_NOPE + M_V)
    return jnp.concatenate([w[..., :M_NOPE].reshape(L, M_KVL, MH * M_NOPE),
                            w[..., M_NOPE:].reshape(L, M_KVL, MH * M_V)], axis=2).astype(BF)


def kernel(x_prompt, x_sample, state_gla_fwd, state_gla_bwd, cache_mla_ckv, cache_mla_krope, c, c_ctx, w_mod, b_mod, g_norm1, g_norm2, w_in, w_gla_gate_f, b_gla_gate_f, w_gla_gate_b, b_gla_gate_b, g_gla_out, g_q_a, w_q_up, g_kv_a, w_kv_up, w_br_fourier, w_br_gla, w_br_mla, w_out, w_router, b_router, w_exp_gate, w_exp_up, w_exp_down, w_sh_gate, w_sh_up, w_sh_down, g_final):
    nb, sl, _ = x_prompt.shape
    db, dl, _ = x_sample.shape
    L = w_mod.shape[0]
    t_ctx, t_lat = nb * sl, db * dl
    T = t_ctx + t_lat
    assert sl % G_CHUNK == 0 and dl % G_CHUNK == 0 and dl % GRID_W == 0
    assert t_ctx % dl == 0 and dl % TOKEN_BLOCK == 0 and TOKEN_BLOCK % DISP_BLOCK == 0
    assert t_ctx % TOKEN_BLOCK == 0 and dl % Q_BLOCK == 0 and 1 + db <= 8

    x_pair = (x_prompt.reshape(t_ctx, D), x_sample.reshape(t_lat, D), 0)
    cond = jnp.concatenate([c_ctx[None, :], c, jnp.zeros((7 - db, D), F32)], axis=0)
    mods_all = _modulation(cond, w_mod, b_mod)

    w_in_r = [_reorder_w_in(w_in[l]) for l in range(L)]
    wq_ctx = _reorder_w_q(w_q_up, False)
    wq_lat = _reorder_w_q(w_q_up, True)
    wkv_r = _reorder_w_kv(w_kv_up)
    lat_off = t_ctx // dl

    new_f, new_b, new_ckv, new_kr = [], [], [], []
    for l in range(L):
        mods = mods_all[l].reshape(8, 1, 6 * D)
        tok = (t_ctx // TOKEN_BLOCK, dl // TOKEN_BLOCK)
        zf, zqk, zvr, zmla, zsm, zg = _in_projection(x_pair, T, mods, g_norm1[l][None, :], w_in_r[l], *tok)

        (f_c,) = _fourier(zf, sl, nb, 0)
        (f_l,) = _fourier(zf, dl, db, lat_off)

        gate_w = (w_gla_gate_f[l], b_gla_gate_f[l][None, :], w_gla_gate_b[l], b_gla_gate_b[l][None, :],
                  g_gla_out[l].reshape(1, GH * GDV))
        og_c, s_f, s_b = _gla(zqk, zvr, zsm, gate_w, sl, nb, 0, None)
        og_l, _, _ = _gla(zqk, zvr, zsm, gate_w, dl, db, lat_off, (state_gla_fwd[:, l], state_gla_bwd[:, l]))

        gq, gkv = g_q_a[l][None, :], g_kv_a[l][None, :]
        om_c, ckv = _mla(zmla, zsm, (gq, wq_ctx[l], gkv, wkv_r[l]), sl, nb, 0, None)
        (om_l,) = _mla(zmla, zsm, (gq, wq_lat[l], gkv, wkv_r[l]), dl, db, lat_off,
                       (cache_mla_ckv[:, l], cache_mla_krope[:, l]))

        mw = (w_br_fourier[l].astype(BF), w_br_gla[l].astype(BF), w_br_mla[l].astype(BF), w_out[l].astype(BF),
              g_norm2[l][None, :], w_router[l].T, b_router[l][:, None])
        xm, h2, pos8, gate8, cnt = _merge(x_pair, T, (f_c, og_c, om_c), (f_l, og_l, om_l), zg, mods, mw, *tok)

        x = _moe(l, l == L - 1, h2, pos8, gate8, cnt, w_exp_gate, w_exp_up, w_exp_down, w_sh_gate, w_sh_up, w_sh_down,
                 xm, mods, g_final[None, :], (t_ctx // DISP_BLOCK, dl // DISP_BLOCK))
        if l < L - 1:
            x_pair = (x, x, t_ctx // TOKEN_BLOCK)

        new_f.append(s_f)
        new_b.append(s_b)
        new_ckv.append(ckv.reshape(nb, sl, M_KVL))
        new_kr.append(zsm[:t_ctx, 32:64].reshape(nb, sl, M_ROPE))

    y_prompt = x[0].reshape(nb, sl, D)
    y_sample = x[1].reshape(db, dl, D)
    return (y_prompt, y_sample, jnp.stack(new_f, axis=1), jnp.stack(new_b, axis=1),
            jnp.stack(new_ckv, axis=1), jnp.stack(new_kr, axis=1))
```

```python
import functools

import numpy as np
import jax
import jax.numpy as jnp
from jax import lax
from jax.experimental import pallas as pl
from jax.experimental.pallas import tpu as pltpu

F32 = jnp.float32
BF = jnp.bfloat16

D = 1024
GRID_W = 64
FN_G, FN_GW = 4, 96
FN_W = FN_G * FN_GW
GH, GDK, GDV = 4, 64, 128
G_RANK = 16
G_TAU = 16.0
G_CHUNK = 64
MH, M_NOPE, M_ROPE, M_V = 8, 64, 32, 64
M_QL, M_KVL = 384, 256
ROPE_BASE = 10000.0
N_EXP, TOP_K, N_GRP, TOPK_GRP = 64, 8, 8, 4
E_DIM = 256
ROUTED_SCALE = 2.5
EPS = 1e-6

C_F = (0, 384)
C_QK = (384, 896)
C_VR = (896, 1920)
C_MLA = (1920, 2560)
C_SM = (2560, 2688)
C_G = (2688, 5760)
IN_COLS_R = 5760

VMEM_LIMIT_V7X = 56 * 1024 * 1024
TOKEN_BLOCK = 512
Q_BLOCK = 256
DISP_BLOCK = 256
ROW_GRANULE = 16
DISP_ROWS = -(-(TOP_K * DISP_BLOCK + N_EXP * (ROW_GRANULE - 1)) // 256) * 256
EXP_TILE = 512
CHUNKS_PER_TILE = EXP_TILE // ROW_GRANULE
BLOCK_CHUNKS = DISP_ROWS // ROW_GRANULE


def _params(*sem):
    return pltpu.CompilerParams(dimension_semantics=sem, vmem_limit_bytes=VMEM_LIMIT_V7X)


def _dot(a, b):
    return jnp.dot(a, b, preferred_element_type=F32)


def _dot_hi(a, b):
    return jnp.dot(a, b, precision=lax.Precision.HIGHEST, preferred_element_type=F32)


def _dot_nt(a, b, precision=None):
    return lax.dot_general(a, b, (((1,), (1,)), ((), ())), precision=precision, preferred_element_type=F32)


def _dot_tn(a, b):
    return lax.dot_general(a, b, (((0,), (0,)), ((), ())), preferred_element_type=F32)


def _sigmoid(x):
    return 1.0 / (1.0 + jnp.exp(-x))


def _rms(x, g):
    return x * lax.rsqrt(jnp.mean(x * x, axis=-1, keepdims=True) + EPS) * g


def _iota(shape, dim):
    return lax.broadcasted_iota(jnp.int32, shape, dim)


def _mod_row(n_ctx_blocks, blocks_per_lat, i):
    return jnp.where(i < n_ctx_blocks, 0, 1 + (i - n_ctx_blocks) // blocks_per_lat)


def _mod_kernel(c_ref, w_ref, b_ref, o_ref):
    c = c_ref[...]
    o_ref[0] = _dot_hi(c * _sigmoid(c), w_ref[0]) + b_ref[0]


def _modulation(cond, w_mod, b_mod):
    L = w_mod.shape[0]
    rows = cond.shape[0]
    tn = 1536
    return pl.pallas_call(
        _mod_kernel,
        grid=(L, 6 * D // tn),
        in_specs=[pl.BlockSpec((rows, D), lambda l, j: (0, 0)),
                  pl.BlockSpec((1, D, tn), lambda l, j: (l, 0, j)),
                  pl.BlockSpec((1, 1, tn), lambda l, j: (l, 0, j))],
        out_specs=pl.BlockSpec((1, rows, tn), lambda l, j: (l, 0, j)),
        out_shape=jax.ShapeDtypeStruct((L, rows, 6 * D), F32),
        compiler_params=_params("parallel", "parallel"),
        name="modulation",
    )(cond, w_mod, b_mod.reshape(L, 1, 6 * D))


def _inproj_kernel(n_ctx_blocks, xc_ref, xl_ref, m_ref, g_ref, w_ref, of_ref, oqk_ref, ovr_ref, omla_ref, osm_ref, og_ref):
    y = _rms(jnp.where(pl.program_id(0) < n_ctx_blocks, xc_ref[...], xl_ref[...]), g_ref[...])
    h = (y * (1.0 + m_ref[0, :, D:2 * D]) + m_ref[0, :, 0:D]).astype(BF)
    of_ref[...] = _dot(h, w_ref[:, C_F[0]:C_F[1]]).astype(BF)
    oqk_ref[...] = _dot(h, w_ref[:, C_QK[0]:C_QK[1]])
    ovr_ref[...] = _dot(h, w_ref[:, C_VR[0]:C_VR[1]]).astype(BF)
    omla_ref[...] = _dot(h, w_ref[:, C_MLA[0]:C_MLA[1]])
    osm_ref[...] = _dot(h, w_ref[:, C_SM[0]:C_SM[1]])
    og_ref[...] = _dot(h, w_ref[:, C_G[0]:C_G[1]]).astype(BF)


def _x_pair_specs(x_pair, tb, n_ctx_blocks):
    _, _, lat_off = x_pair
    return [pl.BlockSpec((tb, D), lambda i: (jnp.minimum(i, n_ctx_blocks - 1), 0)),
            pl.BlockSpec((tb, D), lambda i: (jnp.maximum(i - n_ctx_blocks, 0) + lat_off, 0))]


def _in_projection(x_pair, T, mods, g1, w_in_r, n_ctx_blocks, blocks_per_lat):
    tb = TOKEN_BLOCK
    row = functools.partial(_mod_row, n_ctx_blocks, blocks_per_lat)
    widths = [(C_F, BF), (C_QK, F32), (C_VR, BF), (C_MLA, F32), (C_SM, F32), (C_G, BF)]
    return pl.pallas_call(
        functools.partial(_inproj_kernel, n_ctx_blocks),
        grid=(T // tb,),
        in_specs=_x_pair_specs(x_pair, tb, n_ctx_blocks) + [
                  pl.BlockSpec((1, 1, 2 * D), lambda i: (row(i), 0, 0)),
                  pl.BlockSpec((1, D), lambda i: (0, 0)),
                  pl.BlockSpec((D, IN_COLS_R), lambda i: (0, 0))],
        out_specs=[pl.BlockSpec((tb, c[1] - c[0]), lambda i: (i, 0)) for c, _ in widths],
        out_shape=[jax.ShapeDtypeStruct((T, c[1] - c[0]), dt) for c, dt in widths],
        compiler_params=_params("parallel"),
        name="in_projection",
    )(x_pair[0], x_pair[1], mods, g1, w_in_r)


def _seq_call(kernel, name, n, nseq, blk_off, seq_ins, const_ins, out_widths, extra_outs=(), scratch=()):
    in_specs = [pl.BlockSpec((n, a.shape[1]), lambda i: (i + blk_off, 0)) for a in seq_ins]
    in_specs += [pl.BlockSpec(bs, im) for _, bs, im in const_ins]
    args = list(seq_ins) + [a for a, _, _ in const_ins]
    out_specs = [pl.BlockSpec((n, w), lambda i: (i, 0)) for w, _ in out_widths]
    out_shape = [jax.ShapeDtypeStruct((nseq * n, w), dt) for w, dt in out_widths]
    out_specs += [pl.BlockSpec(bs, im) for _, _, bs, im in extra_outs]
    out_shape += [jax.ShapeDtypeStruct(s, dt) for s, dt, _, _ in extra_outs]
    return pl.pallas_call(
        kernel, grid=(nseq,), in_specs=in_specs, out_specs=out_specs, out_shape=out_shape,
        scratch_shapes=list(scratch), compiler_params=_params("parallel"), name=name,
    )(*args)


def _fourier_kernel(u_ref, r_ref, lc_ref, ls_ref, o_ref):
    y = _dot(u_ref[...], r_ref[...])
    o_ref[...] = (_dot(lc_ref[...], y[:, :FN_W].astype(BF)) + _dot(ls_ref[...], y[:, FN_W:].astype(BF))).astype(BF)


def _dft_tables(n):
    k = np.arange(FN_GW)
    ang = 2.0 * np.pi * ((k[:, None] * k[None, :]) % FN_GW) / FN_GW
    eye = np.eye(FN_G)
    right = np.concatenate([np.kron(eye, np.cos(ang)), np.kron(eye, np.sin(ang))], axis=1)
    p = np.arange(n)
    angn = 2.0 * np.pi * ((p[:, None] * p[None, :]) % n) / n
    scale = 1.0 / np.sqrt(float(n * FN_GW))
    return (jnp.asarray(right, F32).astype(BF), jnp.asarray(np.cos(angn) * scale, F32).astype(BF),
            jnp.asarray(-np.sin(angn) * scale, F32).astype(BF))


def _fourier(zf, n, nseq, blk_off):
    right, lc, ls = _dft_tables(n)
    consts = [(right, (FN_W, 2 * FN_W), lambda i: (0, 0)), (lc, (n, n), lambda i: (0, 0)), (ls, (n, n), lambda i: (0, 0))]
    return _seq_call(_fourier_kernel, "fourier_mix", n, nseq, blk_off, [zf], consts, [(FN_W, BF)])


def _bf_parts(x, n):
    parts, rest = [], x
    for _ in range(n):
        p = rest.astype(BF)
        parts.append(p)
        rest = rest - p.astype(F32)
    return parts


def _dot_f32(a, b):
    a1, a2 = _bf_parts(a, 2)
    b1, b2 = _bf_parts(b, 2)
    return (_dot(a1, b2) + _dot(a2, b1)) + _dot(a1, b1)


def _cumulate(tri, g):
    g1, g2, g3 = _bf_parts(g, 3)
    return (_dot(tri, g3) + _dot(tri, g2)) + _dot(tri, g1)


def _log_gate(z, w_ref, b_ref):
    pre = _dot_f32(z, w_ref[...]) + b_ref[...]
    return (jnp.minimum(pre, 0.0) - jnp.log1p(jnp.exp(-jnp.abs(pre)))) * (1.0 / G_TAU)


def _gla_kernel(has_state, n, *refs):
    if has_state:
        (zqk, zvr, zsm, wgf, bgf, wgb, bgb, gout, s0f, s0b, o_ref, sf_ref, sb_ref,
         oacc_f, oacc_b, lg_f, lg_b, st_f, st_b) = refs
    else:
        (zqk, zvr, zsm, wgf, bgf, wgb, bgb, gout, o_ref, sf_ref, sb_ref,
         oacc_f, oacc_b, lg_f, lg_b, st_f, st_b) = refs
        s0f = s0b = None
    C = G_CHUNK
    nc = n // C
    ri, ci = _iota((C, C), 0), _iota((C, C), 1)
    t_idx, lane = _iota((C, 128), 0), _iota((C, 128), 1)
    s_idx = lane & (C - 1)
    left = lane < GDK
    vleft = _iota((C, 2 * GDV), 1) < GDV
    blockdiag = (_iota((2 * GDV, 2 * GDK), 0) >> 7) == (_iota((2 * GDV, 2 * GDK), 1) >> 6)
    row_w = _iota((C, GH * GDK), 0)

    def block_reference(cum, blk, off):
        if blk >= 8:
            return jnp.concatenate([jnp.broadcast_to(cum[j * blk + off:j * blk + off + 1], (blk, cum.shape[1]))
                                    for j in range(C // blk)], axis=0)
        out = cum
        for m in range(blk):
            if m != off:
                out = jnp.where((row_w & (blk - 1)) == m, pltpu.roll(cum, (m - off) % C, axis=0), out)
        return out

    def pair_blocks(x, ls):
        xp = x[:, ls]
        z = jnp.zeros_like(xp)
        return jnp.concatenate([jnp.where(left, xp, z), jnp.where(left, z, xp)], axis=0)

    def intra_scores(q, k, cum, reverse):
        levels = []
        blk = C
        while blk >= 2:
            half = blk // 2
            ref = block_reference(cum, blk, half if reverse else half - 1)
            d = cum - ref
            w = jnp.exp(jnp.minimum(d, -d))
            qs = (q * w).astype(BF)
            ks = (k * w).astype(BF)
            t_in, s_in = t_idx & (blk - 1), s_idx & (blk - 1)
            same = (t_idx & -blk) == (s_idx & -blk)
            split = ((s_in >= half) & (t_in < half)) if reverse else ((t_in >= half) & (s_in < half))
            levels.append((qs, ks, same & split))
            blk = half
        levels.append((q.astype(BF), k.astype(BF), t_idx == s_idx))
        out = []
        for p in range(2):
            ls = slice(128 * p, 128 * p + 128)
            sc = jnp.zeros((C, 128), F32)
            for qs, ks, m in levels:
                sc = jnp.where(m, _dot_nt(qs[:, ls], pair_blocks(ks, ls)), sc)
            out.append(sc.astype(BF))
        return out

    def load_state(st, s0_ref):
        for p in range(2):
            if s0_ref is None:
                st[p] = jnp.zeros((2 * GDV, 2 * GDK), F32)
            else:
                z = jnp.zeros((GDK, GDV), F32)
                blk = jnp.concatenate([jnp.concatenate([s0_ref[0, 2 * p], z], axis=1),
                                       jnp.concatenate([z, s0_ref[0, 2 * p + 1]], axis=1)], axis=0)
                st[p] = blk.T

    def store_state(st, out_ref):
        for p in range(2):
            blk = st[p].T
            out_ref[0, 2 * p] = blk[0:GDK, 0:GDV]
            out_ref[0, 2 * p + 1] = blk[GDK:2 * GDK, GDV:2 * GDV]

    def chunk_scores(qh, k, cum, reverse):
        mask = (s_idx >= t_idx) if reverse else (s_idx <= t_idx)
        kh = (k * jnp.exp(-cum)).astype(BF)
        return [jnp.where(mask, _dot_nt(qh[:, 128 * p:128 * p + 128], pair_blocks(kh, slice(128 * p, 128 * p + 128))),
                          0.0).astype(BF) for p in range(2)]

    def chunk(small_decay, c, reverse, lg, st, oacc):
        tri = (ci >= ri).astype(BF) if reverse else (ci <= ri).astype(BF)
        rows = pl.ds(pl.multiple_of(c * C, C), C)
        cum = _cumulate(tri, lg[rows, :])
        tot = cum[0:1] if reverse else cum[C - 1:C]
        q = zqk[rows, 0:GH * GDK] * (GDK ** -0.5)
        k = zqk[rows, GH * GDK:2 * GH * GDK]
        qh = (q * jnp.exp(cum)).astype(BF)
        kb = (k * jnp.exp(tot - cum)).astype(BF)
        dec = jnp.exp(tot)
        scores = chunk_scores(qh, k, cum, reverse) if small_decay else intra_scores(q, k, cum, reverse)
        for p in range(2):
            ls = slice(128 * p, 128 * p + 128)
            vs = slice(256 * p, 256 * p + 256)
            vp = zvr[rows, vs]
            zv = jnp.zeros_like(vp)
            vblk = jnp.concatenate([jnp.where(vleft, vp, zv), jnp.where(vleft, zv, vp)], axis=0)
            stp = st[p]
            oacc[rows, vs] = _dot(scores[p], vblk) + _dot_nt(qh[:, ls], stp.astype(BF))
            st[p] = dec[:, ls] * stp + jnp.where(blockdiag, _dot_tn(vp, kb[:, ls]), 0.0)

    def both_directions(small_decay, step, carry):
        chunk(small_decay, step, False, lg_f, st_f, oacc_f)
        chunk(small_decay, nc - 1 - step, True, lg_b, st_b, oacc_b)
        return carry

    gf = _log_gate(zsm[:, 0:G_RANK], wgf, bgf)
    gb = _log_gate(zsm[:, G_RANK:2 * G_RANK], wgb, bgb)
    lg_f[...] = gf
    lg_b[...] = gb
    load_state(st_f, s0f)
    load_state(st_b, s0b)
    small = jnp.minimum(jnp.min(gf), jnp.min(gb)) * C > -60.0
    for small_decay in (True, False):
        @pl.when(small if small_decay else jnp.logical_not(small))
        def _():
            lax.fori_loop(0, nc, functools.partial(both_directions, small_decay), 0, unroll=min(nc, 4))
    store_state(st_f, sf_ref)
    store_state(st_b, sb_ref)

    rb = 128
    for r0 in range(0, n, rb):
        for h in range(GH):
            hs = slice(GDV * h, GDV * h + GDV)
            oh = oacc_f[r0:r0 + rb, hs] + oacc_b[r0:r0 + rb, hs]
            oh = oh * lax.rsqrt(jnp.mean(oh * oh, axis=-1, keepdims=True) + EPS) * gout[:, hs]
            r = zvr[r0:r0 + rb, GH * GDV + hs.start:GH * GDV + hs.stop].astype(F32)
            o_ref[r0:r0 + rb, hs] = (oh * (r * _sigmoid(r))).astype(BF)


def _gla(zqk, zvr, zsm, gate_w, n, nseq, blk_off, states):
    wgf, bgf, wgb, bgb, gout = gate_w
    c2 = lambda i: (0, 0)
    consts = [(wgf, wgf.shape, c2), (bgf, bgf.shape, c2), (wgb, wgb.shape, c2), (bgb, bgb.shape, c2), (gout, gout.shape, c2)]
    st_blk = (1, GH, GDK, GDV)
    st_map = lambda i: (i, 0, 0, 0)
    if states is not None:
        consts += [(s, st_blk, st_map) for s in states]
    extra = [((nseq, GH, GDK, GDV), F32, st_blk, st_map)] * 2
    scratch = ([pltpu.VMEM((n, GH * GDV), F32)] * 2 + [pltpu.VMEM((n, GH * GDK), F32)] * 2
               + [pltpu.VMEM((2, 2 * GDV, 2 * GDK), F32)] * 2)
    return _seq_call(functools.partial(_gla_kernel, states is not None, n), "gla_mixer", n, nseq, blk_off,
                     [zqk, zvr, zsm], consts, [(GH * GDV, BF)], extra_outs=extra, scratch=scratch)


def _mla_kernel(latent, n, past, *refs):
    if latent:
        (zmla, zsm, gq, wq, gkv, wkv, cckv, ckr, cosq, sinq, cosk, sink, o_ref, qs, kns, vs, krs) = refs
    else:
        (zmla, zsm, gq, wq, gkv, wkv, o_ref, ckv_ref, qs, kns, vs, krs) = refs
    sk = past + n
    scale = (M_NOPE + M_ROPE) ** -0.5
    nw, rw = MH * M_NOPE, MH * M_ROPE
    qa = _dot(_rms(zmla[:, 0:M_QL], gq[...]).astype(BF), wq[...])
    qr = qa[:, nw:nw + rw]
    if latent:
        qr = qr * cosq[...] + qa[:, nw + rw:nw + 2 * rw] * sinq[...]
    qs[:, 0:nw] = qa[:, 0:nw] * scale
    qs[:, nw:nw + rw] = qr * scale
    ckv = _rms(zmla[:, M_QL:M_QL + M_KVL], gkv[...])
    kv = _dot(ckv.astype(BF), wkv[...])
    kr = zsm[:, 32:64]
    if latent:
        kr = kr * cosk[...] + zsm[:, 64:96] * sink[...]
        kvc = _dot(cckv[0].astype(BF), wkv[...])
        kns[0:past, :] = kvc[:, 0:nw].astype(BF)
        vs[0:past, :] = kvc[:, nw:].astype(BF)
        krs[0:past, :] = jnp.concatenate([ckr[0]] * 4, axis=1).astype(BF)
    else:
        ckv_ref[...] = ckv
    kns[past:sk, :] = kv[:, 0:nw].astype(BF)
    vs[past:sk, :] = kv[:, nw:].astype(BF)
    krs[past:sk, :] = jnp.concatenate([kr] * 4, axis=1).astype(BF)

    qb = min(Q_BLOCK, n)
    lane = _iota((qb, 128), 1)

    def block(step, carry):
        rows = pl.ds(pl.multiple_of(step * qb, qb), qb)
        for p in range(MH // 2):
            ls = slice(128 * p, 128 * p + 128)
            qn = qs[rows, ls]
            quad = (2 * p) // 4
            qrp = qs[rows, nw + 128 * quad:nw + 128 * quad + 128]
            rhs = jnp.concatenate([kns[:, ls], krs[...]], axis=1)
            vp = vs[:, ls]
            o_pair = None
            for hh in range(2):
                j = (2 * p + hh) % 4
                qn_m = jnp.where((lane >> 6) == hh, qn, 0.0).astype(BF)
                qr_m = jnp.where((lane >> 5) == j, qrp, 0.0).astype(BF)
                s = _dot_nt(jnp.concatenate([qn_m, qr_m], axis=1), rhs)
                e = jnp.exp(s - jnp.max(s, axis=-1, keepdims=True))
                pv = _dot(e.astype(BF), vp) / jnp.sum(e, axis=-1, keepdims=True)
                o_pair = pv if hh == 0 else jnp.where(lane < M_V, o_pair, pv)
            o_ref[rows, ls] = o_pair.astype(BF)
        return carry

    lax.fori_loop(0, n // qb, block, 0)


def _rope_tables(n):
    half = M_ROPE // 2
    pos = jnp.arange(n)
    row = (pos // GRID_W).astype(F32)
    col = (pos % GRID_W).astype(F32)
    inv = ROPE_BASE ** (-jnp.arange(0, half, 2, dtype=F32) / half)
    ang = jnp.concatenate([row[:, None] * inv, col[:, None] * inv], axis=-1)
    cos = jnp.repeat(jnp.cos(ang), 2, axis=-1)
    sin = jnp.repeat(jnp.sin(ang), 2, axis=-1) * jnp.tile(jnp.asarray([-1.0, 1.0], F32), half)
    return jnp.tile(cos, (1, MH)), jnp.tile(sin, (1, MH)), cos, sin


def _mla(zmla, zsm, w, n, nseq, blk_off, cache):
    gq, wq, gkv, wkv = w
    c2 = lambda i: (0, 0)
    consts = [(gq, gq.shape, c2), (wq, wq.shape, c2), (gkv, gkv.shape, c2), (wkv, wkv.shape, c2)]
    past = 0
    extra = []
    if cache is not None:
        cckv, ckr = cache
        past = cckv.shape[1]
        c3 = lambda i: (i, 0, 0)
        consts += [(cckv, (1, past, M_KVL), c3), (ckr, (1, past, M_ROPE), c3)]
        consts += [(t, t.shape, c2) for t in _rope_tables(n)]
    else:
        extra = [((nseq * n, M_KVL), F32, (n, M_KVL), lambda i: (i, 0))]
    sk = past + n
    scratch = [pltpu.VMEM((n, MH * (M_NOPE + M_ROPE)), F32), pltpu.VMEM((sk, MH * M_NOPE), BF),
               pltpu.VMEM((sk, MH * M_V), BF), pltpu.VMEM((sk, 128), BF)]
    return _seq_call(functools.partial(_mla_kernel, cache is not None, n, past), "mla_mixer", n, nseq, blk_off,
                     [zmla, zsm], consts, [(MH * M_V, BF)], extra_outs=extra, scratch=scratch)


def _route(logits_t, bias):
    nt = logits_t.shape[1]
    gsz = N_EXP // N_GRP
    scores = _sigmoid(logits_t)
    sel = scores + bias
    neg = -jnp.inf
    sub = _iota((gsz, nt), 0)
    tops = []
    for g in range(N_GRP):
        blk = sel[gsz * g:gsz * g + gsz]
        m1 = jnp.max(blk, axis=0, keepdims=True)
        first = jnp.min(jnp.where(blk == m1, sub, gsz), axis=0, keepdims=True)
        m2 = jnp.max(jnp.where(sub == first, neg, blk), axis=0, keepdims=True)
        tops.append(m1 + m2)
    gs = jnp.concatenate(tops, axis=0)
    gidx = _iota((N_GRP, nt), 0)
    grank = jnp.zeros((N_GRP, nt), jnp.int32)
    for j in range(N_GRP):
        rj = gs[j:j + 1]
        grank += ((rj > gs) | ((rj == gs) & (gidx > j))).astype(jnp.int32)
    keep = grank < TOPK_GRP
    masked = jnp.concatenate(
        [jnp.where(jnp.broadcast_to(keep[g:g + 1], (gsz, nt)), sel[gsz * g:gsz * g + gsz], neg) for g in range(N_GRP)], axis=0)
    eidx = _iota((N_EXP, nt), 0)
    chosen = eidx < 0
    work = masked
    for _ in range(TOP_K):
        top = jnp.max(work, axis=0, keepdims=True)
        first = jnp.min(jnp.where(work == top, eidx, N_EXP), axis=0, keepdims=True)
        hit = eidx == first
        chosen = chosen | hit
        work = jnp.where(hit, neg, work)
    w = jnp.where(chosen, scores, 0.0)
    return chosen, w / jnp.sum(w, axis=0, keepdims=True) * ROUTED_SCALE


def _dispatch_meta(chosen, gates_t):
    tb = chosen.shape[1]
    sel = chosen.astype(F32)
    selb = sel.astype(BF)
    earlier = (_iota((tb, tb), 0) < _iota((tb, tb), 1)).astype(BF)
    rank = _dot(selb, earlier)
    cnt = jnp.sum(sel, axis=1, keepdims=True)
    padded = jnp.floor((cnt + (ROW_GRANULE - 1)) * (1.0 / ROW_GRANULE)) * ROW_GRANULE
    below = (_iota((N_EXP, N_EXP), 1) < _iota((N_EXP, N_EXP), 0)).astype(BF)
    start = _dot(below, jnp.broadcast_to(padded, (N_EXP, 128)).astype(BF))[:, 0:1]
    pos = start + rank
    kidx = _dot(below, selb)
    pos8, gate8 = [], []
    for k in range(TOP_K):
        hit = chosen & (kidx == float(k))
        pos8.append(jnp.sum(jnp.where(hit, pos, 0.0), axis=0, keepdims=True))
        gate8.append(jnp.sum(jnp.where(hit, gates_t, 0.0), axis=0, keepdims=True))
    return (jnp.concatenate(pos8, axis=0).astype(jnp.int32), jnp.concatenate(gate8, axis=0), cnt)


def _merge_kernel(n_ctx_blocks, xc_ref, xl_ref, fc_ref, fl_ref, ogc_ref, ogl_ref, omc_ref, oml_ref, zg_ref, m_ref,
                  wbf, wbg, wbm, wout, gn2, wrt, brt, xm_ref, h2_ref, pos_ref, gate_ref, cnt_ref):
    is_ctx = pl.program_id(0) < n_ctx_blocks
    ya = _dot(jnp.where(is_ctx, fc_ref[...], fl_ref[...]), wbf[...])
    yb = _dot(jnp.where(is_ctx, ogc_ref[...], ogl_ref[...]), wbg[...])
    yc = _dot(jnp.where(is_ctx, omc_ref[...], oml_ref[...]), wbm[...])
    merged = (_sigmoid(zg_ref[:, 0:D]) * ya.astype(BF) + _sigmoid(zg_ref[:, D:2 * D]) * yb.astype(BF)
              + _sigmoid(zg_ref[:, 2 * D:3 * D]) * yc.astype(BF))
    xm = jnp.where(is_ctx, xc_ref[...], xl_ref[...]) + m_ref[0, :, 2 * D:3 * D] * _dot(merged, wout[...])
    xm_ref[...] = xm
    h2 = _rms(xm, gn2[...]) * (1.0 + m_ref[0, :, 4 * D:5 * D]) + m_ref[0, :, 3 * D:4 * D]
    h2_ref[...] = h2.astype(BF)
    chosen, gates_t = _route(_dot_nt(wrt[...], h2, precision=lax.Precision.HIGHEST), brt[...])
    for sb in range(gates_t.shape[1] // DISP_BLOCK):
        ls = slice(sb * DISP_BLOCK, (sb + 1) * DISP_BLOCK)
        pos8, gate8, cnt = _dispatch_meta(chosen[:, ls], gates_t[:, ls])
        pos_ref[:, ls] = pos8
        gate_ref[:, ls] = gate8
        cnt_ref[sb] = jnp.broadcast_to(cnt, (N_EXP, 128))


def _merge(x_pair, T, mix_ctx, mix_lat, zg, mods, w, n_ctx_blocks, blocks_per_lat):
    tb = TOKEN_BLOCK
    row = functools.partial(_mod_row, n_ctx_blocks, blocks_per_lat)
    rb = lambda wd: pl.BlockSpec((tb, wd), lambda i: (i, 0))
    cb = lambda a: pl.BlockSpec(a.shape, lambda i: (0, 0))
    ctx_b = lambda wd: pl.BlockSpec((tb, wd), lambda i: (jnp.minimum(i, n_ctx_blocks - 1), 0))
    lat_b = lambda wd: pl.BlockSpec((tb, wd), lambda i: (jnp.maximum(i - n_ctx_blocks, 0), 0))
    mix_specs, mix_args = [], []
    for a_c, a_l in zip(mix_ctx, mix_lat):
        mix_specs += [ctx_b(a_c.shape[1]), lat_b(a_l.shape[1])]
        mix_args += [a_c, a_l]
    return pl.pallas_call(
        functools.partial(_merge_kernel, n_ctx_blocks),
        grid=(T // tb,),
        in_specs=_x_pair_specs(x_pair, tb, n_ctx_blocks) + mix_specs + [rb(3 * D),
                  pl.BlockSpec((1, 1, 6 * D), lambda i: (row(i), 0, 0))] + [cb(a) for a in w],
        out_specs=[rb(D), rb(D), pl.BlockSpec((TOP_K, tb), lambda i: (0, i)), pl.BlockSpec((TOP_K, tb), lambda i: (0, i)),
                   pl.BlockSpec((tb // DISP_BLOCK, N_EXP, 128), lambda i: (i, 0, 0))],
        out_shape=[jax.ShapeDtypeStruct((T, D), F32), jax.ShapeDtypeStruct((T, D), BF),
                   jax.ShapeDtypeStruct((TOP_K, T), jnp.int32), jax.ShapeDtypeStruct((TOP_K, T), F32),
                   jax.ShapeDtypeStruct((T // DISP_BLOCK, N_EXP, 128), F32)],
        compiler_params=_params("parallel"),
        name="merge_route",
    )(x_pair[0], x_pair[1], *mix_args, zg, mods, *w)


def _silu_mul(a, b):
    return a * _sigmoid(a) * b


DISP_ROW_CHUNK = 256


def _placement(pos_ref, weight_ref, r0):
    rows = _iota((DISP_ROW_CHUNK, DISP_BLOCK), 0) + r0
    p = jnp.zeros((DISP_ROW_CHUNK, DISP_BLOCK), F32)
    for k in range(TOP_K):
        w = 1.0 if weight_ref is None else weight_ref[k:k + 1, :]
        p = jnp.where(rows == pos_ref[k:k + 1, :], w, p)
    return p.astype(BF)


def _dispatch_kernel(n_blocks, h_ref, pos_ref, xs_ref):
    h = jnp.where(pl.program_id(0) < n_blocks, h_ref[...], jnp.zeros_like(h_ref))
    for r0 in range(0, DISP_ROWS, DISP_ROW_CHUNK):
        xs_ref[r0:r0 + DISP_ROW_CHUNK, :] = _dot(_placement(pos_ref, None, r0), h).astype(BF)


def _dispatch(h2, pos8):
    T = h2.shape[0]
    nblk = T // DISP_BLOCK
    last = lambda b: jnp.minimum(b, nblk - 1)
    return pl.pallas_call(
        functools.partial(_dispatch_kernel, nblk),
        grid=(nblk + 1,),
        in_specs=[pl.BlockSpec((DISP_BLOCK, D), lambda b: (last(b), 0)),
                  pl.BlockSpec((TOP_K, DISP_BLOCK), lambda b: (0, last(b)))],
        out_specs=pl.BlockSpec((DISP_ROWS, D), lambda b: (b, 0)),
        out_shape=jax.ShapeDtypeStruct(((nblk + 1) * DISP_ROWS, D), BF),
        compiler_params=_params("parallel"),
        name="moe_dispatch",
    )(h2, pos8)


def _tile_tables(cnt, n_chunks_max, n_tiles_max):
    nblk = cnt.shape[0]
    nch = (cnt + (ROW_GRANULE - 1)) // ROW_GRANULE
    first = jnp.arange(nblk, dtype=jnp.int32)[:, None] * BLOCK_CHUNKS + jnp.cumsum(nch, axis=1) - nch
    tiles_e = (jnp.sum(nch, axis=0) + (CHUNKS_PER_TILE - 1)) // CHUNKS_PER_TILE
    span_e = tiles_e * CHUNKS_PER_TILE
    exp_start = jnp.cumsum(span_e) - span_e
    j = jnp.arange(n_chunks_max, dtype=jnp.int32)
    e_j = jnp.sum((exp_start[None, :] <= j[:, None]).astype(jnp.int32), axis=1) - 1
    onehot = (e_j[:, None] == jnp.arange(N_EXP, dtype=jnp.int32)[None, :]).astype(F32)
    rows_of = lambda tab: jnp.dot(onehot, tab.astype(F32), precision=lax.Precision.HIGHEST).astype(jnp.int32)
    local = j - rows_of(exp_start[:, None])[:, 0]
    blk_len = rows_of(nch.T)
    blk_start = rows_of(jnp.cumsum(nch.T, axis=1) - nch.T)
    inside = (blk_start <= local[:, None]) & (local[:, None] < blk_start + blk_len)
    real = jnp.any(inside, axis=1)
    chunk = jnp.sum(jnp.where(inside, rows_of(first.T) + (local[:, None] - blk_start), 0), axis=1)
    pad_rank = jnp.cumsum(jnp.where(real, 0, 1)) - 1
    src = jnp.where(real, chunk, BLOCK_CHUNKS - 1).astype(jnp.int32)
    dst = jnp.where(real, chunk, nblk * BLOCK_CHUNKS + pad_rank % BLOCK_CHUNKS).astype(jnp.int32)
    tile_end = jnp.cumsum(tiles_e)
    i = jnp.arange(n_tiles_max, dtype=jnp.int32)
    tile_expert = jnp.minimum(jnp.sum((tile_end[None, :] <= i[:, None]).astype(jnp.int32), axis=1), N_EXP - 1)
    return src, dst, tile_expert.astype(jnp.int32), tile_end[-1:].astype(jnp.int32)


def _expert_kernel(src_ref, dst_ref, texp_ref, nused_ref, xs_hbm, wg_ref, wu_ref, wd_ref, ys_hbm,
                   xbuf, ybuf, wgu_bf, wd_bf, gsem, ssem):
    i = pl.program_id(0)
    n_used = nused_ref[0]
    slot = lax.rem(i, 2)

    def chunk_copies(tile, slot_, to_buffer, do):
        for c in range(CHUNKS_PER_TILE):
            j = tile * CHUNKS_PER_TILE + c
            if to_buffer:
                cp = pltpu.make_async_copy(xs_hbm.at[src_ref[j]], xbuf.at[slot_, c], gsem.at[slot_])
            else:
                cp = pltpu.make_async_copy(ybuf.at[slot_, c], ys_hbm.at[dst_ref[j]], ssem.at[slot_])
            do(cp, c)

    start = lambda cp, c: cp.start(priority=c % 2)
    wait = lambda cp, c: cp.wait()

    @pl.when(i == 0)
    def _():
        chunk_copies(0, 0, True, start)

    @pl.when(i == n_used)
    def _():
        chunk_copies(i, slot, True, wait)

    @pl.when(i < n_used)
    def _():
        chunk_copies(i, slot, True, wait)
        chunk_copies(i + 1, 1 - slot, True, start)

        @pl.when((i == 0) | (texp_ref[i] != texp_ref[jnp.maximum(i - 1, 0)]))
        def _():
            wgu_bf[:, 0:E_DIM] = wg_ref[0, 0].astype(BF)
            wgu_bf[:, E_DIM:2 * E_DIM] = wu_ref[0, 0].astype(BF)
            wd_bf[...] = wd_ref[0, 0].astype(BF)

        gu = _dot(xbuf[slot].reshape(EXP_TILE, D), wgu_bf[...])
        hid = _silu_mul(gu[:, 0:E_DIM], gu[:, E_DIM:2 * E_DIM])
        ybuf[slot] = _dot(hid.astype(BF), wd_bf[...]).astype(BF).reshape(CHUNKS_PER_TILE, ROW_GRANULE, D)
        chunk_copies(i, slot, False, start)

        @pl.when(i >= 1)
        def _():
            chunk_copies(i - 1, 1 - slot, False, wait)

        @pl.when(i == n_used - 1)
        def _():
            chunk_copies(i, slot, False, wait)


def _experts(layer, xs, tables, w_eg, w_eu, w_ed, n_tiles_max):
    src, dst, tile_expert, n_used = tables
    chunks = xs.reshape(-1, ROW_GRANULE, D)
    wmap = lambda i, src_, dst_, texp, nu: (layer, texp[i], 0, 0)
    grid_spec = pltpu.PrefetchScalarGridSpec(
        num_scalar_prefetch=4,
        grid=(n_tiles_max + 1,),
        in_specs=[pl.BlockSpec(memory_space=pl.ANY),
                  pl.BlockSpec((1, 1, D, E_DIM), wmap),
                  pl.BlockSpec((1, 1, D, E_DIM), wmap),
                  pl.BlockSpec((1, 1, E_DIM, D), wmap)],
        out_specs=pl.BlockSpec(memory_space=pl.ANY),
        scratch_shapes=[pltpu.VMEM((2, CHUNKS_PER_TILE, ROW_GRANULE, D), BF),
                        pltpu.VMEM((2, CHUNKS_PER_TILE, ROW_GRANULE, D), BF),
                        pltpu.VMEM((D, 2 * E_DIM), BF), pltpu.VMEM((E_DIM, D), BF),
                        pltpu.SemaphoreType.DMA((2,)), pltpu.SemaphoreType.DMA((2,))],
    )
    return pl.pallas_call(
        _expert_kernel,
        grid_spec=grid_spec,
        out_shape=jax.ShapeDtypeStruct(chunks.shape, chunks.dtype),
        input_output_aliases={4: 0},
        compiler_params=_params("arbitrary"),
        name="moe_experts",
    )(src, dst, tile_expert, n_used, chunks, w_eg, w_eu, w_ed).reshape(xs.shape)


def _combine_kernel(final, ys_ref, pos_ref, gate_ref, h_ref, sg_ref, su_ref, sd_ref, x_ref, m_ref, gf_ref, o_ref):
    routed = jnp.zeros((DISP_BLOCK, D), F32)
    for r0 in range(0, DISP_ROWS, DISP_ROW_CHUNK):
        routed = routed + _dot_tn(_placement(pos_ref, gate_ref, r0), ys_ref[r0:r0 + DISP_ROW_CHUNK, :])
    h = h_ref[...]
    sh = _silu_mul(_dot(h, sg_ref[0].astype(BF)), _dot(h, su_ref[0].astype(BF)))
    out = x_ref[...] + m_ref[0] * (routed + _dot(sh.astype(BF), sd_ref[0].astype(BF)))
    if final:
        out = _rms(out, gf_ref[...])
    o_ref[...] = out


def _combine(layer, final, ys, pos8, gate8, h2, w_sg, w_su, w_sd, xm, mods, g_final, n_ctx_blocks, blocks_per_lat,
             first_block, n_blocks):
    tb = DISP_BLOCK
    row = functools.partial(_mod_row, n_ctx_blocks, blocks_per_lat)
    at = lambda b: b + first_block
    return pl.pallas_call(
        functools.partial(_combine_kernel, final),
        grid=(n_blocks,),
        in_specs=[pl.BlockSpec((DISP_ROWS, D), lambda b: (at(b), 0)),
                  pl.BlockSpec((TOP_K, tb), lambda b: (0, at(b))),
                  pl.BlockSpec((TOP_K, tb), lambda b: (0, at(b))),
                  pl.BlockSpec((tb, D), lambda b: (at(b), 0)),
                  pl.BlockSpec((1, D, E_DIM), lambda b: (layer, 0, 0)),
                  pl.BlockSpec((1, D, E_DIM), lambda b: (layer, 0, 0)),
                  pl.BlockSpec((1, E_DIM, D), lambda b: (layer, 0, 0)),
                  pl.BlockSpec((tb, D), lambda b: (at(b), 0)),
                  pl.BlockSpec((1, 1, D), lambda b: (row(at(b)), 0, 5)),
                  pl.BlockSpec((1, D), lambda b: (0, 0))],
        out_specs=pl.BlockSpec((tb, D), lambda b: (b, 0)),
        out_shape=jax.ShapeDtypeStruct((n_blocks * tb, D), F32),
        compiler_params=_params("parallel"),
        name="moe_combine",
    )(ys, pos8, gate8, h2, w_sg, w_su, w_sd, xm, mods, g_final)


def _moe(layer, final, h2, pos8, gate8, cnt, w_eg, w_eu, w_ed, w_sg, w_su, w_sd, xm, mods, g_final, disp_blocks):
    T = h2.shape[0]
    nblk = T // DISP_BLOCK
    n_chunks_max = (TOP_K * T + N_EXP * nblk * (ROW_GRANULE - 1)) // ROW_GRANULE + N_EXP * (CHUNKS_PER_TILE - 1)
    n_tiles_max = -(-n_chunks_max // CHUNKS_PER_TILE)
    xs = _dispatch(h2, pos8)
    tables = _tile_tables(cnt[:, :, 0].astype(jnp.int32), (n_tiles_max + 1) * CHUNKS_PER_TILE, n_tiles_max + 1)
    ys = _experts(layer, xs, tables, w_eg, w_eu, w_ed, n_tiles_max)
    comb = functools.partial(_combine, layer, final, ys, pos8, gate8, h2, w_sg, w_su, w_sd, xm, mods, g_final,
                             *disp_blocks)
    if not final:
        return comb(0, nblk)
    n_ctx = disp_blocks[0]
    return comb(0, n_ctx), comb(n_ctx, nblk - n_ctx)


def _reorder_w_in(w):
    kr = w[:, 2592:2624]
    kr_sw = kr.reshape(D, M_ROPE // 2, 2)[..., ::-1].reshape(kr.shape)
    pad = jnp.zeros(kr.shape, w.dtype)
    parts = [w[:, :1920], w[:, 1952:2592], w[:, 1920:1952], kr, kr_sw, pad, w[:, 2624:]]
    return jnp.concatenate([p.astype(BF) for p in parts], axis=1)


def _reorder_w_q(w_q_up, with_swap):
    L = w_q_up.shape[0]
    w = w_q_up.reshape(L, M_QL, MH, M_NOPE + M_ROPE)
    nope = w[..., :M_NOPE].reshape(L, M_QL, MH * M_NOPE)
    rope = w[..., M_NOPE:]
    parts = [nope, rope.reshape(L, M_QL, MH * M_ROPE)]
    if with_swap:
        parts.append(rope.reshape(L, M_QL, MH, M_ROPE // 2, 2)[..., ::-1].reshape(L, M_QL, MH * M_ROPE))
    return jnp.concatenate(parts, axis=2).astype(BF)


def _reorder_w_kv(w_kv_up):
    L = w_kv_up.shape[0]
    w = w_kv_up.reshape(L, M_KVL, MH, M_NOPE + M_V)
    return jnp.concatenate([w[..., :M_NOPE].reshape(L, M_KVL, MH * M_NOPE),
                            w[..., M_NOPE:].reshape(L, M_KVL, MH * M_V)], axis=2).astype(BF)


def kernel(x_prompt, x_sample, state_gla_fwd, state_gla_bwd, cache_mla_ckv, cache_mla_krope, c, c_ctx, w_mod, b_mod, g_norm1, g_norm2, w_in, w_gla_gate_f, b_gla_gate_f, w_gla_gate_b, b_gla_gate_b, g_gla_out, g_q_a, w_q_up, g_kv_a, w_kv_up, w_br_fourier, w_br_gla, w_br_mla, w_out, w_router, b_router, w_exp_gate, w_exp_up, w_exp_down, w_sh_gate, w_sh_up, w_sh_down, g_final):
    nb, sl, _ = x_prompt.shape
    db, dl, _ = x_sample.shape
    L = w_mod.shape[0]
    t_ctx, t_lat = nb * sl, db * dl
    T = t_ctx + t_lat
    assert sl % G_CHUNK == 0 and dl % G_CHUNK == 0 and dl % GRID_W == 0
    assert t_ctx % dl == 0 and dl % TOKEN_BLOCK == 0 and TOKEN_BLOCK % DISP_BLOCK == 0
    assert t_ctx % TOKEN_BLOCK == 0 and dl % Q_BLOCK == 0 and 1 + db <= 8

    x_pair = (x_prompt.reshape(t_ctx, D), x_sample.reshape(t_lat, D), 0)
    cond = jnp.concatenate([c_ctx[None, :], c, jnp.zeros((7 - db, D), F32)], axis=0)
    mods_all = _modulation(cond, w_mod, b_mod)

    w_in_r = [_reorder_w_in(w_in[l]) for l in range(L)]
    wq_ctx = _reorder_w_q(w_q_up, False)
    wq_lat = _reorder_w_q(w_q_up, True)
    wkv_r = _reorder_w_kv(w_kv_up)
    lat_off = t_ctx // dl

    new_f, new_b, new_ckv, new_kr = [], [], [], []
    for l in range(L):
        mods = mods_all[l].reshape(8, 1, 6 * D)
        tok = (t_ctx // TOKEN_BLOCK, dl // TOKEN_BLOCK)
        zf, zqk, zvr, zmla, zsm, zg = _in_projection(x_pair, T, mods, g_norm1[l][None, :], w_in_r[l], *tok)

        (f_c,) = _fourier(zf, sl, nb, 0)
        (f_l,) = _fourier(zf, dl, db, lat_off)

        gate_w = (w_gla_gate_f[l], b_gla_gate_f[l][None, :], w_gla_gate_b[l], b_gla_gate_b[l][None, :],
                  g_gla_out[l].reshape(1, GH * GDV))
        og_c, s_f, s_b = _gla(zqk, zvr, zsm, gate_w, sl, nb, 0, None)
        og_l, _, _ = _gla(zqk, zvr, zsm, gate_w, dl, db, lat_off, (state_gla_fwd[:, l], state_gla_bwd[:, l]))

        gq, gkv = g_q_a[l][None, :], g_kv_a[l][None, :]
        om_c, ckv = _mla(zmla, zsm, (gq, wq_ctx[l], gkv, wkv_r[l]), sl, nb, 0, None)
        (om_l,) = _mla(zmla, zsm, (gq, wq_lat[l], gkv, wkv_r[l]), dl, db, lat_off,
                       (cache_mla_ckv[:, l], cache_mla_krope[:, l]))

        mw = (w_br_fourier[l].astype(BF), w_br_gla[l].astype(BF), w_br_mla[l].astype(BF), w_out[l].astype(BF),
              g_norm2[l][None, :], w_router[l].T, b_router[l][:, None])
        xm, h2, pos8, gate8, cnt = _merge(x_pair, T, (f_c, og_c, om_c), (f_l, og_l, om_l), zg, mods, mw, *tok)

        x = _moe(l, l == L - 1, h2, pos8, gate8, cnt, w_exp_gate, w_exp_up, w_exp_down, w_sh_gate, w_sh_up, w_sh_down,
                 xm, mods, g_final[None, :], (t_ctx // DISP_BLOCK, dl // DISP_BLOCK))
        if l < L - 1:
            x_pair = (x, x, t_ctx // TOKEN_BLOCK)

        new_f.append(s_f)
        new_b.append(s_b)
        new_ckv.append(ckv.reshape(nb, sl, M_KVL))
        new_kr.append(zsm[:t_ctx, 32:64].reshape(nb, sl, M_ROPE))

    y_prompt = x[0].reshape(nb, sl, D)
    y_sample = x[1].reshape(db, dl, D)
    return (y_prompt, y_sample, jnp.stack(new_f, axis=1), jnp.stack(new_b, axis=1),
            jnp.stack(new_ckv, axis=1), jnp.stack(new_kr, axis=1))
```

```python
import functools

import numpy as np
import jax
import jax.numpy as jnp
from jax import lax
from jax.experimental import pallas as pl
from jax.experimental.pallas import tpu as pltpu

F32 = jnp.float32
BF = jnp.bfloat16

D = 1024
GRID_W = 64
FN_G, FN_GW = 4, 96
FN_W = FN_G * FN_GW
GH, GDK, GDV = 4, 64, 128
G_RANK = 16
G_TAU = 16.0
G_CHUNK = 64
MH, M_NOPE, M_ROPE, M_V = 8, 64, 32, 64
M_QL, M_KVL = 384, 256
ROPE_BASE = 10000.0
N_EXP, TOP_K, N_GRP, TOPK_GRP = 64, 8, 8, 4
E_DIM = 256
ROUTED_SCALE = 2.5
EPS = 1e-6

C_F = (0, 384)
C_QK = (384, 896)
C_VR = (896, 1920)
C_MLA = (1920, 2560)
C_SM = (2560, 2688)
C_G = (2688, 5760)
IN_COLS_R = 5760

VMEM_LIMIT_V7X = 56 * 1024 * 1024
TOKEN_BLOCK = 512
Q_BLOCK = 256
DISP_BLOCK = 256
ROW_GRANULE = 16
DISP_ROWS = -(-(TOP_K * DISP_BLOCK + N_EXP * (ROW_GRANULE - 1)) // 256) * 256
EXP_TILE = 512
CHUNKS_PER_TILE = EXP_TILE // ROW_GRANULE
BLOCK_CHUNKS = DISP_ROWS // ROW_GRANULE
DISP_ROW_CHUNK = 256
MAX_ONE_FACTOR_DECAY = 60.0


def _params(*sem):
    return pltpu.CompilerParams(dimension_semantics=sem, vmem_limit_bytes=VMEM_LIMIT_V7X)


def _dot(a, b):
    return jnp.dot(a, b, preferred_element_type=F32)


def _dot_nt(a, b, precision=None):
    return lax.dot_general(a, b, (((1,), (1,)), ((), ())), precision=precision, preferred_element_type=F32)


def _dot_tn(a, b):
    return lax.dot_general(a, b, (((0,), (0,)), ((), ())), preferred_element_type=F32)


def _sigmoid(x):
    return 1.0 / (1.0 + jnp.exp(-x))


def _rms(x, g):
    return x * lax.rsqrt(jnp.mean(x * x, axis=-1, keepdims=True) + EPS) * g


def _iota(shape, dim):
    return lax.broadcasted_iota(jnp.int32, shape, dim)


def _mod_row(n_ctx_blocks, blocks_per_lat, i):
    return jnp.where(i < n_ctx_blocks, 0, 1 + (i - n_ctx_blocks) // blocks_per_lat)


def _mod_kernel(c_ref, w_ref, b_ref, o_ref):
    c = c_ref[...]
    o_ref[0] = _dot_f32(c * _sigmoid(c), w_ref[0]) + b_ref[0]


def _modulation(cond, w_mod, b_mod):
    L = w_mod.shape[0]
    rows = cond.shape[0]
    tn = 1536
    return pl.pallas_call(
        _mod_kernel,
        grid=(L, 6 * D // tn),
        in_specs=[pl.BlockSpec((rows, D), lambda l, j: (0, 0)),
                  pl.BlockSpec((1, D, tn), lambda l, j: (l, 0, j)),
                  pl.BlockSpec((1, 1, tn), lambda l, j: (l, 0, j))],
        out_specs=pl.BlockSpec((1, rows, tn), lambda l, j: (l, 0, j)),
        out_shape=jax.ShapeDtypeStruct((L, rows, 6 * D), F32),
        compiler_params=_params("parallel", "parallel"),
        name="modulation",
    )(cond, w_mod, b_mod.reshape(L, 1, 6 * D))


def _inproj_kernel(n_ctx_blocks, xc_ref, xl_ref, m_ref, g_ref, w_ref, of_ref, oqk_ref, ovr_ref, omla_ref, osm_ref, og_ref):
    y = _rms(jnp.where(pl.program_id(0) < n_ctx_blocks, xc_ref[...], xl_ref[...]), g_ref[...])
    h = (y * (1.0 + m_ref[0, :, D:2 * D]) + m_ref[0, :, 0:D]).astype(BF)
    of_ref[...] = _dot(h, w_ref[:, C_F[0]:C_F[1]]).astype(BF)
    oqk_ref[...] = _dot(h, w_ref[:, C_QK[0]:C_QK[1]])
    ovr_ref[...] = _dot(h, w_ref[:, C_VR[0]:C_VR[1]]).astype(BF)
    omla_ref[...] = _dot(h, w_ref[:, C_MLA[0]:C_MLA[1]])
    osm_ref[...] = _dot(h, w_ref[:, C_SM[0]:C_SM[1]])
    og_ref[...] = _dot(h, w_ref[:, C_G[0]:C_G[1]]).astype(BF)


def _x_pair_specs(x_pair, tb, n_ctx_blocks):
    _, _, lat_off = x_pair
    return [pl.BlockSpec((tb, D), lambda i: (jnp.minimum(i, n_ctx_blocks - 1), 0)),
            pl.BlockSpec((tb, D), lambda i: (jnp.maximum(i - n_ctx_blocks, 0) + lat_off, 0))]


def _in_projection(x_pair, T, mods, g1, w_in_r, n_ctx_blocks, blocks_per_lat):
    tb = TOKEN_BLOCK
    row = functools.partial(_mod_row, n_ctx_blocks, blocks_per_lat)
    widths = [(C_F, BF), (C_QK, F32), (C_VR, BF), (C_MLA, F32), (C_SM, F32), (C_G, BF)]
    return pl.pallas_call(
        functools.partial(_inproj_kernel, n_ctx_blocks),
        grid=(T // tb,),
        in_specs=_x_pair_specs(x_pair, tb, n_ctx_blocks) + [
                  pl.BlockSpec((1, 1, 2 * D), lambda i: (row(i), 0, 0)),
                  pl.BlockSpec((1, D), lambda i: (0, 0)),
                  pl.BlockSpec((D, IN_COLS_R), lambda i: (0, 0))],
        out_specs=[pl.BlockSpec((tb, c[1] - c[0]), lambda i: (i, 0)) for c, _ in widths],
        out_shape=[jax.ShapeDtypeStruct((T, c[1] - c[0]), dt) for c, dt in widths],
        compiler_params=_params("parallel"),
        name="in_projection",
    )(x_pair[0], x_pair[1], mods, g1, w_in_r)


def _seq_call(kernel, name, n, nseq, blk_off, seq_ins, const_ins, out_widths, extra_outs=(), scratch=()):
    in_specs = [pl.BlockSpec((n, a.shape[1]), lambda i: (i + blk_off, 0)) for a in seq_ins]
    in_specs += [pl.BlockSpec(bs, im) for _, bs, im in const_ins]
    args = list(seq_ins) + [a for a, _, _ in const_ins]
    out_specs = [pl.BlockSpec((n, w), lambda i: (i, 0)) for w, _ in out_widths]
    out_shape = [jax.ShapeDtypeStruct((nseq * n, w), dt) for w, dt in out_widths]
    out_specs += [pl.BlockSpec(bs, im) for _, _, bs, im in extra_outs]
    out_shape += [jax.ShapeDtypeStruct(s, dt) for s, dt, _, _ in extra_outs]
    return pl.pallas_call(
        kernel, grid=(nseq,), in_specs=in_specs, out_specs=out_specs, out_shape=out_shape,
        scratch_shapes=list(scratch), compiler_params=_params("parallel"), name=name,
    )(*args)


def _fourier_kernel(u_ref, r_ref, lc_ref, ls_ref, o_ref):
    y = _dot(u_ref[...], r_ref[...])
    o_ref[...] = (_dot(lc_ref[...], y[:, :FN_W].astype(BF)) + _dot(ls_ref[...], y[:, FN_W:].astype(BF))).astype(BF)


def _dft_tables(n):
    k = np.arange(FN_GW)
    ang = 2.0 * np.pi * ((k[:, None] * k[None, :]) % FN_GW) / FN_GW
    eye = np.eye(FN_G)
    right = np.concatenate([np.kron(eye, np.cos(ang)), np.kron(eye, np.sin(ang))], axis=1)
    p = np.arange(n)
    angn = 2.0 * np.pi * ((p[:, None] * p[None, :]) % n) / n
    scale = 1.0 / np.sqrt(float(n * FN_GW))
    return (jnp.asarray(right, F32).astype(BF), jnp.asarray(np.cos(angn) * scale, F32).astype(BF),
            jnp.asarray(-np.sin(angn) * scale, F32).astype(BF))


def _fourier(zf, n, nseq, blk_off):
    right, lc, ls = _dft_tables(n)
    consts = [(right, (FN_W, 2 * FN_W), lambda i: (0, 0)), (lc, (n, n), lambda i: (0, 0)), (ls, (n, n), lambda i: (0, 0))]
    return _seq_call(_fourier_kernel, "fourier_mix", n, nseq, blk_off, [zf], consts, [(FN_W, BF)])


def _bf_parts(x, n):
    parts, rest = [], x
    for _ in range(n):
        p = rest.astype(BF)
        parts.append(p)
        rest = rest - p.astype(F32)
    return parts


def _dot_f32(a, b):
    a1, a2 = _bf_parts(a, 2)
    b1, b2 = _bf_parts(b, 2)
    return (_dot(a1, b2) + _dot(a2, b1)) + _dot(a1, b1)


def _cumulate(tri, g):
    g1, g2, g3 = _bf_parts(g, 3)
    return (_dot(tri, g3) + _dot(tri, g2)) + _dot(tri, g1)


def _log_gate(z, w_ref, b_ref):
    pre = _dot_f32(z, w_ref[...]) + b_ref[...]
    return (jnp.minimum(pre, 0.0) - jnp.log1p(jnp.exp(-jnp.abs(pre)))) * (1.0 / G_TAU)


def _gla_kernel(has_state, n, *refs):
    if has_state:
        (zqk, zvr, zsm, wgf, bgf, wgb, bgb, gout, s0f, s0b, o_ref, sf_ref, sb_ref,
         oacc_f, oacc_b, lg_f, lg_b, st_f, st_b) = refs
    else:
        (zqk, zvr, zsm, wgf, bgf, wgb, bgb, gout, o_ref, sf_ref, sb_ref,
         oacc_f, oacc_b, lg_f, lg_b, st_f, st_b) = refs
        s0f = s0b = None
    C = G_CHUNK
    nc = n // C
    ri, ci = _iota((C, C), 0), _iota((C, C), 1)
    t_idx, lane = _iota((C, 128), 0), _iota((C, 128), 1)
    s_idx = lane & (C - 1)
    left = lane < GDK
    vleft = _iota((C, 2 * GDV), 1) < GDV
    blockdiag = (_iota((2 * GDV, 2 * GDK), 0) >> 7) == (_iota((2 * GDV, 2 * GDK), 1) >> 6)
    row_w = _iota((C, GH * GDK), 0)

    def block_reference(cum, blk, off):
        if blk >= 8:
            return jnp.concatenate([jnp.broadcast_to(cum[j * blk + off:j * blk + off + 1], (blk, cum.shape[1]))
                                    for j in range(C // blk)], axis=0)
        out = cum
        for m in range(blk):
            if m != off:
                out = jnp.where((row_w & (blk - 1)) == m, pltpu.roll(cum, (m - off) % C, axis=0), out)
        return out

    def pair_blocks(x, ls):
        xp = x[:, ls]
        z = jnp.zeros_like(xp)
        return jnp.concatenate([jnp.where(left, xp, z), jnp.where(left, z, xp)], axis=0)

    def intra_scores(q, k, cum, reverse):
        levels = []
        blk = C
        while blk >= 2:
            half = blk // 2
            ref = block_reference(cum, blk, half if reverse else half - 1)
            d = cum - ref
            w = jnp.exp(jnp.minimum(d, -d))
            qs = (q * w).astype(BF)
            ks = (k * w).astype(BF)
            t_in, s_in = t_idx & (blk - 1), s_idx & (blk - 1)
            same = (t_idx & -blk) == (s_idx & -blk)
            split = ((s_in >= half) & (t_in < half)) if reverse else ((t_in >= half) & (s_in < half))
            levels.append((qs, ks, same & split))
            blk = half
        levels.append((q.astype(BF), k.astype(BF), t_idx == s_idx))
        out = []
        for p in range(2):
            ls = slice(128 * p, 128 * p + 128)
            sc = jnp.zeros((C, 128), F32)
            for qs, ks, m in levels:
                sc = jnp.where(m, _dot_nt(qs[:, ls], pair_blocks(ks, ls)), sc)
            out.append(sc.astype(BF))
        return out

    def load_state(st, s0_ref):
        for p in range(2):
            if s0_ref is None:
                st[p] = jnp.zeros((2 * GDV, 2 * GDK), F32)
            else:
                z = jnp.zeros((GDK, GDV), F32)
                blk = jnp.concatenate([jnp.concatenate([s0_ref[0, 2 * p], z], axis=1),
                                       jnp.concatenate([z, s0_ref[0, 2 * p + 1]], axis=1)], axis=0)
                st[p] = blk.T

    def store_state(st, out_ref):
        for p in range(2):
            blk = st[p].T
            out_ref[0, 2 * p] = blk[0:GDK, 0:GDV]
            out_ref[0, 2 * p + 1] = blk[GDK:2 * GDK, GDV:2 * GDV]

    def chunk_scores(qh, k, cum, reverse):
        mask = (s_idx >= t_idx) if reverse else (s_idx <= t_idx)
        kh = (k * jnp.exp(-cum)).astype(BF)
        return [jnp.where(mask, _dot_nt(qh[:, 128 * p:128 * p + 128], pair_blocks(kh, slice(128 * p, 128 * p + 128))),
                          0.0).astype(BF) for p in range(2)]

    def chunk(small_decay, c, reverse, lg, st, oacc):
        tri = (ci >= ri).astype(BF) if reverse else (ci <= ri).astype(BF)
        rows = pl.ds(pl.multiple_of(c * C, C), C)
        cum = _cumulate(tri, lg[rows, :])
        tot = cum[0:1] if reverse else cum[C - 1:C]
        q = zqk[rows, 0:GH * GDK] * (GDK ** -0.5)
        k = zqk[rows, GH * GDK:2 * GH * GDK]
        qh = (q * jnp.exp(cum)).astype(BF)
        kb = (k * jnp.exp(tot - cum)).astype(BF)
        dec = jnp.exp(tot)
        scores = chunk_scores(qh, k, cum, reverse) if small_decay else intra_scores(q, k, cum, reverse)
        for p in range(2):
            ls = slice(128 * p, 128 * p + 128)
            vs = slice(256 * p, 256 * p + 256)
            vp = zvr[rows, vs]
            zv = jnp.zeros_like(vp)
            vblk = jnp.concatenate([jnp.where(vleft, vp, zv), jnp.where(vleft, zv, vp)], axis=0)
            stp = st[p]
            oacc[rows, vs] = _dot(scores[p], vblk) + _dot_nt(qh[:, ls], stp.astype(BF))
            st[p] = dec[:, ls] * stp + jnp.where(blockdiag, _dot_tn(vp, kb[:, ls]), 0.0)

    def both_directions(small_decay, step, carry):
        chunk(small_decay, step, False, lg_f, st_f, oacc_f)
        chunk(small_decay, nc - 1 - step, True, lg_b, st_b, oacc_b)
        return carry

    gf = _log_gate(zsm[:, 0:G_RANK], wgf, bgf)
    gb = _log_gate(zsm[:, G_RANK:2 * G_RANK], wgb, bgb)
    lg_f[...] = gf
    lg_b[...] = gb
    load_state(st_f, s0f)
    load_state(st_b, s0b)
    small = jnp.minimum(jnp.min(gf), jnp.min(gb)) * C > -MAX_ONE_FACTOR_DECAY
    for small_decay in (True, False):
        @pl.when(small if small_decay else jnp.logical_not(small))
        def _():
            lax.fori_loop(0, nc, functools.partial(both_directions, small_decay), 0, unroll=min(nc, 4))
    store_state(st_f, sf_ref)
    store_state(st_b, sb_ref)

    rb = 128
    for r0 in range(0, n, rb):
        for h in range(GH):
            hs = slice(GDV * h, GDV * h + GDV)
            oh = oacc_f[r0:r0 + rb, hs] + oacc_b[r0:r0 + rb, hs]
            oh = oh * lax.rsqrt(jnp.mean(oh * oh, axis=-1, keepdims=True) + EPS) * gout[:, hs]
            r = zvr[r0:r0 + rb, GH * GDV + hs.start:GH * GDV + hs.stop].astype(F32)
            o_ref[r0:r0 + rb, hs] = (oh * (r * _sigmoid(r))).astype(BF)


def _gla(zqk, zvr, zsm, gate_w, n, nseq, blk_off, states):
    wgf, bgf, wgb, bgb, gout = gate_w
    c2 = lambda i: (0, 0)
    consts = [(wgf, wgf.shape, c2), (bgf, bgf.shape, c2), (wgb, wgb.shape, c2), (bgb, bgb.shape, c2), (gout, gout.shape, c2)]
    st_blk = (1, GH, GDK, GDV)
    st_map = lambda i: (i, 0, 0, 0)
    if states is not None:
        consts += [(s, st_blk, st_map) for s in states]
    extra = [((nseq, GH, GDK, GDV), F32, st_blk, st_map)] * 2
    scratch = ([pltpu.VMEM((n, GH * GDV), F32)] * 2 + [pltpu.VMEM((n, GH * GDK), F32)] * 2
               + [pltpu.VMEM((2, 2 * GDV, 2 * GDK), F32)] * 2)
    return _seq_call(functools.partial(_gla_kernel, states is not None, n), "gla_mixer", n, nseq, blk_off,
                     [zqk, zvr, zsm], consts, [(GH * GDV, BF)], extra_outs=extra, scratch=scratch)


def _mla_kernel(latent, n, past, *refs):
    if latent:
        (zmla, zsm, gq, wq, gkv, wkv, cckv, ckr, cosq, sinq, cosk, sink, o_ref, qs, kns, vs, krs) = refs
    else:
        (zmla, zsm, gq, wq, gkv, wkv, o_ref, ckv_ref, qs, kns, vs, krs) = refs
    sk = past + n
    scale = (M_NOPE + M_ROPE) ** -0.5
    nw, rw = MH * M_NOPE, MH * M_ROPE
    qa = _dot(_rms(zmla[:, 0:M_QL], gq[...]).astype(BF), wq[...])
    qr = qa[:, nw:nw + rw]
    if latent:
        qr = qr * cosq[...] + qa[:, nw + rw:nw + 2 * rw] * sinq[...]
    qs[:, 0:nw] = qa[:, 0:nw] * scale
    qs[:, nw:nw + rw] = qr * scale
    ckv = _rms(zmla[:, M_QL:M_QL + M_KVL], gkv[...])
    kv = _dot(ckv.astype(BF), wkv[...])
    kr = zsm[:, 32:64]
    if latent:
        kr = kr * cosk[...] + zsm[:, 64:96] * sink[...]
        kvc = _dot(cckv[0].astype(BF), wkv[...])
        kns[0:past, :] = kvc[:, 0:nw].astype(BF)
        vs[0:past, :] = kvc[:, nw:].astype(BF)
        krs[0:past, :] = jnp.concatenate([ckr[0]] * 4, axis=1).astype(BF)
    else:
        ckv_ref[...] = ckv
    kns[past:sk, :] = kv[:, 0:nw].astype(BF)
    vs[past:sk, :] = kv[:, nw:].astype(BF)
    krs[past:sk, :] = jnp.concatenate([kr] * 4, axis=1).astype(BF)

    qb = min(Q_BLOCK, n)
    lane = _iota((qb, 128), 1)

    def block(step, carry):
        rows = pl.ds(pl.multiple_of(step * qb, qb), qb)
        for p in range(MH // 2):
            ls = slice(128 * p, 128 * p + 128)
            qn = qs[rows, ls]
            quad = (2 * p) // 4
            qrp = qs[rows, nw + 128 * quad:nw + 128 * quad + 128]
            rhs = jnp.concatenate([kns[:, ls], krs[...]], axis=1)
            vp = vs[:, ls]
            o_pair = None
            for hh in range(2):
                j = (2 * p + hh) % 4
                qn_m = jnp.where((lane >> 6) == hh, qn, 0.0).astype(BF)
                qr_m = jnp.where((lane >> 5) == j, qrp, 0.0).astype(BF)
                s = _dot_nt(jnp.concatenate([qn_m, qr_m], axis=1), rhs)
                e = jnp.exp(s - jnp.max(s, axis=-1, keepdims=True))
                pv = _dot(e.astype(BF), vp) / jnp.sum(e, axis=-1, keepdims=True)
                o_pair = pv if hh == 0 else jnp.where(lane < M_V, o_pair, pv)
            o_ref[rows, ls] = o_pair.astype(BF)
        return carry

    lax.fori_loop(0, n // qb, block, 0)


def _rope_tables(n):
    half = M_ROPE // 2
    pos = jnp.arange(n)
    row = (pos // GRID_W).astype(F32)
    col = (pos % GRID_W).astype(F32)
    inv = ROPE_BASE ** (-jnp.arange(0, half, 2, dtype=F32) / half)
    ang = jnp.concatenate([row[:, None] * inv, col[:, None] * inv], axis=-1)
    cos = jnp.repeat(jnp.cos(ang), 2, axis=-1)
    sin = jnp.repeat(jnp.sin(ang), 2, axis=-1) * jnp.tile(jnp.asarray([-1.0, 1.0], F32), half)
    return jnp.tile(cos, (1, MH)), jnp.tile(sin, (1, MH)), cos, sin


def _mla(zmla, zsm, w, n, nseq, blk_off, cache):
    gq, wq, gkv, wkv = w
    c2 = lambda i: (0, 0)
    consts = [(gq, gq.shape, c2), (wq, wq.shape, c2), (gkv, gkv.shape, c2), (wkv, wkv.shape, c2)]
    past = 0
    extra = []
    if cache is not None:
        cckv, ckr = cache
        past = cckv.shape[1]
        c3 = lambda i: (i, 0, 0)
        consts += [(cckv, (1, past, M_KVL), c3), (ckr, (1, past, M_ROPE), c3)]
        consts += [(t, t.shape, c2) for t in _rope_tables(n)]
    else:
        extra = [((nseq * n, M_KVL), F32, (n, M_KVL), lambda i: (i, 0))]
    sk = past + n
    scratch = [pltpu.VMEM((n, MH * (M_NOPE + M_ROPE)), F32), pltpu.VMEM((sk, MH * M_NOPE), BF),
               pltpu.VMEM((sk, MH * M_V), BF), pltpu.VMEM((sk, 128), BF)]
    return _seq_call(functools.partial(_mla_kernel, cache is not None, n, past), "mla_mixer", n, nseq, blk_off,
                     [zmla, zsm], consts, [(MH * M_V, BF)], extra_outs=extra, scratch=scratch)


def _route(logits_t, bias):
    nt = logits_t.shape[1]
    gsz = N_EXP // N_GRP
    scores = _sigmoid(logits_t)
    sel = scores + bias
    neg = -jnp.inf
    sub = _iota((gsz, nt), 0)
    tops = []
    for g in range(N_GRP):
        blk = sel[gsz * g:gsz * g + gsz]
        m1 = jnp.max(blk, axis=0, keepdims=True)
        first = jnp.min(jnp.where(blk == m1, sub, gsz), axis=0, keepdims=True)
        m2 = jnp.max(jnp.where(sub == first, neg, blk), axis=0, keepdims=True)
        tops.append(m1 + m2)
    gs = jnp.concatenate(tops, axis=0)
    gidx = _iota((N_GRP, nt), 0)
    grank = jnp.zeros((N_GRP, nt), jnp.int32)
    for j in range(N_GRP):
        rj = gs[j:j + 1]
        grank += ((rj > gs) | ((rj == gs) & (gidx > j))).astype(jnp.int32)
    keep = grank < TOPK_GRP
    masked = jnp.concatenate(
        [jnp.where(jnp.broadcast_to(keep[g:g + 1], (gsz, nt)), sel[gsz * g:gsz * g + gsz], neg) for g in range(N_GRP)], axis=0)
    eidx = _iota((N_EXP, nt), 0)
    chosen = eidx < 0
    work = masked
    for _ in range(TOP_K):
        top = jnp.max(work, axis=0, keepdims=True)
        first = jnp.min(jnp.where(work == top, eidx, N_EXP), axis=0, keepdims=True)
        hit = eidx == first
        chosen = chosen | hit
        work = jnp.where(hit, neg, work)
    w = jnp.where(chosen, scores, 0.0)
    return chosen, w / jnp.sum(w, axis=0, keepdims=True) * ROUTED_SCALE


def _dispatch_meta(chosen, gates_t):
    tb = chosen.shape[1]
    sel = chosen.astype(F32)
    selb = sel.astype(BF)
    earlier = (_iota((tb, tb), 0) < _iota((tb, tb), 1)).astype(BF)
    rank = _dot(selb, earlier)
    cnt = jnp.sum(sel, axis=1, keepdims=True)
    padded = jnp.floor((cnt + (ROW_GRANULE - 1)) * (1.0 / ROW_GRANULE)) * ROW_GRANULE
    below = (_iota((N_EXP, N_EXP), 1) < _iota((N_EXP, N_EXP), 0)).astype(BF)
    start = _dot(below, jnp.broadcast_to(padded, (N_EXP, 128)).astype(BF))[:, 0:1]
    pos = start + rank
    kidx = _dot(below, selb)
    pos8, gate8 = [], []
    for k in range(TOP_K):
        hit = chosen & (kidx == float(k))
        pos8.append(jnp.sum(jnp.where(hit, pos, 0.0), axis=0, keepdims=True))
        gate8.append(jnp.sum(jnp.where(hit, gates_t, 0.0), axis=0, keepdims=True))
    return (jnp.concatenate(pos8, axis=0).astype(jnp.int32), jnp.concatenate(gate8, axis=0), cnt)


def _merge_kernel(n_ctx_blocks, xc_ref, xl_ref, fc_ref, fl_ref, ogc_ref, ogl_ref, omc_ref, oml_ref, zg_ref, m_ref,
                  wbf, wbg, wbm, wout, gn2, wrt, brt, xm_ref, h2_ref, pos_ref, gate_ref, cnt_ref):
    is_ctx = pl.program_id(0) < n_ctx_blocks
    ya = _dot(jnp.where(is_ctx, fc_ref[...], fl_ref[...]), wbf[...])
    yb = _dot(jnp.where(is_ctx, ogc_ref[...], ogl_ref[...]), wbg[...])
    yc = _dot(jnp.where(is_ctx, omc_ref[...], oml_ref[...]), wbm[...])
    merged = (_sigmoid(zg_ref[:, 0:D]) * ya.astype(BF) + _sigmoid(zg_ref[:, D:2 * D]) * yb.astype(BF)
              + _sigmoid(zg_ref[:, 2 * D:3 * D]) * yc.astype(BF))
    xm = jnp.where(is_ctx, xc_ref[...], xl_ref[...]) + m_ref[0, :, 2 * D:3 * D] * _dot(merged, wout[...])
    xm_ref[...] = xm
    h2 = _rms(xm, gn2[...]) * (1.0 + m_ref[0, :, 4 * D:5 * D]) + m_ref[0, :, 3 * D:4 * D]
    h2_ref[...] = h2.astype(BF)
    chosen, gates_t = _route(_dot_nt(wrt[...], h2, precision=lax.Precision.HIGHEST), brt[...])
    for sb in range(gates_t.shape[1] // DISP_BLOCK):
        ls = slice(sb * DISP_BLOCK, (sb + 1) * DISP_BLOCK)
        pos8, gate8, cnt = _dispatch_meta(chosen[:, ls], gates_t[:, ls])
        pos_ref[:, ls] = pos8
        gate_ref[:, ls] = gate8
        cnt_ref[sb] = jnp.broadcast_to(cnt, (N_EXP, 128))


def _merge(x_pair, T, mix_ctx, mix_lat, zg, mods, w, n_ctx_blocks, blocks_per_lat):
    tb = TOKEN_BLOCK
    row = functools.partial(_mod_row, n_ctx_blocks, blocks_per_lat)
    rb = lambda wd: pl.BlockSpec((tb, wd), lambda i: (i, 0))
    cb = lambda a: pl.BlockSpec(a.shape, lambda i: (0, 0))
    ctx_b = lambda wd: pl.BlockSpec((tb, wd), lambda i: (jnp.minimum(i, n_ctx_blocks - 1), 0))
    lat_b = lambda wd: pl.BlockSpec((tb, wd), lambda i: (jnp.maximum(i - n_ctx_blocks, 0), 0))
    mix_specs, mix_args = [], []
    for a_c, a_l in zip(mix_ctx, mix_lat):
        mix_specs += [ctx_b(a_c.shape[1]), lat_b(a_l.shape[1])]
        mix_args += [a_c, a_l]
    return pl.pallas_call(
        functools.partial(_merge_kernel, n_ctx_blocks),
        grid=(T // tb,),
        in_specs=_x_pair_specs(x_pair, tb, n_ctx_blocks) + mix_specs + [rb(3 * D),
                  pl.BlockSpec((1, 1, 6 * D), lambda i: (row(i), 0, 0))] + [cb(a) for a in w],
        out_specs=[rb(D), rb(D), pl.BlockSpec((TOP_K, tb), lambda i: (0, i)), pl.BlockSpec((TOP_K, tb), lambda i: (0, i)),
                   pl.BlockSpec((tb // DISP_BLOCK, N_EXP, 128), lambda i: (i, 0, 0))],
        out_shape=[jax.ShapeDtypeStruct((T, D), F32), jax.ShapeDtypeStruct((T, D), BF),
                   jax.ShapeDtypeStruct((TOP_K, T), jnp.int32), jax.ShapeDtypeStruct((TOP_K, T), F32),
                   jax.ShapeDtypeStruct((T // DISP_BLOCK, N_EXP, 128), F32)],
        compiler_params=_params("parallel"),
        name="merge_route",
    )(x_pair[0], x_pair[1], *mix_args, zg, mods, *w)


def _silu_mul(a, b):
    return a * _sigmoid(a) * b


def _placement(pos_ref, weight_ref, r0):
    rows = _iota((DISP_ROW_CHUNK, DISP_BLOCK), 0) + r0
    p = jnp.zeros((DISP_ROW_CHUNK, DISP_BLOCK), F32)
    for k in range(TOP_K):
        w = 1.0 if weight_ref is None else weight_ref[k:k + 1, :]
        p = jnp.where(rows == pos_ref[k:k + 1, :], w, p)
    return p.astype(BF)


def _dispatch_kernel(n_blocks, h_ref, pos_ref, xs_ref):
    h = jnp.where(pl.program_id(0) < n_blocks, h_ref[...], jnp.zeros_like(h_ref))
    for r0 in range(0, DISP_ROWS, DISP_ROW_CHUNK):
        xs_ref[r0:r0 + DISP_ROW_CHUNK, :] = _dot(_placement(pos_ref, None, r0), h).astype(BF)


def _dispatch(h2, pos8):
    T = h2.shape[0]
    nblk = T // DISP_BLOCK
    last = lambda b: jnp.minimum(b, nblk - 1)
    return pl.pallas_call(
        functools.partial(_dispatch_kernel, nblk),
        grid=(nblk + 1,),
        in_specs=[pl.BlockSpec((DISP_BLOCK, D), lambda b: (last(b), 0)),
                  pl.BlockSpec((TOP_K, DISP_BLOCK), lambda b: (0, last(b)))],
        out_specs=pl.BlockSpec((DISP_ROWS, D), lambda b: (b, 0)),
        out_shape=jax.ShapeDtypeStruct(((nblk + 1) * DISP_ROWS, D), BF),
        compiler_params=_params("parallel"),
        name="moe_dispatch",
    )(h2, pos8)


def _tile_tables(cnt, n_chunks_max, n_tiles_max):
    nblk = cnt.shape[0]
    nch = (cnt + (ROW_GRANULE - 1)) // ROW_GRANULE
    first = jnp.arange(nblk, dtype=jnp.int32)[:, None] * BLOCK_CHUNKS + jnp.cumsum(nch, axis=1) - nch
    tiles_e = (jnp.sum(nch, axis=0) + (CHUNKS_PER_TILE - 1)) // CHUNKS_PER_TILE
    span_e = tiles_e * CHUNKS_PER_TILE
    exp_start = jnp.cumsum(span_e) - span_e
    j = jnp.arange(n_chunks_max, dtype=jnp.int32)
    e_j = jnp.sum((exp_start[None, :] <= j[:, None]).astype(jnp.int32), axis=1) - 1
    onehot = (e_j[:, None] == jnp.arange(N_EXP, dtype=jnp.int32)[None, :]).astype(F32)
    rows_of = lambda tab: jnp.dot(onehot, tab.astype(F32), precision=lax.Precision.HIGHEST).astype(jnp.int32)
    local = j - rows_of(exp_start[:, None])[:, 0]
    blk_len = rows_of(nch.T)
    blk_start = rows_of(jnp.cumsum(nch.T, axis=1) - nch.T)
    inside = (blk_start <= local[:, None]) & (local[:, None] < blk_start + blk_len)
    real = jnp.any(inside, axis=1)
    chunk = jnp.sum(jnp.where(inside, rows_of(first.T) + (local[:, None] - blk_start), 0), axis=1)
    pad_rank = jnp.cumsum(jnp.where(real, 0, 1)) - 1
    src = jnp.where(real, chunk, BLOCK_CHUNKS - 1).astype(jnp.int32)
    dst = jnp.where(real, chunk, nblk * BLOCK_CHUNKS + pad_rank % BLOCK_CHUNKS).astype(jnp.int32)
    tile_end = jnp.cumsum(tiles_e)
    i = jnp.arange(n_tiles_max, dtype=jnp.int32)
    tile_expert = jnp.minimum(jnp.sum((tile_end[None, :] <= i[:, None]).astype(jnp.int32), axis=1), N_EXP - 1)
    return src, dst, tile_expert.astype(jnp.int32), tile_end[-1:].astype(jnp.int32)


def _expert_kernel(src_ref, dst_ref, texp_ref, nused_ref, xs_hbm, wg_ref, wu_ref, wd_ref, ys_hbm,
                   xbuf, ybuf, wgu_bf, wd_bf, gsem, ssem):
    i = pl.program_id(0)
    n_used = nused_ref[0]
    slot = lax.rem(i, 2)

    def chunk_copies(tile, slot_, to_buffer, do):
        for c in range(CHUNKS_PER_TILE):
            j = tile * CHUNKS_PER_TILE + c
            if to_buffer:
                cp = pltpu.make_async_copy(xs_hbm.at[src_ref[j]], xbuf.at[slot_, c], gsem.at[slot_])
            else:
                cp = pltpu.make_async_copy(ybuf.at[slot_, c], ys_hbm.at[dst_ref[j]], ssem.at[slot_])
            do(cp)

    start = lambda cp: cp.start()
    wait = lambda cp: cp.wait()

    @pl.when(i == 0)
    def _():
        chunk_copies(0, 0, True, start)

    @pl.when(i == n_used)
    def _():
        chunk_copies(i, slot, True, wait)

    @pl.when(i < n_used)
    def _():
        chunk_copies(i, slot, True, wait)
        chunk_copies(i + 1, 1 - slot, True, start)

        @pl.when((i == 0) | (texp_ref[i] != texp_ref[jnp.maximum(i - 1, 0)]))
        def _():
            wgu_bf[:, 0:E_DIM] = wg_ref[0, 0].astype(BF)
            wgu_bf[:, E_DIM:2 * E_DIM] = wu_ref[0, 0].astype(BF)
            wd_bf[...] = wd_ref[0, 0].astype(BF)

        gu = _dot(xbuf[slot].reshape(EXP_TILE, D), wgu_bf[...])
        hid = _silu_mul(gu[:, 0:E_DIM], gu[:, E_DIM:2 * E_DIM])
        ybuf[slot] = _dot(hid.astype(BF), wd_bf[...]).astype(BF).reshape(CHUNKS_PER_TILE, ROW_GRANULE, D)
        chunk_copies(i, slot, False, start)

        @pl.when(i >= 1)
        def _():
            chunk_copies(i - 1, 1 - slot, False, wait)

        @pl.when(i == n_used - 1)
        def _():
            chunk_copies(i, slot, False, wait)


def _experts(layer, xs, tables, w_eg, w_eu, w_ed, n_tiles_max):
    src, dst, tile_expert, n_used = tables
    chunks = xs.reshape(-1, ROW_GRANULE, D)
    wmap = lambda i, src_, dst_, texp, nu: (layer, texp[i], 0, 0)
    grid_spec = pltpu.PrefetchScalarGridSpec(
        num_scalar_prefetch=4,
        grid=(n_tiles_max + 1,),
        in_specs=[pl.BlockSpec(memory_space=pl.ANY),
                  pl.BlockSpec((1, 1, D, E_DIM), wmap),
                  pl.BlockSpec((1, 1, D, E_DIM), wmap),
                  pl.BlockSpec((1, 1, E_DIM, D), wmap)],
        out_specs=pl.BlockSpec(memory_space=pl.ANY),
        scratch_shapes=[pltpu.VMEM((2, CHUNKS_PER_TILE, ROW_GRANULE, D), BF),
                        pltpu.VMEM((2, CHUNKS_PER_TILE, ROW_GRANULE, D), BF),
                        pltpu.VMEM((D, 2 * E_DIM), BF), pltpu.VMEM((E_DIM, D), BF),
                        pltpu.SemaphoreType.DMA((2,)), pltpu.SemaphoreType.DMA((2,))],
    )
    return pl.pallas_call(
        _expert_kernel,
        grid_spec=grid_spec,
        out_shape=jax.ShapeDtypeStruct(chunks.shape, chunks.dtype),
        input_output_aliases={4: 0},
        compiler_params=_params("arbitrary"),
        name="moe_experts",
    )(src, dst, tile_expert, n_used, chunks, w_eg, w_eu, w_ed).reshape(xs.shape)


def _combine_kernel(final, ys_ref, pos_ref, gate_ref, h_ref, sg_ref, su_ref, sd_ref, x_ref, m_ref, gf_ref, o_ref):
    routed = jnp.zeros((DISP_BLOCK, D), F32)
    for r0 in range(0, DISP_ROWS, DISP_ROW_CHUNK):
        routed = routed + _dot_tn(_placement(pos_ref, gate_ref, r0), ys_ref[r0:r0 + DISP_ROW_CHUNK, :])
    h = h_ref[...]
    sh = _silu_mul(_dot(h, sg_ref[0].astype(BF)), _dot(h, su_ref[0].astype(BF)))
    out = x_ref[...] + m_ref[0] * (routed + _dot(sh.astype(BF), sd_ref[0].astype(BF)))
    if final:
        out = _rms(out, gf_ref[...])
    o_ref[...] = out


def _combine(layer, final, ys, pos8, gate8, h2, w_sg, w_su, w_sd, xm, mods, g_final, n_ctx_blocks, blocks_per_lat,
             first_block, n_blocks):
    tb = DISP_BLOCK
    row = functools.partial(_mod_row, n_ctx_blocks, blocks_per_lat)
    at = lambda b: b + first_block
    return pl.pallas_call(
        functools.partial(_combine_kernel, final),
        grid=(n_blocks,),
        in_specs=[pl.BlockSpec((DISP_ROWS, D), lambda b: (at(b), 0)),
                  pl.BlockSpec((TOP_K, tb), lambda b: (0, at(b))),
                  pl.BlockSpec((TOP_K, tb), lambda b: (0, at(b))),
                  pl.BlockSpec((tb, D), lambda b: (at(b), 0)),
                  pl.BlockSpec((1, D, E_DIM), lambda b: (layer, 0, 0)),
                  pl.BlockSpec((1, D, E_DIM), lambda b: (layer, 0, 0)),
                  pl.BlockSpec((1, E_DIM, D), lambda b: (layer, 0, 0)),
                  pl.BlockSpec((tb, D), lambda b: (at(b), 0)),
                  pl.BlockSpec((1, 1, D), lambda b: (row(at(b)), 0, 5)),
                  pl.BlockSpec((1, D), lambda b: (0, 0))],
        out_specs=pl.BlockSpec((tb, D), lambda b: (b, 0)),
        out_shape=jax.ShapeDtypeStruct((n_blocks * tb, D), F32),
        compiler_params=_params("parallel"),
        name="moe_combine",
    )(ys, pos8, gate8, h2, w_sg, w_su, w_sd, xm, mods, g_final)


def _moe(layer, final, h2, pos8, gate8, cnt, w_eg, w_eu, w_ed, w_sg, w_su, w_sd, xm, mods, g_final, disp_blocks):
    T = h2.shape[0]
    nblk = T // DISP_BLOCK
    n_chunks_max = (TOP_K * T + N_EXP * nblk * (ROW_GRANULE - 1)) // ROW_GRANULE + N_EXP * (CHUNKS_PER_TILE - 1)
    n_tiles_max = -(-n_chunks_max // CHUNKS_PER_TILE)
    xs = _dispatch(h2, pos8)
    tables = _tile_tables(cnt[:, :, 0].astype(jnp.int32), (n_tiles_max + 1) * CHUNKS_PER_TILE, n_tiles_max + 1)
    ys = _experts(layer, xs, tables, w_eg, w_eu, w_ed, n_tiles_max)
    comb = functools.partial(_combine, layer, final, ys, pos8, gate8, h2, w_sg, w_su, w_sd, xm, mods, g_final,
                             *disp_blocks)
    if not final:
        return comb(0, nblk)
    n_ctx = disp_blocks[0]
    return comb(0, n_ctx), comb(n_ctx, nblk - n_ctx)


def _reorder_w_in(w):
    kr = w[:, 2592:2624]
    kr_sw = kr.reshape(D, M_ROPE // 2, 2)[..., ::-1].reshape(kr.shape)
    pad = jnp.zeros(kr.shape, w.dtype)
    parts = [w[:, :1920], w[:, 1952:2592], w[:, 1920:1952], kr, kr_sw, pad, w[:, 2624:]]
    return jnp.concatenate([p.astype(BF) for p in parts], axis=1)


def _reorder_w_q(w_q_up, with_swap):
    L = w_q_up.shape[0]
    w = w_q_up.reshape(L, M_QL, MH, M_NOPE + M_ROPE)
    nope = w[..., :M_NOPE].reshape(L, M_QL, MH * M_NOPE)
    rope = w[..., M_NOPE:]
    parts = [nope, rope.reshape(L, M_QL, MH * M_ROPE)]
    if with_swap:
        parts.append(rope.reshape(L, M_QL, MH, M_ROPE // 2, 2)[..., ::-1].reshape(L, M_QL, MH * M_ROPE))
    return jnp.concatenate(parts, axis=2).astype(BF)


def _reorder_w_kv(w_kv_up):
    L = w_kv_up.shape[0]
    w = w_kv_up.reshape(L, M_KVL, MH, M_NOPE + M_V)
    return jnp.concatenate([w[..., :M_NOPE].reshape(L, M_KVL, MH * M_NOPE),
                            w[..., M_NOPE:].reshape(L, M_KVL, MH * M_V)], axis=2).astype(BF)


def kernel(x_prompt, x_sample, state_gla_fwd, state_gla_bwd, cache_mla_ckv, cache_mla_krope, c, c_ctx, w_mod, b_mod, g_norm1, g_norm2, w_in, w_gla_gate_f, b_gla_gate_f, w_gla_gate_b, b_gla_gate_b, g_gla_out, g_q_a, w_q_up, g_kv_a, w_kv_up, w_br_fourier, w_br_gla, w_br_mla, w_out, w_router, b_router, w_exp_gate, w_exp_up, w_exp_down, w_sh_gate, w_sh_up, w_sh_down, g_final):
    nb, sl, _ = x_prompt.shape
    db, dl, _ = x_sample.shape
    L = w_mod.shape[0]
    t_ctx, t_lat = nb * sl, db * dl
    T = t_ctx + t_lat
    assert sl % G_CHUNK == 0 and dl % G_CHUNK == 0 and dl % GRID_W == 0
    assert t_ctx % dl == 0 and dl % TOKEN_BLOCK == 0 and TOKEN_BLOCK % DISP_BLOCK == 0
    assert t_ctx % TOKEN_BLOCK == 0 and dl % Q_BLOCK == 0 and 1 + db <= 8

    x_pair = (x_prompt.reshape(t_ctx, D), x_sample.reshape(t_lat, D), 0)
    cond = jnp.concatenate([c_ctx[None, :], c, jnp.zeros((7 - db, D), F32)], axis=0)
    mods_all = _modulation(cond, w_mod, b_mod)

    w_in_r = [_reorder_w_in(w_in[l]) for l in range(L)]
    wq_ctx = _reorder_w_q(w_q_up, False)
    wq_lat = _reorder_w_q(w_q_up, True)
    wkv_r = _reorder_w_kv(w_kv_up)
    lat_off = t_ctx // dl

    new_f, new_b, new_ckv, new_kr = [], [], [], []
    for l in range(L):
        mods = mods_all[l].reshape(8, 1, 6 * D)
        tok = (t_ctx // TOKEN_BLOCK, dl // TOKEN_BLOCK)
        zf, zqk, zvr, zmla, zsm, zg = _in_projection(x_pair, T, mods, g_norm1[l][None, :], w_in_r[l], *tok)

        (f_c,) = _fourier(zf, sl, nb, 0)
        (f_l,) = _fourier(zf, dl, db, lat_off)

        gate_w = (w_gla_gate_f[l], b_gla_gate_f[l][None, :], w_gla_gate_b[l], b_gla_gate_b[l][None, :],
                  g_gla_out[l].reshape(1, GH * GDV))
        og_c, s_f, s_b = _gla(zqk, zvr, zsm, gate_w, sl, nb, 0, None)
        og_l, _, _ = _gla(zqk, zvr, zsm, gate_w, dl, db, lat_off, (state_gla_fwd[:, l], state_gla_bwd[:, l]))

        gq, gkv = g_q_a[l][None, :], g_kv_a[l][None, :]
        om_c, ckv = _mla(zmla, zsm, (gq, wq_ctx[l], gkv, wkv_r[l]), sl, nb, 0, None)
        (om_l,) = _mla(zmla, zsm, (gq, wq_lat[l], gkv, wkv_r[l]), dl, db, lat_off,
                       (cache_mla_ckv[:, l], cache_mla_krope[:, l]))

        mw = (w_br_fourier[l].astype(BF), w_br_gla[l].astype(BF), w_br_mla[l].astype(BF), w_out[l].astype(BF),
              g_norm2[l][None, :], w_router[l].T, b_router[l][:, None])
        xm, h2, pos8, gate8, cnt = _merge(x_pair, T, (f_c, og_c, om_c), (f_l, og_l, om_l), zg, mods, mw, *tok)

        x = _moe(l, l == L - 1, h2, pos8, gate8, cnt, w_exp_gate, w_exp_up, w_exp_down, w_sh_gate, w_sh_up, w_sh_down,
                 xm, mods, g_final[None, :], (t_ctx // DISP_BLOCK, dl // DISP_BLOCK))
        if l < L - 1:
            x_pair = (x, x, t_ctx // TOKEN_BLOCK)

        new_f.append(s_f)
        new_b.append(s_b)
        new_ckv.append(ckv.reshape(nb, sl, M_KVL))
        new_kr.append(zsm[:t_ctx, 32:64].reshape(nb, sl, M_ROPE))

    y_prompt = x[0].reshape(nb, sl, D)
    y_sample = x[1].reshape(db, dl, D)
    return (y_prompt, y_sample, jnp.stack(new_f, axis=1), jnp.stack(new_b, axis=1),
            jnp.stack(new_ckv, axis=1), jnp.stack(new_kr, axis=1))
```

```python
import functools

import numpy as np
import jax
import jax.numpy as jnp
from jax import lax
from jax.experimental import pallas as pl
from jax.experimental.pallas import tpu as pltpu

F32 = jnp.float32
BF = jnp.bfloat16

D = 1024
GRID_W = 64
FN_G, FN_GW = 4, 96
FN_W = FN_G * FN_GW
GH, GDK, GDV = 4, 64, 128
G_RANK = 16
G_TAU = 16.0
G_CHUNK = 64
MH, M_NOPE, M_ROPE, M_V = 8, 64, 32, 64
M_QL, M_KVL = 384, 256
ROPE_BASE = 10000.0
N_EXP, TOP_K, N_GRP, TOPK_GRP = 64, 8, 8, 4
E_DIM = 256
ROUTED_SCALE = 2.5
EPS = 1e-6

C_F = (0, 384)
C_QK = (384, 896)
C_VR = (896, 1920)
C_MLA = (1920, 2560)
C_SM = (2560, 2688)
C_G = (2688, 5760)
IN_COLS_R = 5760

VMEM_LIMIT_V7X = 56 * 1024 * 1024
TOKEN_BLOCK = 512
Q_BLOCK = 256
DISP_BLOCK = 256
ROW_GRANULE = 16
DISP_ROWS = -(-(TOP_K * DISP_BLOCK + N_EXP * (ROW_GRANULE - 1)) // 256) * 256
EXP_TILE = 512
CHUNKS_PER_TILE = EXP_TILE // ROW_GRANULE
BLOCK_CHUNKS = DISP_ROWS // ROW_GRANULE
DISP_ROW_CHUNK = 256
MAX_ONE_FACTOR_DECAY = 60.0


def _params(*sem):
    return pltpu.CompilerParams(dimension_semantics=sem, vmem_limit_bytes=VMEM_LIMIT_V7X)


def _dot(a, b):
    return jnp.dot(a, b, preferred_element_type=F32)


def _dot_nt(a, b, precision=None):
    return lax.dot_general(a, b, (((1,), (1,)), ((), ())), precision=precision, preferred_element_type=F32)


def _dot_tn(a, b):
    return lax.dot_general(a, b, (((0,), (0,)), ((), ())), preferred_element_type=F32)


def _sigmoid(x):
    return 1.0 / (1.0 + jnp.exp(-x))


def _rms(x, g):
    return x * lax.rsqrt(jnp.mean(x * x, axis=-1, keepdims=True) + EPS) * g


def _iota(shape, dim):
    return lax.broadcasted_iota(jnp.int32, shape, dim)


def _mod_row(n_ctx_blocks, blocks_per_lat, i):
    return jnp.where(i < n_ctx_blocks, 0, 1 + (i - n_ctx_blocks) // blocks_per_lat)


def _mod_kernel(c_ref, w_ref, b_ref, o_ref):
    c = c_ref[...]
    o_ref[0] = _dot_f32(c * _sigmoid(c), w_ref[0]) + b_ref[0]


def _modulation(cond, w_mod, b_mod):
    L = w_mod.shape[0]
    rows = cond.shape[0]
    tn = 1536
    return pl.pallas_call(
        _mod_kernel,
        grid=(L, 6 * D // tn),
        in_specs=[pl.BlockSpec((rows, D), lambda l, j: (0, 0)),
                  pl.BlockSpec((1, D, tn), lambda l, j: (l, 0, j)),
                  pl.BlockSpec((1, 1, tn), lambda l, j: (l, 0, j))],
        out_specs=pl.BlockSpec((1, rows, tn), lambda l, j: (l, 0, j)),
        out_shape=jax.ShapeDtypeStruct((L, rows, 6 * D), F32),
        compiler_params=_params("parallel", "parallel"),
        name="modulation",
    )(cond, w_mod, b_mod.reshape(L, 1, 6 * D))


def _inproj_kernel(n_ctx_blocks, xc_ref, xl_ref, m_ref, g_ref, w_ref, of_ref, oqk_ref, ovr_ref, omla_ref, osm_ref, og_ref):
    y = _rms(jnp.where(pl.program_id(0) < n_ctx_blocks, xc_ref[...], xl_ref[...]), g_ref[...])
    h = (y * (1.0 + m_ref[0, :, D:2 * D]) + m_ref[0, :, 0:D]).astype(BF)
    of_ref[...] = _dot(h, w_ref[:, C_F[0]:C_F[1]]).astype(BF)
    oqk_ref[...] = _dot(h, w_ref[:, C_QK[0]:C_QK[1]])
    ovr_ref[...] = _dot(h, w_ref[:, C_VR[0]:C_VR[1]]).astype(BF)
    omla_ref[...] = _dot(h, w_ref[:, C_MLA[0]:C_MLA[1]])
    osm_ref[...] = _dot(h, w_ref[:, C_SM[0]:C_SM[1]])
    og_ref[...] = _dot(h, w_ref[:, C_G[0]:C_G[1]]).astype(BF)


def _x_pair_specs(x_pair, tb, n_ctx_blocks):
    _, _, lat_off = x_pair
    return [pl.BlockSpec((tb, D), lambda i: (jnp.minimum(i, n_ctx_blocks - 1), 0)),
            pl.BlockSpec((tb, D), lambda i: (jnp.maximum(i - n_ctx_blocks, 0) + lat_off, 0))]


def _in_projection(x_pair, T, mods, g1, w_in_r, n_ctx_blocks, blocks_per_lat):
    tb = TOKEN_BLOCK
    row = functools.partial(_mod_row, n_ctx_blocks, blocks_per_lat)
    widths = [(C_F, BF), (C_QK, F32), (C_VR, BF), (C_MLA, F32), (C_SM, F32), (C_G, BF)]
    return pl.pallas_call(
        functools.partial(_inproj_kernel, n_ctx_blocks),
        grid=(T // tb,),
        in_specs=_x_pair_specs(x_pair, tb, n_ctx_blocks) + [
                  pl.BlockSpec((1, 1, 2 * D), lambda i: (row(i), 0, 0)),
                  pl.BlockSpec((1, D), lambda i: (0, 0)),
                  pl.BlockSpec((D, IN_COLS_R), lambda i: (0, 0))],
        out_specs=[pl.BlockSpec((tb, c[1] - c[0]), lambda i: (i, 0)) for c, _ in widths],
        out_shape=[jax.ShapeDtypeStruct((T, c[1] - c[0]), dt) for c, dt in widths],
        compiler_params=_params("parallel"),
        name="in_projection",
    )(x_pair[0], x_pair[1], mods, g1, w_in_r)


def _seq_call(kernel, name, n, nseq, blk_off, seq_ins, const_ins, out_widths, extra_outs=(), scratch=()):
    in_specs = [pl.BlockSpec((n, a.shape[1]), lambda i: (i + blk_off, 0)) for a in seq_ins]
    in_specs += [pl.BlockSpec(bs, im) for _, bs, im in const_ins]
    args = list(seq_ins) + [a for a, _, _ in const_ins]
    out_specs = [pl.BlockSpec((n, w), lambda i: (i, 0)) for w, _ in out_widths]
    out_shape = [jax.ShapeDtypeStruct((nseq * n, w), dt) for w, dt in out_widths]
    out_specs += [pl.BlockSpec(bs, im) for _, _, bs, im in extra_outs]
    out_shape += [jax.ShapeDtypeStruct(s, dt) for s, dt, _, _ in extra_outs]
    return pl.pallas_call(
        kernel, grid=(nseq,), in_specs=in_specs, out_specs=out_specs, out_shape=out_shape,
        scratch_shapes=list(scratch), compiler_params=_params("parallel"), name=name,
    )(*args)


def _fourier_kernel(u_ref, r_ref, lc_ref, ls_ref, o_ref):
    y = _dot(u_ref[...], r_ref[...])
    o_ref[...] = (_dot(lc_ref[...], y[:, :FN_W].astype(BF)) + _dot(ls_ref[...], y[:, FN_W:].astype(BF))).astype(BF)


def _dft_tables(n):
    k = np.arange(FN_GW)
    ang = 2.0 * np.pi * ((k[:, None] * k[None, :]) % FN_GW) / FN_GW
    eye = np.eye(FN_G)
    right = np.concatenate([np.kron(eye, np.cos(ang)), np.kron(eye, np.sin(ang))], axis=1)
    p = np.arange(n)
    angn = 2.0 * np.pi * ((p[:, None] * p[None, :]) % n) / n
    scale = 1.0 / np.sqrt(float(n * FN_GW))
    return (jnp.asarray(right, F32).astype(BF), jnp.asarray(np.cos(angn) * scale, F32).astype(BF),
            jnp.asarray(-np.sin(angn) * scale, F32).astype(BF))


def _fourier(zf, n, nseq, blk_off):
    right, lc, ls = _dft_tables(n)
    consts = [(right, (FN_W, 2 * FN_W), lambda i: (0, 0)), (lc, (n, n), lambda i: (0, 0)), (ls, (n, n), lambda i: (0, 0))]
    return _seq_call(_fourier_kernel, "fourier_mix", n, nseq, blk_off, [zf], consts, [(FN_W, BF)])


def _bf_parts(x, n):
    parts, rest = [], x
    for _ in range(n):
        p = rest.astype(BF)
        parts.append(p)
        rest = rest - p.astype(F32)
    return parts


def _dot_f32(a, b):
    a1, a2 = _bf_parts(a, 2)
    b1, b2 = _bf_parts(b, 2)
    return (_dot(a1, b2) + _dot(a2, b1)) + _dot(a1, b1)


def _cumulate(tri, g):
    g1, g2, g3 = _bf_parts(g, 3)
    return (_dot(tri, g3) + _dot(tri, g2)) + _dot(tri, g1)


def _log_gate(z, w_ref, b_ref):
    pre = _dot_f32(z, w_ref[...]) + b_ref[...]
    return (jnp.minimum(pre, 0.0) - jnp.log1p(jnp.exp(-jnp.abs(pre)))) * (1.0 / G_TAU)


def _gla_kernel(has_state, n, *refs):
    if has_state:
        (zqk, zvr, zsm, wgf, bgf, wgb, bgb, gout, s0f, s0b, o_ref, sf_ref, sb_ref,
         oacc_f, oacc_b, lg_f, lg_b, st_f, st_b) = refs
    else:
        (zqk, zvr, zsm, wgf, bgf, wgb, bgb, gout, o_ref, sf_ref, sb_ref,
         oacc_f, oacc_b, lg_f, lg_b, st_f, st_b) = refs
        s0f = s0b = None
    C = G_CHUNK
    nc = n // C
    ri, ci = _iota((C, C), 0), _iota((C, C), 1)
    t_idx, lane = _iota((C, 128), 0), _iota((C, 128), 1)
    s_idx = lane & (C - 1)
    left = lane < GDK
    vleft = _iota((C, 2 * GDV), 1) < GDV
    blockdiag = (_iota((2 * GDV, 2 * GDK), 0) >> 7) == (_iota((2 * GDV, 2 * GDK), 1) >> 6)
    row_w = _iota((C, GH * GDK), 0)

    def block_reference(cum, blk, off):
        if blk >= 8:
            return jnp.concatenate([jnp.broadcast_to(cum[j * blk + off:j * blk + off + 1], (blk, cum.shape[1]))
                                    for j in range(C // blk)], axis=0)
        out = cum
        for m in range(blk):
            if m != off:
                out = jnp.where((row_w & (blk - 1)) == m, pltpu.roll(cum, (m - off) % C, axis=0), out)
        return out

    def pair_blocks(x, ls):
        xp = x[:, ls]
        z = jnp.zeros_like(xp)
        return jnp.concatenate([jnp.where(left, xp, z), jnp.where(left, z, xp)], axis=0)

    def intra_scores(q, k, cum, reverse):
        levels = []
        blk = C
        while blk >= 2:
            half = blk // 2
            ref = block_reference(cum, blk, half if reverse else half - 1)
            d = cum - ref
            w = jnp.exp(jnp.minimum(d, -d))
            qs = (q * w).astype(BF)
            ks = (k * w).astype(BF)
            t_in, s_in = t_idx & (blk - 1), s_idx & (blk - 1)
            same = (t_idx & -blk) == (s_idx & -blk)
            split = ((s_in >= half) & (t_in < half)) if reverse else ((t_in >= half) & (s_in < half))
            levels.append((qs, ks, same & split))
            blk = half
        levels.append((q.astype(BF), k.astype(BF), t_idx == s_idx))
        out = []
        for p in range(2):
            ls = slice(128 * p, 128 * p + 128)
            sc = jnp.zeros((C, 128), F32)
            for qs, ks, m in levels:
                sc = jnp.where(m, _dot_nt(qs[:, ls], pair_blocks(ks, ls)), sc)
            out.append(sc.astype(BF))
        return out

    def load_state(st, s0_ref):
        for p in range(2):
            if s0_ref is None:
                st[p] = jnp.zeros((2 * GDV, 2 * GDK), F32)
            else:
                z = jnp.zeros((GDK, GDV), F32)
                blk = jnp.concatenate([jnp.concatenate([s0_ref[0, 2 * p], z], axis=1),
                                       jnp.concatenate([z, s0_ref[0, 2 * p + 1]], axis=1)], axis=0)
                st[p] = blk.T

    def store_state(st, out_ref):
        for p in range(2):
            blk = st[p].T
            out_ref[0, 2 * p] = blk[0:GDK, 0:GDV]
            out_ref[0, 2 * p + 1] = blk[GDK:2 * GDK, GDV:2 * GDV]

    def chunk_scores(qh, k, cum, reverse):
        mask = (s_idx >= t_idx) if reverse else (s_idx <= t_idx)
        kh = (k * jnp.exp(-cum)).astype(BF)
        return [jnp.where(mask, _dot_nt(qh[:, 128 * p:128 * p + 128], pair_blocks(kh, slice(128 * p, 128 * p + 128))),
                          0.0).astype(BF) for p in range(2)]

    def chunk(small_decay, c, reverse, lg, st, oacc):
        tri = (ci >= ri).astype(BF) if reverse else (ci <= ri).astype(BF)
        rows = pl.ds(pl.multiple_of(c * C, C), C)
        cum = _cumulate(tri, lg[rows, :])
        tot = cum[0:1] if reverse else cum[C - 1:C]
        q = zqk[rows, 0:GH * GDK] * (GDK ** -0.5)
        k = zqk[rows, GH * GDK:2 * GH * GDK]
        qh = (q * jnp.exp(cum)).astype(BF)
        kb = (k * jnp.exp(tot - cum)).astype(BF)
        dec = jnp.exp(tot)
        scores = chunk_scores(qh, k, cum, reverse) if small_decay else intra_scores(q, k, cum, reverse)
        for p in range(2):
            ls = slice(128 * p, 128 * p + 128)
            vs = slice(256 * p, 256 * p + 256)
            vp = zvr[rows, vs]
            zv = jnp.zeros_like(vp)
            vblk = jnp.concatenate([jnp.where(vleft, vp, zv), jnp.where(vleft, zv, vp)], axis=0)
            stp = st[p]
            oacc[rows, vs] = _dot(scores[p], vblk) + _dot_nt(qh[:, ls], stp.astype(BF))
            st[p] = dec[:, ls] * stp + jnp.where(blockdiag, _dot_tn(vp, kb[:, ls]), 0.0)

    def both_directions(small_decay, step, carry):
        chunk(small_decay, step, False, lg_f, st_f, oacc_f)
        chunk(small_decay, nc - 1 - step, True, lg_b, st_b, oacc_b)
        return carry

    gf = _log_gate(zsm[:, 0:G_RANK], wgf, bgf)
    gb = _log_gate(zsm[:, G_RANK:2 * G_RANK], wgb, bgb)
    lg_f[...] = gf
    lg_b[...] = gb

    def mix(small_decay):
        load_state(st_f, s0f)
        load_state(st_b, s0b)
        lax.fori_loop(0, nc, functools.partial(both_directions, small_decay), 0, unroll=min(nc, 4))
        store_state(st_f, sf_ref)
        store_state(st_b, sb_ref)
        rb = 128
        for r0 in range(0, n, rb):
            for h in range(GH):
                hs = slice(GDV * h, GDV * h + GDV)
                oh = oacc_f[r0:r0 + rb, hs] + oacc_b[r0:r0 + rb, hs]
                oh = oh * lax.rsqrt(jnp.mean(oh * oh, axis=-1, keepdims=True) + EPS) * gout[:, hs]
                r = zvr[r0:r0 + rb, GH * GDV + hs.start:GH * GDV + hs.stop].astype(F32)
                o_ref[r0:r0 + rb, hs] = (oh * (r * _sigmoid(r))).astype(BF)

    mix(True)

    @pl.when(jnp.minimum(jnp.min(gf), jnp.min(gb)) * C <= -MAX_ONE_FACTOR_DECAY)
    def _():
        mix(False)


def _gla(zqk, zvr, zsm, gate_w, n, nseq, blk_off, states):
    wgf, bgf, wgb, bgb, gout = gate_w
    c2 = lambda i: (0, 0)
    consts = [(wgf, wgf.shape, c2), (bgf, bgf.shape, c2), (wgb, wgb.shape, c2), (bgb, bgb.shape, c2), (gout, gout.shape, c2)]
    st_blk = (1, GH, GDK, GDV)
    st_map = lambda i: (i, 0, 0, 0)
    if states is not None:
        consts += [(s, st_blk, st_map) for s in states]
    extra = [((nseq, GH, GDK, GDV), F32, st_blk, st_map)] * 2
    scratch = ([pltpu.VMEM((n, GH * GDV), F32)] * 2 + [pltpu.VMEM((n, GH * GDK), F32)] * 2
               + [pltpu.VMEM((2, 2 * GDV, 2 * GDK), F32)] * 2)
    return _seq_call(functools.partial(_gla_kernel, states is not None, n), "gla_mixer", n, nseq, blk_off,
                     [zqk, zvr, zsm], consts, [(GH * GDV, BF)], extra_outs=extra, scratch=scratch)


def _mla_kernel(latent, n, past, *refs):
    if latent:
        (zmla, zsm, gq, wq, gkv, wkv, cckv, ckr, cosq, sinq, cosk, sink, o_ref, qs, kns, vs, krs) = refs
    else:
        (zmla, zsm, gq, wq, gkv, wkv, o_ref, ckv_ref, qs, kns, vs, krs) = refs
    sk = past + n
    scale = (M_NOPE + M_ROPE) ** -0.5
    nw, rw = MH * M_NOPE, MH * M_ROPE
    qa = _dot(_rms(zmla[:, 0:M_QL], gq[...]).astype(BF), wq[...])
    qr = qa[:, nw:nw + rw]
    if latent:
        qr = qr * cosq[...] + qa[:, nw + rw:nw + 2 * rw] * sinq[...]
    qs[:, 0:nw] = qa[:, 0:nw] * scale
    qs[:, nw:nw + rw] = qr * scale
    ckv = _rms(zmla[:, M_QL:M_QL + M_KVL], gkv[...])
    kv = _dot(ckv.astype(BF), wkv[...])
    kr = zsm[:, 32:64]
    if latent:
        kr = kr * cosk[...] + zsm[:, 64:96] * sink[...]
        kvc = _dot(cckv[0].astype(BF), wkv[...])
        kns[0:past, :] = kvc[:, 0:nw].astype(BF)
        vs[0:past, :] = kvc[:, nw:].astype(BF)
        krs[0:past, :] = jnp.concatenate([ckr[0]] * 4, axis=1).astype(BF)
    else:
        ckv_ref[...] = ckv
    kns[past:sk, :] = kv[:, 0:nw].astype(BF)
    vs[past:sk, :] = kv[:, nw:].astype(BF)
    krs[past:sk, :] = jnp.concatenate([kr] * 4, axis=1).astype(BF)

    qb = min(Q_BLOCK, n)
    lane = _iota((qb, 128), 1)

    def block(step, carry):
        rows = pl.ds(pl.multiple_of(step * qb, qb), qb)
        for p in range(MH // 2):
            ls = slice(128 * p, 128 * p + 128)
            qn = qs[rows, ls]
            quad = (2 * p) // 4
            qrp = qs[rows, nw + 128 * quad:nw + 128 * quad + 128]
            rhs = jnp.concatenate([kns[:, ls], krs[...]], axis=1)
            vp = vs[:, ls]
            o_pair = None
            for hh in range(2):
                j = (2 * p + hh) % 4
                qn_m = jnp.where((lane >> 6) == hh, qn, 0.0).astype(BF)
                qr_m = jnp.where((lane >> 5) == j, qrp, 0.0).astype(BF)
                s = _dot_nt(jnp.concatenate([qn_m, qr_m], axis=1), rhs)
                e = jnp.exp(s - jnp.max(s, axis=-1, keepdims=True))
                pv = _dot(e.astype(BF), vp) / jnp.sum(e, axis=-1, keepdims=True)
                o_pair = pv if hh == 0 else jnp.where(lane < M_V, o_pair, pv)
            o_ref[rows, ls] = o_pair.astype(BF)
        return carry

    lax.fori_loop(0, n // qb, block, 0)


def _rope_tables(n):
    half = M_ROPE // 2
    pos = jnp.arange(n)
    row = (pos // GRID_W).astype(F32)
    col = (pos % GRID_W).astype(F32)
    inv = ROPE_BASE ** (-jnp.arange(0, half, 2, dtype=F32) / half)
    ang = jnp.concatenate([row[:, None] * inv, col[:, None] * inv], axis=-1)
    cos = jnp.repeat(jnp.cos(ang), 2, axis=-1)
    sin = jnp.repeat(jnp.sin(ang), 2, axis=-1) * jnp.tile(jnp.asarray([-1.0, 1.0], F32), half)
    return jnp.tile(cos, (1, MH)), jnp.tile(sin, (1, MH)), cos, sin


def _mla(zmla, zsm, w, n, nseq, blk_off, cache):
    gq, wq, gkv, wkv = w
    c2 = lambda i: (0, 0)
    consts = [(gq, gq.shape, c2), (wq, wq.shape, c2), (gkv, gkv.shape, c2), (wkv, wkv.shape, c2)]
    past = 0
    extra = []
    if cache is not None:
        cckv, ckr = cache
        past = cckv.shape[1]
        c3 = lambda i: (i, 0, 0)
        consts += [(cckv, (1, past, M_KVL), c3), (ckr, (1, past, M_ROPE), c3)]
        consts += [(t, t.shape, c2) for t in _rope_tables(n)]
    else:
        extra = [((nseq * n, M_KVL), F32, (n, M_KVL), lambda i: (i, 0))]
    sk = past + n
    scratch = [pltpu.VMEM((n, MH * (M_NOPE + M_ROPE)), F32), pltpu.VMEM((sk, MH * M_NOPE), BF),
               pltpu.VMEM((sk, MH * M_V), BF), pltpu.VMEM((sk, 128), BF)]
    return _seq_call(functools.partial(_mla_kernel, cache is not None, n, past), "mla_mixer", n, nseq, blk_off,
                     [zmla, zsm], consts, [(MH * M_V, BF)], extra_outs=extra, scratch=scratch)


def _route(logits_t, bias):
    nt = logits_t.shape[1]
    gsz = N_EXP // N_GRP
    scores = _sigmoid(logits_t)
    sel = scores + bias
    neg = -jnp.inf
    sub = _iota((gsz, nt), 0)
    tops = []
    for g in range(N_GRP):
        blk = sel[gsz * g:gsz * g + gsz]
        m1 = jnp.max(blk, axis=0, keepdims=True)
        first = jnp.min(jnp.where(blk == m1, sub, gsz), axis=0, keepdims=True)
        m2 = jnp.max(jnp.where(sub == first, neg, blk), axis=0, keepdims=True)
        tops.append(m1 + m2)
    gs = jnp.concatenate(tops, axis=0)
    gidx = _iota((N_GRP, nt), 0)
    grank = jnp.zeros((N_GRP, nt), jnp.int32)
    for j in range(N_GRP):
        rj = gs[j:j + 1]
        grank += ((rj > gs) | ((rj == gs) & (gidx > j))).astype(jnp.int32)
    keep = grank < TOPK_GRP
    masked = jnp.concatenate(
        [jnp.where(jnp.broadcast_to(keep[g:g + 1], (gsz, nt)), sel[gsz * g:gsz * g + gsz], neg) for g in range(N_GRP)], axis=0)
    eidx = _iota((N_EXP, nt), 0)
    chosen = eidx < 0
    work = masked
    for _ in range(TOP_K):
        top = jnp.max(work, axis=0, keepdims=True)
        first = jnp.min(jnp.where(work == top, eidx, N_EXP), axis=0, keepdims=True)
        hit = eidx == first
        chosen = chosen | hit
        work = jnp.where(hit, neg, work)
    w = jnp.where(chosen, scores, 0.0)
    return chosen, w / jnp.sum(w, axis=0, keepdims=True) * ROUTED_SCALE


def _dispatch_meta(chosen, gates_t):
    tb = chosen.shape[1]
    sel = chosen.astype(F32)
    selb = sel.astype(BF)
    earlier = (_iota((tb, tb), 0) < _iota((tb, tb), 1)).astype(BF)
    rank = _dot(selb, earlier)
    cnt = jnp.sum(sel, axis=1, keepdims=True)
    padded = jnp.floor((cnt + (ROW_GRANULE - 1)) * (1.0 / ROW_GRANULE)) * ROW_GRANULE
    below = (_iota((N_EXP, N_EXP), 1) < _iota((N_EXP, N_EXP), 0)).astype(BF)
    start = _dot(below, jnp.broadcast_to(padded, (N_EXP, 128)).astype(BF))[:, 0:1]
    pos = start + rank
    kidx = _dot(below, selb)
    pos8, gate8 = [], []
    for k in range(TOP_K):
        hit = chosen & (kidx == float(k))
        pos8.append(jnp.sum(jnp.where(hit, pos, 0.0), axis=0, keepdims=True))
        gate8.append(jnp.sum(jnp.where(hit, gates_t, 0.0), axis=0, keepdims=True))
    return (jnp.concatenate(pos8, axis=0).astype(jnp.int32), jnp.concatenate(gate8, axis=0), cnt)


def _merge_kernel(n_ctx_blocks, xc_ref, xl_ref, fc_ref, fl_ref, ogc_ref, ogl_ref, omc_ref, oml_ref, zg_ref, m_ref,
                  wbf, wbg, wbm, wout, gn2, wrt, brt, xm_ref, h2_ref, pos_ref, gate_ref, cnt_ref):
    is_ctx = pl.program_id(0) < n_ctx_blocks
    ya = _dot(jnp.where(is_ctx, fc_ref[...], fl_ref[...]), wbf[...])
    yb = _dot(jnp.where(is_ctx, ogc_ref[...], ogl_ref[...]), wbg[...])
    yc = _dot(jnp.where(is_ctx, omc_ref[...], oml_ref[...]), wbm[...])
    merged = (_sigmoid(zg_ref[:, 0:D]) * ya.astype(BF) + _sigmoid(zg_ref[:, D:2 * D]) * yb.astype(BF)
              + _sigmoid(zg_ref[:, 2 * D:3 * D]) * yc.astype(BF))
    xm = jnp.where(is_ctx, xc_ref[...], xl_ref[...]) + m_ref[0, :, 2 * D:3 * D] * _dot(merged, wout[...])
    xm_ref[...] = xm
    h2 = _rms(xm, gn2[...]) * (1.0 + m_ref[0, :, 4 * D:5 * D]) + m_ref[0, :, 3 * D:4 * D]
    h2_ref[...] = h2.astype(BF)
    chosen, gates_t = _route(_dot_nt(wrt[...], h2, precision=lax.Precision.HIGHEST), brt[...])
    for sb in range(gates_t.shape[1] // DISP_BLOCK):
        ls = slice(sb * DISP_BLOCK, (sb + 1) * DISP_BLOCK)
        pos8, gate8, cnt = _dispatch_meta(chosen[:, ls], gates_t[:, ls])
        pos_ref[:, ls] = pos8
        gate_ref[:, ls] = gate8
        cnt_ref[sb] = jnp.broadcast_to(cnt, (N_EXP, 128))


def _merge(x_pair, T, mix_ctx, mix_lat, zg, mods, w, n_ctx_blocks, blocks_per_lat):
    tb = TOKEN_BLOCK
    row = functools.partial(_mod_row, n_ctx_blocks, blocks_per_lat)
    rb = lambda wd: pl.BlockSpec((tb, wd), lambda i: (i, 0))
    cb = lambda a: pl.BlockSpec(a.shape, lambda i: (0, 0))
    ctx_b = lambda wd: pl.BlockSpec((tb, wd), lambda i: (jnp.minimum(i, n_ctx_blocks - 1), 0))
    lat_b = lambda wd: pl.BlockSpec((tb, wd), lambda i: (jnp.maximum(i - n_ctx_blocks, 0), 0))
    mix_specs, mix_args = [], []
    for a_c, a_l in zip(mix_ctx, mix_lat):
        mix_specs += [ctx_b(a_c.shape[1]), lat_b(a_l.shape[1])]
        mix_args += [a_c, a_l]
    return pl.pallas_call(
        functools.partial(_merge_kernel, n_ctx_blocks),
        grid=(T // tb,),
        in_specs=_x_pair_specs(x_pair, tb, n_ctx_blocks) + mix_specs + [rb(3 * D),
                  pl.BlockSpec((1, 1, 6 * D), lambda i: (row(i), 0, 0))] + [cb(a) for a in w],
        out_specs=[rb(D), rb(D), pl.BlockSpec((TOP_K, tb), lambda i: (0, i)), pl.BlockSpec((TOP_K, tb), lambda i: (0, i)),
                   pl.BlockSpec((tb // DISP_BLOCK, N_EXP, 128), lambda i: (i, 0, 0))],
        out_shape=[jax.ShapeDtypeStruct((T, D), F32), jax.ShapeDtypeStruct((T, D), BF),
                   jax.ShapeDtypeStruct((TOP_K, T), jnp.int32), jax.ShapeDtypeStruct((TOP_K, T), F32),
                   jax.ShapeDtypeStruct((T // DISP_BLOCK, N_EXP, 128), F32)],
        compiler_params=_params("parallel"),
        name="merge_route",
    )(x_pair[0], x_pair[1], *mix_args, zg, mods, *w)


def _silu_mul(a, b):
    return a * _sigmoid(a) * b


def _placement(pos_ref, weight_ref, r0):
    rows = _iota((DISP_ROW_CHUNK, DISP_BLOCK), 0) + r0
    p = jnp.zeros((DISP_ROW_CHUNK, DISP_BLOCK), F32)
    for k in range(TOP_K):
        w = 1.0 if weight_ref is None else weight_ref[k:k + 1, :]
        p = jnp.where(rows == pos_ref[k:k + 1, :], w, p)
    return p.astype(BF)


def _dispatch_kernel(n_blocks, h_ref, pos_ref, xs_ref):
    h = jnp.where(pl.program_id(0) < n_blocks, h_ref[...], jnp.zeros_like(h_ref))
    for r0 in range(0, DISP_ROWS, DISP_ROW_CHUNK):
        xs_ref[r0:r0 + DISP_ROW_CHUNK, :] = _dot(_placement(pos_ref, None, r0), h).astype(BF)


def _dispatch(h2, pos8):
    T = h2.shape[0]
    nblk = T // DISP_BLOCK
    last = lambda b: jnp.minimum(b, nblk - 1)
    return pl.pallas_call(
        functools.partial(_dispatch_kernel, nblk),
        grid=(nblk + 1,),
        in_specs=[pl.BlockSpec((DISP_BLOCK, D), lambda b: (last(b), 0)),
                  pl.BlockSpec((TOP_K, DISP_BLOCK), lambda b: (0, last(b)))],
        out_specs=pl.BlockSpec((DISP_ROWS, D), lambda b: (b, 0)),
        out_shape=jax.ShapeDtypeStruct(((nblk + 1) * DISP_ROWS, D), BF),
        compiler_params=_params("parallel"),
        name="moe_dispatch",
    )(h2, pos8)


def _tile_tables(cnt, n_chunks_max, n_tiles_max):
    nblk = cnt.shape[0]
    nch = (cnt + (ROW_GRANULE - 1)) // ROW_GRANULE
    first = jnp.arange(nblk, dtype=jnp.int32)[:, None] * BLOCK_CHUNKS + jnp.cumsum(nch, axis=1) - nch
    tiles_e = (jnp.sum(nch, axis=0) + (CHUNKS_PER_TILE - 1)) // CHUNKS_PER_TILE
    span_e = tiles_e * CHUNKS_PER_TILE
    exp_start = jnp.cumsum(span_e) - span_e
    j = jnp.arange(n_chunks_max, dtype=jnp.int32)
    e_j = jnp.sum((exp_start[None, :] <= j[:, None]).astype(jnp.int32), axis=1) - 1
    onehot = (e_j[:, None] == jnp.arange(N_EXP, dtype=jnp.int32)[None, :]).astype(F32)
    rows_of = lambda tab: jnp.dot(onehot, tab.astype(F32), precision=lax.Precision.HIGHEST).astype(jnp.int32)
    local = j - rows_of(exp_start[:, None])[:, 0]
    blk_len = rows_of(nch.T)
    blk_start = rows_of(jnp.cumsum(nch.T, axis=1) - nch.T)
    inside = (blk_start <= local[:, None]) & (local[:, None] < blk_start + blk_len)
    real = jnp.any(inside, axis=1)
    chunk = jnp.sum(jnp.where(inside, rows_of(first.T) + (local[:, None] - blk_start), 0), axis=1)
    pad_rank = jnp.cumsum(jnp.where(real, 0, 1)) - 1
    src = jnp.where(real, chunk, BLOCK_CHUNKS - 1).astype(jnp.int32)
    dst = jnp.where(real, chunk, nblk * BLOCK_CHUNKS + pad_rank % BLOCK_CHUNKS).astype(jnp.int32)
    tile_end = jnp.cumsum(tiles_e)
    i = jnp.arange(n_tiles_max, dtype=jnp.int32)
    tile_expert = jnp.minimum(jnp.sum((tile_end[None, :] <= i[:, None]).astype(jnp.int32), axis=1), N_EXP - 1)
    return src, dst, tile_expert.astype(jnp.int32), tile_end[-1:].astype(jnp.int32)


def _expert_kernel(src_ref, dst_ref, texp_ref, nused_ref, xs_hbm, wg_ref, wu_ref, wd_ref, ys_hbm,
                   xbuf, ybuf, wgu_bf, wd_bf, gsem, ssem):
    i = pl.program_id(0)
    n_used = nused_ref[0]
    slot = lax.rem(i, 2)

    def chunk_copies(tile, slot_, to_buffer, do):
        for c in range(CHUNKS_PER_TILE):
            j = tile * CHUNKS_PER_TILE + c
            if to_buffer:
                cp = pltpu.make_async_copy(xs_hbm.at[src_ref[j]], xbuf.at[slot_, c], gsem.at[slot_])
            else:
                cp = pltpu.make_async_copy(ybuf.at[slot_, c], ys_hbm.at[dst_ref[j]], ssem.at[slot_])
            do(cp)

    start = lambda cp: cp.start()
    wait = lambda cp: cp.wait()

    @pl.when(i == 0)
    def _():
        chunk_copies(0, 0, True, start)

    @pl.when(i == n_used)
    def _():
        chunk_copies(i, slot, True, wait)

    @pl.when(i < n_used)
    def _():
        chunk_copies(i, slot, True, wait)
        chunk_copies(i + 1, 1 - slot, True, start)

        @pl.when((i == 0) | (texp_ref[i] != texp_ref[jnp.maximum(i - 1, 0)]))
        def _():
            wgu_bf[:, 0:E_DIM] = wg_ref[0, 0].astype(BF)
            wgu_bf[:, E_DIM:2 * E_DIM] = wu_ref[0, 0].astype(BF)
            wd_bf[...] = wd_ref[0, 0].astype(BF)

        gu = _dot(xbuf[slot].reshape(EXP_TILE, D), wgu_bf[...])
        hid = _silu_mul(gu[:, 0:E_DIM], gu[:, E_DIM:2 * E_DIM])
        ybuf[slot] = _dot(hid.astype(BF), wd_bf[...]).astype(BF).reshape(CHUNKS_PER_TILE, ROW_GRANULE, D)
        chunk_copies(i, slot, False, start)

        @pl.when(i >= 1)
        def _():
            chunk_copies(i - 1, 1 - slot, False, wait)

        @pl.when(i == n_used - 1)
        def _():
            chunk_copies(i, slot, False, wait)


def _experts(layer, xs, tables, w_eg, w_eu, w_ed, n_tiles_max):
    src, dst, tile_expert, n_used = tables
    chunks = xs.reshape(-1, ROW_GRANULE, D)
    wmap = lambda i, src_, dst_, texp, nu: (layer, texp[i], 0, 0)
    grid_spec = pltpu.PrefetchScalarGridSpec(
        num_scalar_prefetch=4,
        grid=(n_tiles_max + 1,),
        in_specs=[pl.BlockSpec(memory_space=pl.ANY),
                  pl.BlockSpec((1, 1, D, E_DIM), wmap),
                  pl.BlockSpec((1, 1, D, E_DIM), wmap),
                  pl.BlockSpec((1, 1, E_DIM, D), wmap)],
        out_specs=pl.BlockSpec(memory_space=pl.ANY),
        scratch_shapes=[pltpu.VMEM((2, CHUNKS_PER_TILE, ROW_GRANULE, D), BF),
                        pltpu.VMEM((2, CHUNKS_PER_TILE, ROW_GRANULE, D), BF),
                        pltpu.VMEM((D, 2 * E_DIM), BF), pltpu.VMEM((E_DIM, D), BF),
                        pltpu.SemaphoreType.DMA((2,)), pltpu.SemaphoreType.DMA((2,))],
    )
    return pl.pallas_call(
        _expert_kernel,
        grid_spec=grid_spec,
        out_shape=jax.ShapeDtypeStruct(chunks.shape, chunks.dtype),
        input_output_aliases={4: 0},
        compiler_params=_params("arbitrary"),
        name="moe_experts",
    )(src, dst, tile_expert, n_used, chunks, w_eg, w_eu, w_ed).reshape(xs.shape)


def _combine_kernel(final, ys_ref, pos_ref, gate_ref, h_ref, sg_ref, su_ref, sd_ref, x_ref, m_ref, gf_ref, o_ref):
    routed = jnp.zeros((DISP_BLOCK, D), F32)
    for r0 in range(0, DISP_ROWS, DISP_ROW_CHUNK):
        routed = routed + _dot_tn(_placement(pos_ref, gate_ref, r0), ys_ref[r0:r0 + DISP_ROW_CHUNK, :])
    h = h_ref[...]
    sh = _silu_mul(_dot(h, sg_ref[0].astype(BF)), _dot(h, su_ref[0].astype(BF)))
    out = x_ref[...] + m_ref[0] * (routed + _dot(sh.astype(BF), sd_ref[0].astype(BF)))
    if final:
        out = _rms(out, gf_ref[...])
    o_ref[...] = out


def _combine(layer, final, ys, pos8, gate8, h2, w_sg, w_su, w_sd, xm, mods, g_final, n_ctx_blocks, blocks_per_lat,
             first_block, n_blocks):
    tb = DISP_BLOCK
    row = functools.partial(_mod_row, n_ctx_blocks, blocks_per_lat)
    at = lambda b: b + first_block
    return pl.pallas_call(
        functools.partial(_combine_kernel, final),
        grid=(n_blocks,),
        in_specs=[pl.BlockSpec((DISP_ROWS, D), lambda b: (at(b), 0)),
                  pl.BlockSpec((TOP_K, tb), lambda b: (0, at(b))),
                  pl.BlockSpec((TOP_K, tb), lambda b: (0, at(b))),
                  pl.BlockSpec((tb, D), lambda b: (at(b), 0)),
                  pl.BlockSpec((1, D, E_DIM), lambda b: (layer, 0, 0)),
                  pl.BlockSpec((1, D, E_DIM), lambda b: (layer, 0, 0)),
                  pl.BlockSpec((1, E_DIM, D), lambda b: (layer, 0, 0)),
                  pl.BlockSpec((tb, D), lambda b: (at(b), 0)),
                  pl.BlockSpec((1, 1, D), lambda b: (row(at(b)), 0, 5)),
                  pl.BlockSpec((1, D), lambda b: (0, 0))],
        out_specs=pl.BlockSpec((tb, D), lambda b: (b, 0)),
        out_shape=jax.ShapeDtypeStruct((n_blocks * tb, D), F32),
        compiler_params=_params("parallel"),
        name="moe_combine",
    )(ys, pos8, gate8, h2, w_sg, w_su, w_sd, xm, mods, g_final)


def _moe(layer, final, h2, pos8, gate8, cnt, w_eg, w_eu, w_ed, w_sg, w_su, w_sd, xm, mods, g_final, disp_blocks):
    T = h2.shape[0]
    nblk = T // DISP_BLOCK
    n_chunks_max = (TOP_K * T + N_EXP * nblk * (ROW_GRANULE - 1)) // ROW_GRANULE + N_EXP * (CHUNKS_PER_TILE - 1)
    n_tiles_max = -(-n_chunks_max // CHUNKS_PER_TILE)
    xs = _dispatch(h2, pos8)
    tables = _tile_tables(cnt[:, :, 0].astype(jnp.int32), (n_tiles_max + 1) * CHUNKS_PER_TILE, n_tiles_max + 1)
    ys = _experts(layer, xs, tables, w_eg, w_eu, w_ed, n_tiles_max)
    comb = functools.partial(_combine, layer, final, ys, pos8, gate8, h2, w_sg, w_su, w_sd, xm, mods, g_final,
                             *disp_blocks)
    if not final:
        return comb(0, nblk)
    n_ctx = disp_blocks[0]
    return comb(0, n_ctx), comb(n_ctx, nblk - n_ctx)


def _reorder_w_in(w):
    kr = w[:, 2592:2624]
    kr_sw = kr.reshape(D, M_ROPE // 2, 2)[..., ::-1].reshape(kr.shape)
    pad = jnp.zeros(kr.shape, w.dtype)
    parts = [w[:, :1920], w[:, 1952:2592], w[:, 1920:1952], kr, kr_sw, pad, w[:, 2624:]]
    return jnp.concatenate([p.astype(BF) for p in parts], axis=1)


def _reorder_w_q(w_q_up, with_swap):
    L = w_q_up.shape[0]
    w = w_q_up.reshape(L, M_QL, MH, M_NOPE + M_ROPE)
    nope = w[..., :M_NOPE].reshape(L, M_QL, MH * M_NOPE)
    rope = w[..., M_NOPE:]
    parts = [nope, rope.reshape(L, M_QL, MH * M_ROPE)]
    if with_swap:
        parts.append(rope.reshape(L, M_QL, MH, M_ROPE // 2, 2)[..., ::-1].reshape(L, M_QL, MH * M_ROPE))
    return jnp.concatenate(parts, axis=2).astype(BF)


def _reorder_w_kv(w_kv_up):
    L = w_kv_up.shape[0]
    w = w_kv_up.reshape(L, M_KVL, MH, M_NOPE + M_V)
    return jnp.concatenate([w[..., :M_NOPE].reshape(L, M_KVL, MH * M_NOPE),
                            w[..., M_NOPE:].reshape(L, M_KVL, MH * M_V)], axis=2).astype(BF)


def kernel(x_prompt, x_sample, state_gla_fwd, state_gla_bwd, cache_mla_ckv, cache_mla_krope, c, c_ctx, w_mod, b_mod, g_norm1, g_norm2, w_in, w_gla_gate_f, b_gla_gate_f, w_gla_gate_b, b_gla_gate_b, g_gla_out, g_q_a, w_q_up, g_kv_a, w_kv_up, w_br_fourier, w_br_gla, w_br_mla, w_out, w_router, b_router, w_exp_gate, w_exp_up, w_exp_down, w_sh_gate, w_sh_up, w_sh_down, g_final):
    nb, sl, _ = x_prompt.shape
    db, dl, _ = x_sample.shape
    L = w_mod.shape[0]
    t_ctx, t_lat = nb * sl, db * dl
    T = t_ctx + t_lat
    assert sl % G_CHUNK == 0 and dl % G_CHUNK == 0 and dl % GRID_W == 0
    assert t_ctx % dl == 0 and dl % TOKEN_BLOCK == 0 and TOKEN_BLOCK % DISP_BLOCK == 0
    assert t_ctx % TOKEN_BLOCK == 0 and dl % Q_BLOCK == 0 and 1 + db <= 8

    x_pair = (x_prompt.reshape(t_ctx, D), x_sample.reshape(t_lat, D), 0)
    cond = jnp.concatenate([c_ctx[None, :], c, jnp.zeros((7 - db, D), F32)], axis=0)
    mods_all = _modulation(cond, w_mod, b_mod)

    w_in_r = [_reorder_w_in(w_in[l]) for l in range(L)]
    wq_ctx = _reorder_w_q(w_q_up, False)
    wq_lat = _reorder_w_q(w_q_up, True)
    wkv_r = _reorder_w_kv(w_kv_up)
    lat_off = t_ctx // dl

    new_f, new_b, new_ckv, new_kr = [], [], [], []
    for l in range(L):
        mods = mods_all[l].reshape(8, 1, 6 * D)
        tok = (t_ctx // TOKEN_BLOCK, dl // TOKEN_BLOCK)
        zf, zqk, zvr, zmla, zsm, zg = _in_projection(x_pair, T, mods, g_norm1[l][None, :], w_in_r[l], *tok)

        (f_c,) = _fourier(zf, sl, nb, 0)
        (f_l,) = _fourier(zf, dl, db, lat_off)

        gate_w = (w_gla_gate_f[l], b_gla_gate_f[l][None, :], w_gla_gate_b[l], b_gla_gate_b[l][None, :],
                  g_gla_out[l].reshape(1, GH * GDV))
        og_c, s_f, s_b = _gla(zqk, zvr, zsm, gate_w, sl, nb, 0, None)
        og_l, _, _ = _gla(zqk, zvr, zsm, gate_w, dl, db, lat_off, (state_gla_fwd[:, l], state_gla_bwd[:, l]))

        gq, gkv = g_q_a[l][None, :], g_kv_a[l][None, :]
        om_c, ckv = _mla(zmla, zsm, (gq, wq_ctx[l], gkv, wkv_r[l]), sl, nb, 0, None)
        (om_l,) = _mla(zmla, zsm, (gq, wq_lat[l], gkv, wkv_r[l]), dl, db, lat_off,
                       (cache_mla_ckv[:, l], cache_mla_krope[:, l]))

        mw = (w_br_fourier[l].astype(BF), w_br_gla[l].astype(BF), w_br_mla[l].astype(BF), w_out[l].astype(BF),
              g_norm2[l][None, :], w_router[l].T, b_router[l][:, None])
        xm, h2, pos8, gate8, cnt = _merge(x_pair, T, (f_c, og_c, om_c), (f_l, og_l, om_l), zg, mods, mw, *tok)

        x = _moe(l, l == L - 1, h2, pos8, gate8, cnt, w_exp_gate, w_exp_up, w_exp_down, w_sh_gate, w_sh_up, w_sh_down,
                 xm, mods, g_final[None, :], (t_ctx // DISP_BLOCK, dl // DISP_BLOCK))
        if l < L - 1:
            x_pair = (x, x, t_ctx // TOKEN_BLOCK)

        new_f.append(s_f)
        new_b.append(s_b)
        new_ckv.append(ckv.reshape(nb, sl, M_KVL))
        new_kr.append(zsm[:t_ctx, 32:64].reshape(nb, sl, M_ROPE))

    y_prompt = x[0].reshape(nb, sl, D)
    y_sample = x[1].reshape(db, dl, D)
    return (y_prompt, y_sample, jnp.stack(new_f, axis=1), jnp.stack(new_b, axis=1),
            jnp.stack(new_ckv, axis=1), jnp.stack(new_kr, axis=1))
```

```python
import functools

import numpy as np
import jax
import jax.numpy as jnp
from jax import lax
from jax.experimental import pallas as pl
from jax.experimental.pallas import tpu as pltpu

F32 = jnp.float32
BF = jnp.bfloat16

D = 1024
GRID_W = 64
FN_G, FN_GW = 4, 96
FN_W = FN_G * FN_GW
GH, GDK, GDV = 4, 64, 128
G_RANK = 16
G_TAU = 16.0
G_CHUNK = 64
MH, M_NOPE, M_ROPE, M_V = 8, 64, 32, 64
M_QL, M_KVL = 384, 256
ROPE_BASE = 10000.0
N_EXP, TOP_K, N_GRP, TOPK_GRP = 64, 8, 8, 4
E_DIM = 256
ROUTED_SCALE = 2.5
EPS = 1e-6

C_F = (0, 384)
C_QK = (384, 896)
C_VR = (896, 1920)
C_MLA = (1920, 2560)
C_SM = (2560, 2688)
C_G = (2688, 5760)
IN_COLS_R = 5760

VMEM_LIMIT_V7X = 56 * 1024 * 1024
TOKEN_BLOCK = 512
Q_BLOCK = 256
DISP_BLOCK = 256
ROW_GRANULE = 16
DISP_ROWS = -(-(TOP_K * DISP_BLOCK + N_EXP * (ROW_GRANULE - 1)) // 256) * 256
EXP_TILE = 512
CHUNKS_PER_TILE = EXP_TILE // ROW_GRANULE
BLOCK_CHUNKS = DISP_ROWS // ROW_GRANULE
DISP_ROW_CHUNK = 256
MAX_ONE_FACTOR_DECAY = 60.0


def _params(*sem):
    return pltpu.CompilerParams(dimension_semantics=sem, vmem_limit_bytes=VMEM_LIMIT_V7X)


def _dot(a, b):
    return jnp.dot(a, b, preferred_element_type=F32)


def _dot_nt(a, b, precision=None):
    return lax.dot_general(a, b, (((1,), (1,)), ((), ())), precision=precision, preferred_element_type=F32)


def _dot_tn(a, b):
    return lax.dot_general(a, b, (((0,), (0,)), ((), ())), preferred_element_type=F32)


def _sigmoid(x):
    return 1.0 / (1.0 + jnp.exp(-x))


def _rms(x, g):
    return x * lax.rsqrt(jnp.mean(x * x, axis=-1, keepdims=True) + EPS) * g


def _iota(shape, dim):
    return lax.broadcasted_iota(jnp.int32, shape, dim)


def _mod_row(n_ctx_blocks, blocks_per_lat, i):
    return jnp.where(i < n_ctx_blocks, 0, 1 + (i - n_ctx_blocks) // blocks_per_lat)


def _mod_kernel(c_ref, w_ref, b_ref, o_ref):
    c = c_ref[...]
    o_ref[0] = _dot_f32(c * _sigmoid(c), w_ref[0]) + b_ref[0]


def _modulation(cond, w_mod, b_mod):
    L = w_mod.shape[0]
    rows = cond.shape[0]
    tn = 1536
    return pl.pallas_call(
        _mod_kernel,
        grid=(L, 6 * D // tn),
        in_specs=[pl.BlockSpec((rows, D), lambda l, j: (0, 0)),
                  pl.BlockSpec((1, D, tn), lambda l, j: (l, 0, j)),
                  pl.BlockSpec((1, 1, tn), lambda l, j: (l, 0, j))],
        out_specs=pl.BlockSpec((1, rows, tn), lambda l, j: (l, 0, j)),
        out_shape=jax.ShapeDtypeStruct((L, rows, 6 * D), F32),
        compiler_params=_params("parallel", "parallel"),
        name="modulation",
    )(cond, w_mod, b_mod.reshape(L, 1, 6 * D))


def _inproj_kernel(n_ctx_blocks, xc_ref, xl_ref, m_ref, g_ref, w_ref, of_ref, oqk_ref, ovr_ref, omla_ref, osm_ref, og_ref):
    y = _rms(jnp.where(pl.program_id(0) < n_ctx_blocks, xc_ref[...], xl_ref[...]), g_ref[...])
    h = (y * (1.0 + m_ref[0, :, D:2 * D]) + m_ref[0, :, 0:D]).astype(BF)
    of_ref[...] = _dot(h, w_ref[:, C_F[0]:C_F[1]]).astype(BF)
    oqk_ref[...] = _dot(h, w_ref[:, C_QK[0]:C_QK[1]])
    ovr_ref[...] = _dot(h, w_ref[:, C_VR[0]:C_VR[1]]).astype(BF)
    omla_ref[...] = _dot(h, w_ref[:, C_MLA[0]:C_MLA[1]])
    osm_ref[...] = _dot(h, w_ref[:, C_SM[0]:C_SM[1]])
    og_ref[...] = _dot(h, w_ref[:, C_G[0]:C_G[1]]).astype(BF)


def _x_pair_specs(x_pair, tb, n_ctx_blocks):
    _, _, lat_off = x_pair
    return [pl.BlockSpec((tb, D), lambda i: (jnp.minimum(i, n_ctx_blocks - 1), 0)),
            pl.BlockSpec((tb, D), lambda i: (jnp.maximum(i - n_ctx_blocks, 0) + lat_off, 0))]


def _in_projection(x_pair, T, mods, g1, w_in_r, n_ctx_blocks, blocks_per_lat):
    tb = TOKEN_BLOCK
    row = functools.partial(_mod_row, n_ctx_blocks, blocks_per_lat)
    widths = [(C_F, BF), (C_QK, F32), (C_VR, BF), (C_MLA, F32), (C_SM, F32), (C_G, BF)]
    return pl.pallas_call(
        functools.partial(_inproj_kernel, n_ctx_blocks),
        grid=(T // tb,),
        in_specs=_x_pair_specs(x_pair, tb, n_ctx_blocks) + [
                  pl.BlockSpec((1, 1, 2 * D), lambda i: (row(i), 0, 0)),
                  pl.BlockSpec((1, D), lambda i: (0, 0)),
                  pl.BlockSpec((D, IN_COLS_R), lambda i: (0, 0))],
        out_specs=[pl.BlockSpec((tb, c[1] - c[0]), lambda i: (i, 0)) for c, _ in widths],
        out_shape=[jax.ShapeDtypeStruct((T, c[1] - c[0]), dt) for c, dt in widths],
        compiler_params=_params("parallel"),
        name="in_projection",
    )(x_pair[0], x_pair[1], mods, g1, w_in_r)


def _seq_call(kernel, name, n, nseq, blk_off, seq_ins, const_ins, out_widths, extra_outs=(), scratch=()):
    in_specs = [pl.BlockSpec((n, a.shape[1]), lambda i: (i + blk_off, 0)) for a in seq_ins]
    in_specs += [pl.BlockSpec(bs, im) for _, bs, im in const_ins]
    args = list(seq_ins) + [a for a, _, _ in const_ins]
    out_specs = [pl.BlockSpec((n, w), lambda i: (i, 0)) for w, _ in out_widths]
    out_shape = [jax.ShapeDtypeStruct((nseq * n, w), dt) for w, dt in out_widths]
    out_specs += [pl.BlockSpec(bs, im) for _, _, bs, im in extra_outs]
    out_shape += [jax.ShapeDtypeStruct(s, dt) for s, dt, _, _ in extra_outs]
    return pl.pallas_call(
        kernel, grid=(nseq,), in_specs=in_specs, out_specs=out_specs, out_shape=out_shape,
        scratch_shapes=list(scratch), compiler_params=_params("parallel"), name=name,
    )(*args)


def _fourier_kernel(u_ref, r_ref, lc_ref, ls_ref, o_ref):
    y = _dot(u_ref[...], r_ref[...])
    o_ref[...] = (_dot(lc_ref[...], y[:, :FN_W].astype(BF)) + _dot(ls_ref[...], y[:, FN_W:].astype(BF))).astype(BF)


def _dft_tables(n):
    k = np.arange(FN_GW)
    ang = 2.0 * np.pi * ((k[:, None] * k[None, :]) % FN_GW) / FN_GW
    eye = np.eye(FN_G)
    right = np.concatenate([np.kron(eye, np.cos(ang)), np.kron(eye, np.sin(ang))], axis=1)
    p = np.arange(n)
    angn = 2.0 * np.pi * ((p[:, None] * p[None, :]) % n) / n
    scale = 1.0 / np.sqrt(float(n * FN_GW))
    return (jnp.asarray(right, F32).astype(BF), jnp.asarray(np.cos(angn) * scale, F32).astype(BF),
            jnp.asarray(-np.sin(angn) * scale, F32).astype(BF))


def _fourier(zf, n, nseq, blk_off):
    right, lc, ls = _dft_tables(n)
    consts = [(right, (FN_W, 2 * FN_W), lambda i: (0, 0)), (lc, (n, n), lambda i: (0, 0)), (ls, (n, n), lambda i: (0, 0))]
    return _seq_call(_fourier_kernel, "fourier_mix", n, nseq, blk_off, [zf], consts, [(FN_W, BF)])


def _bf_parts(x, n):
    parts, rest = [], x
    for _ in range(n):
        p = rest.astype(BF)
        parts.append(p)
        rest = rest - p.astype(F32)
    return parts


def _dot_f32(a, b, dot=None):
    dot = dot or _dot
    a1, a2 = _bf_parts(a, 2)
    b1, b2 = _bf_parts(b, 2)
    return (dot(a1, b2) + dot(a2, b1)) + dot(a1, b1)


def _cumulate(tri, g):
    g1, g2, g3 = _bf_parts(g, 3)
    return (_dot(tri, g3) + _dot(tri, g2)) + _dot(tri, g1)


def _log_gate(z, w_ref, b_ref):
    pre = _dot_f32(z, w_ref[...]) + b_ref[...]
    return (jnp.minimum(pre, 0.0) - jnp.log1p(jnp.exp(-jnp.abs(pre)))) * (1.0 / G_TAU)


def _gla_kernel(has_state, n, *refs):
    if has_state:
        (zqk, zvr, zsm, wgf, bgf, wgb, bgb, gout, s0f, s0b, o_ref, sf_ref, sb_ref,
         oacc_f, oacc_b, lg_f, lg_b, st_f, st_b) = refs
    else:
        (zqk, zvr, zsm, wgf, bgf, wgb, bgb, gout, o_ref, sf_ref, sb_ref,
         oacc_f, oacc_b, lg_f, lg_b, st_f, st_b) = refs
        s0f = s0b = None
    C = G_CHUNK
    nc = n // C
    ri, ci = _iota((C, C), 0), _iota((C, C), 1)
    t_idx, lane = _iota((C, 128), 0), _iota((C, 128), 1)
    s_idx = lane & (C - 1)
    left = lane < GDK
    vleft = _iota((C, 2 * GDV), 1) < GDV
    blockdiag = (_iota((2 * GDV, 2 * GDK), 0) >> 7) == (_iota((2 * GDV, 2 * GDK), 1) >> 6)
    row_w = _iota((C, GH * GDK), 0)

    def block_reference(cum, blk, off):
        if blk >= 8:
            return jnp.concatenate([jnp.broadcast_to(cum[j * blk + off:j * blk + off + 1], (blk, cum.shape[1]))
                                    for j in range(C // blk)], axis=0)
        out = cum
        for m in range(blk):
            if m != off:
                out = jnp.where((row_w & (blk - 1)) == m, pltpu.roll(cum, (m - off) % C, axis=0), out)
        return out

    def pair_blocks(x, ls):
        xp = x[:, ls]
        z = jnp.zeros_like(xp)
        return jnp.concatenate([jnp.where(left, xp, z), jnp.where(left, z, xp)], axis=0)

    def intra_scores(q, k, cum, reverse):
        levels = []
        blk = C
        while blk >= 2:
            half = blk // 2
            ref = block_reference(cum, blk, half if reverse else half - 1)
            d = cum - ref
            w = jnp.exp(jnp.minimum(d, -d))
            qs = (q * w).astype(BF)
            ks = (k * w).astype(BF)
            t_in, s_in = t_idx & (blk - 1), s_idx & (blk - 1)
            same = (t_idx & -blk) == (s_idx & -blk)
            split = ((s_in >= half) & (t_in < half)) if reverse else ((t_in >= half) & (s_in < half))
            levels.append((qs, ks, same & split))
            blk = half
        levels.append((q.astype(BF), k.astype(BF), t_idx == s_idx))
        out = []
        for p in range(2):
            ls = slice(128 * p, 128 * p + 128)
            sc = jnp.zeros((C, 128), F32)
            for qs, ks, m in levels:
                sc = jnp.where(m, _dot_nt(qs[:, ls], pair_blocks(ks, ls)), sc)
            out.append(sc.astype(BF))
        return out

    def load_state(st, s0_ref):
        for p in range(2):
            if s0_ref is None:
                st[p] = jnp.zeros((2 * GDV, 2 * GDK), F32)
            else:
                z = jnp.zeros((GDK, GDV), F32)
                blk = jnp.concatenate([jnp.concatenate([s0_ref[0, 2 * p], z], axis=1),
                                       jnp.concatenate([z, s0_ref[0, 2 * p + 1]], axis=1)], axis=0)
                st[p] = blk.T

    def store_state(st, out_ref):
        for p in range(2):
            blk = st[p].T
            out_ref[0, 2 * p] = blk[0:GDK, 0:GDV]
            out_ref[0, 2 * p + 1] = blk[GDK:2 * GDK, GDV:2 * GDV]

    def chunk_scores(qh, k, cum, reverse):
        mask = (s_idx >= t_idx) if reverse else (s_idx <= t_idx)
        kh = (k * jnp.exp(-cum)).astype(BF)
        return [jnp.where(mask, _dot_nt(qh[:, 128 * p:128 * p + 128], pair_blocks(kh, slice(128 * p, 128 * p + 128))),
                          0.0).astype(BF) for p in range(2)]

    def chunk(small_decay, c, reverse, lg, st, oacc):
        tri = (ci >= ri).astype(BF) if reverse else (ci <= ri).astype(BF)
        rows = pl.ds(pl.multiple_of(c * C, C), C)
        cum = _cumulate(tri, lg[rows, :])
        tot = cum[0:1] if reverse else cum[C - 1:C]
        q = zqk[rows, 0:GH * GDK] * (GDK ** -0.5)
        k = zqk[rows, GH * GDK:2 * GH * GDK]
        qh = (q * jnp.exp(cum)).astype(BF)
        kb = (k * jnp.exp(tot - cum)).astype(BF)
        dec = jnp.exp(tot)
        scores = chunk_scores(qh, k, cum, reverse) if small_decay else intra_scores(q, k, cum, reverse)
        for p in range(2):
            ls = slice(128 * p, 128 * p + 128)
            vs = slice(256 * p, 256 * p + 256)
            vp = zvr[rows, vs]
            zv = jnp.zeros_like(vp)
            vblk = jnp.concatenate([jnp.where(vleft, vp, zv), jnp.where(vleft, zv, vp)], axis=0)
            stp = st[p]
            oacc[rows, vs] = _dot(scores[p], vblk) + _dot_nt(qh[:, ls], stp.astype(BF))
            st[p] = dec[:, ls] * stp + jnp.where(blockdiag, _dot_tn(vp, kb[:, ls]), 0.0)

    def both_directions(small_decay, step, carry):
        chunk(small_decay, step, False, lg_f, st_f, oacc_f)
        chunk(small_decay, nc - 1 - step, True, lg_b, st_b, oacc_b)
        return carry

    gf = _log_gate(zsm[:, 0:G_RANK], wgf, bgf)
    gb = _log_gate(zsm[:, G_RANK:2 * G_RANK], wgb, bgb)
    lg_f[...] = gf
    lg_b[...] = gb

    def mix(small_decay):
        load_state(st_f, s0f)
        load_state(st_b, s0b)
        lax.fori_loop(0, nc, functools.partial(both_directions, small_decay), 0, unroll=min(nc, 4))
        store_state(st_f, sf_ref)
        store_state(st_b, sb_ref)
        rb = 128
        for r0 in range(0, n, rb):
            for h in range(GH):
                hs = slice(GDV * h, GDV * h + GDV)
                oh = oacc_f[r0:r0 + rb, hs] + oacc_b[r0:r0 + rb, hs]
                oh = oh * lax.rsqrt(jnp.mean(oh * oh, axis=-1, keepdims=True) + EPS) * gout[:, hs]
                r = zvr[r0:r0 + rb, GH * GDV + hs.start:GH * GDV + hs.stop].astype(F32)
                o_ref[r0:r0 + rb, hs] = (oh * (r * _sigmoid(r))).astype(BF)

    mix(True)

    @pl.when(jnp.minimum(jnp.min(gf), jnp.min(gb)) * C <= -MAX_ONE_FACTOR_DECAY)
    def _():
        mix(False)


def _gla(zqk, zvr, zsm, gate_w, n, nseq, blk_off, states):
    wgf, bgf, wgb, bgb, gout = gate_w
    c2 = lambda i: (0, 0)
    consts = [(wgf, wgf.shape, c2), (bgf, bgf.shape, c2), (wgb, wgb.shape, c2), (bgb, bgb.shape, c2), (gout, gout.shape, c2)]
    st_blk = (1, GH, GDK, GDV)
    st_map = lambda i: (i, 0, 0, 0)
    if states is not None:
        consts += [(s, st_blk, st_map) for s in states]
    extra = [((nseq, GH, GDK, GDV), F32, st_blk, st_map)] * 2
    scratch = ([pltpu.VMEM((n, GH * GDV), F32)] * 2 + [pltpu.VMEM((n, GH * GDK), F32)] * 2
               + [pltpu.VMEM((2, 2 * GDV, 2 * GDK), F32)] * 2)
    return _seq_call(functools.partial(_gla_kernel, states is not None, n), "gla_mixer", n, nseq, blk_off,
                     [zqk, zvr, zsm], consts, [(GH * GDV, BF)], extra_outs=extra, scratch=scratch)


def _mla_kernel(latent, n, past, *refs):
    if latent:
        (zmla, zsm, gq, wq, gkv, wkv, cckv, ckr, cosq, sinq, cosk, sink, o_ref, qs, kns, vs, krs) = refs
    else:
        (zmla, zsm, gq, wq, gkv, wkv, o_ref, ckv_ref, qs, kns, vs, krs) = refs
    sk = past + n
    scale = (M_NOPE + M_ROPE) ** -0.5
    nw, rw = MH * M_NOPE, MH * M_ROPE
    qa = _dot(_rms(zmla[:, 0:M_QL], gq[...]).astype(BF), wq[...])
    qr = qa[:, nw:nw + rw]
    if latent:
        qr = qr * cosq[...] + qa[:, nw + rw:nw + 2 * rw] * sinq[...]
    qs[:, 0:nw] = qa[:, 0:nw] * scale
    qs[:, nw:nw + rw] = qr * scale
    ckv = _rms(zmla[:, M_QL:M_QL + M_KVL], gkv[...])
    kv = _dot(ckv.astype(BF), wkv[...])
    kr = zsm[:, 32:64]
    if latent:
        kr = kr * cosk[...] + zsm[:, 64:96] * sink[...]
        kvc = _dot(cckv[0].astype(BF), wkv[...])
        kns[0:past, :] = kvc[:, 0:nw].astype(BF)
        vs[0:past, :] = kvc[:, nw:].astype(BF)
        krs[0:past, :] = jnp.concatenate([ckr[0]] * 4, axis=1).astype(BF)
    else:
        ckv_ref[...] = ckv
    kns[past:sk, :] = kv[:, 0:nw].astype(BF)
    vs[past:sk, :] = kv[:, nw:].astype(BF)
    krs[past:sk, :] = jnp.concatenate([kr] * 4, axis=1).astype(BF)

    qb = min(Q_BLOCK, n)
    lane = _iota((qb, 128), 1)

    def block(step, carry):
        rows = pl.ds(pl.multiple_of(step * qb, qb), qb)
        for p in range(MH // 2):
            ls = slice(128 * p, 128 * p + 128)
            qn = qs[rows, ls]
            quad = (2 * p) // 4
            qrp = qs[rows, nw + 128 * quad:nw + 128 * quad + 128]
            rhs = jnp.concatenate([kns[:, ls], krs[...]], axis=1)
            vp = vs[:, ls]
            o_pair = None
            for hh in range(2):
                j = (2 * p + hh) % 4
                qn_m = jnp.where((lane >> 6) == hh, qn, 0.0).astype(BF)
                qr_m = jnp.where((lane >> 5) == j, qrp, 0.0).astype(BF)
                s = _dot_nt(jnp.concatenate([qn_m, qr_m], axis=1), rhs)
                e = jnp.exp(s - jnp.max(s, axis=-1, keepdims=True))
                pv = _dot(e.astype(BF), vp) / jnp.sum(e, axis=-1, keepdims=True)
                o_pair = pv if hh == 0 else jnp.where(lane < M_V, o_pair, pv)
            o_ref[rows, ls] = o_pair.astype(BF)
        return carry

    lax.fori_loop(0, n // qb, block, 0)


def _rope_tables(n):
    half = M_ROPE // 2
    pos = jnp.arange(n)
    row = (pos // GRID_W).astype(F32)
    col = (pos % GRID_W).astype(F32)
    inv = ROPE_BASE ** (-jnp.arange(0, half, 2, dtype=F32) / half)
    ang = jnp.concatenate([row[:, None] * inv, col[:, None] * inv], axis=-1)
    cos = jnp.repeat(jnp.cos(ang), 2, axis=-1)
    sin = jnp.repeat(jnp.sin(ang), 2, axis=-1) * jnp.tile(jnp.asarray([-1.0, 1.0], F32), half)
    return jnp.tile(cos, (1, MH)), jnp.tile(sin, (1, MH)), cos, sin


def _mla(zmla, zsm, w, n, nseq, blk_off, cache):
    gq, wq, gkv, wkv = w
    c2 = lambda i: (0, 0)
    consts = [(gq, gq.shape, c2), (wq, wq.shape, c2), (gkv, gkv.shape, c2), (wkv, wkv.shape, c2)]
    past = 0
    extra = []
    if cache is not None:
        cckv, ckr = cache
        past = cckv.shape[1]
        c3 = lambda i: (i, 0, 0)
        consts += [(cckv, (1, past, M_KVL), c3), (ckr, (1, past, M_ROPE), c3)]
        consts += [(t, t.shape, c2) for t in _rope_tables(n)]
    else:
        extra = [((nseq * n, M_KVL), F32, (n, M_KVL), lambda i: (i, 0))]
    sk = past + n
    scratch = [pltpu.VMEM((n, MH * (M_NOPE + M_ROPE)), F32), pltpu.VMEM((sk, MH * M_NOPE), BF),
               pltpu.VMEM((sk, MH * M_V), BF), pltpu.VMEM((sk, 128), BF)]
    return _seq_call(functools.partial(_mla_kernel, cache is not None, n, past), "mla_mixer", n, nseq, blk_off,
                     [zmla, zsm], consts, [(MH * M_V, BF)], extra_outs=extra, scratch=scratch)


def _route(logits_t, bias):
    nt = logits_t.shape[1]
    gsz = N_EXP // N_GRP
    scores = _sigmoid(logits_t)
    sel = scores + bias
    neg = -jnp.inf
    sub = _iota((gsz, nt), 0)
    tops = []
    for g in range(N_GRP):
        blk = sel[gsz * g:gsz * g + gsz]
        m1 = jnp.max(blk, axis=0, keepdims=True)
        first = jnp.min(jnp.where(blk == m1, sub, gsz), axis=0, keepdims=True)
        m2 = jnp.max(jnp.where(sub == first, neg, blk), axis=0, keepdims=True)
        tops.append(m1 + m2)
    gs = jnp.concatenate(tops, axis=0)
    gidx = _iota((N_GRP, nt), 0)
    grank = jnp.zeros((N_GRP, nt), jnp.int32)
    for j in range(N_GRP):
        rj = gs[j:j + 1]
        grank += ((rj > gs) | ((rj == gs) & (gidx > j))).astype(jnp.int32)
    keep = grank < TOPK_GRP
    masked = jnp.concatenate(
        [jnp.where(jnp.broadcast_to(keep[g:g + 1], (gsz, nt)), sel[gsz * g:gsz * g + gsz], neg) for g in range(N_GRP)], axis=0)
    eidx = _iota((N_EXP, nt), 0)
    chosen = eidx < 0
    work = masked
    for _ in range(TOP_K):
        top = jnp.max(work, axis=0, keepdims=True)
        first = jnp.min(jnp.where(work == top, eidx, N_EXP), axis=0, keepdims=True)
        hit = eidx == first
        chosen = chosen | hit
        work = jnp.where(hit, neg, work)
    w = jnp.where(chosen, scores, 0.0)
    return chosen, w / jnp.sum(w, axis=0, keepdims=True) * ROUTED_SCALE


def _dispatch_meta(chosen, gates_t):
    tb = chosen.shape[1]
    sel = chosen.astype(F32)
    selb = sel.astype(BF)
    earlier = (_iota((tb, tb), 0) < _iota((tb, tb), 1)).astype(BF)
    rank = _dot(selb, earlier)
    cnt = jnp.sum(sel, axis=1, keepdims=True)
    padded = jnp.floor((cnt + (ROW_GRANULE - 1)) * (1.0 / ROW_GRANULE)) * ROW_GRANULE
    below = (_iota((N_EXP, N_EXP), 1) < _iota((N_EXP, N_EXP), 0)).astype(BF)
    start = _dot(below, jnp.broadcast_to(padded, (N_EXP, 128)).astype(BF))[:, 0:1]
    pos = start + rank
    kidx = _dot(below, selb)
    pos8, gate8 = [], []
    for k in range(TOP_K):
        hit = chosen & (kidx == float(k))
        pos8.append(jnp.sum(jnp.where(hit, pos, 0.0), axis=0, keepdims=True))
        gate8.append(jnp.sum(jnp.where(hit, gates_t, 0.0), axis=0, keepdims=True))
    return (jnp.concatenate(pos8, axis=0).astype(jnp.int32), jnp.concatenate(gate8, axis=0), cnt)


def _merge_kernel(n_ctx_blocks, xc_ref, xl_ref, fc_ref, fl_ref, ogc_ref, ogl_ref, omc_ref, oml_ref, zg_ref, m_ref,
                  wbf, wbg, wbm, wout, gn2, wrt, brt, xm_ref, h2_ref, pos_ref, gate_ref, cnt_ref):
    is_ctx = pl.program_id(0) < n_ctx_blocks
    ya = _dot(jnp.where(is_ctx, fc_ref[...], fl_ref[...]), wbf[...])
    yb = _dot(jnp.where(is_ctx, ogc_ref[...], ogl_ref[...]), wbg[...])
    yc = _dot(jnp.where(is_ctx, omc_ref[...], oml_ref[...]), wbm[...])
    merged = (_sigmoid(zg_ref[:, 0:D]) * ya.astype(BF) + _sigmoid(zg_ref[:, D:2 * D]) * yb.astype(BF)
              + _sigmoid(zg_ref[:, 2 * D:3 * D]) * yc.astype(BF))
    xm = jnp.where(is_ctx, xc_ref[...], xl_ref[...]) + m_ref[0, :, 2 * D:3 * D] * _dot(merged, wout[...])
    xm_ref[...] = xm
    h2 = _rms(xm, gn2[...]) * (1.0 + m_ref[0, :, 4 * D:5 * D]) + m_ref[0, :, 3 * D:4 * D]
    h2_ref[...] = h2.astype(BF)
    chosen, gates_t = _route(_dot_f32(wrt[...], h2, dot=_dot_nt), brt[...])
    for sb in range(gates_t.shape[1] // DISP_BLOCK):
        ls = slice(sb * DISP_BLOCK, (sb + 1) * DISP_BLOCK)
        pos8, gate8, cnt = _dispatch_meta(chosen[:, ls], gates_t[:, ls])
        pos_ref[:, ls] = pos8
        gate_ref[:, ls] = gate8
        cnt_ref[sb] = jnp.broadcast_to(cnt, (N_EXP, 128))


def _merge(x_pair, T, mix_ctx, mix_lat, zg, mods, w, n_ctx_blocks, blocks_per_lat):
    tb = TOKEN_BLOCK
    row = functools.partial(_mod_row, n_ctx_blocks, blocks_per_lat)
    rb = lambda wd: pl.BlockSpec((tb, wd), lambda i: (i, 0))
    cb = lambda a: pl.BlockSpec(a.shape, lambda i: (0, 0))
    ctx_b = lambda wd: pl.BlockSpec((tb, wd), lambda i: (jnp.minimum(i, n_ctx_blocks - 1), 0))
    lat_b = lambda wd: pl.BlockSpec((tb, wd), lambda i: (jnp.maximum(i - n_ctx_blocks, 0), 0))
    mix_specs, mix_args = [], []
    for a_c, a_l in zip(mix_ctx, mix_lat):
        mix_specs += [ctx_b(a_c.shape[1]), lat_b(a_l.shape[1])]
        mix_args += [a_c, a_l]
    return pl.pallas_call(
        functools.partial(_merge_kernel, n_ctx_blocks),
        grid=(T // tb,),
        in_specs=_x_pair_specs(x_pair, tb, n_ctx_blocks) + mix_specs + [rb(3 * D),
                  pl.BlockSpec((1, 1, 6 * D), lambda i: (row(i), 0, 0))] + [cb(a) for a in w],
        out_specs=[rb(D), rb(D), pl.BlockSpec((TOP_K, tb), lambda i: (0, i)), pl.BlockSpec((TOP_K, tb), lambda i: (0, i)),
                   pl.BlockSpec((tb // DISP_BLOCK, N_EXP, 128), lambda i: (i, 0, 0))],
        out_shape=[jax.ShapeDtypeStruct((T, D), F32), jax.ShapeDtypeStruct((T, D), BF),
                   jax.ShapeDtypeStruct((TOP_K, T), jnp.int32), jax.ShapeDtypeStruct((TOP_K, T), F32),
                   jax.ShapeDtypeStruct((T // DISP_BLOCK, N_EXP, 128), F32)],
        compiler_params=_params("parallel"),
        name="merge_route",
    )(x_pair[0], x_pair[1], *mix_args, zg, mods, *w)


def _silu_mul(a, b):
    return a * _sigmoid(a) * b


def _placement(pos_ref, weight_ref, r0):
    rows = _iota((DISP_ROW_CHUNK, DISP_BLOCK), 0) + r0
    p = jnp.zeros((DISP_ROW_CHUNK, DISP_BLOCK), F32)
    for k in range(TOP_K):
        w = 1.0 if weight_ref is None else weight_ref[k:k + 1, :]
        p = jnp.where(rows == pos_ref[k:k + 1, :], w, p)
    return p.astype(BF)


def _dispatch_kernel(n_blocks, h_ref, pos_ref, xs_ref):
    h = jnp.where(pl.program_id(0) < n_blocks, h_ref[...], jnp.zeros_like(h_ref))
    for r0 in range(0, DISP_ROWS, DISP_ROW_CHUNK):
        xs_ref[r0:r0 + DISP_ROW_CHUNK, :] = _dot(_placement(pos_ref, None, r0), h).astype(BF)


def _dispatch(h2, pos8):
    T = h2.shape[0]
    nblk = T // DISP_BLOCK
    last = lambda b: jnp.minimum(b, nblk - 1)
    return pl.pallas_call(
        functools.partial(_dispatch_kernel, nblk),
        grid=(nblk + 1,),
        in_specs=[pl.BlockSpec((DISP_BLOCK, D), lambda b: (last(b), 0)),
                  pl.BlockSpec((TOP_K, DISP_BLOCK), lambda b: (0, last(b)))],
        out_specs=pl.BlockSpec((DISP_ROWS, D), lambda b: (b, 0)),
        out_shape=jax.ShapeDtypeStruct(((nblk + 1) * DISP_ROWS, D), BF),
        compiler_params=_params("parallel"),
        name="moe_dispatch",
    )(h2, pos8)


def _tile_tables(cnt, n_chunks_max, n_tiles_max):
    nblk = cnt.shape[0]
    nch = (cnt + (ROW_GRANULE - 1)) // ROW_GRANULE
    first = jnp.arange(nblk, dtype=jnp.int32)[:, None] * BLOCK_CHUNKS + jnp.cumsum(nch, axis=1) - nch
    tiles_e = (jnp.sum(nch, axis=0) + (CHUNKS_PER_TILE - 1)) // CHUNKS_PER_TILE
    span_e = tiles_e * CHUNKS_PER_TILE
    exp_start = jnp.cumsum(span_e) - span_e
    j = jnp.arange(n_chunks_max, dtype=jnp.int32)
    e_j = jnp.sum((exp_start[None, :] <= j[:, None]).astype(jnp.int32), axis=1) - 1
    onehot = (e_j[:, None] == jnp.arange(N_EXP, dtype=jnp.int32)[None, :]).astype(F32)
    rows_of = lambda tab: jnp.dot(onehot, tab.astype(F32), precision=lax.Precision.HIGHEST).astype(jnp.int32)
    local = j - rows_of(exp_start[:, None])[:, 0]
    blk_len = rows_of(nch.T)
    blk_start = rows_of(jnp.cumsum(nch.T, axis=1) - nch.T)
    inside = (blk_start <= local[:, None]) & (local[:, None] < blk_start + blk_len)
    real = jnp.any(inside, axis=1)
    chunk = jnp.sum(jnp.where(inside, rows_of(first.T) + (local[:, None] - blk_start), 0), axis=1)
    pad_rank = jnp.cumsum(jnp.where(real, 0, 1)) - 1
    src = jnp.where(real, chunk, BLOCK_CHUNKS - 1).astype(jnp.int32)
    dst = jnp.where(real, chunk, nblk * BLOCK_CHUNKS + pad_rank % BLOCK_CHUNKS).astype(jnp.int32)
    tile_end = jnp.cumsum(tiles_e)
    i = jnp.arange(n_tiles_max, dtype=jnp.int32)
    tile_expert = jnp.minimum(jnp.sum((tile_end[None, :] <= i[:, None]).astype(jnp.int32), axis=1), N_EXP - 1)
    return src, dst, tile_expert.astype(jnp.int32), tile_end[-1:].astype(jnp.int32)


def _expert_kernel(src_ref, dst_ref, texp_ref, nused_ref, xs_hbm, wg_ref, wu_ref, wd_ref, ys_hbm,
                   xbuf, ybuf, wgu_bf, wd_bf, gsem, ssem):
    i = pl.program_id(0)
    n_used = nused_ref[0]
    slot = lax.rem(i, 2)

    def chunk_copies(tile, slot_, to_buffer, do):
        for c in range(CHUNKS_PER_TILE):
            j = tile * CHUNKS_PER_TILE + c
            if to_buffer:
                cp = pltpu.make_async_copy(xs_hbm.at[src_ref[j]], xbuf.at[slot_, c], gsem.at[slot_])
            else:
                cp = pltpu.make_async_copy(ybuf.at[slot_, c], ys_hbm.at[dst_ref[j]], ssem.at[slot_])
            do(cp)

    start = lambda cp: cp.start()
    wait = lambda cp: cp.wait()

    @pl.when(i == 0)
    def _():
        chunk_copies(0, 0, True, start)

    @pl.when(i == n_used)
    def _():
        chunk_copies(i, slot, True, wait)

    @pl.when(i < n_used)
    def _():
        chunk_copies(i, slot, True, wait)
        chunk_copies(i + 1, 1 - slot, True, start)

        @pl.when((i == 0) | (texp_ref[i] != texp_ref[jnp.maximum(i - 1, 0)]))
        def _():
            wgu_bf[:, 0:E_DIM] = wg_ref[0, 0].astype(BF)
            wgu_bf[:, E_DIM:2 * E_DIM] = wu_ref[0, 0].astype(BF)
            wd_bf[...] = wd_ref[0, 0].astype(BF)

        gu = _dot(xbuf[slot].reshape(EXP_TILE, D), wgu_bf[...])
        hid = _silu_mul(gu[:, 0:E_DIM], gu[:, E_DIM:2 * E_DIM])
        ybuf[slot] = _dot(hid.astype(BF), wd_bf[...]).astype(BF).reshape(CHUNKS_PER_TILE, ROW_GRANULE, D)
        chunk_copies(i, slot, False, start)

        @pl.when(i >= 1)
        def _():
            chunk_copies(i - 1, 1 - slot, False, wait)

        @pl.when(i == n_used - 1)
        def _():
            chunk_copies(i, slot, False, wait)


def _experts(layer, xs, tables, w_eg, w_eu, w_ed, n_tiles_max):
    src, dst, tile_expert, n_used = tables
    chunks = xs.reshape(-1, ROW_GRANULE, D)
    wmap = lambda i, src_, dst_, texp, nu: (layer, texp[i], 0, 0)
    grid_spec = pltpu.PrefetchScalarGridSpec(
        num_scalar_prefetch=4,
        grid=(n_tiles_max + 1,),
        in_specs=[pl.BlockSpec(memory_space=pl.ANY),
                  pl.BlockSpec((1, 1, D, E_DIM), wmap),
                  pl.BlockSpec((1, 1, D, E_DIM), wmap),
                  pl.BlockSpec((1, 1, E_DIM, D), wmap)],
        out_specs=pl.BlockSpec(memory_space=pl.ANY),
        scratch_shapes=[pltpu.VMEM((2, CHUNKS_PER_TILE, ROW_GRANULE, D), BF),
                        pltpu.VMEM((2, CHUNKS_PER_TILE, ROW_GRANULE, D), BF),
                        pltpu.VMEM((D, 2 * E_DIM), BF), pltpu.VMEM((E_DIM, D), BF),
                        pltpu.SemaphoreType.DMA((2,)), pltpu.SemaphoreType.DMA((2,))],
    )
    return pl.pallas_call(
        _expert_kernel,
        grid_spec=grid_spec,
        out_shape=jax.ShapeDtypeStruct(chunks.shape, chunks.dtype),
        input_output_aliases={4: 0},
        compiler_params=_params("arbitrary"),
        name="moe_experts",
    )(src, dst, tile_expert, n_used, chunks, w_eg, w_eu, w_ed).reshape(xs.shape)


def _combine_kernel(final, ys_ref, pos_ref, gate_ref, h_ref, sg_ref, su_ref, sd_ref, x_ref, m_ref, gf_ref, o_ref):
    routed = jnp.zeros((DISP_BLOCK, D), F32)
    for r0 in range(0, DISP_ROWS, DISP_ROW_CHUNK):
        routed = routed + _dot_tn(_placement(pos_ref, gate_ref, r0), ys_ref[r0:r0 + DISP_ROW_CHUNK, :])
    h = h_ref[...]
    sh = _silu_mul(_dot(h, sg_ref[0].astype(BF)), _dot(h, su_ref[0].astype(BF)))
    out = x_ref[...] + m_ref[0] * (routed + _dot(sh.astype(BF), sd_ref[0].astype(BF)))
    if final:
        out = _rms(out, gf_ref[...])
    o_ref[...] = out


def _combine(layer, final, ys, pos8, gate8, h2, w_sg, w_su, w_sd, xm, mods, g_final, n_ctx_blocks, blocks_per_lat,
             first_block, n_blocks):
    tb = DISP_BLOCK
    row = functools.partial(_mod_row, n_ctx_blocks, blocks_per_lat)
    at = lambda b: b + first_block
    return pl.pallas_call(
        functools.partial(_combine_kernel, final),
        grid=(n_blocks,),
        in_specs=[pl.BlockSpec((DISP_ROWS, D), lambda b: (at(b), 0)),
                  pl.BlockSpec((TOP_K, tb), lambda b: (0, at(b))),
                  pl.BlockSpec((TOP_K, tb), lambda b: (0, at(b))),
                  pl.BlockSpec((tb, D), lambda b: (at(b), 0)),
                  pl.BlockSpec((1, D, E_DIM), lambda b: (layer, 0, 0)),
                  pl.BlockSpec((1, D, E_DIM), lambda b: (layer, 0, 0)),
                  pl.BlockSpec((1, E_DIM, D), lambda b: (layer, 0, 0)),
                  pl.BlockSpec((tb, D), lambda b: (at(b), 0)),
                  pl.BlockSpec((1, 1, D), lambda b: (row(at(b)), 0, 5)),
                  pl.BlockSpec((1, D), lambda b: (0, 0))],
        out_specs=pl.BlockSpec((tb, D), lambda b: (b, 0)),
        out_shape=jax.ShapeDtypeStruct((n_blocks * tb, D), F32),
        compiler_params=_params("parallel"),
        name="moe_combine",
    )(ys, pos8, gate8, h2, w_sg, w_su, w_sd, xm, mods, g_final)


def _moe(layer, final, h2, pos8, gate8, cnt, w_eg, w_eu, w_ed, w_sg, w_su, w_sd, xm, mods, g_final, disp_blocks):
    T = h2.shape[0]
    nblk = T // DISP_BLOCK
    n_chunks_max = (TOP_K * T + N_EXP * nblk * (ROW_GRANULE - 1)) // ROW_GRANULE + N_EXP * (CHUNKS_PER_TILE - 1)
    n_tiles_max = -(-n_chunks_max // CHUNKS_PER_TILE)
    xs = _dispatch(h2, pos8)
    tables = _tile_tables(cnt[:, :, 0].astype(jnp.int32), (n_tiles_max + 1) * CHUNKS_PER_TILE, n_tiles_max + 1)
    ys = _experts(layer, xs, tables, w_eg, w_eu, w_ed, n_tiles_max)
    comb = functools.partial(_combine, layer, final, ys, pos8, gate8, h2, w_sg, w_su, w_sd, xm, mods, g_final,
                             *disp_blocks)
    if not final:
        return comb(0, nblk)
    n_ctx = disp_blocks[0]
    return comb(0, n_ctx), comb(n_ctx, nblk - n_ctx)


def _reorder_w_in(w):
    kr = w[:, 2592:2624]
    kr_sw = kr.reshape(D, M_ROPE // 2, 2)[..., ::-1].reshape(kr.shape)
    pad = jnp.zeros(kr.shape, w.dtype)
    parts = [w[:, :1920], w[:, 1952:2592], w[:, 1920:1952], kr, kr_sw, pad, w[:, 2624:]]
    return jnp.concatenate([p.astype(BF) for p in parts], axis=1)


def _reorder_w_q(w_q_up, with_swap):
    L = w_q_up.shape[0]
    w = w_q_up.reshape(L, M_QL, MH, M_NOPE + M_ROPE)
    nope = w[..., :M_NOPE].reshape(L, M_QL, MH * M_NOPE)
    rope = w[..., M_NOPE:]
    parts = [nope, rope.reshape(L, M_QL, MH * M_ROPE)]
    if with_swap:
        parts.append(rope.reshape(L, M_QL, MH, M_ROPE // 2, 2)[..., ::-1].reshape(L, M_QL, MH * M_ROPE))
    return jnp.concatenate(parts, axis=2).astype(BF)


def _reorder_w_kv(w_kv_up):
    L = w_kv_up.shape[0]
    w = w_kv_up.reshape(L, M_KVL, MH, M_NOPE + M_V)
    return jnp.concatenate([w[..., :M_NOPE].reshape(L, M_KVL, MH * M_NOPE),
                            w[..., M_NOPE:].reshape(L, M_KVL, MH * M_V)], axis=2).astype(BF)


def kernel(x_prompt, x_sample, state_gla_fwd, state_gla_bwd, cache_mla_ckv, cache_mla_krope, c, c_ctx, w_mod, b_mod, g_norm1, g_norm2, w_in, w_gla_gate_f, b_gla_gate_f, w_gla_gate_b, b_gla_gate_b, g_gla_out, g_q_a, w_q_up, g_kv_a, w_kv_up, w_br_fourier, w_br_gla, w_br_mla, w_out, w_router, b_router, w_exp_gate, w_exp_up, w_exp_down, w_sh_gate, w_sh_up, w_sh_down, g_final):
    nb, sl, _ = x_prompt.shape
    db, dl, _ = x_sample.shape
    L = w_mod.shape[0]
    t_ctx, t_lat = nb * sl, db * dl
    T = t_ctx + t_lat
    assert sl % G_CHUNK == 0 and dl % G_CHUNK == 0 and dl % GRID_W == 0
    assert t_ctx % dl == 0 and dl % TOKEN_BLOCK == 0 and TOKEN_BLOCK % DISP_BLOCK == 0
    assert t_ctx % TOKEN_BLOCK == 0 and dl % Q_BLOCK == 0 and 1 + db <= 8

    x_pair = (x_prompt.reshape(t_ctx, D), x_sample.reshape(t_lat, D), 0)
    cond = jnp.concatenate([c_ctx[None, :], c, jnp.zeros((7 - db, D), F32)], axis=0)
    mods_all = _modulation(cond, w_mod, b_mod)

    w_in_r = [_reorder_w_in(w_in[l]) for l in range(L)]
    wq_ctx = _reorder_w_q(w_q_up, False)
    wq_lat = _reorder_w_q(w_q_up, True)
    wkv_r = _reorder_w_kv(w_kv_up)
    lat_off = t_ctx // dl

    new_f, new_b, new_ckv, new_kr = [], [], [], []
    for l in range(L):
        mods = mods_all[l].reshape(8, 1, 6 * D)
        tok = (t_ctx // TOKEN_BLOCK, dl // TOKEN_BLOCK)
        zf, zqk, zvr, zmla, zsm, zg = _in_projection(x_pair, T, mods, g_norm1[l][None, :], w_in_r[l], *tok)

        (f_c,) = _fourier(zf, sl, nb, 0)
        (f_l,) = _fourier(zf, dl, db, lat_off)

        gate_w = (w_gla_gate_f[l], b_gla_gate_f[l][None, :], w_gla_gate_b[l], b_gla_gate_b[l][None, :],
                  g_gla_out[l].reshape(1, GH * GDV))
        og_c, s_f, s_b = _gla(zqk, zvr, zsm, gate_w, sl, nb, 0, None)
        og_l, _, _ = _gla(zqk, zvr, zsm, gate_w, dl, db, lat_off, (state_gla_fwd[:, l], state_gla_bwd[:, l]))

        gq, gkv = g_q_a[l][None, :], g_kv_a[l][None, :]
        om_c, ckv = _mla(zmla, zsm, (gq, wq_ctx[l], gkv, wkv_r[l]), sl, nb, 0, None)
        (om_l,) = _mla(zmla, zsm, (gq, wq_lat[l], gkv, wkv_r[l]), dl, db, lat_off,
                       (cache_mla_ckv[:, l], cache_mla_krope[:, l]))

        mw = (w_br_fourier[l].astype(BF), w_br_gla[l].astype(BF), w_br_mla[l].astype(BF), w_out[l].astype(BF),
              g_norm2[l][None, :], w_router[l].T, b_router[l][:, None])
        xm, h2, pos8, gate8, cnt = _merge(x_pair, T, (f_c, og_c, om_c), (f_l, og_l, om_l), zg, mods, mw, *tok)

        x = _moe(l, l == L - 1, h2, pos8, gate8, cnt, w_exp_gate, w_exp_up, w_exp_down, w_sh_gate, w_sh_up, w_sh_down,
                 xm, mods, g_final[None, :], (t_ctx // DISP_BLOCK, dl // DISP_BLOCK))
        if l < L - 1:
            x_pair = (x, x, t_ctx // TOKEN_BLOCK)

        new_f.append(s_f)
        new_b.append(s_b)
        new_ckv.append(ckv.reshape(nb, sl, M_KVL))
        new_kr.append(zsm[:t_ctx, 32:64].reshape(nb, sl, M_ROPE))

    y_prompt = x[0].reshape(nb, sl, D)
    y_sample = x[1].reshape(db, dl, D)
    return (y_prompt, y_sample, jnp.stack(new_f, axis=1), jnp.stack(new_b, axis=1),
            jnp.stack(new_ckv, axis=1), jnp.stack(new_kr, axis=1))
```
